```python
import jax
import jax.numpy as jnp
from jax import lax
import numpy as np

D_MODEL = 1024
BATCH = 16
SEQ = 4096
DEPTH = 4

CTX_LEN = 256
GRID_W = 64
HEAD_DIM = 64
BRANCH_W = D_MODEL // 2
N_BRANCH = 4
NA_HEADS = BRANCH_W // HEAD_DIM
NA_WIN_H = 8
NA_WIN_W = 16
ROPE_THETA = 10000.0
ROPE_AXIS_DIM = HEAD_DIM // 2
ROPE_FREQS = ROPE_AXIS_DIM // 2
FNET_GROUPS = 4
FNET_GROUP_W = BRANCH_W // FNET_GROUPS
GMLP_CHUNK = 128
GMLP_GROUPS = 8
GMLP_GROUP_W = BRANCH_W // GMLP_GROUPS
HGRN_HEADS = 8
HGRN_DK = BRANCH_W // HGRN_HEADS
HGRN_DV = BRANCH_W // HGRN_HEADS
HGRN_CHUNK = 64
EPS = 1e-6
F_FLOOR = 1e-30
NEG_INF = -1e30
F32 = jnp.float32

IN_NAMES = ('a_q', 'a_k', 'a_v', 'a_g', 'b_x', 'b_g', 'c_u', 'c_v', 'c_g',
            'd_q', 'd_f_fwd', 'd_f_bwd', 'd_i', 'd_g', 'gate_0', 'gate_1', 'gate_2', 'gate_3')
IN_SIZES = (BRANCH_W,) * 14 + (D_MODEL,) * N_BRANCH
IN_W = 14 * BRANCH_W + N_BRANCH * D_MODEL

kernel_name = 'hybrid_gated_branch_flow_block'


def in_proj(h, w_in_l, name):
    idx = IN_NAMES.index(name)
    start = sum(IN_SIZES[:idx])
    return h @ w_in_l[:, start:start + IN_SIZES[idx]]


def rmsnorm(x, g):
    xf = x.astype(F32)
    y = xf * lax.rsqrt(jnp.mean(xf * xf, axis=-1, keepdims=True) + EPS)
    return (y * g.astype(F32)).astype(x.dtype)


def heads(t, h):
    return t.reshape(t.shape[0], t.shape[1], h, t.shape[2] // h)


def axial_rope(n_tok):
    t = jnp.arange(n_tok, dtype=jnp.int32)
    pos = jnp.stack([t // GRID_W, t % GRID_W], axis=-1).astype(F32)
    inv = ROPE_THETA ** (-jnp.arange(ROPE_FREQS, dtype=F32) * 2.0 / ROPE_AXIS_DIM)
    ang = pos[:, :, None] * inv
    return jnp.cos(ang), jnp.sin(ang)


def rope2d(x, cos, sin):
    b, n, h, dh = x.shape
    xr = x.reshape(b, n, h, 2, 2, ROPE_FREQS).astype(F32)
    x1, x2 = xr[..., 0, :], xr[..., 1, :]
    c_, s_ = cos[:, None], sin[:, None]
    out = jnp.stack([x1 * c_ - x2 * s_, x2 * c_ + x1 * s_], axis=-2)
    return out.reshape(b, n, h, dh).astype(x.dtype)


def dense_attention(q, k, v):
    s = jnp.einsum('bhqd,bhkd->bhqk', q, k).astype(F32) * (q.shape[-1] ** -0.5)
    p = jax.nn.softmax(s, axis=-1).astype(v.dtype)
    return jnp.einsum('bhqk,bhkd->bhqd', p, v)


def neighbourhood_attention(q_rot, q_plain, k_rot, v, k_ctx, v_ctx, rpb, rows):
    b, n, h, dh = q_rot.shape
    kh = min(NA_WIN_H, rows)
    kw = NA_WIN_W
    scale = dh ** -0.5

    def to_grid(t):
        return t.reshape(b, rows, GRID_W, h, dh).transpose(0, 3, 1, 2, 4)

    qg_r, qg_p, kg, vg = to_grid(q_rot), to_grid(q_plain), to_grid(k_rot), to_grid(v)
    col = jnp.arange(GRID_W)
    col_start = jnp.clip(col - kw // 2, 0, GRID_W - kw)
    col_valid = (col[None, :] >= col_start[:, None]) & (col[None, :] < col_start[:, None] + kw)
    col_idx = jnp.clip(col[None, :] - col[:, None] + NA_WIN_W - 1, 0, 2 * NA_WIN_W - 2)
    rpb_f = rpb.astype(F32)
    band = kh * GRID_W

    def row_block(r):
        rs = jnp.clip(r - kh // 2, 0, rows - kh)
        q_r = lax.dynamic_index_in_dim(qg_r, r, axis=2, keepdims=False)
        q_p = lax.dynamic_index_in_dim(qg_p, r, axis=2, keepdims=False)
        k_band = lax.dynamic_slice_in_dim(kg, rs, kh, axis=2)
        v_band = lax.dynamic_slice_in_dim(vg, rs, kh, axis=2)
        row_idx = rs + jnp.arange(kh) - r + NA_WIN_H - 1
        bias = rpb_f[:, row_idx[None, :, None], col_idx[:, None, :]]
        bias = jnp.where(col_valid[:, None, :], bias, NEG_INF)
        s_band = jnp.einsum('bhqd,bhikd->bhqik', q_r, k_band).astype(F32) * scale + bias
        s_ctx = jnp.einsum('bhqd,bhld->bhql', q_p, k_ctx).astype(F32) * scale
        s = jnp.concatenate([s_band.reshape(b, h, GRID_W, band), s_ctx], axis=-1)
        p = jax.nn.softmax(s, axis=-1).astype(v.dtype)
        o = jnp.einsum('bhqik,bhikd->bhqd', p[..., :band].reshape(b, h, GRID_W, kh, GRID_W), v_band)
        return o + jnp.einsum('bhql,bhld->bhqd', p[..., band:], v_ctx)

    out = lax.map(row_block, jnp.arange(rows))
    return out.transpose(1, 0, 3, 2, 4).reshape(b, n, h * dh)


def na_branch(hx, hc, w, rpb, cos, sin, rows, with_ctx):
    q = heads(in_proj(hx, w, 'a_q'), NA_HEADS)
    k = heads(in_proj(hx, w, 'a_k'), NA_HEADS)
    v = heads(in_proj(hx, w, 'a_v'), NA_HEADS)
    kc = heads(in_proj(hc, w, 'a_k'), NA_HEADS).transpose(0, 2, 1, 3)
    vc = heads(in_proj(hc, w, 'a_v'), NA_HEADS).transpose(0, 2, 1, 3)
    o = neighbourhood_attention(rope2d(q, cos, sin), q, rope2d(k, cos, sin), v, kc, vc, rpb, rows)
    y_lat = o * jax.nn.silu(in_proj(hx, w, 'a_g'))
    if not with_ctx:
        return y_lat, None
    qc = heads(in_proj(hc, w, 'a_q'), NA_HEADS).transpose(0, 2, 1, 3)
    oc = dense_attention(qc, kc, vc).transpose(0, 2, 1, 3).reshape(hc.shape[0], hc.shape[1], BRANCH_W)
    return y_lat, oc * jax.nn.silu(in_proj(hc, w, 'a_g'))


def fourier_branch(h, w, w_f):
    b, n, _ = h.shape
    xb = in_proj(h, w, 'b_x')
    z = xb.reshape(b, n, FNET_GROUPS, FNET_GROUP_W).astype(F32)
    spec = jnp.fft.fft2(z, axes=(1, 3), norm='ortho').real
    y = jnp.einsum('bngc,gcd->bngd', spec, w_f.astype(F32)).reshape(b, n, BRANCH_W).astype(h.dtype)
    return y * jax.nn.silu(in_proj(h, w, 'b_g'))


def gmlp_branch(h, w, g_norm, w_s, b_s):
    b, n, _ = h.shape
    u = in_proj(h, w, 'c_u')
    vn = rmsnorm(in_proj(h, w, 'c_v'), g_norm)
    vn = vn.reshape(b, n // GMLP_CHUNK, GMLP_CHUNK, GMLP_GROUPS, GMLP_GROUP_W)
    mixed = jnp.einsum('gts,bksgc->bktgc', w_s, vn) + b_s.T[None, None, :, :, None]
    return u * mixed.reshape(b, n, BRANCH_W) * jax.nn.silu(in_proj(h, w, 'c_g'))


def forget_gate(z, lb):
    zf = z.astype(F32)
    f = lb + (1.0 - lb) * jax.nn.sigmoid(zf)
    logf = jnp.log(jnp.maximum(f, F_FLOOR))
    k = (1.0 - lb) * jax.nn.sigmoid(-zf)
    return heads(logf, HGRN_HEADS), heads(k, HGRN_HEADS)


def hgrn2_scan(k, logf, i, s0, q=None):
    b, n, h, _ = k.shape
    nc = n // HGRN_CHUNK

    def chunks(t):
        return t.reshape(b, nc, HGRN_CHUNK, h, t.shape[-1]).transpose(1, 0, 3, 2, 4)

    seen = jnp.tril(jnp.ones((HGRN_CHUNK, HGRN_CHUNK), dtype=bool))[:, :, None]

    def step(state, xs):
        kc, gc, ic = xs[0], xs[1], xs[2]
        a = jnp.cumsum(gc, axis=2)
        a_last = a[:, :, -1:, :]
        new_state = jnp.exp(a_last[:, :, 0, :])[..., None] * state + jnp.einsum(
            'bhsk,bhsv->bhkv', kc * jnp.exp(a_last - a), ic)
        if q is None:
            return new_state, None
        qc = xs[3]
        diff = jnp.where(seen, a[:, :, :, None, :] - a[:, :, None, :, :], 0.0)
        decay = jnp.where(seen, jnp.exp(diff), 0.0)
        scores = jnp.einsum('bhtk,bhsk,bhtsk->bhts', qc, kc, decay)
        o = jnp.einsum('bhts,bhsv->bhtv', scores, ic) + jnp.einsum('bhtk,bhkv->bhtv', qc * jnp.exp(a), state)
        return new_state, o

    if q is None:
        final, _ = lax.scan(step, s0, (chunks(k), chunks(logf), chunks(i)))
        return None, final
    final, o = lax.scan(step, s0, (chunks(k), chunks(logf), chunks(i), chunks(q)))
    return o.transpose(1, 0, 3, 2, 4).reshape(b, n, h, o.shape[-1]), final


def hgrn2_readout(o, gate, g_norm):
    b, n = o.shape[0], o.shape[1]
    on = o * lax.rsqrt(jnp.mean(o * o, axis=-1, keepdims=True) + EPS)
    y = (on.reshape(b, n, BRANCH_W) * g_norm.astype(F32)).astype(gate.dtype)
    return y * jax.nn.silu(gate)


def hgrn2_branch(hx, hc, w, lb, g_norm, with_ctx):
    def prep(h, with_q):
        q = heads(in_proj(h, w, 'd_q'), HGRN_HEADS).astype(F32) if with_q else None
        i = heads(in_proj(h, w, 'd_i'), HGRN_HEADS).astype(F32)
        lf_f, k_f = forget_gate(in_proj(h, w, 'd_f_fwd'), lb[0])
        lf_b, k_b = forget_gate(in_proj(h, w, 'd_f_bwd'), lb[1])
        return q, i, lf_f, k_f, lf_b, k_b

    def flip(t):
        return None if t is None else jnp.flip(t, axis=1)

    q, i, lf_f, k_f, lf_b, k_b = prep(hx, True)
    qc, ic, lfc_f, kc_f, lfc_b, kc_b = prep(hc, with_ctx)
    s0 = jnp.zeros((hx.shape[0], HGRN_HEADS, HGRN_DK, HGRN_DV), F32)
    oc_f, sc_f = hgrn2_scan(kc_f, lfc_f, ic, s0, qc)
    oc_b, sc_b = hgrn2_scan(flip(kc_b), flip(lfc_b), flip(ic), s0, flip(qc))
    o_f, _ = hgrn2_scan(k_f, lf_f, i, sc_f, q)
    o_b, _ = hgrn2_scan(flip(k_b), flip(lf_b), flip(i), sc_b, flip(q))
    y_lat = hgrn2_readout(o_f + flip(o_b), in_proj(hx, w, 'd_g'), g_norm)
    if not with_ctx:
        return y_lat, None
    return y_lat, hgrn2_readout(oc_f + flip(oc_b), in_proj(hc, w, 'd_g'), g_norm)


def merge_branches(ys, h, w, w_branch_l, w_out_l):
    merged = None
    for r in range(N_BRANCH):
        term = jax.nn.sigmoid(in_proj(h, w, 'gate_%d' % r)) * (ys[r] @ w_branch_l[r])
        merged = term if merged is None else merged + term
    return merged @ w_out_l


def setup_inputs(seed: int = 0) -> dict:
    key = jax.random.key(seed)
    ks = jax.random.split(key, 18)

    def nrm(k, shape, scale):
        return jax.random.normal(k, shape, F32) * scale

    return {
        'x': nrm(ks[0], (BATCH, SEQ, D_MODEL), 1.0),
        'c': nrm(ks[1], (BATCH, D_MODEL), 1.0),
        'ctx': nrm(ks[2], (BATCH, CTX_LEN, D_MODEL), 1.0),
        'c_ctx': nrm(ks[3], (D_MODEL,), 1.0),
        'w_ada': nrm(ks[4], (DEPTH, D_MODEL, 3 * D_MODEL), D_MODEL ** -0.5),
        'b_ada': nrm(ks[5], (DEPTH, 3 * D_MODEL), 0.02),
        'g_pre': 1.0 + nrm(ks[6], (DEPTH, D_MODEL), 0.02),
        'g_post': 1.0 + nrm(ks[7], (DEPTH, D_MODEL), 0.02),
        'w_in': nrm(ks[8], (DEPTH, D_MODEL, IN_W), D_MODEL ** -0.5),
        'na_rpb': nrm(ks[9], (DEPTH, NA_HEADS, 2 * NA_WIN_H - 1, 2 * NA_WIN_W - 1), 0.5),
        'fnet_w': nrm(ks[10], (DEPTH, FNET_GROUPS, FNET_GROUP_W, FNET_GROUP_W), FNET_GROUP_W ** -0.5),
        'gmlp_norm_g': 1.0 + nrm(ks[11], (DEPTH, BRANCH_W), 0.02),
        'gmlp_ws': nrm(ks[12], (DEPTH, GMLP_GROUPS, GMLP_CHUNK, GMLP_CHUNK), GMLP_CHUNK ** -0.5),
        'gmlp_bs': 1.0 + nrm(ks[13], (DEPTH, GMLP_GROUPS, GMLP_CHUNK), 0.02),
        'hgrn_lb_logits': nrm(ks[14], (DEPTH, 2, BRANCH_W), 0.5),
        'hgrn_norm_g': 1.0 + nrm(ks[15], (DEPTH, BRANCH_W), 0.02),
        'w_branch': nrm(ks[16], (DEPTH, N_BRANCH, BRANCH_W, D_MODEL), BRANCH_W ** -0.5),
        'w_out': nrm(ks[17], (DEPTH, D_MODEL, D_MODEL), D_MODEL ** -0.5),
    }


def reference(x, c, ctx, c_ctx, w_ada, b_ada, g_pre, g_post, w_in, na_rpb, fnet_w,
              gmlp_norm_g, gmlp_ws, gmlp_bs, hgrn_lb_logits, hgrn_norm_g, w_branch, w_out):
    n_tok = x.shape[1]
    rows = n_tok // GRID_W
    cos, sin = axial_rope(n_tok)
    lb_sm = jax.nn.softmax(hgrn_lb_logits.astype(F32), axis=0)
    lower_bounds = jnp.maximum(jnp.cumsum(lb_sm, axis=0) - lb_sm[0:1], 0.0)
    silu_c = jax.nn.silu(c)
    silu_cc = jax.nn.silu(c_ctx)
    for l in range(DEPTH):
        with_ctx = l < DEPTH - 1
        sh_x, sc_x, gt_x = jnp.split(silu_c @ w_ada[l] + b_ada[l], 3, axis=-1)
        sh_c, sc_c, gt_c = jnp.split(silu_cc @ w_ada[l] + b_ada[l], 3, axis=-1)
        hx = rmsnorm(x, g_pre[l]) * (1.0 + sc_x[:, None]) + sh_x[:, None]
        hc = rmsnorm(ctx, g_pre[l]) * (1.0 + sc_c) + sh_c
        w = w_in[l]
        ya, ya_c = na_branch(hx, hc, w, na_rpb[l], cos, sin, rows, with_ctx)
        yb = fourier_branch(hx, w, fnet_w[l])
        yc = gmlp_branch(hx, w, gmlp_norm_g[l], gmlp_ws[l], gmlp_bs[l])
        yd, yd_c = hgrn2_branch(hx, hc, w, lower_bounds[l], hgrn_norm_g[l], with_ctx)
        out_x = merge_branches((ya, yb, yc, yd), hx, w, w_branch[l], w_out[l])
        if with_ctx:
            yb_c = fourier_branch(hc, w, fnet_w[l])
            yc_c = gmlp_branch(hc, w, gmlp_norm_g[l], gmlp_ws[l], gmlp_bs[l])
            out_c = merge_branches((ya_c, yb_c, yc_c, yd_c), hc, w, w_branch[l], w_out[l])
            ctx = ctx + gt_c * rmsnorm(out_c, g_post[l])
        x = x + gt_x[:, None] * rmsnorm(out_x, g_post[l])
    return x
```

```python
import functools

import numpy as np
import jax
import jax.numpy as jnp
from jax import lax
from jax.experimental import pallas as pl
from jax.experimental.pallas import tpu as pltpu

F32 = jnp.float32
BF16 = jnp.bfloat16

D_MODEL = 1024
BRANCH_W = 512
N_BRANCH = 4
GRID_W = 64
HEAD_DIM = 64
N_HEADS = 8
LANES = 128
N_PAIRS = BRANCH_W // LANES
NA_WIN_H = 8
NA_WIN_W = 16
ROPE_THETA = 10000.0
ROPE_FREQS = 16
FNET_GROUPS = 4
FNET_GROUP_W = 128
GMLP_CHUNK = 128
GMLP_GROUPS = 8
HGRN_CHUNK = 64
HGRN_SUB = 16
HGRN_EXP_CLAMP = 80.0
HGRN_ROWS = 256
EPS = 1e-6
F_FLOOR = 1e-30
NEG_INF = -1e30

VMEM_LIMIT = 56 * 2**20

_IN_COL = {'a_q': 0, 'a_k': 1, 'a_v': 2, 'a_g': 3, 'b_x': 4, 'b_g': 5, 'c_u': 6, 'c_v': 7, 'c_g': 8,
           'd_q': 9, 'd_f_fwd': 10, 'd_f_bwd': 11, 'd_i': 12, 'd_g': 13}
_PROJ_NAMES = ('a_q', 'a_k', 'a_v', 'b_x', 'c_v', 'd_q', 'd_f_fwd', 'd_f_bwd', 'd_i')
_MERGE_NAMES = ('c_u', 'a_g', 'b_g', 'c_g', 'd_g')
_GATE_COL0 = 14 * BRANCH_W


def _cparams(sem):
    return pltpu.CompilerParams(dimension_semantics=sem, vmem_limit_bytes=VMEM_LIMIT)


def _resident(shape):
    nd = len(shape)
    return pl.BlockSpec(shape, lambda *_: (0,) * nd, pipeline_mode=pl.Buffered(1))


def _silu(t):
    return t * jax.nn.sigmoid(t)


def _lane_iota(shape):
    return lax.broadcasted_iota(jnp.int32, shape, len(shape) - 1)


def _dot(a, b):
    return jnp.dot(a, b, preferred_element_type=F32)


def _dot_nt(a, b):
    return lax.dot_general(a, b, (((1,), (1,)), ((), ())), preferred_element_type=F32)


def _dot_tn(a, b):
    return lax.dot_general(a, b, (((0,), (0,)), ((), ())), preferred_element_type=F32)


def _split3(t):
    hi = t.astype(BF16)
    r1 = t - hi.astype(F32)
    mid = r1.astype(BF16)
    lo = (r1 - mid.astype(F32)).astype(BF16)
    return hi, mid, lo


def _normed_input(x, gpre, sc, sh):
    ms = jnp.mean(x * x, axis=-1, keepdims=True)
    h = x * lax.rsqrt(ms + EPS) * gpre
    return h * (1.0 + sc) + sh


def _mod_kernel(c_ref, w_ref, b_ref, o_ref):
    s = _silu(c_ref[...]).astype(BF16)
    o_ref[0] = _dot(s, w_ref[0].astype(BF16)) + b_ref[0]


def _modulation(c_all, w_ada, b_ada):
    depth = w_ada.shape[0]
    rows = c_all.shape[0]
    tn = 1024
    return pl.pallas_call(
        _mod_kernel,
        out_shape=jax.ShapeDtypeStruct((depth, rows, 3 * D_MODEL), F32),
        grid=(depth, 3 * D_MODEL // tn),
        in_specs=[pl.BlockSpec((rows, D_MODEL), lambda l, j: (0, 0)),
                  pl.BlockSpec((1, D_MODEL, tn), lambda l, j: (l, 0, j)),
                  pl.BlockSpec((1, 1, tn), lambda l, j: (l, 0, j))],
        out_specs=pl.BlockSpec((1, rows, tn), lambda l, j: (l, 0, j)),
        compiler_params=_cparams(("arbitrary", "arbitrary")),
        name="adaln_modulation",
    )(c_all, w_ada, b_ada.reshape(depth, 1, 3 * D_MODEL))


def _lb_kernel(lg_ref, o_ref):
    depth = lg_ref.shape[0]
    lg = [lg_ref[l] for l in range(depth)]
    m = functools.reduce(jnp.maximum, lg)
    e = [jnp.exp(t - m) for t in lg]
    tot = functools.reduce(lambda a, b: a + b, e)
    sm = [t / tot for t in e]
    run = jnp.zeros_like(sm[0])
    for l in range(depth):
        run = run + sm[l]
        o_ref[l] = jnp.maximum(run - sm[0], 0.0)


def _lower_bounds(lb_logits):
    return pl.pallas_call(
        _lb_kernel,
        out_shape=jax.ShapeDtypeStruct(lb_logits.shape, F32),
        name="hgrn_lower_bounds",
    )(lb_logits)


def _rope(t, cos, sin_signed, first_half):
    up = pltpu.roll(t, LANES - ROPE_FREQS, 1)
    down = pltpu.roll(t, ROPE_FREQS, 1)
    return t * cos + jnp.where(first_half, up, down) * sin_signed


def _proj_kernel(x_ref, sh_ref, sc_ref, gpre_ref, w_ref, cos_ref, sin_ref, gn_ref, ws_ref, bs_ref,
                 lb_ref, qr_ref, qp_ref, kr_ref, v_ref, bx_ref, mx_ref, dq_ref, di_ref,
                 lff_ref, kf_ref, lfb_ref, kb_ref):
    tm = x_ref.shape[1]
    hb = _normed_input(x_ref[0], gpre_ref[...], sc_ref[0], sh_ref[0]).astype(BF16)

    def proj(name):
        j = _PROJ_NAMES.index(name)
        return _dot(hb, w_ref[:, j * BRANCH_W:(j + 1) * BRANCH_W])

    cos = cos_ref[...]
    sin_signed = sin_ref[...]
    first_half = (_lane_iota((tm, LANES)) % (2 * ROPE_FREQS)) < ROPE_FREQS

    def rope_all(t):
        return jnp.concatenate(
            [_rope(t[:, p * LANES:(p + 1) * LANES], cos, sin_signed, first_half) for p in range(N_PAIRS)],
            axis=1)

    q = proj('a_q') * (HEAD_DIM ** -0.5)
    qp_ref[0] = q.astype(BF16)
    qr_ref[0] = rope_all(q).astype(BF16)
    kr_ref[0] = rope_all(proj('a_k')).astype(BF16)
    v_ref[0] = proj('a_v').astype(BF16)
    bx_ref[0] = proj('b_x').astype(BF16)
    dq_ref[0] = proj('d_q').astype(BF16)
    di_ref[0] = proj('d_i').astype(BF16)

    cv = proj('c_v')
    vn = (cv * lax.rsqrt(jnp.mean(cv * cv, axis=-1, keepdims=True) + EPS) * gn_ref[...]).astype(BF16)
    low_group = _lane_iota((GMLP_CHUNK, LANES)) < (LANES // 2)
    for ch in range(tm // GMLP_CHUNK):
        r0 = ch * GMLP_CHUNK
        for p in range(N_PAIRS):
            slab = vn[r0:r0 + GMLP_CHUNK, p * LANES:(p + 1) * LANES]
            mixed = jnp.where(low_group, _dot(ws_ref[2 * p], slab), _dot(ws_ref[2 * p + 1], slab))
            mixed = mixed + bs_ref[:, p * LANES:(p + 1) * LANES]
            mx_ref[0, r0:r0 + GMLP_CHUNK, p * LANES:(p + 1) * LANES] = mixed.astype(BF16)

    for name, d, lf_ref, k_ref in (('d_f_fwd', 0, lff_ref, kf_ref), ('d_f_bwd', 1, lfb_ref, kb_ref)):
        z = proj(name)
        lb = lb_ref[0, d:d + 1, :]
        f = lb + (1.0 - lb) * jax.nn.sigmoid(z)
        lf_ref[0] = jnp.log(jnp.maximum(f, F_FLOOR))
        k_ref[0] = ((1.0 - lb) * jax.nn.sigmoid(-z)).astype(BF16)


def _proj_call(x, sh, sc, gpre, w_proj, cos, sin_signed, gn, ws, bs_tab, lb, tm, per_batch_mod):
    b, n, _ = x.shape
    nt = n // tm
    mod_map = (lambda i, j: (i, 0, 0)) if per_batch_mod else (lambda i, j: (0, 0, 0))
    row_spec = pl.BlockSpec((1, tm, BRANCH_W), lambda i, j: (i, j, 0))
    bf = jax.ShapeDtypeStruct((b, n, BRANCH_W), BF16)
    f32 = jax.ShapeDtypeStruct((b, n, BRANCH_W), F32)
    return pl.pallas_call(
        _proj_kernel,
        out_shape=(bf, bf, bf, bf, bf, bf, bf, bf, f32, bf, f32, bf),
        grid=(b, nt),
        in_specs=[pl.BlockSpec((1, tm, D_MODEL), lambda i, j: (i, j, 0)),
                  pl.BlockSpec((1, 1, D_MODEL), mod_map),
                  pl.BlockSpec((1, 1, D_MODEL), mod_map),
                  _resident((1, D_MODEL)),
                  _resident((D_MODEL, len(_PROJ_NAMES) * BRANCH_W)),
                  pl.BlockSpec((tm, LANES), lambda i, j: (j, 0)),
                  pl.BlockSpec((tm, LANES), lambda i, j: (j, 0)),
                  _resident((1, BRANCH_W)),
                  _resident((GMLP_GROUPS, GMLP_CHUNK, GMLP_CHUNK)),
                  _resident((GMLP_CHUNK, BRANCH_W)),
                  _resident((1, 2, BRANCH_W))],
        out_specs=(row_spec,) * 12,
        compiler_params=_cparams(("arbitrary", "arbitrary")),
        name="branch_proj",
    )(x, sh, sc, gpre, w_proj, cos, sin_signed, gn, ws, bs_tab, lb)


def _head_mask(hh):
    lane = _lane_iota((1, LANES))
    return (lane < HEAD_DIM) if hh == 0 else (lane >= HEAD_DIM)


def _softmax_pv(parts):
    mx = functools.reduce(jnp.maximum, [jnp.max(s, axis=-1, keepdims=True) for s, _ in parts])
    ps = [jnp.exp(s - mx) for s, _ in parts]
    den = functools.reduce(lambda a, b: a + b, [jnp.sum(p, axis=-1, keepdims=True) for p in ps])
    acc = functools.reduce(lambda a, b: a + b, [_dot(p.astype(BF16), v) for p, (_, v) in zip(ps, parts)])
    return acc * (1.0 / den)


def _nattn_kernel(qr_ref, qp_ref, k_ref, v_ref, kc_ref, vc_ref, bias_ref, o_ref):
    rows = k_ref.shape[1] // GRID_W
    r = pl.program_id(1)
    rs = jnp.clip(r - NA_WIN_H // 2, 0, rows - NA_WIN_H)
    start = pl.multiple_of(rs * GRID_W, GRID_W)
    band = NA_WIN_H * GRID_W
    low_head = _lane_iota((GRID_W, LANES)) < HEAD_DIM
    for p in range(N_PAIRS):
        ls = slice(p * LANES, (p + 1) * LANES)
        kb = k_ref[0, pl.ds(start, band), ls]
        vb = v_ref[0, pl.ds(start, band), ls]
        kc = kc_ref[0, :, ls]
        vc = vc_ref[0, :, ls]
        qr = qr_ref[0, :, ls]
        qp = qp_ref[0, :, ls]
        outs = []
        for hh in range(2):
            m = _head_mask(hh)
            zero = jnp.zeros_like(qr)
            s_band = _dot_nt(jnp.where(m, qr, zero), kb) + bias_ref[0, 2 * p + hh]
            s_ctx = _dot_nt(jnp.where(m, qp, zero), kc)
            outs.append(_softmax_pv([(s_band, vb), (s_ctx, vc)]))
        o_ref[0, :, ls] = jnp.where(low_head, outs[0], outs[1]).astype(BF16)


def _nattn_call(qr, qp, k, v, kc, vc, bias):
    b, n, _ = qr.shape
    rows = n // GRID_W
    lc = kc.shape[1]
    band = NA_WIN_H * GRID_W

    def bias_map(i, r):
        rs = jnp.clip(r - NA_WIN_H // 2, 0, rows - NA_WIN_H)
        return (rs - r + NA_WIN_H - 1, 0, 0, 0)

    q_spec = pl.BlockSpec((1, GRID_W, BRANCH_W), lambda i, r: (i, r, 0))
    full = pl.BlockSpec((1, n, BRANCH_W), lambda i, r: (i, 0, 0))
    ctx = pl.BlockSpec((1, lc, BRANCH_W), lambda i, r: (i, 0, 0))
    return pl.pallas_call(
        _nattn_kernel,
        out_shape=jax.ShapeDtypeStruct((b, n, BRANCH_W), BF16),
        grid=(b, rows),
        in_specs=[q_spec, q_spec, full, full, ctx, ctx,
                  pl.BlockSpec((1, N_HEADS, GRID_W, band), bias_map)],
        out_specs=q_spec,
        compiler_params=_cparams(("arbitrary", "arbitrary")),
        name="neighbourhood_attention",
    )(qr, qp, k, v, kc, vc, bias)


def _cattn_kernel(q_ref, k_ref, v_ref, o_ref):
    lc = q_ref.shape[1]
    low_head = _lane_iota((lc, LANES)) < HEAD_DIM
    for p in range(N_PAIRS):
        ls = slice(p * LANES, (p + 1) * LANES)
        q = q_ref[0, :, ls]
        k = k_ref[0, :, ls]
        v = v_ref[0, :, ls]
        outs = []
        for hh in range(2):
            s = _dot_nt(jnp.where(_head_mask(hh), q, jnp.zeros_like(q)), k)
            outs.append(_softmax_pv([(s, v)]))
        o_ref[0, :, ls] = jnp.where(low_head, outs[0], outs[1]).astype(BF16)


def _cattn_call(q, k, v):
    b, lc, _ = q.shape
    spec = pl.BlockSpec((1, lc, BRANCH_W), lambda i: (i, 0, 0))
    return pl.pallas_call(
        _cattn_kernel,
        out_shape=jax.ShapeDtypeStruct((b, lc, BRANCH_W), BF16),
        grid=(b,),
        in_specs=[spec, spec, spec],
        out_specs=spec,
        compiler_params=_cparams(("arbitrary",)),
        name="context_attention",
    )(q, k, v)


def _attention_bias(rpb):
    col = jnp.arange(GRID_W)
    col_start = jnp.clip(col - NA_WIN_W // 2, 0, GRID_W - NA_WIN_W)
    valid = (col[None, :] >= col_start[:, None]) & (col[None, :] < col_start[:, None] + NA_WIN_W)
    col_idx = jnp.clip(col[None, :] - col[:, None] + NA_WIN_W - 1, 0, 2 * NA_WIN_W - 2)
    per_row = jnp.where(valid[None, None], rpb.astype(F32)[:, :, col_idx], NEG_INF)
    tabs = []
    for off in range(NA_WIN_H):
        t = per_row[:, off:off + NA_WIN_H]
        tabs.append(t.transpose(0, 2, 1, 3).reshape(N_HEADS, GRID_W, NA_WIN_H * GRID_W))
    return jnp.stack(tabs)


_KRON = 8


@functools.lru_cache(maxsize=None)
def _fourier_consts(n):
    rows = n // GRID_W
    k1 = np.arange(rows)[:, None, None, None]
    l1 = np.arange(_KRON)[None, :, None, None]
    n1 = np.arange(rows)[None, None, :, None]
    l2 = np.arange(_KRON)[None, None, None, :]
    a_cos, a_sin = [], []
    for j in range(GRID_W // _KRON):
        ang = 2.0 * np.pi * k1 * (GRID_W * n1 + _KRON * j + l1) / n
        same = (l1 == l2)
        a_cos.append((np.cos(ang) * same).reshape(rows * _KRON, rows * _KRON))
        a_sin.append((-np.sin(ang) * same).reshape(rows * _KRON, rows * _KRON))
    k2 = np.arange(GRID_W)[:, None, None, None]
    ang = 2.0 * np.pi * k2 * np.arange(GRID_W)[None, None, None, :] / GRID_W
    same = (np.arange(_KRON)[None, :, None, None] == np.arange(_KRON)[None, None, :, None])
    b_cos = (np.cos(ang) * same).reshape(GRID_W * _KRON, _KRON * GRID_W)
    b_sin = (np.sin(ang) * same).reshape(GRID_W * _KRON, _KRON * GRID_W)
    b_re = np.concatenate([b_cos, b_sin], axis=1)
    b_im = np.concatenate([-b_sin, b_cos], axis=1)
    return (np.stack(a_cos).astype(np.float32), np.stack(a_sin).astype(np.float32),
            b_re.astype(np.float32), b_im.astype(np.float32))


@functools.lru_cache(maxsize=None)
def _channel_dft():
    c = np.arange(FNET_GROUP_W)
    ang = 2.0 * np.pi * np.outer(c, c) / FNET_GROUP_W
    return np.concatenate([np.cos(ang), np.sin(ang)], axis=0).astype(np.float32)


@functools.lru_cache(maxsize=None)
def _dense_dft(n):
    t = np.arange(n)
    ang = 2.0 * np.pi * np.outer(t, t) / n
    return np.cos(ang).astype(np.float32), (-np.sin(ang)).astype(np.float32)


def _channel_stage(xr, xi, cs_ref, wf_ref, norm):
    outs = []
    for g in range(FNET_GROUPS):
        ls = slice(g * FNET_GROUP_W, (g + 1) * FNET_GROUP_W)
        xg = jnp.concatenate([xr[:, ls], xi[:, ls]], axis=1).astype(BF16)
        spec = _dot(xg, cs_ref[...]) * norm
        outs.append(_dot(spec.astype(BF16), wf_ref[g]))
    return jnp.concatenate(outs, axis=1)


def _fourier_kernel(x_ref, ac_ref, as_ref, bre_ref, bim_ref, cs_ref, wf_ref, o_ref, s_ref, *, norm):
    rows = x_ref.shape[1]
    blk = rows * _KRON
    pair = 2 * _KRON
    for jj in range(GRID_W // pair):
        xt = x_ref[0, :, jj * pair:(jj + 1) * pair, :].astype(F32)
        re, im = [], []
        for half in range(2):
            xc = xt[:, half * _KRON:(half + 1) * _KRON, :].reshape(blk, BRANCH_W).astype(BF16)
            re.append(_dot(ac_ref[2 * jj + half], xc).reshape(rows, _KRON, BRANCH_W))
            im.append(_dot(as_ref[2 * jj + half], xc).reshape(rows, _KRON, BRANCH_W))
        s_ref[0, :, jj * pair:(jj + 1) * pair, :] = jnp.concatenate(re, axis=1).astype(BF16)
        s_ref[1, :, jj * pair:(jj + 1) * pair, :] = jnp.concatenate(im, axis=1).astype(BF16)
    sblk = _KRON * GRID_W
    for mm in range(rows // pair):
        ys = []
        for half in range(2):
            m0 = (2 * mm + half) * _KRON
            rhs = jnp.concatenate([s_ref[0, m0:m0 + _KRON].reshape(sblk, BRANCH_W),
                                   s_ref[1, m0:m0 + _KRON].reshape(sblk, BRANCH_W)], axis=0)
            xr = _dot(bre_ref[...], rhs)
            xi = _dot(bim_ref[...], rhs)
            ys.append(_channel_stage(xr, xi, cs_ref, wf_ref, norm).reshape(GRID_W, _KRON, BRANCH_W))
        o_ref[0, :, mm * pair:(mm + 1) * pair, :] = jnp.concatenate(ys, axis=1).astype(BF16)


def _fourier_call(bx, wf):
    b, n, _ = bx.shape
    rows = n // GRID_W
    a_cos, a_sin, b_re, b_im = (jnp.asarray(t, BF16) for t in _fourier_consts(n))
    cs = jnp.asarray(_channel_dft(), BF16)
    norm = float(1.0 / np.sqrt(n * FNET_GROUP_W))
    x4 = bx.reshape(b, rows, GRID_W, BRANCH_W)
    out = pl.pallas_call(
        functools.partial(_fourier_kernel, norm=norm),
        out_shape=jax.ShapeDtypeStruct((b, GRID_W, rows, BRANCH_W), BF16),
        grid=(b,),
        in_specs=[pl.BlockSpec((1, rows, GRID_W, BRANCH_W), lambda i: (i, 0, 0, 0)),
                  _resident(a_cos.shape), _resident(a_sin.shape),
                  _resident(b_re.shape), _resident(b_im.shape),
                  _resident(cs.shape), _resident(wf.shape)],
        out_specs=pl.BlockSpec((1, GRID_W, rows, BRANCH_W), lambda i: (i, 0, 0, 0)),
        scratch_shapes=[pltpu.VMEM((2, rows, GRID_W, BRANCH_W), BF16)],
        compiler_params=_cparams(("arbitrary",)),
        name="fourier_mix",
    )(x4, a_cos, a_sin, b_re, b_im, cs, wf)
    return out.reshape(b, n, BRANCH_W)


def _fourier_ctx_kernel(x_ref, c_ref, s_ref, cs_ref, wf_ref, o_ref, *, norm):
    x = x_ref[0]
    xr = _dot(c_ref[...], x)
    xi = _dot(s_ref[...], x)
    o_ref[0] = _channel_stage(xr, xi, cs_ref, wf_ref, norm).astype(BF16)


def _fourier_ctx_call(bx, wf):
    b, n, _ = bx.shape
    cn, sn = (jnp.asarray(t, BF16) for t in _dense_dft(n))
    cs = jnp.asarray(_channel_dft(), BF16)
    norm = float(1.0 / np.sqrt(n * FNET_GROUP_W))
    spec = pl.BlockSpec((1, n, BRANCH_W), lambda i: (i, 0, 0))
    return pl.pallas_call(
        functools.partial(_fourier_ctx_kernel, norm=norm),
        out_shape=jax.ShapeDtypeStruct((b, n, BRANCH_W), BF16),
        grid=(b,),
        in_specs=[spec, _resident(cn.shape), _resident(sn.shape), _resident(cs.shape), _resident(wf.shape)],
        out_specs=spec,
        compiler_params=_cparams(("arbitrary",)),
        name="fourier_mix_context",
    )(bx, cn, sn, cs, wf)


def _hgrn_chunk(q, k, i, lf, state_ref, d, reverse):
    c = HGRN_CHUNK
    nsub = c // HGRN_SUB
    row = lax.broadcasted_iota(jnp.int32, (c, c), 0)
    col = lax.broadcasted_iota(jnp.int32, (c, c), 1)
    tri = ((col >= row) if reverse else (col <= row)).astype(BF16)
    a = functools.reduce(lambda u, w: u + w, [_dot(tri, piece) for piece in _split3(lf)])

    def bcast(r):
        return jnp.broadcast_to(a[r:r + 1, :], (HGRN_SUB, BRANCH_W))

    zeros = jnp.zeros((HGRN_SUB, BRANCH_W), F32)
    if reverse:
        refs = [bcast((s + 1) * HGRN_SUB) if s + 1 < nsub else zeros for s in range(nsub)]
        a_end = a[0:1, :]
    else:
        refs = [bcast(s * HGRN_SUB - 1) if s > 0 else zeros for s in range(nsub)]
        a_end = a[c - 1:c, :]
    ref_rows = jnp.concatenate(refs, axis=0)
    qf = q.astype(F32)
    kf = k.astype(F32)
    q_sub = (qf * jnp.exp(a - ref_rows)).astype(BF16)
    q_in = (qf * jnp.exp(a)).astype(BF16)
    k_out = (kf * jnp.exp(a_end - a)).astype(BF16)
    k_sub = [(kf * jnp.exp(jnp.minimum(refs[s][0:1, :] - a, HGRN_EXP_CLAMP))).astype(BF16)
             for s in range(nsub)]
    decay_end = jnp.exp(a_end)

    lane = _lane_iota((c, LANES))
    src = lane % HEAD_DIM
    step = lax.broadcasted_iota(jnp.int32, (c, LANES), 0)
    seen = (src >= step) if reverse else (src <= step)
    low_rows = lax.broadcasted_iota(jnp.int32, (LANES, LANES), 0) < HEAD_DIM
    low_cols = _lane_iota((LANES, LANES)) < HEAD_DIM
    same_head = low_rows == low_cols
    lo = _lane_iota((c, LANES)) < HEAD_DIM

    def block_diag(t):
        z = jnp.zeros_like(t)
        return jnp.concatenate([jnp.where(lo, t, z), jnp.where(lo, z, t)], axis=0)

    outs = []
    for p in range(N_PAIRS):
        ls = slice(p * LANES, (p + 1) * LANES)
        qs = q_sub[:, ls]
        sc = jnp.concatenate(
            [_dot_nt(qs[s * HGRN_SUB:(s + 1) * HGRN_SUB], block_diag(k_sub[s][:, ls])) for s in range(nsub)],
            axis=0)
        scores = jnp.where(seen, sc, 0.0).astype(BF16)
        ip = i[:, ls]
        st = state_ref[d, p]
        o = _dot(scores, block_diag(ip)) + _dot_nt(q_in[:, ls], st.astype(BF16))
        upd = _dot_tn(ip, k_out[:, ls])
        state_ref[d, p] = decay_end[:, ls] * st + jnp.where(same_head, upd, 0.0)
        outs.append(o)
    return jnp.concatenate(outs, axis=1)


def _hgrn_kernel(qf_ref, if_ref, lff_ref, kf_ref, qb_ref, ib_ref, lfb_ref, kb_ref, s0f_ref, s0b_ref,
                 of_ref, ob_ref, sf_ref, sb_ref, state_ref):
    j = pl.program_id(1)
    nchunk = qf_ref.shape[1] // HGRN_CHUNK

    @pl.when(j == 0)
    def _():
        state_ref[0] = s0f_ref[0]
        state_ref[1] = s0b_ref[0]

    for cix in range(nchunk):
        fs = slice(cix * HGRN_CHUNK, (cix + 1) * HGRN_CHUNK)
        of_ref[0, fs, :] = _hgrn_chunk(qf_ref[0, fs, :], kf_ref[0, fs, :], if_ref[0, fs, :], lff_ref[0, fs, :],
                                       state_ref, 0, False).astype(BF16)
        bix = nchunk - 1 - cix
        bs = slice(bix * HGRN_CHUNK, (bix + 1) * HGRN_CHUNK)
        ob_ref[0, bs, :] = _hgrn_chunk(qb_ref[0, bs, :], kb_ref[0, bs, :], ib_ref[0, bs, :], lfb_ref[0, bs, :],
                                       state_ref, 1, True).astype(BF16)

    @pl.when(j == pl.num_programs(1) - 1)
    def _():
        sf_ref[0] = state_ref[0]
        sb_ref[0] = state_ref[1]


def _hgrn_call(q, i, lff, kf, lfb, kb, s0f, s0b, tm):
    b, n, _ = q.shape
    nt = n // tm
    fwd = pl.BlockSpec((1, tm, BRANCH_W), lambda bi, j: (bi, j, 0))
    bwd = pl.BlockSpec((1, tm, BRANCH_W), lambda bi, j: (bi, nt - 1 - j, 0))
    st = pl.BlockSpec((1, N_PAIRS, LANES, LANES), lambda bi, j: (bi, 0, 0, 0))
    o_shape = jax.ShapeDtypeStruct((b, n, BRANCH_W), BF16)
    s_shape = jax.ShapeDtypeStruct((b, N_PAIRS, LANES, LANES), F32)
    return pl.pallas_call(
        _hgrn_kernel,
        out_shape=(o_shape, o_shape, s_shape, s_shape),
        grid=(b, nt),
        in_specs=[fwd, fwd, fwd, fwd, bwd, bwd, bwd, bwd, st, st],
        out_specs=(fwd, bwd, st, st),
        scratch_shapes=[pltpu.VMEM((2, N_PAIRS, LANES, LANES), F32)],
        compiler_params=_cparams(("arbitrary", "arbitrary")),
        name="hgrn_scan",
    )(q, i, lff, kf, q, i, lfb, kb, s0f, s0b)


def _merge_kernel(x_ref, sh_ref, sc_ref, gt_ref, gpre_ref, gpost_ref, w_ref, oa_ref, ob_ref, mx_ref,
                  of_ref, obk_ref, hn_ref, hm_ref, wb_ref, wo_ref, o_ref):
    x = x_ref[0]
    hb = _normed_input(x, gpre_ref[...], sc_ref[0], sh_ref[0]).astype(BF16)

    def proj(name):
        j = _MERGE_NAMES.index(name)
        return _dot(hb, w_ref[:, j * BRANCH_W:(j + 1) * BRANCH_W])

    ya = oa_ref[0].astype(F32) * _silu(proj('a_g'))
    yb = ob_ref[0].astype(F32) * _silu(proj('b_g'))
    yc = proj('c_u') * mx_ref[0].astype(F32) * _silu(proj('c_g'))
    o = of_ref[0].astype(F32) + obk_ref[0].astype(F32)
    sq_hi, sq_mid, _ = _split3(o * o)
    ms = _dot(sq_hi, hm_ref[...]) + _dot(sq_mid, hm_ref[...])
    yd = o * lax.rsqrt(ms + EPS) * hn_ref[...] * _silu(proj('d_g'))

    g0 = len(_MERGE_NAMES) * BRANCH_W
    merged = None
    for r, y in enumerate((ya, yb, yc, yd)):
        gate = _dot(hb, w_ref[:, g0 + r * D_MODEL:g0 + (r + 1) * D_MODEL])
        term = jax.nn.sigmoid(gate) * _dot(y.astype(BF16), wb_ref[r])
        merged = term if merged is None else merged + term
    out = _dot(merged.astype(BF16), wo_ref[...])
    post = out * lax.rsqrt(jnp.mean(out * out, axis=-1, keepdims=True) + EPS) * gpost_ref[...]
    o_ref[0] = x + gt_ref[0] * post


def _merge_call(x, sh, sc, gt, gpre, gpost, w_merge, oa, ob, mx, of, obk, hn, hmean, wb, wo, tm,
                per_batch_mod):
    b, n, _ = x.shape
    nt = n // tm
    mod_map = (lambda i, j: (i, 0, 0)) if per_batch_mod else (lambda i, j: (0, 0, 0))
    x_spec = pl.BlockSpec((1, tm, D_MODEL), lambda i, j: (i, j, 0))
    br_spec = pl.BlockSpec((1, tm, BRANCH_W), lambda i, j: (i, j, 0))
    mod_spec = pl.BlockSpec((1, 1, D_MODEL), mod_map)
    return pl.pallas_call(
        _merge_kernel,
        out_shape=jax.ShapeDtypeStruct((b, n, D_MODEL), F32),
        grid=(b, nt),
        in_specs=[x_spec, mod_spec, mod_spec, mod_spec,
                  _resident((1, D_MODEL)), _resident((1, D_MODEL)),
                  _resident(w_merge.shape),
                  br_spec, br_spec, br_spec, br_spec, br_spec,
                  _resident((1, BRANCH_W)), _resident((BRANCH_W, BRANCH_W)),
                  _resident(wb.shape), _resident(wo.shape)],
        out_specs=x_spec,
        compiler_params=_cparams(("arbitrary", "arbitrary")),
        name="branch_merge",
    )(x, sh, sc, gt, gpre, gpost, w_merge, oa, ob, mx, of, obk, hn, hmean, wb, wo)


def _rope_tables(n_tok, rotate):
    if not rotate:
        return jnp.ones((n_tok, LANES), F32), jnp.zeros((n_tok, LANES), F32)
    t = jnp.arange(n_tok, dtype=jnp.int32)
    pos = jnp.stack([t // GRID_W, t % GRID_W], axis=-1).astype(F32)
    inv = ROPE_THETA ** (-jnp.arange(ROPE_FREQS, dtype=F32) * 2.0 / (2 * ROPE_FREQS))
    ang = pos[:, :, None] * inv
    cos = jnp.repeat(jnp.cos(ang)[:, :, None, :], 2, axis=2).reshape(n_tok, HEAD_DIM)
    sin = jnp.sin(ang)
    sin_signed = jnp.stack([-sin, sin], axis=2).reshape(n_tok, HEAD_DIM)
    return jnp.tile(cos, (1, 2)), jnp.tile(sin_signed, (1, 2))


def _gather_cols(w_in_l, names):
    return jnp.concatenate([w_in_l[:, _IN_COL[nm] * BRANCH_W:(_IN_COL[nm] + 1) * BRANCH_W] for nm in names],
                           axis=1)


def _row_tile(n):
    return 512 if n % 512 == 0 else 256


def kernel(x, c, ctx, c_ctx, w_ada, b_ada, g_pre, g_post, w_in, na_rpb, fnet_w, gmlp_norm_g, gmlp_ws,
           gmlp_bs, hgrn_lb_logits, hgrn_norm_g, w_branch, w_out):
    batch, n_tok, _ = x.shape
    n_ctx = ctx.shape[1]
    depth = w_in.shape[0]

    w_in_b = w_in.astype(BF16)
    w_proj = [_gather_cols(w_in_b[l], _PROJ_NAMES) for l in range(depth)]
    w_merge = [jnp.concatenate([_gather_cols(w_in_b[l], _MERGE_NAMES), w_in_b[l][:, _GATE_COL0:]], axis=1)
               for l in range(depth)]
    w_branch_b = w_branch.astype(BF16)
    w_out_b = w_out.astype(BF16)
    fnet_b = fnet_w.astype(BF16)
    gmlp_ws_b = gmlp_ws.astype(BF16)
    bs_tab = jnp.repeat(jnp.swapaxes(gmlp_bs, 1, 2), BRANCH_W // GMLP_GROUPS, axis=2)
    head_mean = jnp.asarray(np.kron(np.eye(N_HEADS), np.ones((HEAD_DIM, HEAD_DIM)) / HEAD_DIM), BF16)
    cos_x, sin_x = _rope_tables(n_tok, True)
    cos_c, sin_c = _rope_tables(n_ctx, False)

    c_all = jnp.concatenate([c, jnp.broadcast_to(c_ctx[None, :], (8, D_MODEL))], axis=0)
    mod = _modulation(c_all, w_ada, b_ada)
    lower = _lower_bounds(hgrn_lb_logits)

    zero_state = jnp.zeros((batch, N_PAIRS, LANES, LANES), F32)
    tm_x = _row_tile(n_tok)
    tm_c = _row_tile(n_ctx)

    for l in range(depth):
        with_ctx = l < depth - 1
        mod_x = [mod[l, :batch, i * D_MODEL:(i + 1) * D_MODEL].reshape(batch, 1, D_MODEL) for i in range(3)]
        mod_c = [mod[l, batch:batch + 1, i * D_MODEL:(i + 1) * D_MODEL].reshape(1, 1, D_MODEL) for i in range(3)]
        gpre = g_pre[l].reshape(1, D_MODEL)
        gpost = g_post[l].reshape(1, D_MODEL)
        gn = gmlp_norm_g[l].reshape(1, BRANCH_W)
        hn = hgrn_norm_g[l].reshape(1, BRANCH_W)
        lb = lower[l].reshape(1, 2, BRANCH_W)
        bias = _attention_bias(na_rpb[l])

        (_, qp_c, k_c, v_c, bx_c, mx_c, dq_c, di_c, lff_c, kf_c, lfb_c, kb_c) = _proj_call(
            ctx, mod_c[0], mod_c[1], gpre, w_proj[l], cos_c, sin_c, gn, gmlp_ws_b[l], bs_tab[l], lb,
            tm_c, False)
        of_c, ob_c, st_f, st_b = _hgrn_call(dq_c, di_c, lff_c, kf_c, lfb_c, kb_c, zero_state, zero_state,
                                            HGRN_ROWS)

        (qr, qp, k, v, bx, mx, dq, di, lff, kf, lfb, kb) = _proj_call(
            x, mod_x[0], mod_x[1], gpre, w_proj[l], cos_x, sin_x, gn, gmlp_ws_b[l], bs_tab[l], lb,
            tm_x, True)
        oa = _nattn_call(qr, qp, k, v, k_c, v_c, bias)
        ob = _fourier_call(bx, fnet_b[l])
        of, obk, _, _ = _hgrn_call(dq, di, lff, kf, lfb, kb, st_f, st_b, HGRN_ROWS)
        x = _merge_call(x, mod_x[0], mod_x[1], mod_x[2], gpre, gpost, w_merge[l], oa, ob, mx, of, obk,
                        hn, head_mean, w_branch_b[l], w_out_b[l], tm_x, True)

        if with_ctx:
            oa_c = _cattn_call(qp_c, k_c, v_c)
            ob_c2 = _fourier_ctx_call(bx_c, fnet_b[l])
            ctx = _merge_call(ctx, mod_c[0], mod_c[1], mod_c[2], gpre, gpost, w_merge[l], oa_c, ob_c2, mx_c,
                              of_c, ob_c, hn, head_mean, w_branch_b[l], w_out_b[l], tm_c, False)
    return x
```

```python
import functools

import numpy as np
import jax
import jax.numpy as jnp
from jax import lax
from jax.experimental import pallas as pl
from jax.experimental.pallas import tpu as pltpu

F32 = jnp.float32
BF16 = jnp.bfloat16

D_MODEL = 1024
BRANCH_W = 512
N_BRANCH = 4
GRID_W = 64
HEAD_DIM = 64
N_HEADS = 8
LANES = 128
N_PAIRS = BRANCH_W // LANES
NA_WIN_H = 8
NA_WIN_W = 16
NA_TILE_ROWS = 4
NA_BAND_ROWS = 12
NA_SOFTMAX_ROWS = 32
LOG2E = 1.4426950408889634
ROPE_THETA = 10000.0
ROPE_FREQS = 16
FNET_GROUPS = 4
FNET_GROUP_W = 128
GMLP_CHUNK = 128
GMLP_GROUPS = 8
HGRN_CHUNK = 64
HGRN_SUB = 16
HGRN_EXP_CLAMP = 115.0
HGRN_ROWS = 256
EPS = 1e-6
F_FLOOR = 1e-30
NEG_INF = -1e30

VMEM_LIMIT = 56 * 2**20

_IN_COL = {'a_q': 0, 'a_k': 1, 'a_v': 2, 'a_g': 3, 'b_x': 4, 'b_g': 5, 'c_u': 6, 'c_v': 7, 'c_g': 8,
           'd_q': 9, 'd_f_fwd': 10, 'd_f_bwd': 11, 'd_i': 12, 'd_g': 13}
_PROJ_NAMES = ('a_q', 'a_k', 'a_v', 'b_x', 'c_v', 'd_q', 'd_f_fwd', 'd_f_bwd', 'd_i')
_MERGE_NAMES = ('c_u', 'a_g', 'b_g', 'c_g', 'd_g')
_GATE_COL0 = 14 * BRANCH_W


def _cparams(sem):
    return pltpu.CompilerParams(dimension_semantics=sem, vmem_limit_bytes=VMEM_LIMIT)


def _resident(shape):
    nd = len(shape)
    return pl.BlockSpec(shape, lambda *_: (0,) * nd, pipeline_mode=pl.Buffered(1))


def _silu(t):
    return t * jax.nn.sigmoid(t)


def _lane_iota(shape):
    return lax.broadcasted_iota(jnp.int32, shape, len(shape) - 1)


def _dot(a, b):
    return jnp.dot(a, b, preferred_element_type=F32)


def _dot_nt(a, b):
    return lax.dot_general(a, b, (((1,), (1,)), ((), ())), preferred_element_type=F32)


def _dot_tn(a, b):
    return lax.dot_general(a, b, (((0,), (0,)), ((), ())), preferred_element_type=F32)


def _split3(t):
    hi = t.astype(BF16)
    r1 = t - hi.astype(F32)
    mid = r1.astype(BF16)
    lo = (r1 - mid.astype(F32)).astype(BF16)
    return hi, mid, lo


def _normed_input(x, gpre, sc, sh):
    ms = jnp.mean(x * x, axis=-1, keepdims=True)
    h = x * lax.rsqrt(ms + EPS) * gpre
    return h * (1.0 + sc) + sh


def _mod_kernel(c_ref, w_ref, b_ref, o_ref):
    s = _silu(c_ref[...]).astype(BF16)
    o_ref[0] = _dot(s, w_ref[0].astype(BF16)) + b_ref[0]


def _modulation(c_all, w_ada, b_ada):
    depth = w_ada.shape[0]
    rows = c_all.shape[0]
    tn = 1024
    return pl.pallas_call(
        _mod_kernel,
        out_shape=jax.ShapeDtypeStruct((depth, rows, 3 * D_MODEL), F32),
        grid=(depth, 3 * D_MODEL // tn),
        in_specs=[pl.BlockSpec((rows, D_MODEL), lambda l, j: (0, 0)),
                  pl.BlockSpec((1, D_MODEL, tn), lambda l, j: (l, 0, j)),
                  pl.BlockSpec((1, 1, tn), lambda l, j: (l, 0, j))],
        out_specs=pl.BlockSpec((1, rows, tn), lambda l, j: (l, 0, j)),
        compiler_params=_cparams(("arbitrary", "arbitrary")),
        name="adaln_modulation",
    )(c_all, w_ada, b_ada.reshape(depth, 1, 3 * D_MODEL))


def _lb_kernel(lg_ref, o_ref):
    depth = lg_ref.shape[0]
    lg = [lg_ref[l] for l in range(depth)]
    m = functools.reduce(jnp.maximum, lg)
    e = [jnp.exp(t - m) for t in lg]
    tot = functools.reduce(lambda a, b: a + b, e)
    sm = [t / tot for t in e]
    run = jnp.zeros_like(sm[0])
    for l in range(depth):
        run = run + sm[l]
        o_ref[l] = jnp.maximum(run - sm[0], 0.0)


def _lower_bounds(lb_logits):
    return pl.pallas_call(
        _lb_kernel,
        out_shape=jax.ShapeDtypeStruct(lb_logits.shape, F32),
        name="hgrn_lower_bounds",
    )(lb_logits)


def _rope(t, cos, sin_signed, first_half):
    up = pltpu.roll(t, LANES - ROPE_FREQS, 1)
    down = pltpu.roll(t, ROPE_FREQS, 1)
    return t * cos + jnp.where(first_half, up, down) * sin_signed


def _proj_kernel(x_ref, sh_ref, sc_ref, gpre_ref, w_ref, cos_ref, sin_ref, gn_ref, ws_ref, bs_ref,
                 lb_ref, qr_ref, qp_ref, kr_ref, v_ref, bx_ref, mx_ref, dq_ref, di_ref,
                 lff_ref, kf_ref, lfb_ref, kb_ref):
    tm = x_ref.shape[1]
    hb = _normed_input(x_ref[0], gpre_ref[...], sc_ref[0], sh_ref[0]).astype(BF16)

    def proj(name):
        j = _PROJ_NAMES.index(name)
        return _dot(hb, w_ref[:, j * BRANCH_W:(j + 1) * BRANCH_W])

    cos = cos_ref[...]
    sin_signed = sin_ref[...]
    first_half = (_lane_iota((tm, LANES)) % (2 * ROPE_FREQS)) < ROPE_FREQS

    def rope_all(t):
        return jnp.concatenate(
            [_rope(t[:, p * LANES:(p + 1) * LANES], cos, sin_signed, first_half) for p in range(N_PAIRS)],
            axis=1)

    q = proj('a_q') * (HEAD_DIM ** -0.5 * LOG2E)
    qp_ref[0] = q.astype(BF16)
    qr_ref[0] = rope_all(q).astype(BF16)
    kr_ref[0] = rope_all(proj('a_k')).astype(BF16)
    v_ref[0] = proj('a_v').astype(BF16)
    bx_ref[0] = proj('b_x').astype(BF16)
    dq_ref[0] = proj('d_q').astype(BF16)
    di_ref[0] = proj('d_i').astype(BF16)

    cv = proj('c_v')
    vn = (cv * lax.rsqrt(jnp.mean(cv * cv, axis=-1, keepdims=True) + EPS) * gn_ref[...]).astype(BF16)
    low_group = _lane_iota((GMLP_CHUNK, LANES)) < (LANES // 2)
    for ch in range(tm // GMLP_CHUNK):
        r0 = ch * GMLP_CHUNK
        for p in range(N_PAIRS):
            slab = vn[r0:r0 + GMLP_CHUNK, p * LANES:(p + 1) * LANES]
            mixed = jnp.where(low_group, _dot(ws_ref[2 * p], slab), _dot(ws_ref[2 * p + 1], slab))
            mixed = mixed + bs_ref[:, p * LANES:(p + 1) * LANES]
            mx_ref[0, r0:r0 + GMLP_CHUNK, p * LANES:(p + 1) * LANES] = mixed.astype(BF16)

    for name, d, lf_ref, k_ref in (('d_f_fwd', 0, lff_ref, kf_ref), ('d_f_bwd', 1, lfb_ref, kb_ref)):
        z = proj(name)
        lb = lb_ref[0, d:d + 1, :]
        f = lb + (1.0 - lb) * jax.nn.sigmoid(z)
        lf_ref[0] = jnp.log(jnp.maximum(f, F_FLOOR))
        k_ref[0] = ((1.0 - lb) * jax.nn.sigmoid(-z)).astype(BF16)


def _proj_call(x, sh, sc, gpre, w_proj, cos, sin_signed, gn, ws, bs_tab, lb, tm, per_batch_mod):
    b, n, _ = x.shape
    nt = n // tm
    mod_map = (lambda i, j: (i, 0, 0)) if per_batch_mod else (lambda i, j: (0, 0, 0))
    row_spec = pl.BlockSpec((1, tm, BRANCH_W), lambda i, j: (i, j, 0))
    bf = jax.ShapeDtypeStruct((b, n, BRANCH_W), BF16)
    f32 = jax.ShapeDtypeStruct((b, n, BRANCH_W), F32)
    return pl.pallas_call(
        _proj_kernel,
        out_shape=(bf, bf, bf, bf, bf, bf, bf, bf, f32, bf, f32, bf),
        grid=(b, nt),
        in_specs=[pl.BlockSpec((1, tm, D_MODEL), lambda i, j: (i, j, 0)),
                  pl.BlockSpec((1, 1, D_MODEL), mod_map),
                  pl.BlockSpec((1, 1, D_MODEL), mod_map),
                  _resident((1, D_MODEL)),
                  _resident((D_MODEL, len(_PROJ_NAMES) * BRANCH_W)),
                  pl.BlockSpec((tm, LANES), lambda i, j: (j, 0)),
                  pl.BlockSpec((tm, LANES), lambda i, j: (j, 0)),
                  _resident((1, BRANCH_W)),
                  _resident((GMLP_GROUPS, GMLP_CHUNK, GMLP_CHUNK)),
                  _resident((GMLP_CHUNK, BRANCH_W)),
                  _resident((1, 2, BRANCH_W))],
        out_specs=(row_spec,) * 12,
        compiler_params=_cparams(("arbitrary", "arbitrary")),
        name="branch_proj",
    )(x, sh, sc, gpre, w_proj, cos, sin_signed, gn, ws, bs_tab, lb)


def _head_mask(hh):
    lane = _lane_iota((1, LANES))
    return (lane < HEAD_DIM) if hh == 0 else (lane >= HEAD_DIM)


def _softmax_pv(parts):
    mx = functools.reduce(jnp.maximum, [jnp.max(s, axis=-1, keepdims=True) for s, _ in parts])
    ps = [jnp.exp2(s - mx) for s, _ in parts]
    den = functools.reduce(lambda a, b: a + b, [jnp.sum(p, axis=-1, keepdims=True) for p in ps])
    acc = functools.reduce(lambda a, b: a + b, [_dot(p.astype(BF16), v) for p, (_, v) in zip(ps, parts)])
    return acc * (1.0 / den)


def _nattn_kernel(qr_ref, qp_ref, k_ref, v_ref, kc_ref, vc_ref, bias_ref, o_ref, s_ref, p_ref):
    rows = k_ref.shape[1] // GRID_W
    r0 = pl.program_id(1) * NA_TILE_ROWS
    kb0 = jnp.clip(r0 - NA_WIN_H // 2, 0, rows - NA_BAND_ROWS)
    start = pl.multiple_of(kb0 * GRID_W, GRID_W)
    band = NA_BAND_ROWS * GRID_W
    lc = kc_ref.shape[1]
    nq = NA_TILE_ROWS * GRID_W
    low_head = _lane_iota((nq, LANES)) < HEAD_DIM
    for p in range(N_PAIRS):
        ls = slice(p * LANES, (p + 1) * LANES)
        kb = k_ref[0, pl.ds(start, band), ls]
        vb = v_ref[0, pl.ds(start, band), ls]
        kc = kc_ref[0, :, ls]
        vc = vc_ref[0, :, ls]
        qr = qr_ref[0, :, ls]
        qp = qp_ref[0, :, ls]
        outs = []
        for hh in range(2):
            m = _head_mask(hh)
            zero = jnp.zeros_like(qr)
            s_ref[hh, :, 0:lc] = _dot_nt(jnp.where(m, qp, zero), kc)
            s_ref[hh, :, lc:] = _dot_nt(jnp.where(m, qr, zero), kb) + bias_ref[0, 2 * p + hh]
            dens = []
            for rb in range(nq // NA_SOFTMAX_ROWS):
                rsl = slice(rb * NA_SOFTMAX_ROWS, (rb + 1) * NA_SOFTMAX_ROWS)
                s = s_ref[hh, rsl, :]
                e = jnp.exp2(s - jnp.max(s, axis=-1, keepdims=True))
                dens.append(jnp.sum(e, axis=-1, keepdims=True))
                p_ref[hh, rsl, :] = e.astype(BF16)
            acc = _dot(p_ref[hh, :, 0:lc], vc) + _dot(p_ref[hh, :, lc:], vb)
            outs.append(acc * (1.0 / jnp.concatenate(dens, axis=0)))
        o_ref[0, :, ls] = jnp.where(low_head, outs[0], outs[1]).astype(BF16)


def _nattn_call(qr, qp, k, v, kc, vc, bias):
    b, n, _ = qr.shape
    rows = n // GRID_W
    nt = rows // NA_TILE_ROWS
    lc = kc.shape[1]
    nq = NA_TILE_ROWS * GRID_W

    def bias_map(i, t):
        return (jnp.where(t == 0, 0, jnp.where(t == nt - 1, 2, 1)), 0, 0, 0)

    q_spec = pl.BlockSpec((1, nq, BRANCH_W), lambda i, t: (i, t, 0))
    full = pl.BlockSpec((1, n, BRANCH_W), lambda i, t: (i, 0, 0))
    ctx = pl.BlockSpec((1, lc, BRANCH_W), lambda i, t: (i, 0, 0))
    return pl.pallas_call(
        _nattn_kernel,
        out_shape=jax.ShapeDtypeStruct((b, n, BRANCH_W), BF16),
        grid=(b, nt),
        in_specs=[q_spec, q_spec, full, full, ctx, ctx,
                  pl.BlockSpec((1, N_HEADS, nq, NA_BAND_ROWS * GRID_W), bias_map)],
        out_specs=q_spec,
        scratch_shapes=[pltpu.VMEM((2, nq, lc + NA_BAND_ROWS * GRID_W), F32),
                        pltpu.VMEM((2, nq, lc + NA_BAND_ROWS * GRID_W), BF16)],
        compiler_params=_cparams(("arbitrary", "arbitrary")),
        name="neighbourhood_attention",
    )(qr, qp, k, v, kc, vc, bias)


def _cattn_kernel(q_ref, k_ref, v_ref, o_ref):
    lc = q_ref.shape[1]
    low_head = _lane_iota((lc, LANES)) < HEAD_DIM
    for p in range(N_PAIRS):
        ls = slice(p * LANES, (p + 1) * LANES)
        q = q_ref[0, :, ls]
        k = k_ref[0, :, ls]
        v = v_ref[0, :, ls]
        outs = []
        for hh in range(2):
            s = _dot_nt(jnp.where(_head_mask(hh), q, jnp.zeros_like(q)), k)
            outs.append(_softmax_pv([(s, v)]))
        o_ref[0, :, ls] = jnp.where(low_head, outs[0], outs[1]).astype(BF16)


def _cattn_call(q, k, v):
    b, lc, _ = q.shape
    spec = pl.BlockSpec((1, lc, BRANCH_W), lambda i: (i, 0, 0))
    return pl.pallas_call(
        _cattn_kernel,
        out_shape=jax.ShapeDtypeStruct((b, lc, BRANCH_W), BF16),
        grid=(b,),
        in_specs=[spec, spec, spec],
        out_specs=spec,
        compiler_params=_cparams(("arbitrary",)),
        name="context_attention",
    )(q, k, v)


def _attention_bias(rpb, rows):
    col = jnp.arange(GRID_W)
    col_start = jnp.clip(col - NA_WIN_W // 2, 0, GRID_W - NA_WIN_W)
    valid = (col[None, :] >= col_start[:, None]) & (col[None, :] < col_start[:, None] + NA_WIN_W)
    col_idx = jnp.clip(col[None, :] - col[:, None] + NA_WIN_W - 1, 0, 2 * NA_WIN_W - 2)
    per_row = jnp.where(valid[None, None], rpb.astype(F32)[:, :, col_idx] * LOG2E, NEG_INF)
    tabs = []
    for r0 in (0, NA_TILE_ROWS, rows - NA_TILE_ROWS):
        kb0 = int(np.clip(r0 - NA_WIN_H // 2, 0, rows - NA_BAND_ROWS))
        r = r0 + np.arange(NA_TILE_ROWS)[:, None]
        kr = kb0 + np.arange(NA_BAND_ROWS)[None, :]
        rs = np.clip(r - NA_WIN_H // 2, 0, rows - NA_WIN_H)
        in_win = (kr >= rs) & (kr < rs + NA_WIN_H)
        idx = np.clip(kr - r + NA_WIN_H - 1, 0, 2 * NA_WIN_H - 2)
        t = jnp.where(in_win[None, :, :, None, None], per_row[:, idx], NEG_INF)
        tabs.append(t.transpose(0, 1, 3, 2, 4).reshape(N_HEADS, NA_TILE_ROWS * GRID_W, NA_BAND_ROWS * GRID_W))
    return jnp.stack(tabs)


_KRON = 8


@functools.lru_cache(maxsize=None)
def _fourier_consts(n):
    rows = n // GRID_W
    k1 = np.arange(rows)[:, None, None, None]
    l1 = np.arange(_KRON)[None, :, None, None]
    n1 = np.arange(rows)[None, None, :, None]
    l2 = np.arange(_KRON)[None, None, None, :]
    a_cos, a_sin = [], []
    for j in range(GRID_W // _KRON):
        ang = 2.0 * np.pi * k1 * (GRID_W * n1 + _KRON * j + l1) / n
        same = (l1 == l2)
        a_cos.append((np.cos(ang) * same).reshape(rows * _KRON, rows * _KRON))
        a_sin.append((-np.sin(ang) * same).reshape(rows * _KRON, rows * _KRON))
    k2 = np.arange(GRID_W)[:, None, None, None]
    ang = 2.0 * np.pi * k2 * np.arange(GRID_W)[None, None, None, :] / GRID_W
    same = (np.arange(_KRON)[None, :, None, None] == np.arange(_KRON)[None, None, :, None])
    b_cos = (np.cos(ang) * same).reshape(GRID_W * _KRON, _KRON * GRID_W)
    b_sin = (np.sin(ang) * same).reshape(GRID_W * _KRON, _KRON * GRID_W)
    b_re = np.concatenate([b_cos, b_sin], axis=1)
    b_im = np.concatenate([-b_sin, b_cos], axis=1)
    return (np.stack(a_cos).astype(np.float32), np.stack(a_sin).astype(np.float32),
            b_re.astype(np.float32), b_im.astype(np.float32))


@functools.lru_cache(maxsize=None)
def _channel_dft():
    c = np.arange(FNET_GROUP_W)
    ang = 2.0 * np.pi * np.outer(c, c) / FNET_GROUP_W
    return np.concatenate([np.cos(ang), np.sin(ang)], axis=0).astype(np.float32)


@functools.lru_cache(maxsize=None)
def _dense_dft(n):
    t = np.arange(n)
    ang = 2.0 * np.pi * np.outer(t, t) / n
    return np.cos(ang).astype(np.float32), (-np.sin(ang)).astype(np.float32)


def _channel_stage(xr, xi, cs_ref, wf_ref, norm):
    outs = []
    for g in range(FNET_GROUPS):
        ls = slice(g * FNET_GROUP_W, (g + 1) * FNET_GROUP_W)
        xg = jnp.concatenate([xr[:, ls], xi[:, ls]], axis=1).astype(BF16)
        spec = _dot(xg, cs_ref[...]) * norm
        outs.append(_dot(spec.astype(BF16), wf_ref[g]))
    return jnp.concatenate(outs, axis=1)


def _fourier_kernel(x_ref, ac_ref, as_ref, bre_ref, bim_ref, cs_ref, wf_ref, o_ref, s_ref, *, norm):
    rows = x_ref.shape[1]
    blk = rows * _KRON
    pair = 2 * _KRON
    for jj in range(GRID_W // pair):
        xt = x_ref[0, :, jj * pair:(jj + 1) * pair, :].astype(F32)
        re, im = [], []
        for half in range(2):
            xc = xt[:, half * _KRON:(half + 1) * _KRON, :].reshape(blk, BRANCH_W).astype(BF16)
            re.append(_dot(ac_ref[2 * jj + half], xc).reshape(rows, _KRON, BRANCH_W))
            im.append(_dot(as_ref[2 * jj + half], xc).reshape(rows, _KRON, BRANCH_W))
        s_ref[0, :, jj * pair:(jj + 1) * pair, :] = jnp.concatenate(re, axis=1).astype(BF16)
        s_ref[1, :, jj * pair:(jj + 1) * pair, :] = jnp.concatenate(im, axis=1).astype(BF16)
    sblk = _KRON * GRID_W
    for mm in range(rows // pair):
        ys = []
        for half in range(2):
            m0 = (2 * mm + half) * _KRON
            rhs = jnp.concatenate([s_ref[0, m0:m0 + _KRON].reshape(sblk, BRANCH_W),
                                   s_ref[1, m0:m0 + _KRON].reshape(sblk, BRANCH_W)], axis=0)
            xr = _dot(bre_ref[...], rhs)
            xi = _dot(bim_ref[...], rhs)
            ys.append(_channel_stage(xr, xi, cs_ref, wf_ref, norm).reshape(GRID_W, _KRON, BRANCH_W))
        o_ref[0, :, mm * pair:(mm + 1) * pair, :] = jnp.concatenate(ys, axis=1).astype(BF16)


def _fourier_call(bx, wf):
    b, n, _ = bx.shape
    rows = n // GRID_W
    a_cos, a_sin, b_re, b_im = (jnp.asarray(t, BF16) for t in _fourier_consts(n))
    cs = jnp.asarray(_channel_dft(), BF16)
    norm = float(1.0 / np.sqrt(n * FNET_GROUP_W))
    x4 = bx.reshape(b, rows, GRID_W, BRANCH_W)
    out = pl.pallas_call(
        functools.partial(_fourier_kernel, norm=norm),
        out_shape=jax.ShapeDtypeStruct((b, GRID_W, rows, BRANCH_W), BF16),
        grid=(b,),
        in_specs=[pl.BlockSpec((1, rows, GRID_W, BRANCH_W), lambda i: (i, 0, 0, 0)),
                  _resident(a_cos.shape), _resident(a_sin.shape),
                  _resident(b_re.shape), _resident(b_im.shape),
                  _resident(cs.shape), _resident(wf.shape)],
        out_specs=pl.BlockSpec((1, GRID_W, rows, BRANCH_W), lambda i: (i, 0, 0, 0)),
        scratch_shapes=[pltpu.VMEM((2, rows, GRID_W, BRANCH_W), BF16)],
        compiler_params=_cparams(("arbitrary",)),
        name="fourier_mix",
    )(x4, a_cos, a_sin, b_re, b_im, cs, wf)
    return out.reshape(b, n, BRANCH_W)


def _fourier_ctx_kernel(x_ref, c_ref, s_ref, cs_ref, wf_ref, o_ref, *, norm):
    x = x_ref[0]
    xr = _dot(c_ref[...], x)
    xi = _dot(s_ref[...], x)
    o_ref[0] = _channel_stage(xr, xi, cs_ref, wf_ref, norm).astype(BF16)


def _fourier_ctx_call(bx, wf):
    b, n, _ = bx.shape
    cn, sn = (jnp.asarray(t, BF16) for t in _dense_dft(n))
    cs = jnp.asarray(_channel_dft(), BF16)
    norm = float(1.0 / np.sqrt(n * FNET_GROUP_W))
    spec = pl.BlockSpec((1, n, BRANCH_W), lambda i: (i, 0, 0))
    return pl.pallas_call(
        functools.partial(_fourier_ctx_kernel, norm=norm),
        out_shape=jax.ShapeDtypeStruct((b, n, BRANCH_W), BF16),
        grid=(b,),
        in_specs=[spec, _resident(cn.shape), _resident(sn.shape), _resident(cs.shape), _resident(wf.shape)],
        out_specs=spec,
        compiler_params=_cparams(("arbitrary",)),
        name="fourier_mix_context",
    )(bx, cn, sn, cs, wf)


def _hgrn_chunk(q, k, i, a, state_ref, d, reverse):
    c = HGRN_CHUNK
    nsub = c // HGRN_SUB

    def level(r):
        return a[r:r + 1, :]

    zero_row = jnp.zeros((1, BRANCH_W), F32)
    if reverse:
        refs = [level((s + 1) * HGRN_SUB) if s + 1 < nsub else zero_row for s in range(nsub)]
        a_end = a[0:1, :]
    else:
        refs = [level(s * HGRN_SUB - 1) if s > 0 else zero_row for s in range(nsub)]
        a_end = a[c - 1:c, :]
    ref_rows = jnp.concatenate([jnp.broadcast_to(r, (HGRN_SUB, BRANCH_W)) for r in refs], axis=0)
    qf = q.astype(F32)
    kf = k.astype(F32)
    k_own = (kf * jnp.exp2(jnp.minimum(ref_rows - a, HGRN_EXP_CLAMP))).astype(BF16)
    zero_blk = jnp.zeros((HGRN_SUB, BRANCH_W), BF16)

    def q_variant(j):
        parts = []
        for s in range(nsub):
            rs = slice(s * HGRN_SUB, (s + 1) * HGRN_SUB)
            needed = (s <= j) if reverse else (s >= j)
            parts.append((qf[rs] * jnp.exp2(a[rs] - refs[j])).astype(BF16) if needed else zero_blk)
        return jnp.concatenate(parts, axis=0)

    q_var = [q_variant(j) for j in range(nsub)]
    q_in = q_var[nsub - 1] if reverse else q_var[0]
    k_out = (kf * jnp.exp2(a_end - a)).astype(BF16)
    decay_end = jnp.exp2(a_end)

    lane = _lane_iota((c, LANES))
    src = lane % HEAD_DIM
    src_sub = src // HGRN_SUB
    step = lax.broadcasted_iota(jnp.int32, (c, LANES), 0)
    seen = (src >= step) if reverse else (src <= step)
    low_rows = lax.broadcasted_iota(jnp.int32, (LANES, LANES), 0) < HEAD_DIM
    low_cols = _lane_iota((LANES, LANES)) < HEAD_DIM
    same_head = low_rows == low_cols
    lo = _lane_iota((c, LANES)) < HEAD_DIM

    def block_diag(t):
        z = jnp.zeros_like(t)
        return jnp.concatenate([jnp.where(lo, t, z), jnp.where(lo, z, t)], axis=0)

    outs = []
    for p in range(N_PAIRS):
        ls = slice(p * LANES, (p + 1) * LANES)
        res = _dot_nt(jnp.concatenate([qv[:, ls] for qv in q_var], axis=0), block_diag(k_own[:, ls]))
        sc = res[0:c]
        for s in range(1, nsub):
            sc = jnp.where(src_sub == s, res[s * c:(s + 1) * c], sc)
        scores = jnp.where(seen, sc, 0.0).astype(BF16)
        ip = i[:, ls]
        st = state_ref[d, p]
        o = _dot(scores, block_diag(ip)) + _dot_nt(q_in[:, ls], st.astype(BF16))
        upd = _dot_tn(ip, k_out[:, ls])
        state_ref[d, p] = decay_end[:, ls] * st + jnp.where(same_head, upd, 0.0)
        outs.append(o)
    return jnp.concatenate(outs, axis=1)


def _hgrn_kernel(qf_ref, if_ref, lff_ref, kf_ref, qb_ref, ib_ref, lfb_ref, kb_ref, s0f_ref, s0b_ref,
                 of_ref, ob_ref, sf_ref, sb_ref, state_ref):
    j = pl.program_id(1)
    nchunk = qf_ref.shape[1] // HGRN_CHUNK

    @pl.when(j == 0)
    def _():
        state_ref[0] = s0f_ref[0]
        state_ref[1] = s0b_ref[0]

    tm = qf_ref.shape[1]
    row = lax.broadcasted_iota(jnp.int32, (tm, tm), 0)
    col = lax.broadcasted_iota(jnp.int32, (tm, tm), 1)
    same_chunk = (row // HGRN_CHUNK) == (col // HGRN_CHUNK)

    def cum(lf, reverse):
        tri = (same_chunk & ((col >= row) if reverse else (col <= row))).astype(BF16)
        return functools.reduce(lambda u, w: u + w, [_dot(tri, piece) for piece in _split3(lf * LOG2E)])

    a_f = cum(lff_ref[0], False)
    a_b = cum(lfb_ref[0], True)

    for cix in range(nchunk):
        fs = slice(cix * HGRN_CHUNK, (cix + 1) * HGRN_CHUNK)
        of_ref[0, fs, :] = _hgrn_chunk(qf_ref[0, fs, :], kf_ref[0, fs, :], if_ref[0, fs, :], a_f[fs, :],
                                       state_ref, 0, False).astype(BF16)
        bix = nchunk - 1 - cix
        bs = slice(bix * HGRN_CHUNK, (bix + 1) * HGRN_CHUNK)
        ob_ref[0, bs, :] = _hgrn_chunk(qb_ref[0, bs, :], kb_ref[0, bs, :], ib_ref[0, bs, :], a_b[bs, :],
                                       state_ref, 1, True).astype(BF16)

    @pl.when(j == pl.num_programs(1) - 1)
    def _():
        sf_ref[0] = state_ref[0]
        sb_ref[0] = state_ref[1]


def _hgrn_call(q, i, lff, kf, lfb, kb, s0f, s0b, tm):
    b, n, _ = q.shape
    nt = n // tm
    fwd = pl.BlockSpec((1, tm, BRANCH_W), lambda bi, j: (bi, j, 0))
    bwd = pl.BlockSpec((1, tm, BRANCH_W), lambda bi, j: (bi, nt - 1 - j, 0))
    st = pl.BlockSpec((1, N_PAIRS, LANES, LANES), lambda bi, j: (bi, 0, 0, 0))
    o_shape = jax.ShapeDtypeStruct((b, n, BRANCH_W), BF16)
    s_shape = jax.ShapeDtypeStruct((b, N_PAIRS, LANES, LANES), F32)
    return pl.pallas_call(
        _hgrn_kernel,
        out_shape=(o_shape, o_shape, s_shape, s_shape),
        grid=(b, nt),
        in_specs=[fwd, fwd, fwd, fwd, bwd, bwd, bwd, bwd, st, st],
        out_specs=(fwd, bwd, st, st),
        scratch_shapes=[pltpu.VMEM((2, N_PAIRS, LANES, LANES), F32)],
        compiler_params=_cparams(("arbitrary", "arbitrary")),
        name="hgrn_scan",
    )(q, i, lff, kf, q, i, lfb, kb, s0f, s0b)


def _merge_kernel(x_ref, sh_ref, sc_ref, gt_ref, gpre_ref, gpost_ref, w_ref, oa_ref, ob_ref, mx_ref,
                  of_ref, obk_ref, hn_ref, hm_ref, wb_ref, wo_ref, o_ref):
    x = x_ref[0]
    hb = _normed_input(x, gpre_ref[...], sc_ref[0], sh_ref[0]).astype(BF16)

    def proj(name):
        j = _MERGE_NAMES.index(name)
        return _dot(hb, w_ref[:, j * BRANCH_W:(j + 1) * BRANCH_W])

    ya = oa_ref[0].astype(F32) * _silu(proj('a_g'))
    yb = ob_ref[0].astype(F32) * _silu(proj('b_g'))
    yc = proj('c_u') * mx_ref[0].astype(F32) * _silu(proj('c_g'))
    o = of_ref[0].astype(F32) + obk_ref[0].astype(F32)
    sq_hi, sq_mid, _ = _split3(o * o)
    ms = _dot(sq_hi, hm_ref[...]) + _dot(sq_mid, hm_ref[...])
    yd = o * lax.rsqrt(ms + EPS) * hn_ref[...] * _silu(proj('d_g'))

    g0 = len(_MERGE_NAMES) * BRANCH_W
    merged = None
    for r, y in enumerate((ya, yb, yc, yd)):
        gate = _dot(hb, w_ref[:, g0 + r * D_MODEL:g0 + (r + 1) * D_MODEL])
        term = jax.nn.sigmoid(gate) * _dot(y.astype(BF16), wb_ref[r])
        merged = term if merged is None else merged + term
    out = _dot(merged.astype(BF16), wo_ref[...])
    post = out * lax.rsqrt(jnp.mean(out * out, axis=-1, keepdims=True) + EPS) * gpost_ref[...]
    o_ref[0] = x + gt_ref[0] * post


def _merge_call(x, sh, sc, gt, gpre, gpost, w_merge, oa, ob, mx, of, obk, hn, hmean, wb, wo, tm,
                per_batch_mod):
    b, n, _ = x.shape
    nt = n // tm
    mod_map = (lambda i, j: (i, 0, 0)) if per_batch_mod else (lambda i, j: (0, 0, 0))
    x_spec = pl.BlockSpec((1, tm, D_MODEL), lambda i, j: (i, j, 0))
    br_spec = pl.BlockSpec((1, tm, BRANCH_W), lambda i, j: (i, j, 0))
    mod_spec = pl.BlockSpec((1, 1, D_MODEL), mod_map)
    return pl.pallas_call(
        _merge_kernel,
        out_shape=jax.ShapeDtypeStruct((b, n, D_MODEL), F32),
        grid=(b, nt),
        in_specs=[x_spec, mod_spec, mod_spec, mod_spec,
                  _resident((1, D_MODEL)), _resident((1, D_MODEL)),
                  _resident(w_merge.shape),
                  br_spec, br_spec, br_spec, br_spec, br_spec,
                  _resident((1, BRANCH_W)), _resident((BRANCH_W, BRANCH_W)),
                  _resident(wb.shape), _resident(wo.shape)],
        out_specs=x_spec,
        compiler_params=_cparams(("arbitrary", "arbitrary")),
        name="branch_merge",
    )(x, sh, sc, gt, gpre, gpost, w_merge, oa, ob, mx, of, obk, hn, hmean, wb, wo)


def _rope_tables(n_tok, rotate):
    if not rotate:
        return jnp.ones((n_tok, LANES), F32), jnp.zeros((n_tok, LANES), F32)
    t = jnp.arange(n_tok, dtype=jnp.int32)
    pos = jnp.stack([t // GRID_W, t % GRID_W], axis=-1).astype(F32)
    inv = ROPE_THETA ** (-jnp.arange(ROPE_FREQS, dtype=F32) * 2.0 / (2 * ROPE_FREQS))
    ang = pos[:, :, None] * inv
    cos = jnp.repeat(jnp.cos(ang)[:, :, None, :], 2, axis=2).reshape(n_tok, HEAD_DIM)
    sin = jnp.sin(ang)
    sin_signed = jnp.stack([-sin, sin], axis=2).reshape(n_tok, HEAD_DIM)
    return jnp.tile(cos, (1, 2)), jnp.tile(sin_signed, (1, 2))


def _gather_cols(w_in_l, names):
    return jnp.concatenate([w_in_l[:, _IN_COL[nm] * BRANCH_W:(_IN_COL[nm] + 1) * BRANCH_W] for nm in names],
                           axis=1)


def _row_tile(n):
    return 512 if n % 512 == 0 else 256


def kernel(x, c, ctx, c_ctx, w_ada, b_ada, g_pre, g_post, w_in, na_rpb, fnet_w, gmlp_norm_g, gmlp_ws,
           gmlp_bs, hgrn_lb_logits, hgrn_norm_g, w_branch, w_out):
    batch, n_tok, _ = x.shape
    n_ctx = ctx.shape[1]
    depth = w_in.shape[0]

    w_in_b = w_in.astype(BF16)
    w_proj = [_gather_cols(w_in_b[l], _PROJ_NAMES) for l in range(depth)]
    w_merge = [jnp.concatenate([_gather_cols(w_in_b[l], _MERGE_NAMES), w_in_b[l][:, _GATE_COL0:]], axis=1)
               for l in range(depth)]
    w_branch_b = w_branch.astype(BF16)
    w_out_b = w_out.astype(BF16)
    fnet_b = fnet_w.astype(BF16)
    gmlp_ws_b = gmlp_ws.astype(BF16)
    bs_tab = jnp.repeat(jnp.swapaxes(gmlp_bs, 1, 2), BRANCH_W // GMLP_GROUPS, axis=2)
    head_mean = jnp.asarray(np.kron(np.eye(N_HEADS), np.ones((HEAD_DIM, HEAD_DIM)) / HEAD_DIM), BF16)
    cos_x, sin_x = _rope_tables(n_tok, True)
    cos_c, sin_c = _rope_tables(n_ctx, False)

    c_all = jnp.concatenate([c, jnp.broadcast_to(c_ctx[None, :], (8, D_MODEL))], axis=0)
    mod = _modulation(c_all, w_ada, b_ada)
    lower = _lower_bounds(hgrn_lb_logits)

    zero_state = jnp.zeros((batch, N_PAIRS, LANES, LANES), F32)
    tm_x = _row_tile(n_tok)
    tm_c = _row_tile(n_ctx)

    for l in range(depth):
        with_ctx = l < depth - 1
        mod_x = [mod[l, :batch, i * D_MODEL:(i + 1) * D_MODEL].reshape(batch, 1, D_MODEL) for i in range(3)]
        mod_c = [mod[l, batch:batch + 1, i * D_MODEL:(i + 1) * D_MODEL].reshape(1, 1, D_MODEL) for i in range(3)]
        gpre = g_pre[l].reshape(1, D_MODEL)
        gpost = g_post[l].reshape(1, D_MODEL)
        gn = gmlp_norm_g[l].reshape(1, BRANCH_W)
        hn = hgrn_norm_g[l].reshape(1, BRANCH_W)
        lb = lower[l].reshape(1, 2, BRANCH_W)
        bias = _attention_bias(na_rpb[l], n_tok // GRID_W)

        (_, qp_c, k_c, v_c, bx_c, mx_c, dq_c, di_c, lff_c, kf_c, lfb_c, kb_c) = _proj_call(
            ctx, mod_c[0], mod_c[1], gpre, w_proj[l], cos_c, sin_c, gn, gmlp_ws_b[l], bs_tab[l], lb,
            tm_c, False)
        of_c, ob_c, st_f, st_b = _hgrn_call(dq_c, di_c, lff_c, kf_c, lfb_c, kb_c, zero_state, zero_state,
                                            HGRN_ROWS)

        (qr, qp, k, v, bx, mx, dq, di, lff, kf, lfb, kb) = _proj_call(
            x, mod_x[0], mod_x[1], gpre, w_proj[l], cos_x, sin_x, gn, gmlp_ws_b[l], bs_tab[l], lb,
            tm_x, True)
        oa = _nattn_call(qr, qp, k, v, k_c, v_c, bias)
        ob = _fourier_call(bx, fnet_b[l])
        of, obk, _, _ = _hgrn_call(dq, di, lff, kf, lfb, kb, st_f, st_b, HGRN_ROWS)
        x = _merge_call(x, mod_x[0], mod_x[1], mod_x[2], gpre, gpost, w_merge[l], oa, ob, mx, of, obk,
                        hn, head_mean, w_branch_b[l], w_out_b[l], tm_x, True)

        if with_ctx:
            oa_c = _cattn_call(qp_c, k_c, v_c)
            ob_c2 = _fourier_ctx_call(bx_c, fnet_b[l])
            ctx = _merge_call(ctx, mod_c[0], mod_c[1], mod_c[2], gpre, gpost, w_merge[l], oa_c, ob_c2, mx_c,
                              of_c, ob_c, hn, head_mean, w_branch_b[l], w_out_b[l], tm_c, False)
    return x
```

```python
import functools

import numpy as np
import jax
import jax.numpy as jnp
from jax import lax
from jax.experimental import pallas as pl
from jax.experimental.pallas import tpu as pltpu

F32 = jnp.float32
BF16 = jnp.bfloat16

D_MODEL = 1024
BRANCH_W = 512
N_BRANCH = 4
GRID_W = 64
HEAD_DIM = 64
N_HEADS = 8
LANES = 128
N_PAIRS = BRANCH_W // LANES
NA_WIN_H = 8
NA_WIN_W = 16
NA_TILE_ROWS = 4
NA_BAND_ROWS = 12
NA_SOFTMAX_ROWS = 32
LOG2E = 1.4426950408889634
ROPE_THETA = 10000.0
ROPE_FREQS = 16
FNET_GROUPS = 4
FNET_GROUP_W = 128
GMLP_CHUNK = 128
GMLP_GROUPS = 8
HGRN_CHUNK = 64
HGRN_SUB = 16
HGRN_EXP_CLAMP = 115.0
HGRN_ROWS = 256
EPS = 1e-6
F_FLOOR = 1e-30
NEG_INF = -1e30

VMEM_LIMIT = 56 * 2**20

_IN_COL = {'a_q': 0, 'a_k': 1, 'a_v': 2, 'a_g': 3, 'b_x': 4, 'b_g': 5, 'c_u': 6, 'c_v': 7, 'c_g': 8,
           'd_q': 9, 'd_f_fwd': 10, 'd_f_bwd': 11, 'd_i': 12, 'd_g': 13}
_PROJ_NAMES = ('a_q', 'a_k', 'a_v', 'b_x', 'c_v', 'd_q', 'd_f_fwd', 'd_f_bwd', 'd_i')
_MERGE_NAMES = ('c_u', 'a_g', 'b_g', 'c_g', 'd_g')
_GATE_COL0 = 14 * BRANCH_W


def _cparams(sem):
    return pltpu.CompilerParams(dimension_semantics=sem, vmem_limit_bytes=VMEM_LIMIT)


def _resident(shape):
    nd = len(shape)
    return pl.BlockSpec(shape, lambda *_: (0,) * nd, pipeline_mode=pl.Buffered(1))


def _silu(t):
    return t * jax.nn.sigmoid(t)


def _lane_iota(shape):
    return lax.broadcasted_iota(jnp.int32, shape, len(shape) - 1)


def _dot(a, b):
    return jnp.dot(a, b, preferred_element_type=F32)


def _dot_nt(a, b):
    return lax.dot_general(a, b, (((1,), (1,)), ((), ())), preferred_element_type=F32)


def _dot_tn(a, b):
    return lax.dot_general(a, b, (((0,), (0,)), ((), ())), preferred_element_type=F32)


def _split2(t):
    hi = t.astype(BF16)
    return hi, (t - hi.astype(F32)).astype(BF16)


def _normed_input(x, gpre, sc, sh):
    ms = jnp.mean(x * x, axis=-1, keepdims=True)
    h = x * lax.rsqrt(ms + EPS) * gpre
    return h * (1.0 + sc) + sh


def _mod_kernel(c_ref, w_ref, b_ref, o_ref):
    s = _silu(c_ref[...]).astype(BF16)
    o_ref[0] = _dot(s, w_ref[0].astype(BF16)) + b_ref[0]


def _modulation(c_all, w_ada, b_ada):
    depth = w_ada.shape[0]
    rows = c_all.shape[0]
    tn = 1024
    return pl.pallas_call(
        _mod_kernel,
        out_shape=jax.ShapeDtypeStruct((depth, rows, 3 * D_MODEL), F32),
        grid=(depth, 3 * D_MODEL // tn),
        in_specs=[pl.BlockSpec((rows, D_MODEL), lambda l, j: (0, 0)),
                  pl.BlockSpec((1, D_MODEL, tn), lambda l, j: (l, 0, j)),
                  pl.BlockSpec((1, 1, tn), lambda l, j: (l, 0, j))],
        out_specs=pl.BlockSpec((1, rows, tn), lambda l, j: (l, 0, j)),
        compiler_params=_cparams(("arbitrary", "arbitrary")),
        name="adaln_modulation",
    )(c_all, w_ada, b_ada.reshape(depth, 1, 3 * D_MODEL))


def _lb_kernel(lg_ref, o_ref):
    depth = lg_ref.shape[0]
    lg = [lg_ref[l] for l in range(depth)]
    m = functools.reduce(jnp.maximum, lg)
    e = [jnp.exp(t - m) for t in lg]
    tot = functools.reduce(lambda a, b: a + b, e)
    sm = [t / tot for t in e]
    run = jnp.zeros_like(sm[0])
    for l in range(depth):
        run = run + sm[l]
        o_ref[l] = jnp.maximum(run - sm[0], 0.0)


def _lower_bounds(lb_logits):
    return pl.pallas_call(
        _lb_kernel,
        out_shape=jax.ShapeDtypeStruct(lb_logits.shape, F32),
        name="hgrn_lower_bounds",
    )(lb_logits)


def _rope(t, cos, sin_signed, first_half):
    up = pltpu.roll(t, LANES - ROPE_FREQS, 1)
    down = pltpu.roll(t, ROPE_FREQS, 1)
    return t * cos + jnp.where(first_half, up, down) * sin_signed


def _proj_kernel(x_ref, sh_ref, sc_ref, gpre_ref, w_ref, cos_ref, sin_ref, gn_ref, ws_ref, bs_ref,
                 lb_ref, qr_ref, qp_ref, kr_ref, v_ref, bx_ref, mx_ref, dq_ref, di_ref,
                 lff_ref, kf_ref, lfb_ref, kb_ref):
    tm = x_ref.shape[1]
    hb = _normed_input(x_ref[0], gpre_ref[...], sc_ref[0], sh_ref[0]).astype(BF16)

    def proj(name):
        j = _PROJ_NAMES.index(name)
        return _dot(hb, w_ref[:, j * BRANCH_W:(j + 1) * BRANCH_W])

    cos = cos_ref[...]
    sin_signed = sin_ref[...]
    first_half = (_lane_iota((tm, LANES)) % (2 * ROPE_FREQS)) < ROPE_FREQS

    def rope_all(t):
        return jnp.concatenate(
            [_rope(t[:, p * LANES:(p + 1) * LANES], cos, sin_signed, first_half) for p in range(N_PAIRS)],
            axis=1)

    q = proj('a_q') * (HEAD_DIM ** -0.5 * LOG2E)
    qp_ref[0] = q.astype(BF16)
    qr_ref[0] = rope_all(q).astype(BF16)
    kr_ref[0] = rope_all(proj('a_k')).astype(BF16)
    v_ref[0] = proj('a_v').astype(BF16)
    bx_ref[0] = proj('b_x').astype(BF16)
    dq_ref[0] = proj('d_q').astype(BF16)
    di_ref[0] = proj('d_i').astype(BF16)

    cv = proj('c_v')
    vn = (cv * lax.rsqrt(jnp.mean(cv * cv, axis=-1, keepdims=True) + EPS) * gn_ref[...]).astype(BF16)
    low_group = _lane_iota((GMLP_CHUNK, LANES)) < (LANES // 2)
    for ch in range(tm // GMLP_CHUNK):
        r0 = ch * GMLP_CHUNK
        for p in range(N_PAIRS):
            slab = vn[r0:r0 + GMLP_CHUNK, p * LANES:(p + 1) * LANES]
            mixed = jnp.where(low_group, _dot(ws_ref[2 * p], slab), _dot(ws_ref[2 * p + 1], slab))
            mixed = mixed + bs_ref[:, p * LANES:(p + 1) * LANES]
            mx_ref[0, r0:r0 + GMLP_CHUNK, p * LANES:(p + 1) * LANES] = mixed.astype(BF16)

    for name, d, lf_ref, k_ref in (('d_f_fwd', 0, lff_ref, kf_ref), ('d_f_bwd', 1, lfb_ref, kb_ref)):
        z = proj(name)
        lb = lb_ref[0, d:d + 1, :]
        f = lb + (1.0 - lb) * jax.nn.sigmoid(z)
        lf_ref[0] = jnp.log(jnp.maximum(f, F_FLOOR))
        k_ref[0] = ((1.0 - lb) * jax.nn.sigmoid(-z)).astype(BF16)


def _proj_call(x, sh, sc, gpre, w_proj, cos, sin_signed, gn, ws, bs_tab, lb, tm, per_batch_mod):
    b, n, _ = x.shape
    nt = n // tm
    mod_map = (lambda i, j: (i, 0, 0)) if per_batch_mod else (lambda i, j: (0, 0, 0))
    row_spec = pl.BlockSpec((1, tm, BRANCH_W), lambda i, j: (i, j, 0))
    bf = jax.ShapeDtypeStruct((b, n, BRANCH_W), BF16)
    f32 = jax.ShapeDtypeStruct((b, n, BRANCH_W), F32)
    return pl.pallas_call(
        _proj_kernel,
        out_shape=(bf, bf, bf, bf, bf, bf, bf, bf, f32, bf, f32, bf),
        grid=(b, nt),
        in_specs=[pl.BlockSpec((1, tm, D_MODEL), lambda i, j: (i, j, 0)),
                  pl.BlockSpec((1, 1, D_MODEL), mod_map),
                  pl.BlockSpec((1, 1, D_MODEL), mod_map),
                  _resident((1, D_MODEL)),
                  _resident((D_MODEL, len(_PROJ_NAMES) * BRANCH_W)),
                  pl.BlockSpec((tm, LANES), lambda i, j: (j, 0)),
                  pl.BlockSpec((tm, LANES), lambda i, j: (j, 0)),
                  _resident((1, BRANCH_W)),
                  _resident((GMLP_GROUPS, GMLP_CHUNK, GMLP_CHUNK)),
                  _resident((GMLP_CHUNK, BRANCH_W)),
                  _resident((1, 2, BRANCH_W))],
        out_specs=(row_spec,) * 12,
        compiler_params=_cparams(("arbitrary", "arbitrary")),
        name="branch_proj",
    )(x, sh, sc, gpre, w_proj, cos, sin_signed, gn, ws, bs_tab, lb)


def _head_mask(hh):
    lane = _lane_iota((1, LANES))
    return (lane < HEAD_DIM) if hh == 0 else (lane >= HEAD_DIM)


def _softmax_pv(parts):
    mx = functools.reduce(jnp.maximum, [jnp.max(s, axis=-1, keepdims=True) for s, _ in parts])
    ps = [jnp.exp2(s - mx) for s, _ in parts]
    den = functools.reduce(lambda a, b: a + b, [jnp.sum(p, axis=-1, keepdims=True) for p in ps])
    acc = functools.reduce(lambda a, b: a + b, [_dot(p.astype(BF16), v) for p, (_, v) in zip(ps, parts)])
    return acc * (1.0 / den)


def _nattn_kernel(qr_ref, qp_ref, k_ref, v_ref, kc_ref, vc_ref, bias_ref, o_ref):
    rows = k_ref.shape[1] // GRID_W
    r0 = pl.program_id(1) * NA_TILE_ROWS
    kb0 = jnp.clip(r0 - NA_WIN_H // 2, 0, rows - NA_BAND_ROWS)
    start = pl.multiple_of(kb0 * GRID_W, GRID_W)
    band = NA_BAND_ROWS * GRID_W
    nq = NA_TILE_ROWS * GRID_W
    low_head = _lane_iota((nq, LANES)) < HEAD_DIM

    def lanes(p):
        return slice(p * LANES, (p + 1) * LANES)

    def stack_heads(q):
        zero = jnp.zeros_like(q)
        return jnp.concatenate([jnp.where(_head_mask(0), q, zero), jnp.where(_head_mask(1), q, zero)], axis=0)

    def scores(p):
        s_ctx = _dot_nt(stack_heads(qp_ref[0, :, lanes(p)]), kc_ref[0, :, lanes(p)])
        s_band = _dot_nt(stack_heads(qr_ref[0, :, lanes(p)]), k_ref[0, pl.ds(start, band), lanes(p)])
        return s_ctx, s_band + bias_ref[0, 2 * p:2 * p + 2].reshape(2 * nq, band)

    def softmax(s_ctx, s_band):
        p_ctx, p_band, dens = [], [], []
        for rb in range(2 * nq // NA_SOFTMAX_ROWS):
            rsl = slice(rb * NA_SOFTMAX_ROWS, (rb + 1) * NA_SOFTMAX_ROWS)
            sc, sb = s_ctx[rsl], s_band[rsl]
            mx = jnp.maximum(jnp.max(sc, axis=-1, keepdims=True), jnp.max(sb, axis=-1, keepdims=True))
            ec, eb = jnp.exp2(sc - mx), jnp.exp2(sb - mx)
            dens.append(jnp.sum(ec, axis=-1, keepdims=True) + jnp.sum(eb, axis=-1, keepdims=True))
            p_ctx.append(ec.astype(BF16))
            p_band.append(eb.astype(BF16))
        return jnp.concatenate(p_ctx, axis=0), jnp.concatenate(p_band, axis=0), jnp.concatenate(dens, axis=0)

    def values(p, p_ctx, p_band, den):
        acc = _dot(p_ctx, vc_ref[0, :, lanes(p)]) + _dot(p_band, v_ref[0, pl.ds(start, band), lanes(p)])
        acc = acc * (1.0 / den)
        o_ref[0, :, lanes(p)] = jnp.where(low_head, acc[0:nq], acc[nq:2 * nq]).astype(BF16)

    s_val, p_val = {}, {}
    for t in range(N_PAIRS + 2):
        if t < N_PAIRS:
            s_val[t] = scores(t)
        if 0 <= t - 1 < N_PAIRS:
            p_val[t - 1] = softmax(*s_val.pop(t - 1))
        if 0 <= t - 2 < N_PAIRS:
            values(t - 2, *p_val.pop(t - 2))


def _nattn_call(qr, qp, k, v, kc, vc, bias):
    b, n, _ = qr.shape
    rows = n // GRID_W
    nt = rows // NA_TILE_ROWS
    lc = kc.shape[1]
    nq = NA_TILE_ROWS * GRID_W

    def bias_map(i, t):
        return (jnp.where(t == 0, 0, jnp.where(t == nt - 1, 2, 1)), 0, 0, 0)

    q_spec = pl.BlockSpec((1, nq, BRANCH_W), lambda i, t: (i, t, 0))
    full = pl.BlockSpec((1, n, BRANCH_W), lambda i, t: (i, 0, 0))
    ctx = pl.BlockSpec((1, lc, BRANCH_W), lambda i, t: (i, 0, 0))
    return pl.pallas_call(
        _nattn_kernel,
        out_shape=jax.ShapeDtypeStruct((b, n, BRANCH_W), BF16),
        grid=(b, nt),
        in_specs=[q_spec, q_spec, full, full, ctx, ctx,
                  pl.BlockSpec((1, N_HEADS, nq, NA_BAND_ROWS * GRID_W), bias_map)],
        out_specs=q_spec,
        compiler_params=_cparams(("arbitrary", "arbitrary")),
        name="neighbourhood_attention",
    )(qr, qp, k, v, kc, vc, bias)


def _cattn_kernel(q_ref, k_ref, v_ref, o_ref):
    lc = q_ref.shape[1]
    low_head = _lane_iota((lc, LANES)) < HEAD_DIM
    for p in range(N_PAIRS):
        ls = slice(p * LANES, (p + 1) * LANES)
        q = q_ref[0, :, ls]
        k = k_ref[0, :, ls]
        v = v_ref[0, :, ls]
        outs = []
        for hh in range(2):
            s = _dot_nt(jnp.where(_head_mask(hh), q, jnp.zeros_like(q)), k)
            outs.append(_softmax_pv([(s, v)]))
        o_ref[0, :, ls] = jnp.where(low_head, outs[0], outs[1]).astype(BF16)


def _cattn_call(q, k, v):
    b, lc, _ = q.shape
    spec = pl.BlockSpec((1, lc, BRANCH_W), lambda i: (i, 0, 0))
    return pl.pallas_call(
        _cattn_kernel,
        out_shape=jax.ShapeDtypeStruct((b, lc, BRANCH_W), BF16),
        grid=(b,),
        in_specs=[spec, spec, spec],
        out_specs=spec,
        compiler_params=_cparams(("arbitrary",)),
        name="context_attention",
    )(q, k, v)


def _attention_bias(rpb, rows):
    col = jnp.arange(GRID_W)
    col_start = jnp.clip(col - NA_WIN_W // 2, 0, GRID_W - NA_WIN_W)
    valid = (col[None, :] >= col_start[:, None]) & (col[None, :] < col_start[:, None] + NA_WIN_W)
    col_idx = jnp.clip(col[None, :] - col[:, None] + NA_WIN_W - 1, 0, 2 * NA_WIN_W - 2)
    per_row = jnp.where(valid[None, None], rpb.astype(F32)[:, :, col_idx] * LOG2E, NEG_INF)
    tabs = []
    for r0 in (0, NA_TILE_ROWS, rows - NA_TILE_ROWS):
        kb0 = int(np.clip(r0 - NA_WIN_H // 2, 0, rows - NA_BAND_ROWS))
        r = r0 + np.arange(NA_TILE_ROWS)[:, None]
        kr = kb0 + np.arange(NA_BAND_ROWS)[None, :]
        rs = np.clip(r - NA_WIN_H // 2, 0, rows - NA_WIN_H)
        in_win = (kr >= rs) & (kr < rs + NA_WIN_H)
        idx = np.clip(kr - r + NA_WIN_H - 1, 0, 2 * NA_WIN_H - 2)
        t = jnp.where(in_win[None, :, :, None, None], per_row[:, idx], NEG_INF)
        tabs.append(t.transpose(0, 1, 3, 2, 4).reshape(N_HEADS, NA_TILE_ROWS * GRID_W, NA_BAND_ROWS * GRID_W))
    return jnp.stack(tabs)


_KRON = 8


@functools.lru_cache(maxsize=None)
def _fourier_consts(n):
    rows = n // GRID_W
    k1 = np.arange(rows)[:, None, None, None]
    l1 = np.arange(_KRON)[None, :, None, None]
    n1 = np.arange(rows)[None, None, :, None]
    l2 = np.arange(_KRON)[None, None, None, :]
    a_cos, a_sin = [], []
    for j in range(GRID_W // _KRON):
        ang = 2.0 * np.pi * k1 * (GRID_W * n1 + _KRON * j + l1) / n
        same = (l1 == l2)
        a_cos.append((np.cos(ang) * same).reshape(rows * _KRON, rows * _KRON))
        a_sin.append((-np.sin(ang) * same).reshape(rows * _KRON, rows * _KRON))
    k2 = np.arange(GRID_W)[:, None, None, None]
    ang = 2.0 * np.pi * k2 * np.arange(GRID_W)[None, None, None, :] / GRID_W
    same = (np.arange(_KRON)[None, :, None, None] == np.arange(_KRON)[None, None, :, None])
    b_cos = (np.cos(ang) * same).reshape(GRID_W * _KRON, _KRON * GRID_W)
    b_sin = (np.sin(ang) * same).reshape(GRID_W * _KRON, _KRON * GRID_W)
    b_re = np.concatenate([b_cos, b_sin], axis=1)
    b_im = np.concatenate([-b_sin, b_cos], axis=1)
    return (np.stack(a_cos).astype(np.float32), np.stack(a_sin).astype(np.float32),
            b_re.astype(np.float32), b_im.astype(np.float32))


@functools.lru_cache(maxsize=None)
def _channel_dft():
    c = np.arange(FNET_GROUP_W)
    ang = 2.0 * np.pi * np.outer(c, c) / FNET_GROUP_W
    return np.concatenate([np.cos(ang), np.sin(ang)], axis=0).astype(np.float32)


@functools.lru_cache(maxsize=None)
def _dense_dft(n):
    t = np.arange(n)
    ang = 2.0 * np.pi * np.outer(t, t) / n
    return np.cos(ang).astype(np.float32), (-np.sin(ang)).astype(np.float32)


def _channel_stage(xr, xi, cs_ref, wf_ref, norm):
    outs = []
    for g in range(FNET_GROUPS):
        ls = slice(g * FNET_GROUP_W, (g + 1) * FNET_GROUP_W)
        xg = jnp.concatenate([xr[:, ls], xi[:, ls]], axis=1).astype(BF16)
        spec = _dot(xg, cs_ref[...]) * norm
        outs.append(_dot(spec.astype(BF16), wf_ref[g]))
    return jnp.concatenate(outs, axis=1)


def _fourier_kernel(x_ref, ac_ref, as_ref, bre_ref, bim_ref, cs_ref, wf_ref, o_ref, s_ref, *, norm):
    rows = x_ref.shape[1]
    blk = rows * _KRON
    pair = 2 * _KRON
    for jj in range(GRID_W // pair):
        xt = x_ref[0, :, jj * pair:(jj + 1) * pair, :].astype(F32)
        re, im = [], []
        for half in range(2):
            xc = xt[:, half * _KRON:(half + 1) * _KRON, :].reshape(blk, BRANCH_W).astype(BF16)
            re.append(_dot(ac_ref[2 * jj + half], xc).reshape(rows, _KRON, BRANCH_W))
            im.append(_dot(as_ref[2 * jj + half], xc).reshape(rows, _KRON, BRANCH_W))
        s_ref[0, :, jj * pair:(jj + 1) * pair, :] = jnp.concatenate(re, axis=1).astype(BF16)
        s_ref[1, :, jj * pair:(jj + 1) * pair, :] = jnp.concatenate(im, axis=1).astype(BF16)
    sblk = _KRON * GRID_W
    for mm in range(rows // pair):
        ys = []
        for half in range(2):
            m0 = (2 * mm + half) * _KRON
            rhs = jnp.concatenate([s_ref[0, m0:m0 + _KRON].reshape(sblk, BRANCH_W),
                                   s_ref[1, m0:m0 + _KRON].reshape(sblk, BRANCH_W)], axis=0)
            xr = _dot(bre_ref[...], rhs)
            xi = _dot(bim_ref[...], rhs)
            ys.append(_channel_stage(xr, xi, cs_ref, wf_ref, norm).reshape(GRID_W, _KRON, BRANCH_W))
        o_ref[0, :, mm * pair:(mm + 1) * pair, :] = jnp.concatenate(ys, axis=1).astype(BF16)


def _fourier_call(bx, wf):
    b, n, _ = bx.shape
    rows = n // GRID_W
    a_cos, a_sin, b_re, b_im = (jnp.asarray(t, BF16) for t in _fourier_consts(n))
    cs = jnp.asarray(_channel_dft(), BF16)
    norm = float(1.0 / np.sqrt(n * FNET_GROUP_W))
    x4 = bx.reshape(b, rows, GRID_W, BRANCH_W)
    out = pl.pallas_call(
        functools.partial(_fourier_kernel, norm=norm),
        out_shape=jax.ShapeDtypeStruct((b, GRID_W, rows, BRANCH_W), BF16),
        grid=(b,),
        in_specs=[pl.BlockSpec((1, rows, GRID_W, BRANCH_W), lambda i: (i, 0, 0, 0)),
                  _resident(a_cos.shape), _resident(a_sin.shape),
                  _resident(b_re.shape), _resident(b_im.shape),
                  _resident(cs.shape), _resident(wf.shape)],
        out_specs=pl.BlockSpec((1, GRID_W, rows, BRANCH_W), lambda i: (i, 0, 0, 0)),
        scratch_shapes=[pltpu.VMEM((2, rows, GRID_W, BRANCH_W), BF16)],
        compiler_params=_cparams(("arbitrary",)),
        name="fourier_mix",
    )(x4, a_cos, a_sin, b_re, b_im, cs, wf)
    return out.reshape(b, n, BRANCH_W)


def _fourier_ctx_kernel(x_ref, c_ref, s_ref, cs_ref, wf_ref, o_ref, *, norm):
    x = x_ref[0]
    xr = _dot(c_ref[...], x)
    xi = _dot(s_ref[...], x)
    o_ref[0] = _channel_stage(xr, xi, cs_ref, wf_ref, norm).astype(BF16)


def _fourier_ctx_call(bx, wf):
    b, n, _ = bx.shape
    cn, sn = (jnp.asarray(t, BF16) for t in _dense_dft(n))
    cs = jnp.asarray(_channel_dft(), BF16)
    norm = float(1.0 / np.sqrt(n * FNET_GROUP_W))
    spec = pl.BlockSpec((1, n, BRANCH_W), lambda i: (i, 0, 0))
    return pl.pallas_call(
        functools.partial(_fourier_ctx_kernel, norm=norm),
        out_shape=jax.ShapeDtypeStruct((b, n, BRANCH_W), BF16),
        grid=(b,),
        in_specs=[spec, _resident(cn.shape), _resident(sn.shape), _resident(cs.shape), _resident(wf.shape)],
        out_specs=spec,
        compiler_params=_cparams(("arbitrary",)),
        name="fourier_mix_context",
    )(bx, cn, sn, cs, wf)


def _block_diag(t):
    lo = _lane_iota(t.shape) < HEAD_DIM
    z = jnp.zeros_like(t)
    return jnp.concatenate([jnp.where(lo, t, z), jnp.where(lo, z, t)], axis=0)


def _hgrn_needed(j, reverse):
    nsub = HGRN_CHUNK // HGRN_SUB
    return list(range(0, j + 1)) if reverse else list(range(j, nsub))


def _hgrn_prepare(q, k, i, a, reverse):
    c = HGRN_CHUNK
    nsub = c // HGRN_SUB

    def level(r):
        return a[r:r + 1, :]

    zero_row = jnp.zeros((1, BRANCH_W), F32)
    if reverse:
        refs = [level((s + 1) * HGRN_SUB) if s + 1 < nsub else zero_row for s in range(nsub)]
        a_end = a[0:1, :]
    else:
        refs = [level(s * HGRN_SUB - 1) if s > 0 else zero_row for s in range(nsub)]
        a_end = a[c - 1:c, :]
    ref_rows = jnp.concatenate([jnp.broadcast_to(r, (HGRN_SUB, BRANCH_W)) for r in refs], axis=0)
    qf = q.astype(F32)
    kf = k.astype(F32)
    k_own = (kf * jnp.exp2(jnp.minimum(ref_rows - a, HGRN_EXP_CLAMP))).astype(BF16)

    def q_variant(j):
        parts = []
        for s in _hgrn_needed(j, reverse):
            rs = slice(s * HGRN_SUB, (s + 1) * HGRN_SUB)
            parts.append((qf[rs] * jnp.exp2(a[rs] - refs[j])).astype(BF16))
        return jnp.concatenate(parts, axis=0)

    q_var = [q_variant(j) for j in range(nsub)]
    return dict(
        q_stack=jnp.concatenate(q_var, axis=0),
        q_in=q_var[nsub - 1] if reverse else q_var[0],
        k_own=k_own,
        k_out=(kf * jnp.exp2(a_end - a)).astype(BF16),
        decay_end=jnp.exp2(a_end),
        i=i, reverse=reverse)


def _hgrn_scores(ops):
    c = HGRN_CHUNK
    nsub = c // HGRN_SUB
    reverse = ops['reverse']
    src = _lane_iota((c, LANES)) % HEAD_DIM
    step = lax.broadcasted_iota(jnp.int32, (c, LANES), 0)
    seen = (src >= step) if reverse else (src <= step)
    src_sub = (_lane_iota((HGRN_SUB, LANES)) % HEAD_DIM) // HGRN_SUB
    where_blk, off = {}, 0
    for j in range(nsub):
        for s in _hgrn_needed(j, reverse):
            where_blk[(j, s)] = off
            off += HGRN_SUB
    out = []
    for p in range(N_PAIRS):
        ls = slice(p * LANES, (p + 1) * LANES)
        res = _dot_nt(ops['q_stack'][:, ls], _block_diag(ops['k_own'][:, ls]))
        rows = []
        for s in range(nsub):
            blk = None
            for j in range(nsub):
                if (j, s) in where_blk:
                    piece = res[where_blk[(j, s)]:where_blk[(j, s)] + HGRN_SUB]
                    blk = piece if blk is None else jnp.where(src_sub == j, piece, blk)
            rows.append(blk)
        out.append(jnp.where(seen, jnp.concatenate(rows, axis=0), 0.0).astype(BF16))
    return out


def _hgrn_local(ops, scores):
    low_rows = lax.broadcasted_iota(jnp.int32, (LANES, LANES), 0) < HEAD_DIM
    same_head = low_rows == (_lane_iota((LANES, LANES)) < HEAD_DIM)
    o_intra, upd = [], []
    for p in range(N_PAIRS):
        ls = slice(p * LANES, (p + 1) * LANES)
        ip = ops['i'][:, ls]
        o_intra.append(_dot(scores[p], _block_diag(ip)))
        upd.append(jnp.where(same_head, _dot_tn(ip, ops['k_out'][:, ls]), 0.0))
    return o_intra, upd


def _hgrn_carry(ops, o_intra, upd, state_ref, d):
    outs = []
    for p in range(N_PAIRS):
        ls = slice(p * LANES, (p + 1) * LANES)
        st = state_ref[d, p]
        outs.append(o_intra[p] + _dot_nt(ops['q_in'][:, ls], st.astype(BF16)))
        state_ref[d, p] = ops['decay_end'][:, ls] * st + upd[p]
    return jnp.concatenate(outs, axis=1)


def _hgrn_kernel(qf_ref, if_ref, lff_ref, kf_ref, qb_ref, ib_ref, lfb_ref, kb_ref, s0f_ref, s0b_ref,
                 of_ref, ob_ref, sf_ref, sb_ref, state_ref):
    j = pl.program_id(1)
    nchunk = qf_ref.shape[1] // HGRN_CHUNK

    @pl.when(j == 0)
    def _():
        state_ref[0] = s0f_ref[0]
        state_ref[1] = s0b_ref[0]

    tm = qf_ref.shape[1]
    row = lax.broadcasted_iota(jnp.int32, (tm, tm), 0)
    col = lax.broadcasted_iota(jnp.int32, (tm, tm), 1)
    same_chunk = (row // HGRN_CHUNK) == (col // HGRN_CHUNK)

    def cum(lf, reverse):
        tri = (same_chunk & ((col >= row) if reverse else (col <= row))).astype(BF16)
        hi, lo = _split2(lf * LOG2E)
        return _dot(tri, hi) + _dot(tri, lo)

    a_f = cum(lff_ref[0], False)
    a_b = cum(lfb_ref[0], True)

    work = []
    for cix in range(nchunk):
        fs = slice(cix * HGRN_CHUNK, (cix + 1) * HGRN_CHUNK)
        work.append((0, of_ref, fs, _hgrn_prepare(qf_ref[0, fs, :], kf_ref[0, fs, :], if_ref[0, fs, :],
                                                  a_f[fs, :], False)))
        bix = nchunk - 1 - cix
        bs = slice(bix * HGRN_CHUNK, (bix + 1) * HGRN_CHUNK)
        work.append((1, ob_ref, bs, _hgrn_prepare(qb_ref[0, bs, :], kb_ref[0, bs, :], ib_ref[0, bs, :],
                                                  a_b[bs, :], True)))
    scores = [_hgrn_scores(ops) for _, _, _, ops in work]
    local = [_hgrn_local(ops, sc) for (_, _, _, ops), sc in zip(work, scores)]
    for (d, o_ref, rs, ops), (o_intra, upd) in zip(work, local):
        o_ref[0, rs, :] = _hgrn_carry(ops, o_intra, upd, state_ref, d).astype(BF16)

    @pl.when(j == pl.num_programs(1) - 1)
    def _():
        sf_ref[0] = state_ref[0]
        sb_ref[0] = state_ref[1]


def _hgrn_call(q, i, lff, kf, lfb, kb, s0f, s0b, tm):
    b, n, _ = q.shape
    nt = n // tm
    fwd = pl.BlockSpec((1, tm, BRANCH_W), lambda bi, j: (bi, j, 0))
    bwd = pl.BlockSpec((1, tm, BRANCH_W), lambda bi, j: (bi, nt - 1 - j, 0))
    st = pl.BlockSpec((1, N_PAIRS, LANES, LANES), lambda bi, j: (bi, 0, 0, 0))
    o_shape = jax.ShapeDtypeStruct((b, n, BRANCH_W), BF16)
    s_shape = jax.ShapeDtypeStruct((b, N_PAIRS, LANES, LANES), F32)
    return pl.pallas_call(
        _hgrn_kernel,
        out_shape=(o_shape, o_shape, s_shape, s_shape),
        grid=(b, nt),
        in_specs=[fwd, fwd, fwd, fwd, bwd, bwd, bwd, bwd, st, st],
        out_specs=(fwd, bwd, st, st),
        scratch_shapes=[pltpu.VMEM((2, N_PAIRS, LANES, LANES), F32)],
        compiler_params=_cparams(("arbitrary", "arbitrary")),
        name="hgrn_scan",
    )(q, i, lff, kf, q, i, lfb, kb, s0f, s0b)


def _merge_kernel(x_ref, sh_ref, sc_ref, gt_ref, gpre_ref, gpost_ref, w_ref, oa_ref, ob_ref, mx_ref,
                  of_ref, obk_ref, hn_ref, hm_ref, wb_ref, wo_ref, o_ref):
    x = x_ref[0]
    hb = _normed_input(x, gpre_ref[...], sc_ref[0], sh_ref[0]).astype(BF16)

    def proj(name):
        j = _MERGE_NAMES.index(name)
        return _dot(hb, w_ref[:, j * BRANCH_W:(j + 1) * BRANCH_W])

    ya = oa_ref[0].astype(F32) * _silu(proj('a_g'))
    yb = ob_ref[0].astype(F32) * _silu(proj('b_g'))
    yc = proj('c_u') * mx_ref[0].astype(F32) * _silu(proj('c_g'))
    o = of_ref[0].astype(F32) + obk_ref[0].astype(F32)
    ms = _dot((o * o).astype(BF16), hm_ref[...])
    yd = o * lax.rsqrt(ms + EPS) * hn_ref[...] * _silu(proj('d_g'))

    g0 = len(_MERGE_NAMES) * BRANCH_W
    merged = None
    for r, y in enumerate((ya, yb, yc, yd)):
        gate = _dot(hb, w_ref[:, g0 + r * D_MODEL:g0 + (r + 1) * D_MODEL])
        term = jax.nn.sigmoid(gate) * _dot(y.astype(BF16), wb_ref[r])
        merged = term if merged is None else merged + term
    out = _dot(merged.astype(BF16), wo_ref[...])
    post = out * lax.rsqrt(jnp.mean(out * out, axis=-1, keepdims=True) + EPS) * gpost_ref[...]
    o_ref[0] = x + gt_ref[0] * post


def _merge_call(x, sh, sc, gt, gpre, gpost, w_merge, oa, ob, mx, of, obk, hn, hmean, wb, wo, tm,
                per_batch_mod):
    b, n, _ = x.shape
    nt = n // tm
    mod_map = (lambda i, j: (i, 0, 0)) if per_batch_mod else (lambda i, j: (0, 0, 0))
    x_spec = pl.BlockSpec((1, tm, D_MODEL), lambda i, j: (i, j, 0))
    br_spec = pl.BlockSpec((1, tm, BRANCH_W), lambda i, j: (i, j, 0))
    mod_spec = pl.BlockSpec((1, 1, D_MODEL), mod_map)
    return pl.pallas_call(
        _merge_kernel,
        out_shape=jax.ShapeDtypeStruct((b, n, D_MODEL), F32),
        grid=(b, nt),
        in_specs=[x_spec, mod_spec, mod_spec, mod_spec,
                  _resident((1, D_MODEL)), _resident((1, D_MODEL)),
                  _resident(w_merge.shape),
                  br_spec, br_spec, br_spec, br_spec, br_spec,
                  _resident((1, BRANCH_W)), _resident((BRANCH_W, BRANCH_W)),
                  _resident(wb.shape), _resident(wo.shape)],
        out_specs=x_spec,
        compiler_params=_cparams(("arbitrary", "arbitrary")),
        name="branch_merge",
    )(x, sh, sc, gt, gpre, gpost, w_merge, oa, ob, mx, of, obk, hn, hmean, wb, wo)


def _rope_tables(n_tok, rotate):
    if not rotate:
        return jnp.ones((n_tok, LANES), F32), jnp.zeros((n_tok, LANES), F32)
    t = jnp.arange(n_tok, dtype=jnp.int32)
    pos = jnp.stack([t // GRID_W, t % GRID_W], axis=-1).astype(F32)
    inv = ROPE_THETA ** (-jnp.arange(ROPE_FREQS, dtype=F32) * 2.0 / (2 * ROPE_FREQS))
    ang = pos[:, :, None] * inv
    cos = jnp.repeat(jnp.cos(ang)[:, :, None, :], 2, axis=2).reshape(n_tok, HEAD_DIM)
    sin = jnp.sin(ang)
    sin_signed = jnp.stack([-sin, sin], axis=2).reshape(n_tok, HEAD_DIM)
    return jnp.tile(cos, (1, 2)), jnp.tile(sin_signed, (1, 2))


def _gather_cols(w_in_l, names):
    return jnp.concatenate([w_in_l[:, _IN_COL[nm] * BRANCH_W:(_IN_COL[nm] + 1) * BRANCH_W] for nm in names],
                           axis=1)


def _row_tile(n):
    return 512 if n % 512 == 0 else 256


def kernel(x, c, ctx, c_ctx, w_ada, b_ada, g_pre, g_post, w_in, na_rpb, fnet_w, gmlp_norm_g, gmlp_ws,
           gmlp_bs, hgrn_lb_logits, hgrn_norm_g, w_branch, w_out):
    batch, n_tok, _ = x.shape
    n_ctx = ctx.shape[1]
    depth = w_in.shape[0]

    w_in_b = w_in.astype(BF16)
    w_proj = [_gather_cols(w_in_b[l], _PROJ_NAMES) for l in range(depth)]
    w_merge = [jnp.concatenate([_gather_cols(w_in_b[l], _MERGE_NAMES), w_in_b[l][:, _GATE_COL0:]], axis=1)
               for l in range(depth)]
    w_branch_b = w_branch.astype(BF16)
    w_out_b = w_out.astype(BF16)
    fnet_b = fnet_w.astype(BF16)
    gmlp_ws_b = gmlp_ws.astype(BF16)
    bs_tab = jnp.repeat(jnp.swapaxes(gmlp_bs, 1, 2), BRANCH_W // GMLP_GROUPS, axis=2)
    head_mean = jnp.asarray(np.kron(np.eye(N_HEADS), np.ones((HEAD_DIM, HEAD_DIM)) / HEAD_DIM), BF16)
    cos_x, sin_x = _rope_tables(n_tok, True)
    cos_c, sin_c = _rope_tables(n_ctx, False)

    c_all = jnp.concatenate([c, jnp.broadcast_to(c_ctx[None, :], (8, D_MODEL))], axis=0)
    mod = _modulation(c_all, w_ada, b_ada)
    lower = _lower_bounds(hgrn_lb_logits)

    zero_state = jnp.zeros((batch, N_PAIRS, LANES, LANES), F32)
    tm_x = _row_tile(n_tok)
    tm_c = _row_tile(n_ctx)

    for l in range(depth):
        with_ctx = l < depth - 1
        mod_x = [mod[l, :batch, i * D_MODEL:(i + 1) * D_MODEL].reshape(batch, 1, D_MODEL) for i in range(3)]
        mod_c = [mod[l, batch:batch + 1, i * D_MODEL:(i + 1) * D_MODEL].reshape(1, 1, D_MODEL) for i in range(3)]
        gpre = g_pre[l].reshape(1, D_MODEL)
        gpost = g_post[l].reshape(1, D_MODEL)
        gn = gmlp_norm_g[l].reshape(1, BRANCH_W)
        hn = hgrn_norm_g[l].reshape(1, BRANCH_W)
        lb = lower[l].reshape(1, 2, BRANCH_W)
        bias = _attention_bias(na_rpb[l], n_tok // GRID_W)

        (_, qp_c, k_c, v_c, bx_c, mx_c, dq_c, di_c, lff_c, kf_c, lfb_c, kb_c) = _proj_call(
            ctx, mod_c[0], mod_c[1], gpre, w_proj[l], cos_c, sin_c, gn, gmlp_ws_b[l], bs_tab[l], lb,
            tm_c, False)
        of_c, ob_c, st_f, st_b = _hgrn_call(dq_c, di_c, lff_c, kf_c, lfb_c, kb_c, zero_state, zero_state,
                                            HGRN_ROWS)

        (qr, qp, k, v, bx, mx, dq, di, lff, kf, lfb, kb) = _proj_call(
            x, mod_x[0], mod_x[1], gpre, w_proj[l], cos_x, sin_x, gn, gmlp_ws_b[l], bs_tab[l], lb,
            tm_x, True)
        oa = _nattn_call(qr, qp, k, v, k_c, v_c, bias)
        ob = _fourier_call(bx, fnet_b[l])
        of, obk, _, _ = _hgrn_call(dq, di, lff, kf, lfb, kb, st_f, st_b, HGRN_ROWS)
        x = _merge_call(x, mod_x[0], mod_x[1], mod_x[2], gpre, gpost, w_merge[l], oa, ob, mx, of, obk,
                        hn, head_mean, w_branch_b[l], w_out_b[l], tm_x, True)

        if with_ctx:
            oa_c = _cattn_call(qp_c, k_c, v_c)
            ob_c2 = _fourier_ctx_call(bx_c, fnet_b[l])
            ctx = _merge_call(ctx, mod_c[0], mod_c[1], mod_c[2], gpre, gpost, w_merge[l], oa_c, ob_c2, mx_c,
                              of_c, ob_c, hn, head_mean, w_branch_b[l], w_out_b[l], tm_c, False)
    return x
```

```python
import functools

import numpy as np
import jax
import jax.numpy as jnp
from jax import lax
from jax.experimental import pallas as pl
from jax.experimental.pallas import tpu as pltpu

F32 = jnp.float32
BF16 = jnp.bfloat16

D_MODEL = 1024
BRANCH_W = 512
N_BRANCH = 4
GRID_W = 64
HEAD_DIM = 64
N_HEADS = 8
LANES = 128
N_PAIRS = BRANCH_W // LANES
NA_WIN_H = 8
NA_WIN_W = 16
NA_TILE_ROWS = 4
NA_BAND_ROWS = 12
NA_VCHUNK = 256
LOG2E = 1.4426950408889634
ROPE_THETA = 10000.0
ROPE_FREQS = 16
FNET_GROUPS = 4
FNET_GROUP_W = 128
GMLP_CHUNK = 128
GMLP_GROUPS = 8
HGRN_CHUNK = 64
HGRN_SUB = 16
HGRN_EXP_CLAMP = 115.0
HGRN_ROWS = 256
EPS = 1e-6
F_FLOOR = 1e-30
NEG_INF = -1e30

VMEM_LIMIT = 56 * 2**20

_IN_COL = {'a_q': 0, 'a_k': 1, 'a_v': 2, 'a_g': 3, 'b_x': 4, 'b_g': 5, 'c_u': 6, 'c_v': 7, 'c_g': 8,
           'd_q': 9, 'd_f_fwd': 10, 'd_f_bwd': 11, 'd_i': 12, 'd_g': 13}
_PROJ_NAMES = ('a_q', 'a_k', 'a_v', 'b_x', 'c_v', 'd_q', 'd_f_fwd', 'd_f_bwd', 'd_i')
_MERGE_NAMES = ('c_u', 'a_g', 'b_g', 'c_g', 'd_g')
_GATE_COL0 = 14 * BRANCH_W


def _cparams(sem):
    return pltpu.CompilerParams(dimension_semantics=sem, vmem_limit_bytes=VMEM_LIMIT)


def _resident(shape):
    nd = len(shape)
    return pl.BlockSpec(shape, lambda *_: (0,) * nd, pipeline_mode=pl.Buffered(1))


def _silu(t):
    return t * jax.nn.sigmoid(t)


def _lane_iota(shape):
    return lax.broadcasted_iota(jnp.int32, shape, len(shape) - 1)


def _dot(a, b):
    return jnp.dot(a, b, preferred_element_type=F32)


def _dot_nt(a, b):
    return lax.dot_general(a, b, (((1,), (1,)), ((), ())), preferred_element_type=F32)


def _dot_tn(a, b):
    return lax.dot_general(a, b, (((0,), (0,)), ((), ())), preferred_element_type=F32)


def _split2(t):
    hi = t.astype(BF16)
    return hi, (t - hi.astype(F32)).astype(BF16)


def _normed_input(x, gpre, sc, sh):
    ms = jnp.mean(x * x, axis=-1, keepdims=True)
    h = x * lax.rsqrt(ms + EPS) * gpre
    return h * (1.0 + sc) + sh


def _mod_kernel(c_ref, w_ref, b_ref, o_ref):
    s = _silu(c_ref[...]).astype(BF16)
    o_ref[0] = _dot(s, w_ref[0].astype(BF16)) + b_ref[0]


def _modulation(c_all, w_ada, b_ada):
    depth = w_ada.shape[0]
    rows = c_all.shape[0]
    tn = 1024
    return pl.pallas_call(
        _mod_kernel,
        out_shape=jax.ShapeDtypeStruct((depth, rows, 3 * D_MODEL), F32),
        grid=(depth, 3 * D_MODEL // tn),
        in_specs=[pl.BlockSpec((rows, D_MODEL), lambda l, j: (0, 0)),
                  pl.BlockSpec((1, D_MODEL, tn), lambda l, j: (l, 0, j)),
                  pl.BlockSpec((1, 1, tn), lambda l, j: (l, 0, j))],
        out_specs=pl.BlockSpec((1, rows, tn), lambda l, j: (l, 0, j)),
        compiler_params=_cparams(("arbitrary", "arbitrary")),
        name="adaln_modulation",
    )(c_all, w_ada, b_ada.reshape(depth, 1, 3 * D_MODEL))


def _lb_kernel(lg_ref, o_ref):
    depth = lg_ref.shape[0]
    lg = [lg_ref[l] for l in range(depth)]
    m = functools.reduce(jnp.maximum, lg)
    e = [jnp.exp(t - m) for t in lg]
    tot = functools.reduce(lambda a, b: a + b, e)
    sm = [t / tot for t in e]
    run = jnp.zeros_like(sm[0])
    for l in range(depth):
        run = run + sm[l]
        o_ref[l] = jnp.maximum(run - sm[0], 0.0)


def _lower_bounds(lb_logits):
    return pl.pallas_call(
        _lb_kernel,
        out_shape=jax.ShapeDtypeStruct(lb_logits.shape, F32),
        name="hgrn_lower_bounds",
    )(lb_logits)


def _rope(t, cos, sin_signed, first_half):
    up = pltpu.roll(t, LANES - ROPE_FREQS, 1)
    down = pltpu.roll(t, ROPE_FREQS, 1)
    return t * cos + jnp.where(first_half, up, down) * sin_signed


def _proj_kernel(x_ref, sh_ref, sc_ref, gpre_ref, w_ref, cos_ref, sin_ref, gn_ref, ws_ref, bs_ref,
                 lb_ref, qr_ref, qp_ref, kr_ref, v_ref, bx_ref, mx_ref, dq_ref, di_ref,
                 lff_ref, kf_ref, lfb_ref, kb_ref):
    tm = x_ref.shape[1]
    hb = _normed_input(x_ref[0], gpre_ref[...], sc_ref[0], sh_ref[0]).astype(BF16)

    def proj(name):
        j = _PROJ_NAMES.index(name)
        return _dot(hb, w_ref[:, j * BRANCH_W:(j + 1) * BRANCH_W])

    cos = cos_ref[...]
    sin_signed = sin_ref[...]
    first_half = (_lane_iota((tm, LANES)) % (2 * ROPE_FREQS)) < ROPE_FREQS

    def rope_all(t):
        return jnp.concatenate(
            [_rope(t[:, p * LANES:(p + 1) * LANES], cos, sin_signed, first_half) for p in range(N_PAIRS)],
            axis=1)

    for name, d, lf_ref, k_ref in (('d_f_fwd', 0, lff_ref, kf_ref), ('d_f_bwd', 1, lfb_ref, kb_ref)):
        z = proj(name)
        lb = lb_ref[0, d:d + 1, :]
        f = lb + (1.0 - lb) * jax.nn.sigmoid(z)
        lf_ref[0] = jnp.log(jnp.maximum(f, F_FLOOR))
        k_ref[0] = ((1.0 - lb) * jax.nn.sigmoid(-z)).astype(BF16)

    cv = proj('c_v')
    vn = (cv * lax.rsqrt(jnp.mean(cv * cv, axis=-1, keepdims=True) + EPS) * gn_ref[...]).astype(BF16)
    low_group = _lane_iota((GMLP_CHUNK, LANES)) < (LANES // 2)
    for ch in range(tm // GMLP_CHUNK):
        r0 = ch * GMLP_CHUNK
        for p in range(N_PAIRS):
            slab = vn[r0:r0 + GMLP_CHUNK, p * LANES:(p + 1) * LANES]
            mixed = jnp.where(low_group, _dot(ws_ref[2 * p], slab), _dot(ws_ref[2 * p + 1], slab))
            mixed = mixed + bs_ref[:, p * LANES:(p + 1) * LANES]
            mx_ref[0, r0:r0 + GMLP_CHUNK, p * LANES:(p + 1) * LANES] = mixed.astype(BF16)

    q = proj('a_q') * (HEAD_DIM ** -0.5 * LOG2E)
    qp_ref[0] = q.astype(BF16)
    qr_ref[0] = rope_all(q).astype(BF16)
    kr_ref[0] = rope_all(proj('a_k')).astype(BF16)
    v = proj('a_v')
    for ch in range(tm // NA_VCHUNK):
        v_ref[0, ch] = v[ch * NA_VCHUNK:(ch + 1) * NA_VCHUNK].T.astype(BF16)
    bx_ref[0] = proj('b_x').astype(BF16)
    dq_ref[0] = proj('d_q').astype(BF16)
    di_ref[0] = proj('d_i').astype(BF16)


def _proj_call(x, sh, sc, gpre, w_proj, cos, sin_signed, gn, ws, bs_tab, lb, tm, per_batch_mod):
    b, n, _ = x.shape
    nt = n // tm
    mod_map = (lambda i, j: (i, 0, 0)) if per_batch_mod else (lambda i, j: (0, 0, 0))
    row_spec = pl.BlockSpec((1, tm, BRANCH_W), lambda i, j: (i, j, 0))
    vt_spec = pl.BlockSpec((1, tm // NA_VCHUNK, BRANCH_W, NA_VCHUNK), lambda i, j: (i, j, 0, 0))
    bf = jax.ShapeDtypeStruct((b, n, BRANCH_W), BF16)
    vt = jax.ShapeDtypeStruct((b, n // NA_VCHUNK, BRANCH_W, NA_VCHUNK), BF16)
    f32 = jax.ShapeDtypeStruct((b, n, BRANCH_W), F32)
    return pl.pallas_call(
        _proj_kernel,
        out_shape=(bf, bf, bf, vt, bf, bf, bf, bf, f32, bf, f32, bf),
        grid=(b, nt),
        in_specs=[pl.BlockSpec((1, tm, D_MODEL), lambda i, j: (i, j, 0)),
                  pl.BlockSpec((1, 1, D_MODEL), mod_map),
                  pl.BlockSpec((1, 1, D_MODEL), mod_map),
                  _resident((1, D_MODEL)),
                  _resident((D_MODEL, len(_PROJ_NAMES) * BRANCH_W)),
                  pl.BlockSpec((tm, LANES), lambda i, j: (j, 0)),
                  pl.BlockSpec((tm, LANES), lambda i, j: (j, 0)),
                  _resident((1, BRANCH_W)),
                  _resident((GMLP_GROUPS, GMLP_CHUNK, GMLP_CHUNK)),
                  _resident((GMLP_CHUNK, BRANCH_W)),
                  _resident((1, 2, BRANCH_W))],
        out_specs=(row_spec,) * 3 + (vt_spec,) + (row_spec,) * 8,
        compiler_params=_cparams(("arbitrary", "arbitrary")),
        name="branch_proj",
    )(x, sh, sc, gpre, w_proj, cos, sin_signed, gn, ws, bs_tab, lb)


def _head_mask(hh):
    lane = _lane_iota((1, LANES))
    return (lane < HEAD_DIM) if hh == 0 else (lane >= HEAD_DIM)


def _nattn_kernel(qr_ref, qp_ref, k_ref, vt_ref, kc_ref, vct_ref, bias_ref, o_ref):
    rows = k_ref.shape[1] // GRID_W
    r0 = pl.program_id(1) * NA_TILE_ROWS
    kb0 = jnp.clip(r0 - NA_WIN_H // 2, 0, rows - NA_BAND_ROWS)
    start = pl.multiple_of(kb0 * GRID_W, NA_VCHUNK)
    c0 = kb0 // (NA_VCHUNK // GRID_W)
    band = NA_BAND_ROWS * GRID_W
    nq = NA_TILE_ROWS * GRID_W

    def lanes(p):
        return slice(p * LANES, (p + 1) * LANES)

    def stack_heads(q):
        zero = jnp.zeros_like(q)
        return jnp.concatenate([jnp.where(_head_mask(0), q, zero), jnp.where(_head_mask(1), q, zero)], axis=0)

    def scores(p):
        s_ctx = _dot_nt(kc_ref[0, :, lanes(p)], stack_heads(qp_ref[0, :, lanes(p)]))
        s_band = _dot_nt(k_ref[0, pl.ds(start, band), lanes(p)], stack_heads(qr_ref[0, :, lanes(p)]))
        return s_ctx, s_band + bias_ref[0, p]

    def softmax(s_ctx, s_band):
        p_ctx, p_band, dens = [], [], []
        for cb in range(2 * nq // LANES):
            cs = slice(cb * LANES, (cb + 1) * LANES)
            sc, sb = s_ctx[:, cs], s_band[:, cs]
            mx = jnp.maximum(jnp.max(sc, axis=0, keepdims=True), jnp.max(sb, axis=0, keepdims=True))
            ec, eb = jnp.exp2(sc - mx), jnp.exp2(sb - mx)
            dens.append(jnp.sum(ec, axis=0, keepdims=True) + jnp.sum(eb, axis=0, keepdims=True))
            p_ctx.append(ec.astype(BF16))
            p_band.append(eb.astype(BF16))
        return jnp.concatenate(p_ctx, axis=1), jnp.concatenate(p_band, axis=1), jnp.concatenate(dens, axis=1)

    def values(p, p_ctx, p_band, den):
        outs = []
        for hh in range(2):
            qs = slice(hh * nq, (hh + 1) * nq)
            ch = slice(p * LANES + hh * HEAD_DIM, p * LANES + (hh + 1) * HEAD_DIM)
            acc = None
            for j in range(vct_ref.shape[1]):
                term = _dot(vct_ref[0, j, ch, :], p_ctx[j * NA_VCHUNK:(j + 1) * NA_VCHUNK, qs])
                acc = term if acc is None else acc + term
            for j in range(band // NA_VCHUNK):
                acc = acc + _dot(vt_ref[0, c0 + j, ch, :], p_band[j * NA_VCHUNK:(j + 1) * NA_VCHUNK, qs])
            outs.append(acc * (1.0 / den[:, qs]))
        o_ref[0, :, lanes(p)] = jnp.concatenate(outs, axis=0).T.astype(BF16)

    s_val, p_val = {}, {}
    for t in range(N_PAIRS + 2):
        if t < N_PAIRS:
            s_val[t] = scores(t)
        if 0 <= t - 1 < N_PAIRS:
            p_val[t - 1] = softmax(*s_val.pop(t - 1))
        if 0 <= t - 2 < N_PAIRS:
            values(t - 2, *p_val.pop(t - 2))


def _nattn_call(qr, qp, k, vt, kc, vct, bias):
    b, n, _ = qr.shape
    rows = n // GRID_W
    nt = rows // NA_TILE_ROWS
    lc = kc.shape[1]
    nq = NA_TILE_ROWS * GRID_W

    def bias_map(i, t):
        return (jnp.where(t == 0, 0, jnp.where(t == nt - 1, 2, 1)), 0, 0, 0)

    q_spec = pl.BlockSpec((1, nq, BRANCH_W), lambda i, t: (i, t, 0))
    full = pl.BlockSpec((1, n, BRANCH_W), lambda i, t: (i, 0, 0))
    full_t = pl.BlockSpec((1, n // NA_VCHUNK, BRANCH_W, NA_VCHUNK), lambda i, t: (i, 0, 0, 0))
    ctx = pl.BlockSpec((1, lc, BRANCH_W), lambda i, t: (i, 0, 0))
    ctx_t = pl.BlockSpec((1, lc // NA_VCHUNK, BRANCH_W, NA_VCHUNK), lambda i, t: (i, 0, 0, 0))
    return pl.pallas_call(
        _nattn_kernel,
        out_shape=jax.ShapeDtypeStruct((b, n, BRANCH_W), BF16),
        grid=(b, nt),
        in_specs=[q_spec, q_spec, full, full_t, ctx, ctx_t,
                  pl.BlockSpec((1, N_PAIRS, NA_BAND_ROWS * GRID_W, 2 * nq), bias_map)],
        out_specs=q_spec,
        compiler_params=_cparams(("arbitrary", "arbitrary")),
        name="neighbourhood_attention",
    )(qr, qp, k, vt, kc, vct, bias)


def _cattn_kernel(q_ref, k_ref, vt_ref, o_ref):
    lc = q_ref.shape[1]
    low_head = _lane_iota((lc, LANES)) < HEAD_DIM
    for p in range(N_PAIRS):
        ls = slice(p * LANES, (p + 1) * LANES)
        q = q_ref[0, :, ls]
        k = k_ref[0, :, ls]
        vt = jnp.concatenate([vt_ref[0, j, ls, :] for j in range(vt_ref.shape[1])], axis=1)
        outs = []
        for hh in range(2):
            s = _dot_nt(jnp.where(_head_mask(hh), q, jnp.zeros_like(q)), k)
            e = jnp.exp2(s - jnp.max(s, axis=-1, keepdims=True))
            outs.append(_dot_nt(e.astype(BF16), vt) * (1.0 / jnp.sum(e, axis=-1, keepdims=True)))
        o_ref[0, :, ls] = jnp.where(low_head, outs[0], outs[1]).astype(BF16)


def _cattn_call(q, k, vt):
    b, lc, _ = q.shape
    spec = pl.BlockSpec((1, lc, BRANCH_W), lambda i: (i, 0, 0))
    spec_t = pl.BlockSpec((1, lc // NA_VCHUNK, BRANCH_W, NA_VCHUNK), lambda i: (i, 0, 0, 0))
    return pl.pallas_call(
        _cattn_kernel,
        out_shape=jax.ShapeDtypeStruct((b, lc, BRANCH_W), BF16),
        grid=(b,),
        in_specs=[spec, spec, spec_t],
        out_specs=spec,
        compiler_params=_cparams(("arbitrary",)),
        name="context_attention",
    )(q, k, vt)


def _attention_bias(rpb, rows):
    col = jnp.arange(GRID_W)
    col_start = jnp.clip(col - NA_WIN_W // 2, 0, GRID_W - NA_WIN_W)
    valid = (col[None, :] >= col_start[:, None]) & (col[None, :] < col_start[:, None] + NA_WIN_W)
    col_idx = jnp.clip(col[None, :] - col[:, None] + NA_WIN_W - 1, 0, 2 * NA_WIN_W - 2)
    per_row = jnp.where(valid[None, None], rpb.astype(F32)[:, :, col_idx] * LOG2E, NEG_INF)
    tabs = []
    for r0 in (0, NA_TILE_ROWS, rows - NA_TILE_ROWS):
        kb0 = int(np.clip(r0 - NA_WIN_H // 2, 0, rows - NA_BAND_ROWS))
        r = r0 + np.arange(NA_TILE_ROWS)[:, None]
        kr = kb0 + np.arange(NA_BAND_ROWS)[None, :]
        rs = np.clip(r - NA_WIN_H // 2, 0, rows - NA_WIN_H)
        in_win = (kr >= rs) & (kr < rs + NA_WIN_H)
        idx = np.clip(kr - r + NA_WIN_H - 1, 0, 2 * NA_WIN_H - 2)
        t = jnp.where(in_win[None, :, :, None, None], per_row[:, idx], NEG_INF)
        t = t.reshape(N_PAIRS, 2, NA_TILE_ROWS, NA_BAND_ROWS, GRID_W, GRID_W).transpose(0, 3, 5, 1, 2, 4)
        tabs.append(t.reshape(N_PAIRS, NA_BAND_ROWS * GRID_W, 2 * NA_TILE_ROWS * GRID_W))
    return jnp.stack(tabs)


_KRON = 8


@functools.lru_cache(maxsize=None)
def _fourier_consts(n):
    rows = n // GRID_W
    k1 = np.arange(rows)[:, None, None, None]
    l1 = np.arange(_KRON)[None, :, None, None]
    n1 = np.arange(rows)[None, None, :, None]
    l2 = np.arange(_KRON)[None, None, None, :]
    a_cos, a_sin = [], []
    for j in range(GRID_W // _KRON):
        ang = 2.0 * np.pi * k1 * (GRID_W * n1 + _KRON * j + l1) / n
        same = (l1 == l2)
        a_cos.append((np.cos(ang) * same).reshape(rows * _KRON, rows * _KRON))
        a_sin.append((-np.sin(ang) * same).reshape(rows * _KRON, rows * _KRON))
    k2 = np.arange(GRID_W)[:, None, None, None]
    ang = 2.0 * np.pi * k2 * np.arange(GRID_W)[None, None, None, :] / GRID_W
    same = (np.arange(_KRON)[None, :, None, None] == np.arange(_KRON)[None, None, :, None])
    b_cos = (np.cos(ang) * same).reshape(GRID_W * _KRON, _KRON * GRID_W)
    b_sin = (np.sin(ang) * same).reshape(GRID_W * _KRON, _KRON * GRID_W)
    b_re = np.concatenate([b_cos, b_sin], axis=1)
    b_im = np.concatenate([-b_sin, b_cos], axis=1)
    return (np.stack(a_cos).astype(np.float32), np.stack(a_sin).astype(np.float32),
            b_re.astype(np.float32), b_im.astype(np.float32))


@functools.lru_cache(maxsize=None)
def _channel_dft():
    c = np.arange(FNET_GROUP_W)
    ang = 2.0 * np.pi * np.outer(c, c) / FNET_GROUP_W
    return np.concatenate([np.cos(ang), np.sin(ang)], axis=0).astype(np.float32)


@functools.lru_cache(maxsize=None)
def _dense_dft(n):
    t = np.arange(n)
    ang = 2.0 * np.pi * np.outer(t, t) / n
    return np.cos(ang).astype(np.float32), (-np.sin(ang)).astype(np.float32)


def _channel_stage(xr, xi, cs_ref, wf_ref, norm):
    outs = []
    for g in range(FNET_GROUPS):
        ls = slice(g * FNET_GROUP_W, (g + 1) * FNET_GROUP_W)
        xg = jnp.concatenate([xr[:, ls], xi[:, ls]], axis=1).astype(BF16)
        spec = _dot(xg, cs_ref[...]) * norm
        outs.append(_dot(spec.astype(BF16), wf_ref[g]))
    return jnp.concatenate(outs, axis=1)


def _fourier_kernel(x_ref, ac_ref, as_ref, bre_ref, bim_ref, cs_ref, wf_ref, o_ref, s_ref, *, norm):
    rows = x_ref.shape[1]
    blk = rows * _KRON
    pair = 2 * _KRON
    for jj in range(GRID_W // pair):
        xt = x_ref[0, :, jj * pair:(jj + 1) * pair, :].astype(F32)
        re, im = [], []
        for half in range(2):
            xc = xt[:, half * _KRON:(half + 1) * _KRON, :].reshape(blk, BRANCH_W).astype(BF16)
            re.append(_dot(ac_ref[2 * jj + half], xc).reshape(rows, _KRON, BRANCH_W))
            im.append(_dot(as_ref[2 * jj + half], xc).reshape(rows, _KRON, BRANCH_W))
        s_ref[0, :, jj * pair:(jj + 1) * pair, :] = jnp.concatenate(re, axis=1).astype(BF16)
        s_ref[1, :, jj * pair:(jj + 1) * pair, :] = jnp.concatenate(im, axis=1).astype(BF16)
    sblk = _KRON * GRID_W
    for mm in range(rows // pair):
        ys = []
        for half in range(2):
            m0 = (2 * mm + half) * _KRON
            rhs = jnp.concatenate([s_ref[0, m0:m0 + _KRON].reshape(sblk, BRANCH_W),
                                   s_ref[1, m0:m0 + _KRON].reshape(sblk, BRANCH_W)], axis=0)
            xr = _dot(bre_ref[...], rhs)
            xi = _dot(bim_ref[...], rhs)
            ys.append(_channel_stage(xr, xi, cs_ref, wf_ref, norm).reshape(GRID_W, _KRON, BRANCH_W))
        o_ref[0, :, mm * pair:(mm + 1) * pair, :] = jnp.concatenate(ys, axis=1).astype(BF16)


def _fourier_call(bx, wf):
    b, n, _ = bx.shape
    rows = n // GRID_W
    a_cos, a_sin, b_re, b_im = (jnp.asarray(t, BF16) for t in _fourier_consts(n))
    cs = jnp.asarray(_channel_dft(), BF16)
    norm = float(1.0 / np.sqrt(n * FNET_GROUP_W))
    x4 = bx.reshape(b, rows, GRID_W, BRANCH_W)
    out = pl.pallas_call(
        functools.partial(_fourier_kernel, norm=norm),
        out_shape=jax.ShapeDtypeStruct((b, GRID_W, rows, BRANCH_W), BF16),
        grid=(b,),
        in_specs=[pl.BlockSpec((1, rows, GRID_W, BRANCH_W), lambda i: (i, 0, 0, 0)),
                  _resident(a_cos.shape), _resident(a_sin.shape),
                  _resident(b_re.shape), _resident(b_im.shape),
                  _resident(cs.shape), _resident(wf.shape)],
        out_specs=pl.BlockSpec((1, GRID_W, rows, BRANCH_W), lambda i: (i, 0, 0, 0)),
        scratch_shapes=[pltpu.VMEM((2, rows, GRID_W, BRANCH_W), BF16)],
        compiler_params=_cparams(("arbitrary",)),
        name="fourier_mix",
    )(x4, a_cos, a_sin, b_re, b_im, cs, wf)
    return out.reshape(b, n, BRANCH_W)


def _fourier_ctx_kernel(x_ref, c_ref, s_ref, cs_ref, wf_ref, o_ref, *, norm):
    x = x_ref[0]
    xr = _dot(c_ref[...], x)
    xi = _dot(s_ref[...], x)
    o_ref[0] = _channel_stage(xr, xi, cs_ref, wf_ref, norm).astype(BF16)


def _fourier_ctx_call(bx, wf):
    b, n, _ = bx.shape
    cn, sn = (jnp.asarray(t, BF16) for t in _dense_dft(n))
    cs = jnp.asarray(_channel_dft(), BF16)
    norm = float(1.0 / np.sqrt(n * FNET_GROUP_W))
    spec = pl.BlockSpec((1, n, BRANCH_W), lambda i: (i, 0, 0))
    return pl.pallas_call(
        functools.partial(_fourier_ctx_kernel, norm=norm),
        out_shape=jax.ShapeDtypeStruct((b, n, BRANCH_W), BF16),
        grid=(b,),
        in_specs=[spec, _resident(cn.shape), _resident(sn.shape), _resident(cs.shape), _resident(wf.shape)],
        out_specs=spec,
        compiler_params=_cparams(("arbitrary",)),
        name="fourier_mix_context",
    )(bx, cn, sn, cs, wf)


def _block_diag(t):
    lo = _lane_iota(t.shape) < HEAD_DIM
    z = jnp.zeros_like(t)
    return jnp.concatenate([jnp.where(lo, t, z), jnp.where(lo, z, t)], axis=0)


def _hgrn_needed(j, reverse):
    nsub = HGRN_CHUNK // HGRN_SUB
    return list(range(0, j + 1)) if reverse else list(range(j, nsub))


def _hgrn_prepare(q, k, i, a, reverse):
    c = HGRN_CHUNK
    nsub = c // HGRN_SUB

    def level(r):
        return a[r:r + 1, :]

    zero_row = jnp.zeros((1, BRANCH_W), F32)
    if reverse:
        refs = [level((s + 1) * HGRN_SUB) if s + 1 < nsub else zero_row for s in range(nsub)]
        a_end = a[0:1, :]
    else:
        refs = [level(s * HGRN_SUB - 1) if s > 0 else zero_row for s in range(nsub)]
        a_end = a[c - 1:c, :]
    ref_rows = jnp.concatenate([jnp.broadcast_to(r, (HGRN_SUB, BRANCH_W)) for r in refs], axis=0)
    qf = q.astype(F32)
    kf = k.astype(F32)
    k_own = (kf * jnp.exp2(jnp.minimum(ref_rows - a, HGRN_EXP_CLAMP))).astype(BF16)

    def q_variant(j):
        parts = []
        for s in _hgrn_needed(j, reverse):
            rs = slice(s * HGRN_SUB, (s + 1) * HGRN_SUB)
            parts.append((qf[rs] * jnp.exp2(a[rs] - refs[j])).astype(BF16))
        return jnp.concatenate(parts, axis=0)

    q_var = [q_variant(j) for j in range(nsub)]
    return dict(
        q_stack=jnp.concatenate(q_var, axis=0),
        q_in=q_var[nsub - 1] if reverse else q_var[0],
        k_own=k_own,
        k_out=(kf * jnp.exp2(a_end - a)).astype(BF16),
        decay_end=jnp.exp2(a_end),
        i=i, reverse=reverse)


def _hgrn_scores(ops):
    c = HGRN_CHUNK
    nsub = c // HGRN_SUB
    reverse = ops['reverse']
    src = _lane_iota((c, LANES)) % HEAD_DIM
    step = lax.broadcasted_iota(jnp.int32, (c, LANES), 0)
    seen = (src >= step) if reverse else (src <= step)
    src_sub = (_lane_iota((HGRN_SUB, LANES)) % HEAD_DIM) // HGRN_SUB
    where_blk, off = {}, 0
    for j in range(nsub):
        for s in _hgrn_needed(j, reverse):
            where_blk[(j, s)] = off
            off += HGRN_SUB
    out = []
    for p in range(N_PAIRS):
        ls = slice(p * LANES, (p + 1) * LANES)
        res = _dot_nt(ops['q_stack'][:, ls], _block_diag(ops['k_own'][:, ls]))
        rows = []
        for s in range(nsub):
            blk = None
            for j in range(nsub):
                if (j, s) in where_blk:
                    piece = res[where_blk[(j, s)]:where_blk[(j, s)] + HGRN_SUB]
                    blk = piece if blk is None else jnp.where(src_sub == j, piece, blk)
            rows.append(blk)
        out.append(jnp.where(seen, jnp.concatenate(rows, axis=0), 0.0).astype(BF16))
    return out


def _hgrn_local(ops, scores):
    low_rows = lax.broadcasted_iota(jnp.int32, (LANES, LANES), 0) < HEAD_DIM
    same_head = low_rows == (_lane_iota((LANES, LANES)) < HEAD_DIM)
    o_intra, upd = [], []
    for p in range(N_PAIRS):
        ls = slice(p * LANES, (p + 1) * LANES)
        ip = ops['i'][:, ls]
        o_intra.append(_dot(scores[p], _block_diag(ip)))
        upd.append(jnp.where(same_head, _dot_tn(ip, ops['k_out'][:, ls]), 0.0))
    return o_intra, upd


def _hgrn_carry(ops, o_intra, upd, state_ref, d):
    outs = []
    for p in range(N_PAIRS):
        ls = slice(p * LANES, (p + 1) * LANES)
        st = state_ref[d, p]
        outs.append(o_intra[p] + _dot_nt(ops['q_in'][:, ls], st.astype(BF16)))
        state_ref[d, p] = ops['decay_end'][:, ls] * st + upd[p]
    return jnp.concatenate(outs, axis=1)


def _hgrn_kernel(qf_ref, if_ref, lff_ref, kf_ref, qb_ref, ib_ref, lfb_ref, kb_ref, s0f_ref, s0b_ref,
                 of_ref, ob_ref, sf_ref, sb_ref, state_ref):
    j = pl.program_id(1)
    nchunk = qf_ref.shape[1] // HGRN_CHUNK

    @pl.when(j == 0)
    def _():
        state_ref[0] = s0f_ref[0]
        state_ref[1] = s0b_ref[0]

    tm = qf_ref.shape[1]
    row = lax.broadcasted_iota(jnp.int32, (tm, tm), 0)
    col = lax.broadcasted_iota(jnp.int32, (tm, tm), 1)
    same_chunk = (row // HGRN_CHUNK) == (col // HGRN_CHUNK)

    def cum(lf, reverse):
        tri = (same_chunk & ((col >= row) if reverse else (col <= row))).astype(BF16)
        hi, lo = _split2(lf * LOG2E)
        return _dot(tri, hi) + _dot(tri, lo)

    a_f = cum(lff_ref[0], False)
    a_b = cum(lfb_ref[0], True)

    work = []
    for cix in range(nchunk):
        fs = slice(cix * HGRN_CHUNK, (cix + 1) * HGRN_CHUNK)
        work.append((0, of_ref, fs, _hgrn_prepare(qf_ref[0, fs, :], kf_ref[0, fs, :], if_ref[0, fs, :],
                                                  a_f[fs, :], False)))
        bix = nchunk - 1 - cix
        bs = slice(bix * HGRN_CHUNK, (bix + 1) * HGRN_CHUNK)
        work.append((1, ob_ref, bs, _hgrn_prepare(qb_ref[0, bs, :], kb_ref[0, bs, :], ib_ref[0, bs, :],
                                                  a_b[bs, :], True)))
    scores = [_hgrn_scores(ops) for _, _, _, ops in work]
    local = [_hgrn_local(ops, sc) for (_, _, _, ops), sc in zip(work, scores)]
    for (d, o_ref, rs, ops), (o_intra, upd) in zip(work, local):
        o_ref[0, rs, :] = _hgrn_carry(ops, o_intra, upd, state_ref, d).astype(BF16)

    @pl.when(j == pl.num_programs(1) - 1)
    def _():
        sf_ref[0] = state_ref[0]
        sb_ref[0] = state_ref[1]


def _hgrn_call(q, i, lff, kf, lfb, kb, s0f, s0b, tm):
    b, n, _ = q.shape
    nt = n // tm
    fwd = pl.BlockSpec((1, tm, BRANCH_W), lambda bi, j: (bi, j, 0))
    bwd = pl.BlockSpec((1, tm, BRANCH_W), lambda bi, j: (bi, nt - 1 - j, 0))
    st = pl.BlockSpec((1, N_PAIRS, LANES, LANES), lambda bi, j: (bi, 0, 0, 0))
    o_shape = jax.ShapeDtypeStruct((b, n, BRANCH_W), BF16)
    s_shape = jax.ShapeDtypeStruct((b, N_PAIRS, LANES, LANES), F32)
    return pl.pallas_call(
        _hgrn_kernel,
        out_shape=(o_shape, o_shape, s_shape, s_shape),
        grid=(b, nt),
        in_specs=[fwd, fwd, fwd, fwd, bwd, bwd, bwd, bwd, st, st],
        out_specs=(fwd, bwd, st, st),
        scratch_shapes=[pltpu.VMEM((2, N_PAIRS, LANES, LANES), F32)],
        compiler_params=_cparams(("arbitrary", "arbitrary")),
        name="hgrn_scan",
    )(q, i, lff, kf, q, i, lfb, kb, s0f, s0b)


def _merge_kernel(x_ref, sh_ref, sc_ref, gt_ref, gpre_ref, gpost_ref, w_ref, oa_ref, ob_ref, mx_ref,
                  of_ref, obk_ref, hn_ref, hm_ref, wb_ref, wo_ref, o_ref):
    x = x_ref[0]
    hb = _normed_input(x, gpre_ref[...], sc_ref[0], sh_ref[0]).astype(BF16)

    def proj(name):
        j = _MERGE_NAMES.index(name)
        return _dot(hb, w_ref[:, j * BRANCH_W:(j + 1) * BRANCH_W])

    ya = oa_ref[0].astype(F32) * _silu(proj('a_g'))
    yb = ob_ref[0].astype(F32) * _silu(proj('b_g'))
    yc = proj('c_u') * mx_ref[0].astype(F32) * _silu(proj('c_g'))
    o = of_ref[0].astype(F32) + obk_ref[0].astype(F32)
    ms = _dot((o * o).astype(BF16), hm_ref[...])
    yd = o * lax.rsqrt(ms + EPS) * hn_ref[...] * _silu(proj('d_g'))

    g0 = len(_MERGE_NAMES) * BRANCH_W
    merged = None
    for r, y in enumerate((ya, yb, yc, yd)):
        gate = _dot(hb, w_ref[:, g0 + r * D_MODEL:g0 + (r + 1) * D_MODEL])
        term = jax.nn.sigmoid(gate) * _dot(y.astype(BF16), wb_ref[r])
        merged = term if merged is None else merged + term
    out = _dot(merged.astype(BF16), wo_ref[...])
    post = out * lax.rsqrt(jnp.mean(out * out, axis=-1, keepdims=True) + EPS) * gpost_ref[...]
    o_ref[0] = x + gt_ref[0] * post


def _merge_call(x, sh, sc, gt, gpre, gpost, w_merge, oa, ob, mx, of, obk, hn, hmean, wb, wo, tm,
                per_batch_mod):
    b, n, _ = x.shape
    nt = n // tm
    mod_map = (lambda i, j: (i, 0, 0)) if per_batch_mod else (lambda i, j: (0, 0, 0))
    x_spec = pl.BlockSpec((1, tm, D_MODEL), lambda i, j: (i, j, 0))
    br_spec = pl.BlockSpec((1, tm, BRANCH_W), lambda i, j: (i, j, 0))
    mod_spec = pl.BlockSpec((1, 1, D_MODEL), mod_map)
    return pl.pallas_call(
        _merge_kernel,
        out_shape=jax.ShapeDtypeStruct((b, n, D_MODEL), F32),
        grid=(b, nt),
        in_specs=[x_spec, mod_spec, mod_spec, mod_spec,
                  _resident((1, D_MODEL)), _resident((1, D_MODEL)),
                  _resident(w_merge.shape),
                  br_spec, br_spec, br_spec, br_spec, br_spec,
                  _resident((1, BRANCH_W)), _resident((BRANCH_W, BRANCH_W)),
                  _resident(wb.shape), _resident(wo.shape)],
        out_specs=x_spec,
        compiler_params=_cparams(("arbitrary", "arbitrary")),
        name="branch_merge",
    )(x, sh, sc, gt, gpre, gpost, w_merge, oa, ob, mx, of, obk, hn, hmean, wb, wo)


def _rope_tables(n_tok, rotate):
    if not rotate:
        return jnp.ones((n_tok, LANES), F32), jnp.zeros((n_tok, LANES), F32)
    t = jnp.arange(n_tok, dtype=jnp.int32)
    pos = jnp.stack([t // GRID_W, t % GRID_W], axis=-1).astype(F32)
    inv = ROPE_THETA ** (-jnp.arange(ROPE_FREQS, dtype=F32) * 2.0 / (2 * ROPE_FREQS))
    ang = pos[:, :, None] * inv
    cos = jnp.repeat(jnp.cos(ang)[:, :, None, :], 2, axis=2).reshape(n_tok, HEAD_DIM)
    sin = jnp.sin(ang)
    sin_signed = jnp.stack([-sin, sin], axis=2).reshape(n_tok, HEAD_DIM)
    return jnp.tile(cos, (1, 2)), jnp.tile(sin_signed, (1, 2))


def _gather_cols(w_in_l, names):
    return jnp.concatenate([w_in_l[:, _IN_COL[nm] * BRANCH_W:(_IN_COL[nm] + 1) * BRANCH_W] for nm in names],
                           axis=1)


def _row_tile(n):
    return 512 if n % 512 == 0 else 256


def kernel(x, c, ctx, c_ctx, w_ada, b_ada, g_pre, g_post, w_in, na_rpb, fnet_w, gmlp_norm_g, gmlp_ws,
           gmlp_bs, hgrn_lb_logits, hgrn_norm_g, w_branch, w_out):
    batch, n_tok, _ = x.shape
    n_ctx = ctx.shape[1]
    depth = w_in.shape[0]

    w_in_b = w_in.astype(BF16)
    w_proj = [_gather_cols(w_in_b[l], _PROJ_NAMES) for l in range(depth)]
    w_merge = [jnp.concatenate([_gather_cols(w_in_b[l], _MERGE_NAMES), w_in_b[l][:, _GATE_COL0:]], axis=1)
               for l in range(depth)]
    w_branch_b = w_branch.astype(BF16)
    w_out_b = w_out.astype(BF16)
    fnet_b = fnet_w.astype(BF16)
    gmlp_ws_b = gmlp_ws.astype(BF16)
    bs_tab = jnp.repeat(jnp.swapaxes(gmlp_bs, 1, 2), BRANCH_W // GMLP_GROUPS, axis=2)
    head_mean = jnp.asarray(np.kron(np.eye(N_HEADS), np.ones((HEAD_DIM, HEAD_DIM)) / HEAD_DIM), BF16)
    cos_x, sin_x = _rope_tables(n_tok, True)
    cos_c, sin_c = _rope_tables(n_ctx, False)

    c_all = jnp.concatenate([c, jnp.broadcast_to(c_ctx[None, :], (8, D_MODEL))], axis=0)
    mod = _modulation(c_all, w_ada, b_ada)
    lower = _lower_bounds(hgrn_lb_logits)

    zero_state = jnp.zeros((batch, N_PAIRS, LANES, LANES), F32)
    tm_x = _row_tile(n_tok)
    tm_c = _row_tile(n_ctx)

    for l in range(depth):
        with_ctx = l < depth - 1
        mod_x = [mod[l, :batch, i * D_MODEL:(i + 1) * D_MODEL].reshape(batch, 1, D_MODEL) for i in range(3)]
        mod_c = [mod[l, batch:batch + 1, i * D_MODEL:(i + 1) * D_MODEL].reshape(1, 1, D_MODEL) for i in range(3)]
        gpre = g_pre[l].reshape(1, D_MODEL)
        gpost = g_post[l].reshape(1, D_MODEL)
        gn = gmlp_norm_g[l].reshape(1, BRANCH_W)
        hn = hgrn_norm_g[l].reshape(1, BRANCH_W)
        lb = lower[l].reshape(1, 2, BRANCH_W)
        bias = _attention_bias(na_rpb[l], n_tok // GRID_W)

        (_, qp_c, k_c, v_c, bx_c, mx_c, dq_c, di_c, lff_c, kf_c, lfb_c, kb_c) = _proj_call(
            ctx, mod_c[0], mod_c[1], gpre, w_proj[l], cos_c, sin_c, gn, gmlp_ws_b[l], bs_tab[l], lb,
            tm_c, False)
        of_c, ob_c, st_f, st_b = _hgrn_call(dq_c, di_c, lff_c, kf_c, lfb_c, kb_c, zero_state, zero_state,
                                            HGRN_ROWS)

        (qr, qp, k, v, bx, mx, dq, di, lff, kf, lfb, kb) = _proj_call(
            x, mod_x[0], mod_x[1], gpre, w_proj[l], cos_x, sin_x, gn, gmlp_ws_b[l], bs_tab[l], lb,
            tm_x, True)
        oa = _nattn_call(qr, qp, k, v, k_c, v_c, bias)
        ob = _fourier_call(bx, fnet_b[l])
        of, obk, _, _ = _hgrn_call(dq, di, lff, kf, lfb, kb, st_f, st_b, HGRN_ROWS)
        x = _merge_call(x, mod_x[0], mod_x[1], mod_x[2], gpre, gpost, w_merge[l], oa, ob, mx, of, obk,
                        hn, head_mean, w_branch_b[l], w_out_b[l], tm_x, True)

        if with_ctx:
            oa_c = _cattn_call(qp_c, k_c, v_c)
            ob_c2 = _fourier_ctx_call(bx_c, fnet_b[l])
            ctx = _merge_call(ctx, mod_c[0], mod_c[1], mod_c[2], gpre, gpost, w_merge[l], oa_c, ob_c2, mx_c,
                              of_c, ob_c, hn, head_mean, w_branch_b[l], w_out_b[l], tm_c, False)
    return x
```

```python
import functools

import numpy as np
import jax
import jax.numpy as jnp
from jax import lax
from jax.experimental import pallas as pl
from jax.experimental.pallas import tpu as pltpu

F32 = jnp.float32
BF16 = jnp.bfloat16

D_MODEL = 1024
BRANCH_W = 512
N_BRANCH = 4
GRID_W = 64
HEAD_DIM = 64
N_HEADS = 8
LANES = 128
N_PAIRS = BRANCH_W // LANES
NA_WIN_H = 8
NA_WIN_W = 16
NA_TILE_ROWS = 4
NA_BAND_ROWS = 12
NA_VCHUNK = 256
NA_DEN_ROWS = 16
LOG2E = 1.4426950408889634
ROPE_THETA = 10000.0
ROPE_FREQS = 16
FNET_GROUPS = 4
FNET_GROUP_W = 128
GMLP_CHUNK = 128
GMLP_GROUPS = 8
HGRN_CHUNK = 64
HGRN_SUB = 16
HGRN_EXP_CLAMP = 115.0
HGRN_ROWS = 256
EPS = 1e-6
F_FLOOR = 1e-30
NEG_INF = -1e30

VMEM_LIMIT = 56 * 2**20

_IN_COL = {'a_q': 0, 'a_k': 1, 'a_v': 2, 'a_g': 3, 'b_x': 4, 'b_g': 5, 'c_u': 6, 'c_v': 7, 'c_g': 8,
           'd_q': 9, 'd_f_fwd': 10, 'd_f_bwd': 11, 'd_i': 12, 'd_g': 13}
_PROJ_NAMES = ('a_q', 'a_k', 'a_v', 'b_x', 'c_v', 'd_q', 'd_f_fwd', 'd_f_bwd', 'd_i')
_MERGE_NAMES = ('c_u', 'a_g', 'b_g', 'c_g', 'd_g')
_GATE_COL0 = 14 * BRANCH_W


def _cparams(sem):
    return pltpu.CompilerParams(dimension_semantics=sem, vmem_limit_bytes=VMEM_LIMIT)


def _resident(shape):
    nd = len(shape)
    return pl.BlockSpec(shape, lambda *_: (0,) * nd, pipeline_mode=pl.Buffered(1))


def _silu(t):
    return t * jax.nn.sigmoid(t)


def _lane_iota(shape):
    return lax.broadcasted_iota(jnp.int32, shape, len(shape) - 1)


def _dot(a, b):
    return jnp.dot(a, b, preferred_element_type=F32)


def _dot_nt(a, b):
    return lax.dot_general(a, b, (((1,), (1,)), ((), ())), preferred_element_type=F32)


def _dot_tn(a, b):
    return lax.dot_general(a, b, (((0,), (0,)), ((), ())), preferred_element_type=F32)


def _split2(t):
    hi = t.astype(BF16)
    return hi, (t - hi.astype(F32)).astype(BF16)


def _normed_input(x, gpre, sc, sh):
    ms = jnp.mean(x * x, axis=-1, keepdims=True)
    h = x * lax.rsqrt(ms + EPS) * gpre
    return h * (1.0 + sc) + sh


def _mod_kernel(c_ref, w_ref, b_ref, o_ref):
    s = _silu(c_ref[...]).astype(BF16)
    o_ref[0] = _dot(s, w_ref[0].astype(BF16)) + b_ref[0]


def _modulation(c_all, w_ada, b_ada):
    depth = w_ada.shape[0]
    rows = c_all.shape[0]
    tn = 1024
    return pl.pallas_call(
        _mod_kernel,
        out_shape=jax.ShapeDtypeStruct((depth, rows, 3 * D_MODEL), F32),
        grid=(depth, 3 * D_MODEL // tn),
        in_specs=[pl.BlockSpec((rows, D_MODEL), lambda l, j: (0, 0)),
                  pl.BlockSpec((1, D_MODEL, tn), lambda l, j: (l, 0, j)),
                  pl.BlockSpec((1, 1, tn), lambda l, j: (l, 0, j))],
        out_specs=pl.BlockSpec((1, rows, tn), lambda l, j: (l, 0, j)),
        compiler_params=_cparams(("arbitrary", "arbitrary")),
        name="adaln_modulation",
    )(c_all, w_ada, b_ada.reshape(depth, 1, 3 * D_MODEL))


def _lb_kernel(lg_ref, o_ref):
    depth = lg_ref.shape[0]
    lg = [lg_ref[l] for l in range(depth)]
    m = functools.reduce(jnp.maximum, lg)
    e = [jnp.exp(t - m) for t in lg]
    tot = functools.reduce(lambda a, b: a + b, e)
    sm = [t / tot for t in e]
    run = jnp.zeros_like(sm[0])
    for l in range(depth):
        run = run + sm[l]
        o_ref[l] = jnp.maximum(run - sm[0], 0.0)


def _lower_bounds(lb_logits):
    return pl.pallas_call(
        _lb_kernel,
        out_shape=jax.ShapeDtypeStruct(lb_logits.shape, F32),
        name="hgrn_lower_bounds",
    )(lb_logits)


def _rope(t, cos, sin_signed, first_half):
    up = pltpu.roll(t, LANES - ROPE_FREQS, 1)
    down = pltpu.roll(t, ROPE_FREQS, 1)
    return t * cos + jnp.where(first_half, up, down) * sin_signed


def _proj_kernel(x_ref, sh_ref, sc_ref, gpre_ref, w_ref, cos_ref, sin_ref, gn_ref, ws_ref, bs_ref,
                 lb_ref, qr_ref, qp_ref, kr_ref, v_ref, bx_ref, mx_ref, dq_ref, di_ref,
                 lff_ref, kf_ref, lfb_ref, kb_ref):
    tm = x_ref.shape[1]
    hb = _normed_input(x_ref[0], gpre_ref[...], sc_ref[0], sh_ref[0]).astype(BF16)

    def proj(name):
        j = _PROJ_NAMES.index(name)
        return _dot(hb, w_ref[:, j * BRANCH_W:(j + 1) * BRANCH_W])

    cos = cos_ref[...]
    sin_signed = sin_ref[...]
    first_half = (_lane_iota((tm, LANES)) % (2 * ROPE_FREQS)) < ROPE_FREQS

    def rope_all(t):
        return jnp.concatenate(
            [_rope(t[:, p * LANES:(p + 1) * LANES], cos, sin_signed, first_half) for p in range(N_PAIRS)],
            axis=1)

    for name, d, lf_ref, k_ref in (('d_f_fwd', 0, lff_ref, kf_ref), ('d_f_bwd', 1, lfb_ref, kb_ref)):
        z = proj(name)
        lb = lb_ref[0, d:d + 1, :]
        sg = jax.nn.sigmoid(z)
        lf_ref[0] = jnp.log(jnp.maximum(lb + (1.0 - lb) * sg, F_FLOOR))
        k_ref[0] = ((1.0 - lb) * (1.0 - sg)).astype(BF16)

    cv = proj('c_v')
    vn = (cv * lax.rsqrt(jnp.mean(cv * cv, axis=-1, keepdims=True) + EPS) * gn_ref[...]).astype(BF16)
    low_group = _lane_iota((GMLP_CHUNK, LANES)) < (LANES // 2)
    for ch in range(tm // GMLP_CHUNK):
        r0 = ch * GMLP_CHUNK
        for p in range(N_PAIRS):
            slab = vn[r0:r0 + GMLP_CHUNK, p * LANES:(p + 1) * LANES]
            mixed = jnp.where(low_group, _dot(ws_ref[2 * p], slab), _dot(ws_ref[2 * p + 1], slab))
            mixed = mixed + bs_ref[:, p * LANES:(p + 1) * LANES]
            mx_ref[0, r0:r0 + GMLP_CHUNK, p * LANES:(p + 1) * LANES] = mixed.astype(BF16)

    q = proj('a_q') * (HEAD_DIM ** -0.5 * LOG2E)
    qp_ref[0] = q.astype(BF16)
    qr_ref[0] = rope_all(q).astype(BF16)
    kr_ref[0] = rope_all(proj('a_k')).astype(BF16)
    v = proj('a_v')
    for ch in range(tm // NA_VCHUNK):
        v_ref[0, ch] = v[ch * NA_VCHUNK:(ch + 1) * NA_VCHUNK].T.astype(BF16)
    bx_ref[0] = proj('b_x').astype(BF16)
    dq_ref[0] = proj('d_q').astype(BF16)
    di_ref[0] = proj('d_i').astype(BF16)


def _proj_call(x, sh, sc, gpre, w_proj, cos, sin_signed, gn, ws, bs_tab, lb, tm, per_batch_mod):
    b, n, _ = x.shape
    nt = n // tm
    mod_map = (lambda i, j: (i, 0, 0)) if per_batch_mod else (lambda i, j: (0, 0, 0))
    row_spec = pl.BlockSpec((1, tm, BRANCH_W), lambda i, j: (i, j, 0))
    vt_spec = pl.BlockSpec((1, tm // NA_VCHUNK, BRANCH_W, NA_VCHUNK), lambda i, j: (i, j, 0, 0))
    bf = jax.ShapeDtypeStruct((b, n, BRANCH_W), BF16)
    vt = jax.ShapeDtypeStruct((b, n // NA_VCHUNK, BRANCH_W, NA_VCHUNK), BF16)
    f32 = jax.ShapeDtypeStruct((b, n, BRANCH_W), F32)
    return pl.pallas_call(
        _proj_kernel,
        out_shape=(bf, bf, bf, vt, bf, bf, bf, bf, f32, bf, f32, bf),
        grid=(b, nt),
        in_specs=[pl.BlockSpec((1, tm, D_MODEL), lambda i, j: (i, j, 0)),
                  pl.BlockSpec((1, 1, D_MODEL), mod_map),
                  pl.BlockSpec((1, 1, D_MODEL), mod_map),
                  _resident((1, D_MODEL)),
                  _resident((D_MODEL, len(_PROJ_NAMES) * BRANCH_W)),
                  pl.BlockSpec((tm, LANES), lambda i, j: (j, 0)),
                  pl.BlockSpec((tm, LANES), lambda i, j: (j, 0)),
                  _resident((1, BRANCH_W)),
                  _resident((GMLP_GROUPS, GMLP_CHUNK, GMLP_CHUNK)),
                  _resident((GMLP_CHUNK, BRANCH_W)),
                  _resident((1, 2, BRANCH_W))],
        out_specs=(row_spec,) * 3 + (vt_spec,) + (row_spec,) * 8,
        compiler_params=_cparams(("arbitrary", "arbitrary")),
        name="branch_proj",
    )(x, sh, sc, gpre, w_proj, cos, sin_signed, gn, ws, bs_tab, lb)


def _head_mask(hh):
    lane = _lane_iota((1, LANES))
    return (lane < HEAD_DIM) if hh == 0 else (lane >= HEAD_DIM)


def _nattn_kernel(qr_ref, qp_ref, k_ref, vt_ref, kc_ref, vct_ref, bias_ref, o_ref):
    rows = k_ref.shape[1] // GRID_W
    r0 = pl.program_id(1) * NA_TILE_ROWS
    kb0 = jnp.clip(r0 - NA_WIN_H // 2, 0, rows - NA_BAND_ROWS)
    start = pl.multiple_of(kb0 * GRID_W, NA_VCHUNK)
    c0 = kb0 // (NA_VCHUNK // GRID_W)
    band = NA_BAND_ROWS * GRID_W
    nq = NA_TILE_ROWS * GRID_W

    def lanes(p):
        return slice(p * LANES, (p + 1) * LANES)

    def stack_heads(q):
        zero = jnp.zeros_like(q)
        return jnp.concatenate([jnp.where(_head_mask(0), q, zero), jnp.where(_head_mask(1), q, zero)], axis=0)

    def scores(p):
        s_ctx = _dot_nt(kc_ref[0, :, lanes(p)], stack_heads(qp_ref[0, :, lanes(p)]))
        s_band = _dot_nt(k_ref[0, pl.ds(start, band), lanes(p)], stack_heads(qr_ref[0, :, lanes(p)]))
        return s_ctx, s_band + bias_ref[0, p]

    def softmax(s_ctx, s_band):
        p_ctx, p_band = [], []
        for cb in range(2 * nq // LANES):
            cs = slice(cb * LANES, (cb + 1) * LANES)
            sc, sb = s_ctx[:, cs], s_band[:, cs]
            mx = jnp.maximum(jnp.max(sc, axis=0, keepdims=True), jnp.max(sb, axis=0, keepdims=True))
            p_ctx.append(jnp.exp2(sc - mx).astype(BF16))
            p_band.append(jnp.exp2(sb - mx).astype(BF16))
        return jnp.concatenate(p_ctx, axis=1), jnp.concatenate(p_band, axis=1)

    ones_rows = jnp.ones((NA_DEN_ROWS, NA_VCHUNK), BF16)

    def values(p, p_ctx, p_band):
        outs = []
        for hh in range(2):
            qs = slice(hh * nq, (hh + 1) * nq)
            ch = slice(p * LANES + hh * HEAD_DIM, p * LANES + (hh + 1) * HEAD_DIM)
            acc = None
            for j in range(vct_ref.shape[1]):
                lhs = jnp.concatenate([vct_ref[0, j, ch, :], ones_rows], axis=0)
                term = _dot(lhs, p_ctx[j * NA_VCHUNK:(j + 1) * NA_VCHUNK, qs])
                acc = term if acc is None else acc + term
            for j in range(band // NA_VCHUNK):
                lhs = jnp.concatenate([vt_ref[0, c0 + j, ch, :], ones_rows], axis=0)
                acc = acc + _dot(lhs, p_band[j * NA_VCHUNK:(j + 1) * NA_VCHUNK, qs])
            outs.append(acc[0:HEAD_DIM] * (1.0 / acc[HEAD_DIM:HEAD_DIM + 1]))
        o_ref[0, :, lanes(p)] = jnp.concatenate(outs, axis=0).T.astype(BF16)

    s_val, p_val = {}, {}
    for t in range(N_PAIRS + 2):
        if t < N_PAIRS:
            s_val[t] = scores(t)
        if 0 <= t - 1 < N_PAIRS:
            p_val[t - 1] = softmax(*s_val.pop(t - 1))
        if 0 <= t - 2 < N_PAIRS:
            values(t - 2, *p_val.pop(t - 2))


def _nattn_call(qr, qp, k, vt, kc, vct, bias):
    b, n, _ = qr.shape
    rows = n // GRID_W
    nt = rows // NA_TILE_ROWS
    lc = kc.shape[1]
    nq = NA_TILE_ROWS * GRID_W

    def bias_map(i, t):
        return (jnp.where(t == 0, 0, jnp.where(t == nt - 1, 2, 1)), 0, 0, 0)

    q_spec = pl.BlockSpec((1, nq, BRANCH_W), lambda i, t: (i, t, 0))
    full = pl.BlockSpec((1, n, BRANCH_W), lambda i, t: (i, 0, 0))
    full_t = pl.BlockSpec((1, n // NA_VCHUNK, BRANCH_W, NA_VCHUNK), lambda i, t: (i, 0, 0, 0))
    ctx = pl.BlockSpec((1, lc, BRANCH_W), lambda i, t: (i, 0, 0))
    ctx_t = pl.BlockSpec((1, lc // NA_VCHUNK, BRANCH_W, NA_VCHUNK), lambda i, t: (i, 0, 0, 0))
    return pl.pallas_call(
        _nattn_kernel,
        out_shape=jax.ShapeDtypeStruct((b, n, BRANCH_W), BF16),
        grid=(b, nt),
        in_specs=[q_spec, q_spec, full, full_t, ctx, ctx_t,
                  pl.BlockSpec((1, N_PAIRS, NA_BAND_ROWS * GRID_W, 2 * nq), bias_map)],
        out_specs=q_spec,
        compiler_params=_cparams(("arbitrary", "arbitrary")),
        name="neighbourhood_attention",
    )(qr, qp, k, vt, kc, vct, bias)


def _cattn_kernel(q_ref, k_ref, vt_ref, o_ref):
    lc = q_ref.shape[1]
    low_head = _lane_iota((lc, LANES)) < HEAD_DIM
    for p in range(N_PAIRS):
        ls = slice(p * LANES, (p + 1) * LANES)
        q = q_ref[0, :, ls]
        k = k_ref[0, :, ls]
        vt = jnp.concatenate([vt_ref[0, j, ls, :] for j in range(vt_ref.shape[1])], axis=1)
        outs = []
        for hh in range(2):
            s = _dot_nt(jnp.where(_head_mask(hh), q, jnp.zeros_like(q)), k)
            e = jnp.exp2(s - jnp.max(s, axis=-1, keepdims=True))
            outs.append(_dot_nt(e.astype(BF16), vt) * (1.0 / jnp.sum(e, axis=-1, keepdims=True)))
        o_ref[0, :, ls] = jnp.where(low_head, outs[0], outs[1]).astype(BF16)


def _cattn_call(q, k, vt):
    b, lc, _ = q.shape
    spec = pl.BlockSpec((1, lc, BRANCH_W), lambda i: (i, 0, 0))
    spec_t = pl.BlockSpec((1, lc // NA_VCHUNK, BRANCH_W, NA_VCHUNK), lambda i: (i, 0, 0, 0))
    return pl.pallas_call(
        _cattn_kernel,
        out_shape=jax.ShapeDtypeStruct((b, lc, BRANCH_W), BF16),
        grid=(b,),
        in_specs=[spec, spec, spec_t],
        out_specs=spec,
        compiler_params=_cparams(("arbitrary",)),
        name="context_attention",
    )(q, k, vt)


def _attention_bias(rpb, rows):
    col = jnp.arange(GRID_W)
    col_start = jnp.clip(col - NA_WIN_W // 2, 0, GRID_W - NA_WIN_W)
    valid = (col[None, :] >= col_start[:, None]) & (col[None, :] < col_start[:, None] + NA_WIN_W)
    col_idx = jnp.clip(col[None, :] - col[:, None] + NA_WIN_W - 1, 0, 2 * NA_WIN_W - 2)
    per_row = jnp.where(valid[None, None], rpb.astype(F32)[:, :, col_idx] * LOG2E, NEG_INF)
    tabs = []
    for r0 in (0, NA_TILE_ROWS, rows - NA_TILE_ROWS):
        kb0 = int(np.clip(r0 - NA_WIN_H // 2, 0, rows - NA_BAND_ROWS))
        r = r0 + np.arange(NA_TILE_ROWS)[:, None]
        kr = kb0 + np.arange(NA_BAND_ROWS)[None, :]
        rs = np.clip(r - NA_WIN_H // 2, 0, rows - NA_WIN_H)
        in_win = (kr >= rs) & (kr < rs + NA_WIN_H)
        idx = np.clip(kr - r + NA_WIN_H - 1, 0, 2 * NA_WIN_H - 2)
        t = jnp.where(in_win[None, :, :, None, None], per_row[:, idx], NEG_INF)
        t = t.reshape(N_PAIRS, 2, NA_TILE_ROWS, NA_BAND_ROWS, GRID_W, GRID_W).transpose(0, 3, 5, 1, 2, 4)
        tabs.append(t.reshape(N_PAIRS, NA_BAND_ROWS * GRID_W, 2 * NA_TILE_ROWS * GRID_W))
    return jnp.stack(tabs)


_KRON = 8


@functools.lru_cache(maxsize=None)
def _fourier_consts(n):
    rows = n // GRID_W
    k1 = np.arange(rows)[:, None, None, None]
    l1 = np.arange(_KRON)[None, :, None, None]
    n1 = np.arange(rows)[None, None, :, None]
    l2 = np.arange(_KRON)[None, None, None, :]
    a_cos, a_sin = [], []
    for j in range(GRID_W // _KRON):
        ang = 2.0 * np.pi * k1 * (GRID_W * n1 + _KRON * j + l1) / n
        same = (l1 == l2)
        a_cos.append((np.cos(ang) * same).reshape(rows * _KRON, rows * _KRON))
        a_sin.append((-np.sin(ang) * same).reshape(rows * _KRON, rows * _KRON))
    k2 = np.arange(GRID_W)[:, None, None, None]
    ang = 2.0 * np.pi * k2 * np.arange(GRID_W)[None, None, None, :] / GRID_W
    same = (np.arange(_KRON)[None, :, None, None] == np.arange(_KRON)[None, None, :, None])
    b_cos = (np.cos(ang) * same).reshape(GRID_W * _KRON, _KRON * GRID_W)
    b_sin = (np.sin(ang) * same).reshape(GRID_W * _KRON, _KRON * GRID_W)
    b_re = np.concatenate([b_cos, b_sin], axis=1)
    b_im = np.concatenate([-b_sin, b_cos], axis=1)
    return (np.stack(a_cos).astype(np.float32), np.stack(a_sin).astype(np.float32),
            b_re.astype(np.float32), b_im.astype(np.float32))


@functools.lru_cache(maxsize=None)
def _channel_dft():
    c = np.arange(FNET_GROUP_W)
    ang = 2.0 * np.pi * np.outer(c, c) / FNET_GROUP_W
    return np.concatenate([np.cos(ang), np.sin(ang)], axis=0).astype(np.float32)


@functools.lru_cache(maxsize=None)
def _dense_dft(n):
    t = np.arange(n)
    ang = 2.0 * np.pi * np.outer(t, t) / n
    return np.cos(ang).astype(np.float32), (-np.sin(ang)).astype(np.float32)


def _fold_channel_map(cs_ref, wf_ref, fold_ref, norm):
    c_hi, c_lo = _split2(cs_ref[...] * norm)
    for g in range(FNET_GROUPS):
        w_hi, w_lo = _split2(wf_ref[g])
        fold_ref[g] = (_dot(c_hi, w_hi) + _dot(c_hi, w_lo) + _dot(c_lo, w_hi)).astype(BF16)


def _channel_stage(xr, xi, fold_ref):
    outs = []
    for g in range(FNET_GROUPS):
        ls = slice(g * FNET_GROUP_W, (g + 1) * FNET_GROUP_W)
        xg = jnp.concatenate([xr[:, ls], xi[:, ls]], axis=1).astype(BF16)
        outs.append(_dot(xg, fold_ref[g]))
    return jnp.concatenate(outs, axis=1)


def _fourier_kernel(x_ref, ac_ref, as_ref, bre_ref, bim_ref, cs_ref, wf_ref, o_ref, s_ref, fold_ref, *, norm):
    @pl.when(pl.program_id(0) == 0)
    def _():
        _fold_channel_map(cs_ref, wf_ref, fold_ref, norm)

    rows = x_ref.shape[1]
    blk = rows * _KRON
    pair = 2 * _KRON
    for jj in range(GRID_W // pair):
        xt = x_ref[0, :, jj * pair:(jj + 1) * pair, :].astype(F32)
        re, im = [], []
        for half in range(2):
            xc = xt[:, half * _KRON:(half + 1) * _KRON, :].reshape(blk, BRANCH_W).astype(BF16)
            re.append(_dot(ac_ref[2 * jj + half], xc).reshape(rows, _KRON, BRANCH_W))
            im.append(_dot(as_ref[2 * jj + half], xc).reshape(rows, _KRON, BRANCH_W))
        s_ref[0, :, jj * pair:(jj + 1) * pair, :] = jnp.concatenate(re, axis=1).astype(BF16)
        s_ref[1, :, jj * pair:(jj + 1) * pair, :] = jnp.concatenate(im, axis=1).astype(BF16)
    sblk = _KRON * GRID_W
    for mm in range(rows // pair):
        ys = []
        for half in range(2):
            m0 = (2 * mm + half) * _KRON
            rhs = jnp.concatenate([s_ref[0, m0:m0 + _KRON].reshape(sblk, BRANCH_W),
                                   s_ref[1, m0:m0 + _KRON].reshape(sblk, BRANCH_W)], axis=0)
            xr = _dot(bre_ref[...], rhs)
            xi = _dot(bim_ref[...], rhs)
            ys.append(_channel_stage(xr, xi, fold_ref).reshape(GRID_W, _KRON, BRANCH_W))
        o_ref[0, :, mm * pair:(mm + 1) * pair, :] = jnp.concatenate(ys, axis=1).astype(BF16)


def _fourier_call(bx, wf):
    b, n, _ = bx.shape
    rows = n // GRID_W
    a_cos, a_sin, b_re, b_im = (jnp.asarray(t, BF16) for t in _fourier_consts(n))
    cs = jnp.asarray(_channel_dft(), F32)
    norm = float(1.0 / np.sqrt(n * FNET_GROUP_W))
    x4 = bx.reshape(b, rows, GRID_W, BRANCH_W)
    out = pl.pallas_call(
        functools.partial(_fourier_kernel, norm=norm),
        out_shape=jax.ShapeDtypeStruct((b, GRID_W, rows, BRANCH_W), BF16),
        grid=(b,),
        in_specs=[pl.BlockSpec((1, rows, GRID_W, BRANCH_W), lambda i: (i, 0, 0, 0)),
                  _resident(a_cos.shape), _resident(a_sin.shape),
                  _resident(b_re.shape), _resident(b_im.shape),
                  _resident(cs.shape), _resident(wf.shape)],
        out_specs=pl.BlockSpec((1, GRID_W, rows, BRANCH_W), lambda i: (i, 0, 0, 0)),
        scratch_shapes=[pltpu.VMEM((2, rows, GRID_W, BRANCH_W), BF16),
                        pltpu.VMEM((FNET_GROUPS, 2 * FNET_GROUP_W, FNET_GROUP_W), BF16)],
        compiler_params=_cparams(("arbitrary",)),
        name="fourier_mix",
    )(x4, a_cos, a_sin, b_re, b_im, cs, wf)
    return out.reshape(b, n, BRANCH_W)


def _fourier_ctx_kernel(x_ref, c_ref, s_ref, cs_ref, wf_ref, o_ref, fold_ref, *, norm):
    @pl.when(pl.program_id(0) == 0)
    def _():
        _fold_channel_map(cs_ref, wf_ref, fold_ref, norm)

    x = x_ref[0]
    xr = _dot(c_ref[...], x)
    xi = _dot(s_ref[...], x)
    o_ref[0] = _channel_stage(xr, xi, fold_ref).astype(BF16)


def _fourier_ctx_call(bx, wf):
    b, n, _ = bx.shape
    cn, sn = (jnp.asarray(t, BF16) for t in _dense_dft(n))
    cs = jnp.asarray(_channel_dft(), F32)
    norm = float(1.0 / np.sqrt(n * FNET_GROUP_W))
    spec = pl.BlockSpec((1, n, BRANCH_W), lambda i: (i, 0, 0))
    return pl.pallas_call(
        functools.partial(_fourier_ctx_kernel, norm=norm),
        out_shape=jax.ShapeDtypeStruct((b, n, BRANCH_W), BF16),
        grid=(b,),
        in_specs=[spec, _resident(cn.shape), _resident(sn.shape), _resident(cs.shape), _resident(wf.shape)],
        out_specs=spec,
        scratch_shapes=[pltpu.VMEM((FNET_GROUPS, 2 * FNET_GROUP_W, FNET_GROUP_W), BF16)],
        compiler_params=_cparams(("arbitrary",)),
        name="fourier_mix_context",
    )(bx, cn, sn, cs, wf)


def _block_diag(t):
    lo = _lane_iota(t.shape) < HEAD_DIM
    z = jnp.zeros_like(t)
    return jnp.concatenate([jnp.where(lo, t, z), jnp.where(lo, z, t)], axis=0)


def _hgrn_needed(j, reverse):
    nsub = HGRN_CHUNK // HGRN_SUB
    return list(range(0, j + 1)) if reverse else list(range(j, nsub))


def _hgrn_prepare(q, k, i, a, reverse):
    c = HGRN_CHUNK
    nsub = c // HGRN_SUB

    def level(r):
        return a[r:r + 1, :]

    zero_row = jnp.zeros((1, BRANCH_W), F32)
    if reverse:
        refs = [level((s + 1) * HGRN_SUB) if s + 1 < nsub else zero_row for s in range(nsub)]
        a_end = a[0:1, :]
    else:
        refs = [level(s * HGRN_SUB - 1) if s > 0 else zero_row for s in range(nsub)]
        a_end = a[c - 1:c, :]
    ref_rows = jnp.concatenate([jnp.broadcast_to(r, (HGRN_SUB, BRANCH_W)) for r in refs], axis=0)
    qf = q.astype(F32)
    kf = k.astype(F32)
    k_own = (kf * jnp.exp2(jnp.minimum(ref_rows - a, HGRN_EXP_CLAMP))).astype(BF16)

    def q_variant(j):
        parts = []
        for s in _hgrn_needed(j, reverse):
            rs = slice(s * HGRN_SUB, (s + 1) * HGRN_SUB)
            parts.append((qf[rs] * jnp.exp2(a[rs] - refs[j])).astype(BF16))
        return jnp.concatenate(parts, axis=0)

    q_var = [q_variant(j) for j in range(nsub)]
    return dict(
        q_stack=jnp.concatenate(q_var, axis=0),
        q_in=q_var[nsub - 1] if reverse else q_var[0],
        k_own=k_own,
        k_out=(kf * jnp.exp2(a_end - a)).astype(BF16),
        decay_end=jnp.exp2(a_end),
        i=i, reverse=reverse)


def _hgrn_scores(ops):
    c = HGRN_CHUNK
    nsub = c // HGRN_SUB
    reverse = ops['reverse']
    src = _lane_iota((c, LANES)) % HEAD_DIM
    step = lax.broadcasted_iota(jnp.int32, (c, LANES), 0)
    seen = (src >= step) if reverse else (src <= step)
    src_sub = (_lane_iota((HGRN_SUB, LANES)) % HEAD_DIM) // HGRN_SUB
    where_blk, off = {}, 0
    for j in range(nsub):
        for s in _hgrn_needed(j, reverse):
            where_blk[(j, s)] = off
            off += HGRN_SUB
    out = []
    for p in range(N_PAIRS):
        ls = slice(p * LANES, (p + 1) * LANES)
        res = _dot_nt(ops['q_stack'][:, ls], _block_diag(ops['k_own'][:, ls]))
        rows = []
        for s in range(nsub):
            blk = None
            for j in range(nsub):
                if (j, s) in where_blk:
                    piece = res[where_blk[(j, s)]:where_blk[(j, s)] + HGRN_SUB]
                    blk = piece if blk is None else jnp.where(src_sub == j, piece, blk)
            rows.append(blk)
        out.append(jnp.where(seen, jnp.concatenate(rows, axis=0), 0.0).astype(BF16))
    return out


def _hgrn_local(ops, scores):
    low_rows = lax.broadcasted_iota(jnp.int32, (LANES, LANES), 0) < HEAD_DIM
    same_head = low_rows == (_lane_iota((LANES, LANES)) < HEAD_DIM)
    o_intra, upd = [], []
    for p in range(N_PAIRS):
        ls = slice(p * LANES, (p + 1) * LANES)
        ip = ops['i'][:, ls]
        o_intra.append(_dot(scores[p], _block_diag(ip)))
        upd.append(jnp.where(same_head, _dot_tn(ip, ops['k_out'][:, ls]), 0.0))
    return o_intra, upd


def _hgrn_carry(ops, o_intra, upd, state_ref, d):
    outs = []
    for p in range(N_PAIRS):
        ls = slice(p * LANES, (p + 1) * LANES)
        st = state_ref[d, p]
        outs.append(o_intra[p] + _dot_nt(ops['q_in'][:, ls], st.astype(BF16)))
        state_ref[d, p] = ops['decay_end'][:, ls] * st + upd[p]
    return jnp.concatenate(outs, axis=1)


def _hgrn_kernel(qf_ref, if_ref, lff_ref, kf_ref, qb_ref, ib_ref, lfb_ref, kb_ref, s0f_ref, s0b_ref,
                 of_ref, ob_ref, sf_ref, sb_ref, state_ref):
    j = pl.program_id(1)
    nchunk = qf_ref.shape[1] // HGRN_CHUNK

    @pl.when(j == 0)
    def _():
        state_ref[0] = s0f_ref[0]
        state_ref[1] = s0b_ref[0]

    tm = qf_ref.shape[1]
    row = lax.broadcasted_iota(jnp.int32, (tm, tm), 0)
    col = lax.broadcasted_iota(jnp.int32, (tm, tm), 1)
    same_chunk = (row // HGRN_CHUNK) == (col // HGRN_CHUNK)

    def cum(lf, reverse):
        tri = (same_chunk & ((col >= row) if reverse else (col <= row))).astype(BF16)
        hi, lo = _split2(lf * LOG2E)
        return _dot(tri, hi) + _dot(tri, lo)

    a_f = cum(lff_ref[0], False)
    a_b = cum(lfb_ref[0], True)

    todo = []
    for cix in range(nchunk):
        bix = nchunk - 1 - cix
        todo.append((0, of_ref, slice(cix * HGRN_CHUNK, (cix + 1) * HGRN_CHUNK), qf_ref, kf_ref, if_ref, a_f, False))
        todo.append((1, ob_ref, slice(bix * HGRN_CHUNK, (bix + 1) * HGRN_CHUNK), qb_ref, kb_ref, ib_ref, a_b, True))
    ops, scores, local = {}, {}, {}
    for t in range(len(todo) + 3):
        if t < len(todo):
            _, _, rs, q_ref, k_ref, i_ref, a, reverse = todo[t]
            ops[t] = _hgrn_prepare(q_ref[0, rs, :], k_ref[0, rs, :], i_ref[0, rs, :], a[rs, :], reverse)
        if 0 <= t - 1 < len(todo):
            scores[t - 1] = _hgrn_scores(ops[t - 1])
        if 0 <= t - 2 < len(todo):
            local[t - 2] = _hgrn_local(ops[t - 2], scores.pop(t - 2))
        if 0 <= t - 3 < len(todo):
            d, o_ref, rs = todo[t - 3][:3]
            o_ref[0, rs, :] = _hgrn_carry(ops.pop(t - 3), *local.pop(t - 3), state_ref, d).astype(BF16)

    @pl.when(j == pl.num_programs(1) - 1)
    def _():
        sf_ref[0] = state_ref[0]
        sb_ref[0] = state_ref[1]


def _hgrn_call(q, i, lff, kf, lfb, kb, s0f, s0b, tm):
    b, n, _ = q.shape
    nt = n // tm
    fwd = pl.BlockSpec((1, tm, BRANCH_W), lambda bi, j: (bi, j, 0))
    bwd = pl.BlockSpec((1, tm, BRANCH_W), lambda bi, j: (bi, nt - 1 - j, 0))
    st = pl.BlockSpec((1, N_PAIRS, LANES, LANES), lambda bi, j: (bi, 0, 0, 0))
    o_shape = jax.ShapeDtypeStruct((b, n, BRANCH_W), BF16)
    s_shape = jax.ShapeDtypeStruct((b, N_PAIRS, LANES, LANES), F32)
    return pl.pallas_call(
        _hgrn_kernel,
        out_shape=(o_shape, o_shape, s_shape, s_shape),
        grid=(b, nt),
        in_specs=[fwd, fwd, fwd, fwd, bwd, bwd, bwd, bwd, st, st],
        out_specs=(fwd, bwd, st, st),
        scratch_shapes=[pltpu.VMEM((2, N_PAIRS, LANES, LANES), F32)],
        compiler_params=_cparams(("arbitrary", "arbitrary")),
        name="hgrn_scan",
    )(q, i, lff, kf, q, i, lfb, kb, s0f, s0b)


def _merge_kernel(x_ref, sh_ref, sc_ref, gt_ref, gpre_ref, gpost_ref, w_ref, oa_ref, ob_ref, mx_ref,
                  of_ref, obk_ref, hn_ref, hm_ref, wb_ref, wo_ref, o_ref):
    x = x_ref[0]
    hb = _normed_input(x, gpre_ref[...], sc_ref[0], sh_ref[0]).astype(BF16)

    def proj(name):
        j = _MERGE_NAMES.index(name)
        return _dot(hb, w_ref[:, j * BRANCH_W:(j + 1) * BRANCH_W])

    ya = oa_ref[0].astype(F32) * _silu(proj('a_g'))
    yb = ob_ref[0].astype(F32) * _silu(proj('b_g'))
    yc = proj('c_u') * mx_ref[0].astype(F32) * _silu(proj('c_g'))
    o = of_ref[0].astype(F32) + obk_ref[0].astype(F32)
    ms = _dot((o * o).astype(BF16), hm_ref[...])
    yd = o * lax.rsqrt(ms + EPS) * hn_ref[...] * _silu(proj('d_g'))

    g0 = len(_MERGE_NAMES) * BRANCH_W
    merged = None
    for r, y in enumerate((ya, yb, yc, yd)):
        gate = _dot(hb, w_ref[:, g0 + r * D_MODEL:g0 + (r + 1) * D_MODEL])
        term = jax.nn.sigmoid(gate) * _dot(y.astype(BF16), wb_ref[r])
        merged = term if merged is None else merged + term
    out = _dot(merged.astype(BF16), wo_ref[...])
    post = out * lax.rsqrt(jnp.mean(out * out, axis=-1, keepdims=True) + EPS) * gpost_ref[...]
    o_ref[0] = x + gt_ref[0] * post


def _merge_call(x, sh, sc, gt, gpre, gpost, w_merge, oa, ob, mx, of, obk, hn, hmean, wb, wo, tm,
                per_batch_mod):
    b, n, _ = x.shape
    nt = n // tm
    mod_map = (lambda i, j: (i, 0, 0)) if per_batch_mod else (lambda i, j: (0, 0, 0))
    x_spec = pl.BlockSpec((1, tm, D_MODEL), lambda i, j: (i, j, 0))
    br_spec = pl.BlockSpec((1, tm, BRANCH_W), lambda i, j: (i, j, 0))
    mod_spec = pl.BlockSpec((1, 1, D_MODEL), mod_map)
    return pl.pallas_call(
        _merge_kernel,
        out_shape=jax.ShapeDtypeStruct((b, n, D_MODEL), F32),
        grid=(b, nt),
        in_specs=[x_spec, mod_spec, mod_spec, mod_spec,
                  _resident((1, D_MODEL)), _resident((1, D_MODEL)),
                  _resident(w_merge.shape),
                  br_spec, br_spec, br_spec, br_spec, br_spec,
                  _resident((1, BRANCH_W)), _resident((BRANCH_W, BRANCH_W)),
                  _resident(wb.shape), _resident(wo.shape)],
        out_specs=x_spec,
        compiler_params=_cparams(("arbitrary", "arbitrary")),
        name="branch_merge",
    )(x, sh, sc, gt, gpre, gpost, w_merge, oa, ob, mx, of, obk, hn, hmean, wb, wo)


def _rope_tables(n_tok, rotate):
    if not rotate:
        return jnp.ones((n_tok, LANES), F32), jnp.zeros((n_tok, LANES), F32)
    t = jnp.arange(n_tok, dtype=jnp.int32)
    pos = jnp.stack([t // GRID_W, t % GRID_W], axis=-1).astype(F32)
    inv = ROPE_THETA ** (-jnp.arange(ROPE_FREQS, dtype=F32) * 2.0 / (2 * ROPE_FREQS))
    ang = pos[:, :, None] * inv
    cos = jnp.repeat(jnp.cos(ang)[:, :, None, :], 2, axis=2).reshape(n_tok, HEAD_DIM)
    sin = jnp.sin(ang)
    sin_signed = jnp.stack([-sin, sin], axis=2).reshape(n_tok, HEAD_DIM)
    return jnp.tile(cos, (1, 2)), jnp.tile(sin_signed, (1, 2))


def _gather_cols(w_in_l, names):
    return jnp.concatenate([w_in_l[:, _IN_COL[nm] * BRANCH_W:(_IN_COL[nm] + 1) * BRANCH_W] for nm in names],
                           axis=1)


def _row_tile(n):
    return 512 if n % 512 == 0 else 256


def kernel(x, c, ctx, c_ctx, w_ada, b_ada, g_pre, g_post, w_in, na_rpb, fnet_w, gmlp_norm_g, gmlp_ws,
           gmlp_bs, hgrn_lb_logits, hgrn_norm_g, w_branch, w_out):
    batch, n_tok, _ = x.shape
    n_ctx = ctx.shape[1]
    depth = w_in.shape[0]

    w_in_b = w_in.astype(BF16)
    w_proj = [_gather_cols(w_in_b[l], _PROJ_NAMES) for l in range(depth)]
    w_merge = [jnp.concatenate([_gather_cols(w_in_b[l], _MERGE_NAMES), w_in_b[l][:, _GATE_COL0:]], axis=1)
               for l in range(depth)]
    w_branch_b = w_branch.astype(BF16)
    w_out_b = w_out.astype(BF16)
    gmlp_ws_b = gmlp_ws.astype(BF16)
    bs_tab = jnp.repeat(jnp.swapaxes(gmlp_bs, 1, 2), BRANCH_W // GMLP_GROUPS, axis=2)
    head_mean = jnp.asarray(np.kron(np.eye(N_HEADS), np.ones((HEAD_DIM, HEAD_DIM)) / HEAD_DIM), BF16)
    cos_x, sin_x = _rope_tables(n_tok, True)
    cos_c, sin_c = _rope_tables(n_ctx, False)

    c_all = jnp.concatenate([c, jnp.broadcast_to(c_ctx[None, :], (8, D_MODEL))], axis=0)
    mod = _modulation(c_all, w_ada, b_ada)
    lower = _lower_bounds(hgrn_lb_logits)

    zero_state = jnp.zeros((batch, N_PAIRS, LANES, LANES), F32)
    tm_x = _row_tile(n_tok)
    tm_c = _row_tile(n_ctx)

    for l in range(depth):
        with_ctx = l < depth - 1
        mod_x = [mod[l, :batch, i * D_MODEL:(i + 1) * D_MODEL].reshape(batch, 1, D_MODEL) for i in range(3)]
        mod_c = [mod[l, batch:batch + 1, i * D_MODEL:(i + 1) * D_MODEL].reshape(1, 1, D_MODEL) for i in range(3)]
        gpre = g_pre[l].reshape(1, D_MODEL)
        gpost = g_post[l].reshape(1, D_MODEL)
        gn = gmlp_norm_g[l].reshape(1, BRANCH_W)
        hn = hgrn_norm_g[l].reshape(1, BRANCH_W)
        lb = lower[l].reshape(1, 2, BRANCH_W)
        bias = _attention_bias(na_rpb[l], n_tok // GRID_W)

        (_, qp_c, k_c, v_c, bx_c, mx_c, dq_c, di_c, lff_c, kf_c, lfb_c, kb_c) = _proj_call(
            ctx, mod_c[0], mod_c[1], gpre, w_proj[l], cos_c, sin_c, gn, gmlp_ws_b[l], bs_tab[l], lb,
            tm_c, False)
        of_c, ob_c, st_f, st_b = _hgrn_call(dq_c, di_c, lff_c, kf_c, lfb_c, kb_c, zero_state, zero_state,
                                            HGRN_ROWS)

        (qr, qp, k, v, bx, mx, dq, di, lff, kf, lfb, kb) = _proj_call(
            x, mod_x[0], mod_x[1], gpre, w_proj[l], cos_x, sin_x, gn, gmlp_ws_b[l], bs_tab[l], lb,
            tm_x, True)
        oa = _nattn_call(qr, qp, k, v, k_c, v_c, bias)
        ob = _fourier_call(bx, fnet_w[l])
        of, obk, _, _ = _hgrn_call(dq, di, lff, kf, lfb, kb, st_f, st_b, HGRN_ROWS)
        x = _merge_call(x, mod_x[0], mod_x[1], mod_x[2], gpre, gpost, w_merge[l], oa, ob, mx, of, obk,
                        hn, head_mean, w_branch_b[l], w_out_b[l], tm_x, True)

        if with_ctx:
            oa_c = _cattn_call(qp_c, k_c, v_c)
            ob_c2 = _fourier_ctx_call(bx_c, fnet_w[l])
            ctx = _merge_call(ctx, mod_c[0], mod_c[1], mod_c[2], gpre, gpost, w_merge[l], oa_c, ob_c2, mx_c,
                              of_c, ob_c, hn, head_mean, w_branch_b[l], w_out_b[l], tm_c, False)
    return x
```

```python
import functools

import numpy as np
import jax
import jax.numpy as jnp
from jax import lax
from jax.experimental import pallas as pl
from jax.experimental.pallas import tpu as pltpu

F32 = jnp.float32
BF16 = jnp.bfloat16

D_MODEL = 1024
BRANCH_W = 512
N_BRANCH = 4
GRID_W = 64
HEAD_DIM = 64
N_HEADS = 8
LANES = 128
N_PAIRS = BRANCH_W // LANES
NA_WIN_H = 8
NA_WIN_W = 16
NA_TILE_ROWS = 4
NA_BAND_ROWS = 12
NA_VCHUNK = 256
NA_DEN_ROWS = 16
LOG2E = 1.4426950408889634
ROPE_THETA = 10000.0
ROPE_FREQS = 16
FNET_GROUPS = 4
FNET_GROUP_W = 128
GMLP_CHUNK = 128
GMLP_GROUPS = 8
HGRN_CHUNK = 64
HGRN_SUB = 16
HGRN_EXP_CLAMP = 115.0
HGRN_ROWS = 256
EPS = 1e-6
F_FLOOR = 1e-30
NEG_INF = -1e30

VMEM_LIMIT = 56 * 2**20

_IN_COL = {'a_q': 0, 'a_k': 1, 'a_v': 2, 'a_g': 3, 'b_x': 4, 'b_g': 5, 'c_u': 6, 'c_v': 7, 'c_g': 8,
           'd_q': 9, 'd_f_fwd': 10, 'd_f_bwd': 11, 'd_i': 12, 'd_g': 13}
_PROJ_NAMES = ('a_q', 'a_k', 'a_v', 'b_x', 'c_v', 'd_q', 'd_f_fwd', 'd_f_bwd', 'd_i')
_MERGE_NAMES = ('c_u', 'a_g', 'b_g', 'c_g', 'd_g')
_GATE_COL0 = 14 * BRANCH_W


def _cparams(sem):
    return pltpu.CompilerParams(dimension_semantics=sem, vmem_limit_bytes=VMEM_LIMIT)


def _resident(shape):
    nd = len(shape)
    return pl.BlockSpec(shape, lambda *_: (0,) * nd, pipeline_mode=pl.Buffered(1))


def _silu(t):
    return t * jax.nn.sigmoid(t)


def _lane_iota(shape):
    return lax.broadcasted_iota(jnp.int32, shape, len(shape) - 1)


def _dot(a, b):
    return jnp.dot(a, b, preferred_element_type=F32)


def _dot_nt(a, b):
    return lax.dot_general(a, b, (((1,), (1,)), ((), ())), preferred_element_type=F32)


def _dot_tn(a, b):
    return lax.dot_general(a, b, (((0,), (0,)), ((), ())), preferred_element_type=F32)


def _split2(t):
    hi = t.astype(BF16)
    return hi, (t - hi.astype(F32)).astype(BF16)


def _normed_input(x, gpre, sc, sh):
    ms = jnp.mean(x * x, axis=-1, keepdims=True)
    h = x * lax.rsqrt(ms + EPS) * gpre
    return h * (1.0 + sc) + sh


def _mod_kernel(c_ref, w_ref, b_ref, o_ref):
    s = _silu(c_ref[...]).astype(BF16)
    o_ref[0] = _dot(s, w_ref[0].astype(BF16)) + b_ref[0]


def _modulation(c_all, w_ada, b_ada):
    depth = w_ada.shape[0]
    rows = c_all.shape[0]
    tn = 1024
    return pl.pallas_call(
        _mod_kernel,
        out_shape=jax.ShapeDtypeStruct((depth, rows, 3 * D_MODEL), F32),
        grid=(depth, 3 * D_MODEL // tn),
        in_specs=[pl.BlockSpec((rows, D_MODEL), lambda l, j: (0, 0)),
                  pl.BlockSpec((1, D_MODEL, tn), lambda l, j: (l, 0, j)),
                  pl.BlockSpec((1, 1, tn), lambda l, j: (l, 0, j))],
        out_specs=pl.BlockSpec((1, rows, tn), lambda l, j: (l, 0, j)),
        compiler_params=_cparams(("arbitrary", "arbitrary")),
        name="adaln_modulation",
    )(c_all, w_ada, b_ada.reshape(depth, 1, 3 * D_MODEL))


def _lb_kernel(lg_ref, o_ref):
    depth = lg_ref.shape[0]
    lg = [lg_ref[l] for l in range(depth)]
    m = functools.reduce(jnp.maximum, lg)
    e = [jnp.exp(t - m) for t in lg]
    tot = functools.reduce(lambda a, b: a + b, e)
    sm = [t / tot for t in e]
    run = jnp.zeros_like(sm[0])
    for l in range(depth):
        run = run + sm[l]
        o_ref[l] = jnp.maximum(run - sm[0], 0.0)


def _lower_bounds(lb_logits):
    return pl.pallas_call(
        _lb_kernel,
        out_shape=jax.ShapeDtypeStruct(lb_logits.shape, F32),
        name="hgrn_lower_bounds",
    )(lb_logits)


def _rope(t, cos, sin_signed, first_half):
    up = pltpu.roll(t, LANES - ROPE_FREQS, 1)
    down = pltpu.roll(t, ROPE_FREQS, 1)
    return t * cos + jnp.where(first_half, up, down) * sin_signed


def _proj_kernel(*refs, scan):
    (x_ref, sh_ref, sc_ref, gpre_ref, w_ref, cos_ref, sin_ref, gn_ref, ws_ref, bs_ref, lb_ref) = refs[:11]
    if scan:
        (tri_ref, s0_ref, qr_ref, qp_ref, kr_ref, v_ref, bx_ref, mx_ref, dq_ref, di_ref, lff_ref, kf_ref,
         ob_ref, state_ref) = refs[11:]
    else:
        (qr_ref, qp_ref, kr_ref, v_ref, bx_ref, mx_ref, dq_ref, di_ref, lff_ref, kf_ref, lfb_ref,
         kb_ref) = refs[11:]
    tm = x_ref.shape[1]
    hb = _normed_input(x_ref[0], gpre_ref[...], sc_ref[0], sh_ref[0]).astype(BF16)

    half_w = BRANCH_W // 2

    def proj(name, half=None):
        c0 = _PROJ_NAMES.index(name) * BRANCH_W
        if half is None:
            return _dot(hb, w_ref[:, c0:c0 + BRANCH_W])
        return _dot(hb, w_ref[:, c0 + half * half_w:c0 + (half + 1) * half_w])

    cos = cos_ref[...]
    sin_signed = sin_ref[...]
    first_half = (_lane_iota((tm, LANES)) % (2 * ROPE_FREQS)) < ROPE_FREQS

    def rope_half(t):
        return jnp.concatenate(
            [_rope(t[:, p * LANES:(p + 1) * LANES], cos, sin_signed, first_half) for p in range(half_w // LANES)],
            axis=1)

    gates = []
    for name, d in (('d_f_fwd', 0), ('d_f_bwd', 1)):
        z = proj(name)
        lb = lb_ref[0, d:d + 1, :]
        sg = jax.nn.sigmoid(z)
        gates.append((jnp.log(jnp.maximum(lb + (1.0 - lb) * sg, F_FLOOR)),
                      ((1.0 - lb) * (1.0 - sg)).astype(BF16)))
    lff_ref[0], kf_ref[0] = gates[0]
    dq = proj('d_q').astype(BF16)
    di = proj('d_i').astype(BF16)
    dq_ref[0] = dq
    di_ref[0] = di
    scan_out = []
    if scan:
        @pl.when(pl.program_id(1) == 0)
        def _():
            state_ref[...] = s0_ref[...]

        steps = _scan_steps(dq, gates[1][1], di, gates[1][0], tri_ref[...], state_ref, True, scan_out)
    else:
        lfb_ref[0], kb_ref[0] = gates[1]
        steps = iter(())

    def tick():
        next(steps, None)

    cv0 = proj('c_v', 0)
    tick()
    cv = jnp.concatenate([cv0, proj('c_v', 1)], axis=1)
    tick()
    vn = (cv * lax.rsqrt(jnp.mean(cv * cv, axis=-1, keepdims=True) + EPS) * gn_ref[...]).astype(BF16)
    low_group = _lane_iota((GMLP_CHUNK, LANES)) < (LANES // 2)
    for ch in range(tm // GMLP_CHUNK):
        r0 = ch * GMLP_CHUNK
        for p in range(N_PAIRS):
            slab = vn[r0:r0 + GMLP_CHUNK, p * LANES:(p + 1) * LANES]
            mixed = jnp.where(low_group, _dot(ws_ref[2 * p], slab), _dot(ws_ref[2 * p + 1], slab))
            mixed = mixed + bs_ref[:, p * LANES:(p + 1) * LANES]
            mx_ref[0, r0:r0 + GMLP_CHUNK, p * LANES:(p + 1) * LANES] = mixed.astype(BF16)
        tick()

    for half in range(2):
        cols = slice(half * half_w, (half + 1) * half_w)
        q = proj('a_q', half) * (HEAD_DIM ** -0.5 * LOG2E)
        qp_ref[0, :, cols] = q.astype(BF16)
        qr_ref[0, :, cols] = rope_half(q).astype(BF16)
        tick()
        kr_ref[0, :, cols] = rope_half(proj('a_k', half)).astype(BF16)
        tick()
        v = proj('a_v', half)
        for ch in range(tm // NA_VCHUNK):
            v_ref[0, ch, cols, :] = v[ch * NA_VCHUNK:(ch + 1) * NA_VCHUNK].T.astype(BF16)
        tick()
        bx_ref[0, :, cols] = proj('b_x', half).astype(BF16)
        tick()
    if scan:
        for _ in steps:
            pass
        ob_ref[0] = scan_out[0].astype(BF16)


def _proj_call(x, sh, sc, gpre, w_proj, cos, sin_signed, gn, ws, bs_tab, lb, tm, per_batch_mod,
               scan_state=None):
    b, n, _ = x.shape
    nt = n // tm
    scan = scan_state is not None
    tile = (lambda j: nt - 1 - j) if scan else (lambda j: j)
    mod_map = (lambda i, j: (i, 0, 0)) if per_batch_mod else (lambda i, j: (0, 0, 0))
    row_spec = pl.BlockSpec((1, tm, BRANCH_W), lambda i, j: (i, tile(j), 0))
    vt_spec = pl.BlockSpec((1, tm // NA_VCHUNK, BRANCH_W, NA_VCHUNK), lambda i, j: (i, tile(j), 0, 0))
    bf = jax.ShapeDtypeStruct((b, n, BRANCH_W), BF16)
    vt = jax.ShapeDtypeStruct((b, n // NA_VCHUNK, BRANCH_W, NA_VCHUNK), BF16)
    f32 = jax.ShapeDtypeStruct((b, n, BRANCH_W), F32)
    in_specs = [pl.BlockSpec((1, tm, D_MODEL), lambda i, j: (i, tile(j), 0)),
                pl.BlockSpec((1, 1, D_MODEL), mod_map),
                pl.BlockSpec((1, 1, D_MODEL), mod_map),
                _resident((1, D_MODEL)),
                _resident((D_MODEL, len(_PROJ_NAMES) * BRANCH_W)),
                pl.BlockSpec((tm, LANES), lambda i, j: (tile(j), 0)),
                pl.BlockSpec((tm, LANES), lambda i, j: (tile(j), 0)),
                _resident((1, BRANCH_W)),
                _resident((GMLP_GROUPS, GMLP_CHUNK, GMLP_CHUNK)),
                _resident((GMLP_CHUNK, BRANCH_W)),
                _resident((1, 2, BRANCH_W))]
    args = [x, sh, sc, gpre, w_proj, cos, sin_signed, gn, ws, bs_tab, lb]
    if scan:
        in_specs += [_resident((tm, tm)), pl.BlockSpec((1, N_PAIRS, LANES, LANES), lambda i, j: (i, 0, 0, 0))]
        args += [jnp.asarray(_scan_tri(tm, True), BF16), scan_state]
        out_shape = (bf, bf, bf, vt, bf, bf, bf, bf, f32, bf, bf)
        out_specs = (row_spec,) * 3 + (vt_spec,) + (row_spec,) * 7
        scratch = [pltpu.VMEM((1, N_PAIRS, LANES, LANES), F32)]
    else:
        out_shape = (bf, bf, bf, vt, bf, bf, bf, bf, f32, bf, f32, bf)
        out_specs = (row_spec,) * 3 + (vt_spec,) + (row_spec,) * 8
        scratch = []
    return pl.pallas_call(
        functools.partial(_proj_kernel, scan=scan),
        out_shape=out_shape,
        grid=(b, nt),
        in_specs=in_specs,
        out_specs=out_specs,
        scratch_shapes=scratch,
        compiler_params=_cparams(("arbitrary", "arbitrary")),
        name="branch_proj",
    )(*args)


def _head_mask(hh):
    lane = _lane_iota((1, LANES))
    return (lane < HEAD_DIM) if hh == 0 else (lane >= HEAD_DIM)


def _nattn_kernel(qr_ref, qp_ref, k_ref, vt_ref, kc_ref, vct_ref, bias_ref, o_ref):
    rows = k_ref.shape[1] // GRID_W
    r0 = pl.program_id(1) * NA_TILE_ROWS
    kb0 = jnp.clip(r0 - NA_WIN_H // 2, 0, rows - NA_BAND_ROWS)
    start = pl.multiple_of(kb0 * GRID_W, NA_VCHUNK)
    c0 = kb0 // (NA_VCHUNK // GRID_W)
    band = NA_BAND_ROWS * GRID_W
    nq = NA_TILE_ROWS * GRID_W

    def lanes(p):
        return slice(p * LANES, (p + 1) * LANES)

    def stack_heads(q):
        zero = jnp.zeros_like(q)
        return jnp.concatenate([jnp.where(_head_mask(0), q, zero), jnp.where(_head_mask(1), q, zero)], axis=0)

    def scores(p):
        s_ctx = _dot_nt(kc_ref[0, :, lanes(p)], stack_heads(qp_ref[0, :, lanes(p)]))
        s_band = _dot_nt(k_ref[0, pl.ds(start, band), lanes(p)], stack_heads(qr_ref[0, :, lanes(p)]))
        return s_ctx, s_band + bias_ref[0, p]

    def softmax(s_ctx, s_band):
        p_ctx, p_band = [], []
        for cb in range(2 * nq // LANES):
            cs = slice(cb * LANES, (cb + 1) * LANES)
            sc, sb = s_ctx[:, cs], s_band[:, cs]
            mx = jnp.maximum(jnp.max(sc, axis=0, keepdims=True), jnp.max(sb, axis=0, keepdims=True))
            p_ctx.append(jnp.exp2(sc - mx).astype(BF16))
            p_band.append(jnp.exp2(sb - mx).astype(BF16))
        return jnp.concatenate(p_ctx, axis=1), jnp.concatenate(p_band, axis=1)

    ones_rows = jnp.ones((NA_DEN_ROWS, NA_VCHUNK), BF16)

    def values(p, p_ctx, p_band):
        outs = []
        for hh in range(2):
            qs = slice(hh * nq, (hh + 1) * nq)
            ch = slice(p * LANES + hh * HEAD_DIM, p * LANES + (hh + 1) * HEAD_DIM)
            acc = None
            for j in range(vct_ref.shape[1]):
                lhs = jnp.concatenate([vct_ref[0, j, ch, :], ones_rows], axis=0)
                term = _dot(lhs, p_ctx[j * NA_VCHUNK:(j + 1) * NA_VCHUNK, qs])
                acc = term if acc is None else acc + term
            for j in range(band // NA_VCHUNK):
                lhs = jnp.concatenate([vt_ref[0, c0 + j, ch, :], ones_rows], axis=0)
                acc = acc + _dot(lhs, p_band[j * NA_VCHUNK:(j + 1) * NA_VCHUNK, qs])
            outs.append(acc[0:HEAD_DIM] * (1.0 / acc[HEAD_DIM:HEAD_DIM + 1]))
        o_ref[0, :, lanes(p)] = jnp.concatenate(outs, axis=0).T.astype(BF16)

    s_val, p_val = {}, {}
    for t in range(N_PAIRS + 2):
        if t < N_PAIRS:
            s_val[t] = scores(t)
        if 0 <= t - 1 < N_PAIRS:
            p_val[t - 1] = softmax(*s_val.pop(t - 1))
        if 0 <= t - 2 < N_PAIRS:
            values(t - 2, *p_val.pop(t - 2))


def _nattn_call(qr, qp, k, vt, kc, vct, bias):
    b, n, _ = qr.shape
    rows = n // GRID_W
    nt = rows // NA_TILE_ROWS
    lc = kc.shape[1]
    nq = NA_TILE_ROWS * GRID_W

    def bias_map(i, t):
        return (jnp.where(t == 0, 0, jnp.where(t == nt - 1, 2, 1)), 0, 0, 0)

    q_spec = pl.BlockSpec((1, nq, BRANCH_W), lambda i, t: (i, t, 0))
    full = pl.BlockSpec((1, n, BRANCH_W), lambda i, t: (i, 0, 0))
    full_t = pl.BlockSpec((1, n // NA_VCHUNK, BRANCH_W, NA_VCHUNK), lambda i, t: (i, 0, 0, 0))
    ctx = pl.BlockSpec((1, lc, BRANCH_W), lambda i, t: (i, 0, 0))
    ctx_t = pl.BlockSpec((1, lc // NA_VCHUNK, BRANCH_W, NA_VCHUNK), lambda i, t: (i, 0, 0, 0))
    return pl.pallas_call(
        _nattn_kernel,
        out_shape=jax.ShapeDtypeStruct((b, n, BRANCH_W), BF16),
        grid=(b, nt),
        in_specs=[q_spec, q_spec, full, full_t, ctx, ctx_t,
                  pl.BlockSpec((1, N_PAIRS, NA_BAND_ROWS * GRID_W, 2 * nq), bias_map)],
        out_specs=q_spec,
        compiler_params=_cparams(("arbitrary", "arbitrary")),
        name="neighbourhood_attention",
    )(qr, qp, k, vt, kc, vct, bias)


def _cattn_kernel(q_ref, k_ref, vt_ref, o_ref):
    lc = q_ref.shape[1]
    low_head = _lane_iota((lc, LANES)) < HEAD_DIM
    for p in range(N_PAIRS):
        ls = slice(p * LANES, (p + 1) * LANES)
        q = q_ref[0, :, ls]
        k = k_ref[0, :, ls]
        vt = jnp.concatenate([vt_ref[0, j, ls, :] for j in range(vt_ref.shape[1])], axis=1)
        outs = []
        for hh in range(2):
            s = _dot_nt(jnp.where(_head_mask(hh), q, jnp.zeros_like(q)), k)
            e = jnp.exp2(s - jnp.max(s, axis=-1, keepdims=True))
            outs.append(_dot_nt(e.astype(BF16), vt) * (1.0 / jnp.sum(e, axis=-1, keepdims=True)))
        o_ref[0, :, ls] = jnp.where(low_head, outs[0], outs[1]).astype(BF16)


def _cattn_call(q, k, vt):
    b, lc, _ = q.shape
    spec = pl.BlockSpec((1, lc, BRANCH_W), lambda i: (i, 0, 0))
    spec_t = pl.BlockSpec((1, lc // NA_VCHUNK, BRANCH_W, NA_VCHUNK), lambda i: (i, 0, 0, 0))
    return pl.pallas_call(
        _cattn_kernel,
        out_shape=jax.ShapeDtypeStruct((b, lc, BRANCH_W), BF16),
        grid=(b,),
        in_specs=[spec, spec, spec_t],
        out_specs=spec,
        compiler_params=_cparams(("arbitrary",)),
        name="context_attention",
    )(q, k, vt)


def _attention_bias(rpb, rows):
    col = jnp.arange(GRID_W)
    col_start = jnp.clip(col - NA_WIN_W // 2, 0, GRID_W - NA_WIN_W)
    valid = (col[None, :] >= col_start[:, None]) & (col[None, :] < col_start[:, None] + NA_WIN_W)
    col_idx = jnp.clip(col[None, :] - col[:, None] + NA_WIN_W - 1, 0, 2 * NA_WIN_W - 2)
    per_row = jnp.where(valid[None, None], rpb.astype(F32)[:, :, col_idx] * LOG2E, NEG_INF)
    tabs = []
    for r0 in (0, NA_TILE_ROWS, rows - NA_TILE_ROWS):
        kb0 = int(np.clip(r0 - NA_WIN_H // 2, 0, rows - NA_BAND_ROWS))
        r = r0 + np.arange(NA_TILE_ROWS)[:, None]
        kr = kb0 + np.arange(NA_BAND_ROWS)[None, :]
        rs = np.clip(r - NA_WIN_H // 2, 0, rows - NA_WIN_H)
        in_win = (kr >= rs) & (kr < rs + NA_WIN_H)
        idx = np.clip(kr - r + NA_WIN_H - 1, 0, 2 * NA_WIN_H - 2)
        t = jnp.where(in_win[None, :, :, None, None], per_row[:, idx], NEG_INF)
        t = t.reshape(N_PAIRS, 2, NA_TILE_ROWS, NA_BAND_ROWS, GRID_W, GRID_W).transpose(0, 3, 5, 1, 2, 4)
        tabs.append(t.reshape(N_PAIRS, NA_BAND_ROWS * GRID_W, 2 * NA_TILE_ROWS * GRID_W))
    return jnp.stack(tabs)


_KRON = 8


@functools.lru_cache(maxsize=None)
def _fourier_consts(n):
    rows = n // GRID_W
    k1 = np.arange(rows)[:, None, None, None]
    l1 = np.arange(_KRON)[None, :, None, None]
    n1 = np.arange(rows)[None, None, :, None]
    l2 = np.arange(_KRON)[None, None, None, :]
    a_cos, a_sin = [], []
    for j in range(GRID_W // _KRON):
        ang = 2.0 * np.pi * k1 * (GRID_W * n1 + _KRON * j + l1) / n
        same = (l1 == l2)
        a_cos.append((np.cos(ang) * same).reshape(rows * _KRON, rows * _KRON))
        a_sin.append((-np.sin(ang) * same).reshape(rows * _KRON, rows * _KRON))
    k2 = np.arange(GRID_W)[:, None, None, None]
    ang = 2.0 * np.pi * k2 * np.arange(GRID_W)[None, None, None, :] / GRID_W
    same = (np.arange(_KRON)[None, :, None, None] == np.arange(_KRON)[None, None, :, None])
    b_cos = (np.cos(ang) * same).reshape(GRID_W * _KRON, _KRON * GRID_W)
    b_sin = (np.sin(ang) * same).reshape(GRID_W * _KRON, _KRON * GRID_W)
    b_re = np.concatenate([b_cos, b_sin], axis=1)
    b_im = np.concatenate([-b_sin, b_cos], axis=1)
    return (np.stack(a_cos).astype(np.float32), np.stack(a_sin).astype(np.float32),
            b_re.astype(np.float32), b_im.astype(np.float32))


@functools.lru_cache(maxsize=None)
def _channel_dft():
    c = np.arange(FNET_GROUP_W)
    ang = 2.0 * np.pi * np.outer(c, c) / FNET_GROUP_W
    return np.concatenate([np.cos(ang), np.sin(ang)], axis=0).astype(np.float32)


@functools.lru_cache(maxsize=None)
def _dense_dft(n):
    t = np.arange(n)
    ang = 2.0 * np.pi * np.outer(t, t) / n
    return np.cos(ang).astype(np.float32), (-np.sin(ang)).astype(np.float32)


def _fold_channel_map(cs_ref, wf_ref, fold_ref, norm):
    c_hi, c_lo = _split2(cs_ref[...] * norm)
    for g in range(FNET_GROUPS):
        w_hi, w_lo = _split2(wf_ref[g])
        fold_ref[g] = (_dot(c_hi, w_hi) + _dot(c_hi, w_lo) + _dot(c_lo, w_hi)).astype(BF16)


def _channel_stage(xr, xi, fold_ref):
    outs = []
    for g in range(FNET_GROUPS):
        ls = slice(g * FNET_GROUP_W, (g + 1) * FNET_GROUP_W)
        xg = jnp.concatenate([xr[:, ls], xi[:, ls]], axis=1).astype(BF16)
        outs.append(_dot(xg, fold_ref[g]))
    return jnp.concatenate(outs, axis=1)


def _fourier_kernel(x_ref, ac_ref, as_ref, bre_ref, bim_ref, cs_ref, wf_ref, o_ref, s_ref, fold_ref, *, norm):
    @pl.when(pl.program_id(0) == 0)
    def _():
        _fold_channel_map(cs_ref, wf_ref, fold_ref, norm)

    rows = x_ref.shape[1]
    blk = rows * _KRON
    pair = 2 * _KRON
    for jj in range(GRID_W // pair):
        xt = x_ref[0, :, jj * pair:(jj + 1) * pair, :].astype(F32)
        re, im = [], []
        for half in range(2):
            xc = xt[:, half * _KRON:(half + 1) * _KRON, :].reshape(blk, BRANCH_W).astype(BF16)
            re.append(_dot(ac_ref[2 * jj + half], xc).reshape(rows, _KRON, BRANCH_W))
            im.append(_dot(as_ref[2 * jj + half], xc).reshape(rows, _KRON, BRANCH_W))
        s_ref[0, :, jj * pair:(jj + 1) * pair, :] = jnp.concatenate(re, axis=1).astype(BF16)
        s_ref[1, :, jj * pair:(jj + 1) * pair, :] = jnp.concatenate(im, axis=1).astype(BF16)
    sblk = _KRON * GRID_W
    for mm in range(rows // pair):
        ys = []
        for half in range(2):
            m0 = (2 * mm + half) * _KRON
            rhs = jnp.concatenate([s_ref[0, m0:m0 + _KRON].reshape(sblk, BRANCH_W),
                                   s_ref[1, m0:m0 + _KRON].reshape(sblk, BRANCH_W)], axis=0)
            xr = _dot(bre_ref[...], rhs)
            xi = _dot(bim_ref[...], rhs)
            ys.append(_channel_stage(xr, xi, fold_ref).reshape(GRID_W, _KRON, BRANCH_W))
        o_ref[0, :, mm * pair:(mm + 1) * pair, :] = jnp.concatenate(ys, axis=1).astype(BF16)


def _fourier_call(bx, wf):
    b, n, _ = bx.shape
    rows = n // GRID_W
    a_cos, a_sin, b_re, b_im = (jnp.asarray(t, BF16) for t in _fourier_consts(n))
    cs = jnp.asarray(_channel_dft(), F32)
    norm = float(1.0 / np.sqrt(n * FNET_GROUP_W))
    x4 = bx.reshape(b, rows, GRID_W, BRANCH_W)
    out = pl.pallas_call(
        functools.partial(_fourier_kernel, norm=norm),
        out_shape=jax.ShapeDtypeStruct((b, GRID_W, rows, BRANCH_W), BF16),
        grid=(b,),
        in_specs=[pl.BlockSpec((1, rows, GRID_W, BRANCH_W), lambda i: (i, 0, 0, 0)),
                  _resident(a_cos.shape), _resident(a_sin.shape),
                  _resident(b_re.shape), _resident(b_im.shape),
                  _resident(cs.shape), _resident(wf.shape)],
        out_specs=pl.BlockSpec((1, GRID_W, rows, BRANCH_W), lambda i: (i, 0, 0, 0)),
        scratch_shapes=[pltpu.VMEM((2, rows, GRID_W, BRANCH_W), BF16),
                        pltpu.VMEM((FNET_GROUPS, 2 * FNET_GROUP_W, FNET_GROUP_W), BF16)],
        compiler_params=_cparams(("arbitrary",)),
        name="fourier_mix",
    )(x4, a_cos, a_sin, b_re, b_im, cs, wf)
    return out.reshape(b, n, BRANCH_W)


def _fourier_ctx_kernel(x_ref, c_ref, s_ref, cs_ref, wf_ref, o_ref, fold_ref, *, norm):
    @pl.when(pl.program_id(0) == 0)
    def _():
        _fold_channel_map(cs_ref, wf_ref, fold_ref, norm)

    x = x_ref[0]
    xr = _dot(c_ref[...], x)
    xi = _dot(s_ref[...], x)
    o_ref[0] = _channel_stage(xr, xi, fold_ref).astype(BF16)


def _fourier_ctx_call(bx, wf):
    b, n, _ = bx.shape
    cn, sn = (jnp.asarray(t, BF16) for t in _dense_dft(n))
    cs = jnp.asarray(_channel_dft(), F32)
    norm = float(1.0 / np.sqrt(n * FNET_GROUP_W))
    spec = pl.BlockSpec((1, n, BRANCH_W), lambda i: (i, 0, 0))
    return pl.pallas_call(
        functools.partial(_fourier_ctx_kernel, norm=norm),
        out_shape=jax.ShapeDtypeStruct((b, n, BRANCH_W), BF16),
        grid=(b,),
        in_specs=[spec, _resident(cn.shape), _resident(sn.shape), _resident(cs.shape), _resident(wf.shape)],
        out_specs=spec,
        scratch_shapes=[pltpu.VMEM((FNET_GROUPS, 2 * FNET_GROUP_W, FNET_GROUP_W), BF16)],
        compiler_params=_cparams(("arbitrary",)),
        name="fourier_mix_context",
    )(bx, cn, sn, cs, wf)


def _block_diag(t):
    lo = _lane_iota(t.shape) < HEAD_DIM
    z = jnp.zeros_like(t)
    return jnp.concatenate([jnp.where(lo, t, z), jnp.where(lo, z, t)], axis=0)


def _hgrn_needed(j, reverse):
    nsub = HGRN_CHUNK // HGRN_SUB
    return list(range(0, j + 1)) if reverse else list(range(j, nsub))


def _hgrn_prepare(q, k, i, a, reverse):
    c = HGRN_CHUNK
    nsub = c // HGRN_SUB

    def level(r):
        return a[r:r + 1, :]

    zero_row = jnp.zeros((1, BRANCH_W), F32)
    if reverse:
        refs = [level((s + 1) * HGRN_SUB) if s + 1 < nsub else zero_row for s in range(nsub)]
        a_end = a[0:1, :]
    else:
        refs = [level(s * HGRN_SUB - 1) if s > 0 else zero_row for s in range(nsub)]
        a_end = a[c - 1:c, :]
    ref_rows = jnp.concatenate([jnp.broadcast_to(r, (HGRN_SUB, BRANCH_W)) for r in refs], axis=0)
    qf = q.astype(F32)
    kf = k.astype(F32)
    k_own = (kf * jnp.exp2(jnp.minimum(ref_rows - a, HGRN_EXP_CLAMP))).astype(BF16)

    def q_variant(j):
        parts = []
        for s in _hgrn_needed(j, reverse):
            rs = slice(s * HGRN_SUB, (s + 1) * HGRN_SUB)
            parts.append((qf[rs] * jnp.exp2(a[rs] - refs[j])).astype(BF16))
        return jnp.concatenate(parts, axis=0)

    q_var = [q_variant(j) for j in range(nsub)]
    return dict(
        q_stack=jnp.concatenate(q_var, axis=0),
        q_in=q_var[nsub - 1] if reverse else q_var[0],
        k_own=k_own,
        k_out=(kf * jnp.exp2(a_end - a)).astype(BF16),
        decay_end=jnp.exp2(a_end),
        i=i, reverse=reverse)


def _hgrn_scores(ops):
    c = HGRN_CHUNK
    nsub = c // HGRN_SUB
    reverse = ops['reverse']
    src = _lane_iota((c, LANES)) % HEAD_DIM
    step = lax.broadcasted_iota(jnp.int32, (c, LANES), 0)
    seen = (src >= step) if reverse else (src <= step)
    src_sub = (_lane_iota((HGRN_SUB, LANES)) % HEAD_DIM) // HGRN_SUB
    where_blk, off = {}, 0
    for j in range(nsub):
        for s in _hgrn_needed(j, reverse):
            where_blk[(j, s)] = off
            off += HGRN_SUB
    out = []
    for p in range(N_PAIRS):
        ls = slice(p * LANES, (p + 1) * LANES)
        res = _dot_nt(ops['q_stack'][:, ls], _block_diag(ops['k_own'][:, ls]))
        rows = []
        for s in range(nsub):
            blk = None
            for j in range(nsub):
                if (j, s) in where_blk:
                    piece = res[where_blk[(j, s)]:where_blk[(j, s)] + HGRN_SUB]
                    blk = piece if blk is None else jnp.where(src_sub == j, piece, blk)
            rows.append(blk)
        out.append(jnp.where(seen, jnp.concatenate(rows, axis=0), 0.0).astype(BF16))
    return out


def _hgrn_local(ops, scores):
    low_rows = lax.broadcasted_iota(jnp.int32, (LANES, LANES), 0) < HEAD_DIM
    same_head = low_rows == (_lane_iota((LANES, LANES)) < HEAD_DIM)
    o_intra, upd = [], []
    for p in range(N_PAIRS):
        ls = slice(p * LANES, (p + 1) * LANES)
        ip = ops['i'][:, ls]
        o_intra.append(_dot(scores[p], _block_diag(ip)))
        upd.append(jnp.where(same_head, _dot_tn(ip, ops['k_out'][:, ls]), 0.0))
    return o_intra, upd


def _hgrn_carry(ops, o_intra, upd, state_ref, d):
    outs = []
    for p in range(N_PAIRS):
        ls = slice(p * LANES, (p + 1) * LANES)
        st = state_ref[d, p]
        outs.append(o_intra[p] + _dot_nt(ops['q_in'][:, ls], st.astype(BF16)))
        state_ref[d, p] = ops['decay_end'][:, ls] * st + upd[p]
    return jnp.concatenate(outs, axis=1)


@functools.lru_cache(maxsize=None)
def _scan_tri(tm, reverse):
    r = np.arange(tm)
    same_chunk = (r[:, None] // HGRN_CHUNK) == (r[None, :] // HGRN_CHUNK)
    upto = (r[None, :] >= r[:, None]) if reverse else (r[None, :] <= r[:, None])
    return (same_chunk & upto).astype(np.float32)


def _scan_steps(q, k, i, lf, tri, state_ref, reverse, result):
    nchunk = q.shape[0] // HGRN_CHUNK
    hi, lo = _split2(lf * LOG2E)
    a = _dot(tri, hi) + _dot(tri, lo)
    order = list(range(nchunk))[::-1] if reverse else list(range(nchunk))
    ops, scores, local, outs = {}, {}, {}, {}
    for t in range(nchunk + 3):
        if t < nchunk:
            rs = slice(order[t] * HGRN_CHUNK, (order[t] + 1) * HGRN_CHUNK)
            ops[t] = _hgrn_prepare(q[rs], k[rs], i[rs], a[rs], reverse)
        if 0 <= t - 1 < nchunk:
            scores[t - 1] = _hgrn_scores(ops[t - 1])
        if 0 <= t - 2 < nchunk:
            local[t - 2] = _hgrn_local(ops[t - 2], scores.pop(t - 2))
        if 0 <= t - 3 < nchunk:
            outs[order[t - 3]] = _hgrn_carry(ops.pop(t - 3), *local.pop(t - 3), state_ref, 0)
        if t == nchunk + 2:
            result.append(jnp.concatenate([outs[c] for c in range(nchunk)], axis=0))
        yield


def _hgrn_kernel(qf_ref, if_ref, lff_ref, kf_ref, qb_ref, ib_ref, lfb_ref, kb_ref, s0f_ref, s0b_ref,
                 of_ref, ob_ref, sf_ref, sb_ref, state_ref):
    j = pl.program_id(1)
    nchunk = qf_ref.shape[1] // HGRN_CHUNK

    @pl.when(j == 0)
    def _():
        state_ref[0] = s0f_ref[0]
        state_ref[1] = s0b_ref[0]

    tm = qf_ref.shape[1]
    row = lax.broadcasted_iota(jnp.int32, (tm, tm), 0)
    col = lax.broadcasted_iota(jnp.int32, (tm, tm), 1)
    same_chunk = (row // HGRN_CHUNK) == (col // HGRN_CHUNK)

    def cum(lf, reverse):
        tri = (same_chunk & ((col >= row) if reverse else (col <= row))).astype(BF16)
        hi, lo = _split2(lf * LOG2E)
        return _dot(tri, hi) + _dot(tri, lo)

    a_f = cum(lff_ref[0], False)
    a_b = cum(lfb_ref[0], True)

    todo = []
    for cix in range(nchunk):
        bix = nchunk - 1 - cix
        todo.append((0, of_ref, slice(cix * HGRN_CHUNK, (cix + 1) * HGRN_CHUNK), qf_ref, kf_ref, if_ref, a_f, False))
        todo.append((1, ob_ref, slice(bix * HGRN_CHUNK, (bix + 1) * HGRN_CHUNK), qb_ref, kb_ref, ib_ref, a_b, True))
    ops, scores, local = {}, {}, {}
    for t in range(len(todo) + 3):
        if t < len(todo):
            _, _, rs, q_ref, k_ref, i_ref, a, reverse = todo[t]
            ops[t] = _hgrn_prepare(q_ref[0, rs, :], k_ref[0, rs, :], i_ref[0, rs, :], a[rs, :], reverse)
        if 0 <= t - 1 < len(todo):
            scores[t - 1] = _hgrn_scores(ops[t - 1])
        if 0 <= t - 2 < len(todo):
            local[t - 2] = _hgrn_local(ops[t - 2], scores.pop(t - 2))
        if 0 <= t - 3 < len(todo):
            d, o_ref, rs = todo[t - 3][:3]
            o_ref[0, rs, :] = _hgrn_carry(ops.pop(t - 3), *local.pop(t - 3), state_ref, d).astype(BF16)

    @pl.when(j == pl.num_programs(1) - 1)
    def _():
        sf_ref[0] = state_ref[0]
        sb_ref[0] = state_ref[1]


def _hgrn_call(q, i, lff, kf, lfb, kb, s0f, s0b, tm):
    b, n, _ = q.shape
    nt = n // tm
    fwd = pl.BlockSpec((1, tm, BRANCH_W), lambda bi, j: (bi, j, 0))
    bwd = pl.BlockSpec((1, tm, BRANCH_W), lambda bi, j: (bi, nt - 1 - j, 0))
    st = pl.BlockSpec((1, N_PAIRS, LANES, LANES), lambda bi, j: (bi, 0, 0, 0))
    o_shape = jax.ShapeDtypeStruct((b, n, BRANCH_W), BF16)
    s_shape = jax.ShapeDtypeStruct((b, N_PAIRS, LANES, LANES), F32)
    return pl.pallas_call(
        _hgrn_kernel,
        out_shape=(o_shape, o_shape, s_shape, s_shape),
        grid=(b, nt),
        in_specs=[fwd, fwd, fwd, fwd, bwd, bwd, bwd, bwd, st, st],
        out_specs=(fwd, bwd, st, st),
        scratch_shapes=[pltpu.VMEM((2, N_PAIRS, LANES, LANES), F32)],
        compiler_params=_cparams(("arbitrary", "arbitrary")),
        name="hgrn_scan",
    )(q, i, lff, kf, q, i, lfb, kb, s0f, s0b)


def _merge_kernel(*refs, scan):
    (x_ref, sh_ref, sc_ref, gt_ref, gpre_ref, gpost_ref, w_ref, oa_ref, ob_ref, mx_ref) = refs[:10]
    if scan:
        (dq_ref, di_ref, lff_ref, kf_ref, tri_ref, s0_ref, obk_ref, hn_ref, hm_ref, wb_ref, wo_ref, o_ref,
         state_ref) = refs[10:]

        @pl.when(pl.program_id(1) == 0)
        def _():
            state_ref[...] = s0_ref[...]

        scan_out = []
        steps = _scan_steps(dq_ref[0], kf_ref[0], di_ref[0], lff_ref[0], tri_ref[...], state_ref, False,
                            scan_out)
    else:
        (of_ref, obk_ref, hn_ref, hm_ref, wb_ref, wo_ref, o_ref) = refs[10:]
        steps = iter(())

    def tick():
        next(steps, None)

    x = x_ref[0]
    hb = _normed_input(x, gpre_ref[...], sc_ref[0], sh_ref[0]).astype(BF16)

    def proj(name):
        j = _MERGE_NAMES.index(name)
        res = _dot(hb, w_ref[:, j * BRANCH_W:(j + 1) * BRANCH_W])
        tick()
        return res

    g0 = len(_MERGE_NAMES) * BRANCH_W

    def gated(r, y):
        gate = _dot(hb, w_ref[:, g0 + r * D_MODEL:g0 + (r + 1) * D_MODEL])
        tick()
        term = jax.nn.sigmoid(gate) * _dot(y.astype(BF16), wb_ref[r])
        tick()
        return term

    ya = oa_ref[0].astype(F32) * _silu(proj('a_g'))
    yb = ob_ref[0].astype(F32) * _silu(proj('b_g'))
    yc = proj('c_u') * mx_ref[0].astype(F32) * _silu(proj('c_g'))
    silu_dg = _silu(proj('d_g'))
    merged = gated(0, ya) + gated(1, yb) + gated(2, yc)
    if scan:
        for _ in steps:
            pass
        o_fwd = scan_out[0]
    else:
        o_fwd = of_ref[0].astype(F32)
    o = o_fwd + obk_ref[0].astype(F32)
    ms = _dot((o * o).astype(BF16), hm_ref[...])
    yd = o * lax.rsqrt(ms + EPS) * hn_ref[...] * silu_dg
    merged = merged + gated(3, yd)
    out = _dot(merged.astype(BF16), wo_ref[...])
    post = out * lax.rsqrt(jnp.mean(out * out, axis=-1, keepdims=True) + EPS) * gpost_ref[...]
    o_ref[0] = x + gt_ref[0] * post


def _merge_call(x, sh, sc, gt, gpre, gpost, w_merge, oa, ob, mx, fwd, obk, hn, hmean, wb, wo, tm,
                per_batch_mod):
    b, n, _ = x.shape
    nt = n // tm
    scan = isinstance(fwd, tuple)
    mod_map = (lambda i, j: (i, 0, 0)) if per_batch_mod else (lambda i, j: (0, 0, 0))
    x_spec = pl.BlockSpec((1, tm, D_MODEL), lambda i, j: (i, j, 0))
    br_spec = pl.BlockSpec((1, tm, BRANCH_W), lambda i, j: (i, j, 0))
    mod_spec = pl.BlockSpec((1, 1, D_MODEL), mod_map)
    in_specs = [x_spec, mod_spec, mod_spec, mod_spec, _resident((1, D_MODEL)), _resident((1, D_MODEL)),
                _resident(w_merge.shape), br_spec, br_spec, br_spec]
    args = [x, sh, sc, gt, gpre, gpost, w_merge, oa, ob, mx]
    if scan:
        dq, di, lff, kf, state0 = fwd
        in_specs += [br_spec, br_spec, br_spec, br_spec, _resident((tm, tm)),
                     pl.BlockSpec((1, N_PAIRS, LANES, LANES), lambda i, j: (i, 0, 0, 0))]
        args += [dq, di, lff, kf, jnp.asarray(_scan_tri(tm, False), BF16), state0]
        scratch = [pltpu.VMEM((1, N_PAIRS, LANES, LANES), F32)]
    else:
        in_specs += [br_spec]
        args += [fwd]
        scratch = []
    in_specs += [br_spec, _resident((1, BRANCH_W)), _resident((BRANCH_W, BRANCH_W)),
                 _resident(wb.shape), _resident(wo.shape)]
    args += [obk, hn, hmean, wb, wo]
    return pl.pallas_call(
        functools.partial(_merge_kernel, scan=scan),
        out_shape=jax.ShapeDtypeStruct((b, n, D_MODEL), F32),
        grid=(b, nt),
        in_specs=in_specs,
        out_specs=x_spec,
        scratch_shapes=scratch,
        compiler_params=_cparams(("arbitrary", "arbitrary")),
        name="branch_merge",
    )(*args)


def _rope_tables(n_tok, rotate):
    if not rotate:
        return jnp.ones((n_tok, LANES), F32), jnp.zeros((n_tok, LANES), F32)
    t = jnp.arange(n_tok, dtype=jnp.int32)
    pos = jnp.stack([t // GRID_W, t % GRID_W], axis=-1).astype(F32)
    inv = ROPE_THETA ** (-jnp.arange(ROPE_FREQS, dtype=F32) * 2.0 / (2 * ROPE_FREQS))
    ang = pos[:, :, None] * inv
    cos = jnp.repeat(jnp.cos(ang)[:, :, None, :], 2, axis=2).reshape(n_tok, HEAD_DIM)
    sin = jnp.sin(ang)
    sin_signed = jnp.stack([-sin, sin], axis=2).reshape(n_tok, HEAD_DIM)
    return jnp.tile(cos, (1, 2)), jnp.tile(sin_signed, (1, 2))


def _gather_cols(w_in_l, names):
    return jnp.concatenate([w_in_l[:, _IN_COL[nm] * BRANCH_W:(_IN_COL[nm] + 1) * BRANCH_W] for nm in names],
                           axis=1)


def _row_tile(n):
    return 512 if n % 512 == 0 else 256


def kernel(x, c, ctx, c_ctx, w_ada, b_ada, g_pre, g_post, w_in, na_rpb, fnet_w, gmlp_norm_g, gmlp_ws,
           gmlp_bs, hgrn_lb_logits, hgrn_norm_g, w_branch, w_out):
    batch, n_tok, _ = x.shape
    n_ctx = ctx.shape[1]
    depth = w_in.shape[0]

    w_in_b = w_in.astype(BF16)
    w_proj = [_gather_cols(w_in_b[l], _PROJ_NAMES) for l in range(depth)]
    w_merge = [jnp.concatenate([_gather_cols(w_in_b[l], _MERGE_NAMES), w_in_b[l][:, _GATE_COL0:]], axis=1)
               for l in range(depth)]
    w_branch_b = w_branch.astype(BF16)
    w_out_b = w_out.astype(BF16)
    gmlp_ws_b = gmlp_ws.astype(BF16)
    bs_tab = jnp.repeat(jnp.swapaxes(gmlp_bs, 1, 2), BRANCH_W // GMLP_GROUPS, axis=2)
    head_mean = jnp.asarray(np.kron(np.eye(N_HEADS), np.ones((HEAD_DIM, HEAD_DIM)) / HEAD_DIM), BF16)
    cos_x, sin_x = _rope_tables(n_tok, True)
    cos_c, sin_c = _rope_tables(n_ctx, False)

    c_all = jnp.concatenate([c, jnp.broadcast_to(c_ctx[None, :], (8, D_MODEL))], axis=0)
    mod = _modulation(c_all, w_ada, b_ada)
    lower = _lower_bounds(hgrn_lb_logits)

    zero_state = jnp.zeros((batch, N_PAIRS, LANES, LANES), F32)
    tm_x = _row_tile(n_tok)
    tm_c = _row_tile(n_ctx)

    for l in range(depth):
        with_ctx = l < depth - 1
        mod_x = [mod[l, :batch, i * D_MODEL:(i + 1) * D_MODEL].reshape(batch, 1, D_MODEL) for i in range(3)]
        mod_c = [mod[l, batch:batch + 1, i * D_MODEL:(i + 1) * D_MODEL].reshape(1, 1, D_MODEL) for i in range(3)]
        gpre = g_pre[l].reshape(1, D_MODEL)
        gpost = g_post[l].reshape(1, D_MODEL)
        gn = gmlp_norm_g[l].reshape(1, BRANCH_W)
        hn = hgrn_norm_g[l].reshape(1, BRANCH_W)
        lb = lower[l].reshape(1, 2, BRANCH_W)
        bias = _attention_bias(na_rpb[l], n_tok // GRID_W)

        (_, qp_c, k_c, v_c, bx_c, mx_c, dq_c, di_c, lff_c, kf_c, lfb_c, kb_c) = _proj_call(
            ctx, mod_c[0], mod_c[1], gpre, w_proj[l], cos_c, sin_c, gn, gmlp_ws_b[l], bs_tab[l], lb,
            tm_c, False)
        of_c, ob_c, st_f, st_b = _hgrn_call(dq_c, di_c, lff_c, kf_c, lfb_c, kb_c, zero_state, zero_state,
                                            HGRN_ROWS)

        (qr, qp, k, v, bx, mx, dq, di, lff, kf, obk) = _proj_call(
            x, mod_x[0], mod_x[1], gpre, w_proj[l], cos_x, sin_x, gn, gmlp_ws_b[l], bs_tab[l], lb,
            tm_x, True, scan_state=st_b)
        oa = _nattn_call(qr, qp, k, v, k_c, v_c, bias)
        ob = _fourier_call(bx, fnet_w[l])
        x = _merge_call(x, mod_x[0], mod_x[1], mod_x[2], gpre, gpost, w_merge[l], oa, ob, mx,
                        (dq, di, lff, kf, st_f), obk, hn, head_mean, w_branch_b[l], w_out_b[l], tm_x, True)

        if with_ctx:
            oa_c = _cattn_call(qp_c, k_c, v_c)
            ob_c2 = _fourier_ctx_call(bx_c, fnet_w[l])
            ctx = _merge_call(ctx, mod_c[0], mod_c[1], mod_c[2], gpre, gpost, w_merge[l], oa_c, ob_c2, mx_c,
                              of_c, ob_c, hn, head_mean, w_branch_b[l], w_out_b[l], tm_c, False)
    return x
```

```python
import functools

import numpy as np
import jax
import jax.numpy as jnp
from jax import lax
from jax.experimental import pallas as pl
from jax.experimental.pallas import tpu as pltpu

F32 = jnp.float32
BF16 = jnp.bfloat16

D_MODEL = 1024
BRANCH_W = 512
N_BRANCH = 4
GRID_W = 64
HEAD_DIM = 64
N_HEADS = 8
LANES = 128
N_PAIRS = BRANCH_W // LANES
NA_WIN_H = 8
NA_WIN_W = 16
NA_TILE_ROWS = 4
NA_BAND_ROWS = 12
NA_VCHUNK = 256
NA_DEN_ROWS = 16
LOG2E = 1.4426950408889634
ROPE_THETA = 10000.0
ROPE_FREQS = 16
FNET_GROUPS = 4
FNET_GROUP_W = 128
GMLP_CHUNK = 128
GMLP_GROUPS = 8
HGRN_CHUNK = 64
HGRN_SUB = 16
HGRN_EXP_CLAMP = 115.0
HGRN_ROWS = 512
EPS = 1e-6
F_FLOOR = 1e-30
NEG_INF = -1e30

VMEM_LIMIT = 56 * 2**20

_IN_COL = {'a_q': 0, 'a_k': 1, 'a_v': 2, 'a_g': 3, 'b_x': 4, 'b_g': 5, 'c_u': 6, 'c_v': 7, 'c_g': 8,
           'd_q': 9, 'd_f_fwd': 10, 'd_f_bwd': 11, 'd_i': 12, 'd_g': 13}
_PROJ_NAMES = ('a_q', 'a_k', 'a_v', 'b_x', 'c_v', 'd_q', 'd_f_fwd', 'd_f_bwd', 'd_i')
_MERGE_NAMES = ('c_u', 'a_g', 'b_g', 'c_g', 'd_g')
_GATE_COL0 = 14 * BRANCH_W


def _cparams(sem):
    return pltpu.CompilerParams(dimension_semantics=sem, vmem_limit_bytes=VMEM_LIMIT)


def _resident(shape):
    nd = len(shape)
    return pl.BlockSpec(shape, lambda *_: (0,) * nd, pipeline_mode=pl.Buffered(1))


def _silu(t):
    return t * jax.nn.sigmoid(t)


def _lane_iota(shape):
    return lax.broadcasted_iota(jnp.int32, shape, len(shape) - 1)


def _dot(a, b):
    return jnp.dot(a, b, preferred_element_type=F32)


def _dot_nt(a, b):
    return lax.dot_general(a, b, (((1,), (1,)), ((), ())), preferred_element_type=F32)


def _dot_tn(a, b):
    return lax.dot_general(a, b, (((0,), (0,)), ((), ())), preferred_element_type=F32)


def _split2(t):
    hi = t.astype(BF16)
    return hi, (t - hi.astype(F32)).astype(BF16)


def _normed_input(x, gpre, sc, sh):
    ms = jnp.mean(x * x, axis=-1, keepdims=True)
    h = x * lax.rsqrt(ms + EPS) * gpre
    return h * (1.0 + sc) + sh


def _mod_kernel(c_ref, w_ref, b_ref, o_ref):
    s = _silu(c_ref[...]).astype(BF16)
    o_ref[0] = _dot(s, w_ref[0].astype(BF16)) + b_ref[0]


def _modulation(c_all, w_ada, b_ada):
    depth = w_ada.shape[0]
    rows = c_all.shape[0]
    tn = 1024
    return pl.pallas_call(
        _mod_kernel,
        out_shape=jax.ShapeDtypeStruct((depth, rows, 3 * D_MODEL), F32),
        grid=(depth, 3 * D_MODEL // tn),
        in_specs=[pl.BlockSpec((rows, D_MODEL), lambda l, j: (0, 0)),
                  pl.BlockSpec((1, D_MODEL, tn), lambda l, j: (l, 0, j)),
                  pl.BlockSpec((1, 1, tn), lambda l, j: (l, 0, j))],
        out_specs=pl.BlockSpec((1, rows, tn), lambda l, j: (l, 0, j)),
        compiler_params=_cparams(("arbitrary", "arbitrary")),
        name="adaln_modulation",
    )(c_all, w_ada, b_ada.reshape(depth, 1, 3 * D_MODEL))


def _lb_kernel(lg_ref, o_ref):
    depth = lg_ref.shape[0]
    lg = [lg_ref[l] for l in range(depth)]
    m = functools.reduce(jnp.maximum, lg)
    e = [jnp.exp(t - m) for t in lg]
    tot = functools.reduce(lambda a, b: a + b, e)
    sm = [t / tot for t in e]
    run = jnp.zeros_like(sm[0])
    for l in range(depth):
        run = run + sm[l]
        o_ref[l] = jnp.maximum(run - sm[0], 0.0)


def _lower_bounds(lb_logits):
    return pl.pallas_call(
        _lb_kernel,
        out_shape=jax.ShapeDtypeStruct(lb_logits.shape, F32),
        name="hgrn_lower_bounds",
    )(lb_logits)


def _rope(t, cos, sin_signed, first_half):
    up = pltpu.roll(t, LANES - ROPE_FREQS, 1)
    down = pltpu.roll(t, ROPE_FREQS, 1)
    return t * cos + jnp.where(first_half, up, down) * sin_signed


def _proj_kernel(x_ref, sh_ref, sc_ref, gpre_ref, w_ref, cos_ref, sin_ref, gn_ref, ws_ref, bs_ref,
                 lb_ref, qr_ref, qp_ref, kr_ref, v_ref, bx_ref, mx_ref, dq_ref, di_ref,
                 lff_ref, kf_ref, lfb_ref, kb_ref):
    tm = x_ref.shape[1]
    hb = _normed_input(x_ref[0], gpre_ref[...], sc_ref[0], sh_ref[0]).astype(BF16)

    def proj(name):
        j = _PROJ_NAMES.index(name)
        return _dot(hb, w_ref[:, j * BRANCH_W:(j + 1) * BRANCH_W])

    cos = cos_ref[...]
    sin_signed = sin_ref[...]
    first_half = (_lane_iota((tm, LANES)) % (2 * ROPE_FREQS)) < ROPE_FREQS

    def rope_all(t):
        return jnp.concatenate(
            [_rope(t[:, p * LANES:(p + 1) * LANES], cos, sin_signed, first_half) for p in range(N_PAIRS)],
            axis=1)

    for name, d, lf_ref, k_ref in (('d_f_fwd', 0, lff_ref, kf_ref), ('d_f_bwd', 1, lfb_ref, kb_ref)):
        z = proj(name)
        lb = lb_ref[0, d:d + 1, :]
        sg = jax.nn.sigmoid(z)
        lf_ref[0] = jnp.log(jnp.maximum(lb + (1.0 - lb) * sg, F_FLOOR))
        k_ref[0] = ((1.0 - lb) * (1.0 - sg)).astype(BF16)

    cv = proj('c_v')
    vn = (cv * lax.rsqrt(jnp.mean(cv * cv, axis=-1, keepdims=True) + EPS) * gn_ref[...]).astype(BF16)
    low_group = _lane_iota((GMLP_CHUNK, LANES)) < (LANES // 2)
    for ch in range(tm // GMLP_CHUNK):
        r0 = ch * GMLP_CHUNK
        for p in range(N_PAIRS):
            slab = vn[r0:r0 + GMLP_CHUNK, p * LANES:(p + 1) * LANES]
            mixed = jnp.where(low_group, _dot(ws_ref[2 * p], slab), _dot(ws_ref[2 * p + 1], slab))
            mixed = mixed + bs_ref[:, p * LANES:(p + 1) * LANES]
            mx_ref[0, r0:r0 + GMLP_CHUNK, p * LANES:(p + 1) * LANES] = mixed.astype(BF16)

    q = proj('a_q') * (HEAD_DIM ** -0.5 * LOG2E)
    qp_ref[0] = q.astype(BF16)
    qr_ref[0] = rope_all(q).astype(BF16)
    kr_ref[0] = rope_all(proj('a_k')).astype(BF16)
    v = proj('a_v')
    for ch in range(tm // NA_VCHUNK):
        v_ref[0, ch] = v[ch * NA_VCHUNK:(ch + 1) * NA_VCHUNK].T.astype(BF16)
    bx_ref[0] = proj('b_x').astype(BF16)
    dq_ref[0] = proj('d_q').astype(BF16)
    di_ref[0] = proj('d_i').astype(BF16)


def _proj_call(x, sh, sc, gpre, w_proj, cos, sin_signed, gn, ws, bs_tab, lb, tm, per_batch_mod):
    b, n, _ = x.shape
    nt = n // tm
    mod_map = (lambda i, j: (i, 0, 0)) if per_batch_mod else (lambda i, j: (0, 0, 0))
    row_spec = pl.BlockSpec((1, tm, BRANCH_W), lambda i, j: (i, j, 0))
    vt_spec = pl.BlockSpec((1, tm // NA_VCHUNK, BRANCH_W, NA_VCHUNK), lambda i, j: (i, j, 0, 0))
    bf = jax.ShapeDtypeStruct((b, n, BRANCH_W), BF16)
    vt = jax.ShapeDtypeStruct((b, n // NA_VCHUNK, BRANCH_W, NA_VCHUNK), BF16)
    f32 = jax.ShapeDtypeStruct((b, n, BRANCH_W), F32)
    return pl.pallas_call(
        _proj_kernel,
        out_shape=(bf, bf, bf, vt, bf, bf, bf, bf, f32, bf, f32, bf),
        grid=(b, nt),
        in_specs=[pl.BlockSpec((1, tm, D_MODEL), lambda i, j: (i, j, 0)),
                  pl.BlockSpec((1, 1, D_MODEL), mod_map),
                  pl.BlockSpec((1, 1, D_MODEL), mod_map),
                  _resident((1, D_MODEL)),
                  _resident((D_MODEL, len(_PROJ_NAMES) * BRANCH_W)),
                  pl.BlockSpec((tm, LANES), lambda i, j: (j, 0)),
                  pl.BlockSpec((tm, LANES), lambda i, j: (j, 0)),
                  _resident((1, BRANCH_W)),
                  _resident((GMLP_GROUPS, GMLP_CHUNK, GMLP_CHUNK)),
                  _resident((GMLP_CHUNK, BRANCH_W)),
                  _resident((1, 2, BRANCH_W))],
        out_specs=(row_spec,) * 3 + (vt_spec,) + (row_spec,) * 8,
        compiler_params=_cparams(("arbitrary", "arbitrary")),
        name="branch_proj",
    )(x, sh, sc, gpre, w_proj, cos, sin_signed, gn, ws, bs_tab, lb)


def _head_mask(hh):
    lane = _lane_iota((1, LANES))
    return (lane < HEAD_DIM) if hh == 0 else (lane >= HEAD_DIM)


def _nattn_kernel(qr_ref, qp_ref, k_ref, vt_ref, kc_ref, vct_ref, bias_ref, o_ref):
    rows = k_ref.shape[1] // GRID_W
    r0 = pl.program_id(1) * NA_TILE_ROWS
    kb0 = jnp.clip(r0 - NA_WIN_H // 2, 0, rows - NA_BAND_ROWS)
    start = pl.multiple_of(kb0 * GRID_W, NA_VCHUNK)
    c0 = kb0 // (NA_VCHUNK // GRID_W)
    band = NA_BAND_ROWS * GRID_W
    nq = NA_TILE_ROWS * GRID_W

    def lanes(p):
        return slice(p * LANES, (p + 1) * LANES)

    def stack_heads(q):
        zero = jnp.zeros_like(q)
        return jnp.concatenate([jnp.where(_head_mask(0), q, zero), jnp.where(_head_mask(1), q, zero)], axis=0)

    def scores(p):
        s_ctx = _dot_nt(kc_ref[0, :, lanes(p)], stack_heads(qp_ref[0, :, lanes(p)]))
        s_band = _dot_nt(k_ref[0, pl.ds(start, band), lanes(p)], stack_heads(qr_ref[0, :, lanes(p)]))
        return s_ctx, s_band + bias_ref[0, p]

    def softmax(s_ctx, s_band):
        p_ctx, p_band = [], []
        for cb in range(2 * nq // LANES):
            cs = slice(cb * LANES, (cb + 1) * LANES)
            sc, sb = s_ctx[:, cs], s_band[:, cs]
            mx = jnp.maximum(jnp.max(sc, axis=0, keepdims=True), jnp.max(sb, axis=0, keepdims=True))
            p_ctx.append(jnp.exp2(sc - mx).astype(BF16))
            p_band.append(jnp.exp2(sb - mx).astype(BF16))
        return jnp.concatenate(p_ctx, axis=1), jnp.concatenate(p_band, axis=1)

    ones_rows = jnp.ones((NA_DEN_ROWS, NA_VCHUNK), BF16)

    def values(p, p_ctx, p_band):
        outs = []
        for hh in range(2):
            qs = slice(hh * nq, (hh + 1) * nq)
            ch = slice(p * LANES + hh * HEAD_DIM, p * LANES + (hh + 1) * HEAD_DIM)
            acc = None
            for j in range(vct_ref.shape[1]):
                lhs = jnp.concatenate([vct_ref[0, j, ch, :], ones_rows], axis=0)
                term = _dot(lhs, p_ctx[j * NA_VCHUNK:(j + 1) * NA_VCHUNK, qs])
                acc = term if acc is None else acc + term
            for j in range(band // NA_VCHUNK):
                lhs = jnp.concatenate([vt_ref[0, c0 + j, ch, :], ones_rows], axis=0)
                acc = acc + _dot(lhs, p_band[j * NA_VCHUNK:(j + 1) * NA_VCHUNK, qs])
            outs.append(acc[0:HEAD_DIM] * (1.0 / acc[HEAD_DIM:HEAD_DIM + 1]))
        o_ref[0, :, lanes(p)] = jnp.concatenate(outs, axis=0).T.astype(BF16)

    s_val, p_val = {}, {}
    for t in range(N_PAIRS + 2):
        if t < N_PAIRS:
            s_val[t] = scores(t)
        if 0 <= t - 1 < N_PAIRS:
            p_val[t - 1] = softmax(*s_val.pop(t - 1))
        if 0 <= t - 2 < N_PAIRS:
            values(t - 2, *p_val.pop(t - 2))


def _nattn_call(qr, qp, k, vt, kc, vct, bias):
    b, n, _ = qr.shape
    rows = n // GRID_W
    nt = rows // NA_TILE_ROWS
    lc = kc.shape[1]
    nq = NA_TILE_ROWS * GRID_W

    def bias_map(i, t):
        return (jnp.where(t == 0, 0, jnp.where(t == nt - 1, 2, 1)), 0, 0, 0)

    q_spec = pl.BlockSpec((1, nq, BRANCH_W), lambda i, t: (i, t, 0))
    full = pl.BlockSpec((1, n, BRANCH_W), lambda i, t: (i, 0, 0))
    full_t = pl.BlockSpec((1, n // NA_VCHUNK, BRANCH_W, NA_VCHUNK), lambda i, t: (i, 0, 0, 0))
    ctx = pl.BlockSpec((1, lc, BRANCH_W), lambda i, t: (i, 0, 0))
    ctx_t = pl.BlockSpec((1, lc // NA_VCHUNK, BRANCH_W, NA_VCHUNK), lambda i, t: (i, 0, 0, 0))
    return pl.pallas_call(
        _nattn_kernel,
        out_shape=jax.ShapeDtypeStruct((b, n, BRANCH_W), BF16),
        grid=(b, nt),
        in_specs=[q_spec, q_spec, full, full_t, ctx, ctx_t,
                  pl.BlockSpec((1, N_PAIRS, NA_BAND_ROWS * GRID_W, 2 * nq), bias_map)],
        out_specs=q_spec,
        compiler_params=_cparams(("arbitrary", "arbitrary")),
        name="neighbourhood_attention",
    )(qr, qp, k, vt, kc, vct, bias)


def _cattn_kernel(q_ref, k_ref, vt_ref, o_ref):
    lc = q_ref.shape[1]
    low_head = _lane_iota((lc, LANES)) < HEAD_DIM
    for p in range(N_PAIRS):
        ls = slice(p * LANES, (p + 1) * LANES)
        q = q_ref[0, :, ls]
        k = k_ref[0, :, ls]
        vt = jnp.concatenate([vt_ref[0, j, ls, :] for j in range(vt_ref.shape[1])], axis=1)
        outs = []
        for hh in range(2):
            s = _dot_nt(jnp.where(_head_mask(hh), q, jnp.zeros_like(q)), k)
            e = jnp.exp2(s - jnp.max(s, axis=-1, keepdims=True))
            outs.append(_dot_nt(e.astype(BF16), vt) * (1.0 / jnp.sum(e, axis=-1, keepdims=True)))
        o_ref[0, :, ls] = jnp.where(low_head, outs[0], outs[1]).astype(BF16)


def _cattn_call(q, k, vt):
    b, lc, _ = q.shape
    spec = pl.BlockSpec((1, lc, BRANCH_W), lambda i: (i, 0, 0))
    spec_t = pl.BlockSpec((1, lc // NA_VCHUNK, BRANCH_W, NA_VCHUNK), lambda i: (i, 0, 0, 0))
    return pl.pallas_call(
        _cattn_kernel,
        out_shape=jax.ShapeDtypeStruct((b, lc, BRANCH_W), BF16),
        grid=(b,),
        in_specs=[spec, spec, spec_t],
        out_specs=spec,
        compiler_params=_cparams(("arbitrary",)),
        name="context_attention",
    )(q, k, vt)


def _attention_bias(rpb, rows):
    col = jnp.arange(GRID_W)
    col_start = jnp.clip(col - NA_WIN_W // 2, 0, GRID_W - NA_WIN_W)
    valid = (col[None, :] >= col_start[:, None]) & (col[None, :] < col_start[:, None] + NA_WIN_W)
    col_idx = jnp.clip(col[None, :] - col[:, None] + NA_WIN_W - 1, 0, 2 * NA_WIN_W - 2)
    per_row = jnp.where(valid[None, None], rpb.astype(F32)[:, :, col_idx] * LOG2E, NEG_INF)
    tabs = []
    for r0 in (0, NA_TILE_ROWS, rows - NA_TILE_ROWS):
        kb0 = int(np.clip(r0 - NA_WIN_H // 2, 0, rows - NA_BAND_ROWS))
        r = r0 + np.arange(NA_TILE_ROWS)[:, None]
        kr = kb0 + np.arange(NA_BAND_ROWS)[None, :]
        rs = np.clip(r - NA_WIN_H // 2, 0, rows - NA_WIN_H)
        in_win = (kr >= rs) & (kr < rs + NA_WIN_H)
        idx = np.clip(kr - r + NA_WIN_H - 1, 0, 2 * NA_WIN_H - 2)
        t = jnp.where(in_win[None, :, :, None, None], per_row[:, idx], NEG_INF)
        t = t.reshape(N_PAIRS, 2, NA_TILE_ROWS, NA_BAND_ROWS, GRID_W, GRID_W).transpose(0, 3, 5, 1, 2, 4)
        tabs.append(t.reshape(N_PAIRS, NA_BAND_ROWS * GRID_W, 2 * NA_TILE_ROWS * GRID_W))
    return jnp.stack(tabs)


_KRON = 8


@functools.lru_cache(maxsize=None)
def _fourier_consts(n):
    rows = n // GRID_W
    k1 = np.arange(rows)[:, None, None, None]
    l1 = np.arange(_KRON)[None, :, None, None]
    n1 = np.arange(rows)[None, None, :, None]
    l2 = np.arange(_KRON)[None, None, None, :]
    a_cos, a_sin = [], []
    for j in range(GRID_W // _KRON):
        ang = 2.0 * np.pi * k1 * (GRID_W * n1 + _KRON * j + l1) / n
        same = (l1 == l2)
        a_cos.append((np.cos(ang) * same).reshape(rows * _KRON, rows * _KRON))
        a_sin.append((-np.sin(ang) * same).reshape(rows * _KRON, rows * _KRON))
    k2 = np.arange(GRID_W)[:, None, None, None]
    ang = 2.0 * np.pi * k2 * np.arange(GRID_W)[None, None, None, :] / GRID_W
    same = (np.arange(_KRON)[None, :, None, None] == np.arange(_KRON)[None, None, :, None])
    b_cos = (np.cos(ang) * same).reshape(GRID_W * _KRON, _KRON * GRID_W)
    b_sin = (np.sin(ang) * same).reshape(GRID_W * _KRON, _KRON * GRID_W)
    b_re = np.concatenate([b_cos, b_sin], axis=1)
    b_im = np.concatenate([-b_sin, b_cos], axis=1)
    return (np.stack(a_cos).astype(np.float32), np.stack(a_sin).astype(np.float32),
            b_re.astype(np.float32), b_im.astype(np.float32))


@functools.lru_cache(maxsize=None)
def _channel_dft():
    c = np.arange(FNET_GROUP_W)
    ang = 2.0 * np.pi * np.outer(c, c) / FNET_GROUP_W
    return np.concatenate([np.cos(ang), np.sin(ang)], axis=0).astype(np.float32)


@functools.lru_cache(maxsize=None)
def _dense_dft(n):
    t = np.arange(n)
    ang = 2.0 * np.pi * np.outer(t, t) / n
    return np.cos(ang).astype(np.float32), (-np.sin(ang)).astype(np.float32)


def _fold_channel_map(cs_ref, wf_ref, fold_ref, norm):
    c_hi, c_lo = _split2(cs_ref[...] * norm)
    for g in range(FNET_GROUPS):
        w_hi, w_lo = _split2(wf_ref[g])
        fold_ref[g] = (_dot(c_hi, w_hi) + _dot(c_hi, w_lo) + _dot(c_lo, w_hi)).astype(BF16)


def _channel_stage(xr, xi, fold_ref):
    outs = []
    for g in range(FNET_GROUPS):
        ls = slice(g * FNET_GROUP_W, (g + 1) * FNET_GROUP_W)
        xg = jnp.concatenate([xr[:, ls], xi[:, ls]], axis=1).astype(BF16)
        outs.append(_dot(xg, fold_ref[g]))
    return jnp.concatenate(outs, axis=1)


def _fourier_kernel(x_ref, ac_ref, as_ref, bre_ref, bim_ref, cs_ref, wf_ref, o_ref, s_ref, fold_ref, *, norm):
    @pl.when(pl.program_id(0) == 0)
    def _():
        _fold_channel_map(cs_ref, wf_ref, fold_ref, norm)

    rows = x_ref.shape[1]
    blk = rows * _KRON
    pair = 2 * _KRON
    for jj in range(GRID_W // pair):
        xt = x_ref[0, :, jj * pair:(jj + 1) * pair, :].astype(F32)
        re, im = [], []
        for half in range(2):
            xc = xt[:, half * _KRON:(half + 1) * _KRON, :].reshape(blk, BRANCH_W).astype(BF16)
            re.append(_dot(ac_ref[2 * jj + half], xc).reshape(rows, _KRON, BRANCH_W))
            im.append(_dot(as_ref[2 * jj + half], xc).reshape(rows, _KRON, BRANCH_W))
        s_ref[0, :, jj * pair:(jj + 1) * pair, :] = jnp.concatenate(re, axis=1).astype(BF16)
        s_ref[1, :, jj * pair:(jj + 1) * pair, :] = jnp.concatenate(im, axis=1).astype(BF16)
    sblk = _KRON * GRID_W
    for mm in range(rows // pair):
        ys = []
        for half in range(2):
            m0 = (2 * mm + half) * _KRON
            rhs = jnp.concatenate([s_ref[0, m0:m0 + _KRON].reshape(sblk, BRANCH_W),
                                   s_ref[1, m0:m0 + _KRON].reshape(sblk, BRANCH_W)], axis=0)
            xr = _dot(bre_ref[...], rhs)
            xi = _dot(bim_ref[...], rhs)
            ys.append(_channel_stage(xr, xi, fold_ref).reshape(GRID_W, _KRON, BRANCH_W))
        o_ref[0, :, mm * pair:(mm + 1) * pair, :] = jnp.concatenate(ys, axis=1).astype(BF16)


def _fourier_call(bx, wf):
    b, n, _ = bx.shape
    rows = n // GRID_W
    a_cos, a_sin, b_re, b_im = (jnp.asarray(t, BF16) for t in _fourier_consts(n))
    cs = jnp.asarray(_channel_dft(), F32)
    norm = float(1.0 / np.sqrt(n * FNET_GROUP_W))
    x4 = bx.reshape(b, rows, GRID_W, BRANCH_W)
    out = pl.pallas_call(
        functools.partial(_fourier_kernel, norm=norm),
        out_shape=jax.ShapeDtypeStruct((b, GRID_W, rows, BRANCH_W), BF16),
        grid=(b,),
        in_specs=[pl.BlockSpec((1, rows, GRID_W, BRANCH_W), lambda i: (i, 0, 0, 0)),
                  _resident(a_cos.shape), _resident(a_sin.shape),
                  _resident(b_re.shape), _resident(b_im.shape),
                  _resident(cs.shape), _resident(wf.shape)],
        out_specs=pl.BlockSpec((1, GRID_W, rows, BRANCH_W), lambda i: (i, 0, 0, 0)),
        scratch_shapes=[pltpu.VMEM((2, rows, GRID_W, BRANCH_W), BF16),
                        pltpu.VMEM((FNET_GROUPS, 2 * FNET_GROUP_W, FNET_GROUP_W), BF16)],
        compiler_params=_cparams(("arbitrary",)),
        name="fourier_mix",
    )(x4, a_cos, a_sin, b_re, b_im, cs, wf)
    return out.reshape(b, n, BRANCH_W)


def _fourier_ctx_kernel(x_ref, c_ref, s_ref, cs_ref, wf_ref, o_ref, fold_ref, *, norm):
    @pl.when(pl.program_id(0) == 0)
    def _():
        _fold_channel_map(cs_ref, wf_ref, fold_ref, norm)

    x = x_ref[0]
    xr = _dot(c_ref[...], x)
    xi = _dot(s_ref[...], x)
    o_ref[0] = _channel_stage(xr, xi, fold_ref).astype(BF16)


def _fourier_ctx_call(bx, wf):
    b, n, _ = bx.shape
    cn, sn = (jnp.asarray(t, BF16) for t in _dense_dft(n))
    cs = jnp.asarray(_channel_dft(), F32)
    norm = float(1.0 / np.sqrt(n * FNET_GROUP_W))
    spec = pl.BlockSpec((1, n, BRANCH_W), lambda i: (i, 0, 0))
    return pl.pallas_call(
        functools.partial(_fourier_ctx_kernel, norm=norm),
        out_shape=jax.ShapeDtypeStruct((b, n, BRANCH_W), BF16),
        grid=(b,),
        in_specs=[spec, _resident(cn.shape), _resident(sn.shape), _resident(cs.shape), _resident(wf.shape)],
        out_specs=spec,
        scratch_shapes=[pltpu.VMEM((FNET_GROUPS, 2 * FNET_GROUP_W, FNET_GROUP_W), BF16)],
        compiler_params=_cparams(("arbitrary",)),
        name="fourier_mix_context",
    )(bx, cn, sn, cs, wf)


def _block_diag(t):
    lo = _lane_iota(t.shape) < HEAD_DIM
    z = jnp.zeros_like(t)
    return jnp.concatenate([jnp.where(lo, t, z), jnp.where(lo, z, t)], axis=0)


def _hgrn_needed(j, reverse):
    nsub = HGRN_CHUNK // HGRN_SUB
    return list(range(0, j + 1)) if reverse else list(range(j, nsub))


def _hgrn_prepare(q, k, i, a, reverse):
    c = HGRN_CHUNK
    nsub = c // HGRN_SUB

    def level(r):
        return a[r:r + 1, :]

    zero_row = jnp.zeros((1, BRANCH_W), F32)
    if reverse:
        refs = [level((s + 1) * HGRN_SUB) if s + 1 < nsub else zero_row for s in range(nsub)]
        a_end = a[0:1, :]
    else:
        refs = [level(s * HGRN_SUB - 1) if s > 0 else zero_row for s in range(nsub)]
        a_end = a[c - 1:c, :]
    ref_rows = jnp.concatenate([jnp.broadcast_to(r, (HGRN_SUB, BRANCH_W)) for r in refs], axis=0)
    qf = q.astype(F32)
    kf = k.astype(F32)
    k_own = (kf * jnp.exp2(jnp.minimum(ref_rows - a, HGRN_EXP_CLAMP))).astype(BF16)

    def q_variant(j):
        parts = []
        for s in _hgrn_needed(j, reverse):
            rs = slice(s * HGRN_SUB, (s + 1) * HGRN_SUB)
            parts.append((qf[rs] * jnp.exp2(a[rs] - refs[j])).astype(BF16))
        return jnp.concatenate(parts, axis=0)

    q_var = [q_variant(j) for j in range(nsub)]
    return dict(
        q_stack=jnp.concatenate(q_var, axis=0),
        q_in=q_var[nsub - 1] if reverse else q_var[0],
        k_own=k_own,
        k_out=(kf * jnp.exp2(a_end - a)).astype(BF16),
        decay_end=jnp.exp2(a_end),
        i=i, reverse=reverse)


def _hgrn_scores(ops):
    c = HGRN_CHUNK
    nsub = c // HGRN_SUB
    reverse = ops['reverse']
    src = _lane_iota((c, LANES)) % HEAD_DIM
    step = lax.broadcasted_iota(jnp.int32, (c, LANES), 0)
    seen = (src >= step) if reverse else (src <= step)
    src_sub = (_lane_iota((HGRN_SUB, LANES)) % HEAD_DIM) // HGRN_SUB
    where_blk, off = {}, 0
    for j in range(nsub):
        for s in _hgrn_needed(j, reverse):
            where_blk[(j, s)] = off
            off += HGRN_SUB
    out = []
    for p in range(N_PAIRS):
        ls = slice(p * LANES, (p + 1) * LANES)
        res = _dot_nt(ops['q_stack'][:, ls], _block_diag(ops['k_own'][:, ls]))
        rows = []
        for s in range(nsub):
            blk = None
            for j in range(nsub):
                if (j, s) in where_blk:
                    piece = res[where_blk[(j, s)]:where_blk[(j, s)] + HGRN_SUB]
                    blk = piece if blk is None else jnp.where(src_sub == j, piece, blk)
            rows.append(blk)
        out.append(jnp.where(seen, jnp.concatenate(rows, axis=0), 0.0).astype(BF16))
    return out


def _hgrn_local(ops, scores):
    low_rows = lax.broadcasted_iota(jnp.int32, (LANES, LANES), 0) < HEAD_DIM
    same_head = low_rows == (_lane_iota((LANES, LANES)) < HEAD_DIM)
    o_intra, upd = [], []
    for p in range(N_PAIRS):
        ls = slice(p * LANES, (p + 1) * LANES)
        ip = ops['i'][:, ls]
        o_intra.append(_dot(scores[p], _block_diag(ip)))
        upd.append(jnp.where(same_head, _dot_tn(ip, ops['k_out'][:, ls]), 0.0))
    return o_intra, upd


def _hgrn_carry(ops, o_intra, upd, state_ref, d):
    outs = []
    for p in range(N_PAIRS):
        ls = slice(p * LANES, (p + 1) * LANES)
        st = state_ref[d, p]
        outs.append(o_intra[p] + _dot_nt(ops['q_in'][:, ls], st.astype(BF16)))
        state_ref[d, p] = ops['decay_end'][:, ls] * st + upd[p]
    return jnp.concatenate(outs, axis=1)


def _hgrn_kernel(qf_ref, if_ref, lff_ref, kf_ref, qb_ref, ib_ref, lfb_ref, kb_ref, s0f_ref, s0b_ref,
                 of_ref, ob_ref, sf_ref, sb_ref, state_ref):
    j = pl.program_id(1)
    nchunk = qf_ref.shape[1] // HGRN_CHUNK

    @pl.when(j == 0)
    def _():
        state_ref[0] = s0f_ref[0]
        state_ref[1] = s0b_ref[0]

    tm = qf_ref.shape[1]
    row = lax.broadcasted_iota(jnp.int32, (tm, tm), 0)
    col = lax.broadcasted_iota(jnp.int32, (tm, tm), 1)
    same_chunk = (row // HGRN_CHUNK) == (col // HGRN_CHUNK)

    def cum(lf, reverse):
        tri = (same_chunk & ((col >= row) if reverse else (col <= row))).astype(BF16)
        hi, lo = _split2(lf * LOG2E)
        return _dot(tri, hi) + _dot(tri, lo)

    a_f = cum(lff_ref[0], False)
    a_b = cum(lfb_ref[0], True)

    todo = []
    for cix in range(nchunk):
        bix = nchunk - 1 - cix
        todo.append((0, of_ref, slice(cix * HGRN_CHUNK, (cix + 1) * HGRN_CHUNK), qf_ref, kf_ref, if_ref, a_f, False))
        todo.append((1, ob_ref, slice(bix * HGRN_CHUNK, (bix + 1) * HGRN_CHUNK), qb_ref, kb_ref, ib_ref, a_b, True))
    ops, scores, local = {}, {}, {}
    for t in range(len(todo) + 3):
        if t < len(todo):
            _, _, rs, q_ref, k_ref, i_ref, a, reverse = todo[t]
            ops[t] = _hgrn_prepare(q_ref[0, rs, :], k_ref[0, rs, :], i_ref[0, rs, :], a[rs, :], reverse)
        if 0 <= t - 1 < len(todo):
            scores[t - 1] = _hgrn_scores(ops[t - 1])
        if 0 <= t - 2 < len(todo):
            local[t - 2] = _hgrn_local(ops[t - 2], scores.pop(t - 2))
        if 0 <= t - 3 < len(todo):
            d, o_ref, rs = todo[t - 3][:3]
            o_ref[0, rs, :] = _hgrn_carry(ops.pop(t - 3), *local.pop(t - 3), state_ref, d).astype(BF16)

    @pl.when(j == pl.num_programs(1) - 1)
    def _():
        sf_ref[0] = state_ref[0]
        sb_ref[0] = state_ref[1]


def _hgrn_call(q, i, lff, kf, lfb, kb, s0f, s0b, tm):
    b, n, _ = q.shape
    nt = n // tm
    fwd = pl.BlockSpec((1, tm, BRANCH_W), lambda bi, j: (bi, j, 0))
    bwd = pl.BlockSpec((1, tm, BRANCH_W), lambda bi, j: (bi, nt - 1 - j, 0))
    st = pl.BlockSpec((1, N_PAIRS, LANES, LANES), lambda bi, j: (bi, 0, 0, 0))
    o_shape = jax.ShapeDtypeStruct((b, n, BRANCH_W), BF16)
    s_shape = jax.ShapeDtypeStruct((b, N_PAIRS, LANES, LANES), F32)
    return pl.pallas_call(
        _hgrn_kernel,
        out_shape=(o_shape, o_shape, s_shape, s_shape),
        grid=(b, nt),
        in_specs=[fwd, fwd, fwd, fwd, bwd, bwd, bwd, bwd, st, st],
        out_specs=(fwd, bwd, st, st),
        scratch_shapes=[pltpu.VMEM((2, N_PAIRS, LANES, LANES), F32)],
        compiler_params=_cparams(("arbitrary", "arbitrary")),
        name="hgrn_scan",
    )(q, i, lff, kf, q, i, lfb, kb, s0f, s0b)


def _merge_kernel(x_ref, sh_ref, sc_ref, gt_ref, gpre_ref, gpost_ref, w_ref, oa_ref, ob_ref, mx_ref,
                  of_ref, obk_ref, hn_ref, hm_ref, wb_ref, wo_ref, o_ref):
    x = x_ref[0]
    hb = _normed_input(x, gpre_ref[...], sc_ref[0], sh_ref[0]).astype(BF16)

    def proj(name):
        j = _MERGE_NAMES.index(name)
        return _dot(hb, w_ref[:, j * BRANCH_W:(j + 1) * BRANCH_W])

    ya = oa_ref[0].astype(F32) * _silu(proj('a_g'))
    yb = ob_ref[0].astype(F32) * _silu(proj('b_g'))
    yc = proj('c_u') * mx_ref[0].astype(F32) * _silu(proj('c_g'))
    o = of_ref[0].astype(F32) + obk_ref[0].astype(F32)
    ms = _dot((o * o).astype(BF16), hm_ref[...])
    yd = o * lax.rsqrt(ms + EPS) * hn_ref[...] * _silu(proj('d_g'))

    g0 = len(_MERGE_NAMES) * BRANCH_W
    merged = None
    for r, y in enumerate((ya, yb, yc, yd)):
        gate = _dot(hb, w_ref[:, g0 + r * D_MODEL:g0 + (r + 1) * D_MODEL])
        term = jax.nn.sigmoid(gate) * _dot(y.astype(BF16), wb_ref[r])
        merged = term if merged is None else merged + term
    out = _dot(merged.astype(BF16), wo_ref[...])
    post = out * lax.rsqrt(jnp.mean(out * out, axis=-1, keepdims=True) + EPS) * gpost_ref[...]
    o_ref[0] = x + gt_ref[0] * post


def _merge_call(x, sh, sc, gt, gpre, gpost, w_merge, oa, ob, mx, of, obk, hn, hmean, wb, wo, tm,
                per_batch_mod):
    b, n, _ = x.shape
    nt = n // tm
    mod_map = (lambda i, j: (i, 0, 0)) if per_batch_mod else (lambda i, j: (0, 0, 0))
    x_spec = pl.BlockSpec((1, tm, D_MODEL), lambda i, j: (i, j, 0))
    br_spec = pl.BlockSpec((1, tm, BRANCH_W), lambda i, j: (i, j, 0))
    mod_spec = pl.BlockSpec((1, 1, D_MODEL), mod_map)
    return pl.pallas_call(
        _merge_kernel,
        out_shape=jax.ShapeDtypeStruct((b, n, D_MODEL), F32),
        grid=(b, nt),
        in_specs=[x_spec, mod_spec, mod_spec, mod_spec,
                  _resident((1, D_MODEL)), _resident((1, D_MODEL)),
                  _resident(w_merge.shape),
                  br_spec, br_spec, br_spec, br_spec, br_spec,
                  _resident((1, BRANCH_W)), _resident((BRANCH_W, BRANCH_W)),
                  _resident(wb.shape), _resident(wo.shape)],
        out_specs=x_spec,
        compiler_params=_cparams(("arbitrary", "arbitrary")),
        name="branch_merge",
    )(x, sh, sc, gt, gpre, gpost, w_merge, oa, ob, mx, of, obk, hn, hmean, wb, wo)


def _rope_tables(n_tok, rotate):
    if not rotate:
        return jnp.ones((n_tok, LANES), F32), jnp.zeros((n_tok, LANES), F32)
    t = jnp.arange(n_tok, dtype=jnp.int32)
    pos = jnp.stack([t // GRID_W, t % GRID_W], axis=-1).astype(F32)
    inv = ROPE_THETA ** (-jnp.arange(ROPE_FREQS, dtype=F32) * 2.0 / (2 * ROPE_FREQS))
    ang = pos[:, :, None] * inv
    cos = jnp.repeat(jnp.cos(ang)[:, :, None, :], 2, axis=2).reshape(n_tok, HEAD_DIM)
    sin = jnp.sin(ang)
    sin_signed = jnp.stack([-sin, sin], axis=2).reshape(n_tok, HEAD_DIM)
    return jnp.tile(cos, (1, 2)), jnp.tile(sin_signed, (1, 2))


def _gather_cols(w_in_l, names):
    return jnp.concatenate([w_in_l[:, _IN_COL[nm] * BRANCH_W:(_IN_COL[nm] + 1) * BRANCH_W] for nm in names],
                           axis=1)


def _row_tile(n):
    return 512 if n % 512 == 0 else 256


def kernel(x, c, ctx, c_ctx, w_ada, b_ada, g_pre, g_post, w_in, na_rpb, fnet_w, gmlp_norm_g, gmlp_ws,
           gmlp_bs, hgrn_lb_logits, hgrn_norm_g, w_branch, w_out):
    batch, n_tok, _ = x.shape
    n_ctx = ctx.shape[1]
    depth = w_in.shape[0]

    w_in_b = w_in.astype(BF16)
    w_proj = [_gather_cols(w_in_b[l], _PROJ_NAMES) for l in range(depth)]
    w_merge = [jnp.concatenate([_gather_cols(w_in_b[l], _MERGE_NAMES), w_in_b[l][:, _GATE_COL0:]], axis=1)
               for l in range(depth)]
    w_branch_b = w_branch.astype(BF16)
    w_out_b = w_out.astype(BF16)
    gmlp_ws_b = gmlp_ws.astype(BF16)
    bs_tab = jnp.repeat(jnp.swapaxes(gmlp_bs, 1, 2), BRANCH_W // GMLP_GROUPS, axis=2)
    head_mean = jnp.asarray(np.kron(np.eye(N_HEADS), np.ones((HEAD_DIM, HEAD_DIM)) / HEAD_DIM), BF16)
    cos_x, sin_x = _rope_tables(n_tok, True)
    cos_c, sin_c = _rope_tables(n_ctx, False)

    c_all = jnp.concatenate([c, jnp.broadcast_to(c_ctx[None, :], (8, D_MODEL))], axis=0)
    mod = _modulation(c_all, w_ada, b_ada)
    lower = _lower_bounds(hgrn_lb_logits)

    zero_state = jnp.zeros((batch, N_PAIRS, LANES, LANES), F32)
    tm_x = _row_tile(n_tok)
    tm_c = _row_tile(n_ctx)

    for l in range(depth):
        with_ctx = l < depth - 1
        mod_x = [mod[l, :batch, i * D_MODEL:(i + 1) * D_MODEL].reshape(batch, 1, D_MODEL) for i in range(3)]
        mod_c = [mod[l, batch:batch + 1, i * D_MODEL:(i + 1) * D_MODEL].reshape(1, 1, D_MODEL) for i in range(3)]
        gpre = g_pre[l].reshape(1, D_MODEL)
        gpost = g_post[l].reshape(1, D_MODEL)
        gn = gmlp_norm_g[l].reshape(1, BRANCH_W)
        hn = hgrn_norm_g[l].reshape(1, BRANCH_W)
        lb = lower[l].reshape(1, 2, BRANCH_W)
        bias = _attention_bias(na_rpb[l], n_tok // GRID_W)

        (_, qp_c, k_c, v_c, bx_c, mx_c, dq_c, di_c, lff_c, kf_c, lfb_c, kb_c) = _proj_call(
            ctx, mod_c[0], mod_c[1], gpre, w_proj[l], cos_c, sin_c, gn, gmlp_ws_b[l], bs_tab[l], lb,
            tm_c, False)
        of_c, ob_c, st_f, st_b = _hgrn_call(dq_c, di_c, lff_c, kf_c, lfb_c, kb_c, zero_state, zero_state,
                                            min(HGRN_ROWS, n_ctx))

        (qr, qp, k, v, bx, mx, dq, di, lff, kf, lfb, kb) = _proj_call(
            x, mod_x[0], mod_x[1], gpre, w_proj[l], cos_x, sin_x, gn, gmlp_ws_b[l], bs_tab[l], lb,
            tm_x, True)
        oa = _nattn_call(qr, qp, k, v, k_c, v_c, bias)
        ob = _fourier_call(bx, fnet_w[l])
        of, obk, _, _ = _hgrn_call(dq, di, lff, kf, lfb, kb, st_f, st_b, min(HGRN_ROWS, n_tok))
        x = _merge_call(x, mod_x[0], mod_x[1], mod_x[2], gpre, gpost, w_merge[l], oa, ob, mx, of, obk,
                        hn, head_mean, w_branch_b[l], w_out_b[l], tm_x, True)

        if with_ctx:
            oa_c = _cattn_call(qp_c, k_c, v_c)
            ob_c2 = _fourier_ctx_call(bx_c, fnet_w[l])
            ctx = _merge_call(ctx, mod_c[0], mod_c[1], mod_c[2], gpre, gpost, w_merge[l], oa_c, ob_c2, mx_c,
                              of_c, ob_c, hn, head_mean, w_branch_b[l], w_out_b[l], tm_c, False)
    return x
```

```python
import functools

import numpy as np
import jax
import jax.numpy as jnp
from jax import lax
from jax.experimental import pallas as pl
from jax.experimental.pallas import tpu as pltpu

F32 = jnp.float32
BF16 = jnp.bfloat16

D_MODEL = 1024
BRANCH_W = 512
N_BRANCH = 4
GRID_W = 64
HEAD_DIM = 64
N_HEADS = 8
LANES = 128
N_PAIRS = BRANCH_W // LANES
NA_WIN_H = 8
NA_WIN_W = 16
NA_TILE_ROWS = 4
NA_BAND_ROWS = 12
NA_VCHUNK = 256
NA_DEN_ROWS = 16
LOG2E = 1.4426950408889634
ROPE_THETA = 10000.0
ROPE_FREQS = 16
FNET_GROUPS = 4
FNET_GROUP_W = 128
GMLP_CHUNK = 128
GMLP_GROUPS = 8
HGRN_CHUNK = 64
HGRN_SUB = 16
HGRN_EXP_CLAMP = 115.0
HGRN_ROWS = 256
EPS = 1e-6
F_FLOOR = 1e-30
NEG_INF = -1e30

VMEM_LIMIT = 56 * 2**20

_IN_COL = {'a_q': 0, 'a_k': 1, 'a_v': 2, 'a_g': 3, 'b_x': 4, 'b_g': 5, 'c_u': 6, 'c_v': 7, 'c_g': 8,
           'd_q': 9, 'd_f_fwd': 10, 'd_f_bwd': 11, 'd_i': 12, 'd_g': 13}
_PROJ_NAMES = ('a_q', 'a_k', 'a_v', 'b_x', 'c_v', 'd_q', 'd_f_fwd', 'd_f_bwd', 'd_i')
_MERGE_NAMES = ('c_u', 'a_g', 'b_g', 'c_g', 'd_g')
_GATE_COL0 = 14 * BRANCH_W


def _cparams(sem):
    return pltpu.CompilerParams(dimension_semantics=sem, vmem_limit_bytes=VMEM_LIMIT)


def _resident(shape):
    nd = len(shape)
    return pl.BlockSpec(shape, lambda *_: (0,) * nd, pipeline_mode=pl.Buffered(1))


def _silu(t):
    return t * jax.nn.sigmoid(t)


def _lane_iota(shape):
    return lax.broadcasted_iota(jnp.int32, shape, len(shape) - 1)


def _dot(a, b):
    return jnp.dot(a, b, preferred_element_type=F32)


def _dot_nt(a, b):
    return lax.dot_general(a, b, (((1,), (1,)), ((), ())), preferred_element_type=F32)


def _dot_tn(a, b):
    return lax.dot_general(a, b, (((0,), (0,)), ((), ())), preferred_element_type=F32)


def _split2(t):
    hi = t.astype(BF16)
    return hi, (t - hi.astype(F32)).astype(BF16)


def _normed_input(x, gpre, sc, sh):
    ms = jnp.mean(x * x, axis=-1, keepdims=True)
    h = x * lax.rsqrt(ms + EPS) * gpre
    return h * (1.0 + sc) + sh


def _mod_kernel(c_ref, w_ref, b_ref, o_ref):
    s = _silu(c_ref[...]).astype(BF16)
    o_ref[0] = _dot(s, w_ref[0].astype(BF16)) + b_ref[0]


def _modulation(c_all, w_ada, b_ada):
    depth = w_ada.shape[0]
    rows = c_all.shape[0]
    tn = 1024
    return pl.pallas_call(
        _mod_kernel,
        out_shape=jax.ShapeDtypeStruct((depth, rows, 3 * D_MODEL), F32),
        grid=(depth, 3 * D_MODEL // tn),
        in_specs=[pl.BlockSpec((rows, D_MODEL), lambda l, j: (0, 0)),
                  pl.BlockSpec((1, D_MODEL, tn), lambda l, j: (l, 0, j)),
                  pl.BlockSpec((1, 1, tn), lambda l, j: (l, 0, j))],
        out_specs=pl.BlockSpec((1, rows, tn), lambda l, j: (l, 0, j)),
        compiler_params=_cparams(("arbitrary", "arbitrary")),
        name="adaln_modulation",
    )(c_all, w_ada, b_ada.reshape(depth, 1, 3 * D_MODEL))


def _lb_kernel(lg_ref, o_ref):
    depth = lg_ref.shape[0]
    lg = [lg_ref[l] for l in range(depth)]
    m = functools.reduce(jnp.maximum, lg)
    e = [jnp.exp(t - m) for t in lg]
    tot = functools.reduce(lambda a, b: a + b, e)
    sm = [t / tot for t in e]
    run = jnp.zeros_like(sm[0])
    for l in range(depth):
        run = run + sm[l]
        o_ref[l] = jnp.maximum(run - sm[0], 0.0)


def _lower_bounds(lb_logits):
    return pl.pallas_call(
        _lb_kernel,
        out_shape=jax.ShapeDtypeStruct(lb_logits.shape, F32),
        name="hgrn_lower_bounds",
    )(lb_logits)


def _rope(t, cos, sin_signed, first_half):
    up = pltpu.roll(t, LANES - ROPE_FREQS, 1)
    down = pltpu.roll(t, ROPE_FREQS, 1)
    return t * cos + jnp.where(first_half, up, down) * sin_signed


def _proj_kernel(x_ref, sh_ref, sc_ref, gpre_ref, w_ref, cos_ref, sin_ref, gn_ref, ws_ref, bs_ref,
                 lb_ref, qr_ref, qp_ref, kr_ref, v_ref, bx_ref, mx_ref, dq_ref, di_ref,
                 lff_ref, kf_ref, lfb_ref, kb_ref):
    tm = x_ref.shape[1]
    hb = _normed_input(x_ref[0], gpre_ref[...], sc_ref[0], sh_ref[0]).astype(BF16)

    def proj(name):
        j = _PROJ_NAMES.index(name)
        return _dot(hb, w_ref[:, j * BRANCH_W:(j + 1) * BRANCH_W])

    cos = cos_ref[...]
    sin_signed = sin_ref[...]
    first_half = (_lane_iota((tm, LANES)) % (2 * ROPE_FREQS)) < ROPE_FREQS

    def rope_all(t):
        return jnp.concatenate(
            [_rope(t[:, p * LANES:(p + 1) * LANES], cos, sin_signed, first_half) for p in range(N_PAIRS)],
            axis=1)

    for name, d, lf_ref, k_ref in (('d_f_fwd', 0, lff_ref, kf_ref), ('d_f_bwd', 1, lfb_ref, kb_ref)):
        z = proj(name)
        lb = lb_ref[0, d:d + 1, :]
        sg = jax.nn.sigmoid(z)
        lf_ref[0] = jnp.log(jnp.maximum(lb + (1.0 - lb) * sg, F_FLOOR))
        k_ref[0] = ((1.0 - lb) * (1.0 - sg)).astype(BF16)

    cv = proj('c_v')
    vn = (cv * lax.rsqrt(jnp.mean(cv * cv, axis=-1, keepdims=True) + EPS) * gn_ref[...]).astype(BF16)
    low_group = _lane_iota((GMLP_CHUNK, LANES)) < (LANES // 2)
    for ch in range(tm // GMLP_CHUNK):
        r0 = ch * GMLP_CHUNK
        for p in range(N_PAIRS):
            slab = vn[r0:r0 + GMLP_CHUNK, p * LANES:(p + 1) * LANES]
            mixed = jnp.where(low_group, _dot(ws_ref[2 * p], slab), _dot(ws_ref[2 * p + 1], slab))
            mixed = mixed + bs_ref[:, p * LANES:(p + 1) * LANES]
            mx_ref[0, r0:r0 + GMLP_CHUNK, p * LANES:(p + 1) * LANES] = mixed.astype(BF16)

    q = proj('a_q') * (HEAD_DIM ** -0.5 * LOG2E)
    qp_ref[0] = q.astype(BF16)
    qr_ref[0] = rope_all(q).astype(BF16)
    kr_ref[0] = rope_all(proj('a_k')).astype(BF16)
    v = proj('a_v')
    for ch in range(tm // NA_VCHUNK):
        v_ref[0, ch] = v[ch * NA_VCHUNK:(ch + 1) * NA_VCHUNK].T.astype(BF16)
    bx_ref[0] = proj('b_x').astype(BF16)
    dq_ref[0] = proj('d_q').astype(BF16)
    di_ref[0] = proj('d_i').astype(BF16)


def _proj_call(x, sh, sc, gpre, w_proj, cos, sin_signed, gn, ws, bs_tab, lb, tm, per_batch_mod):
    b, n, _ = x.shape
    nt = n // tm
    mod_map = (lambda i, j: (i, 0, 0)) if per_batch_mod else (lambda i, j: (0, 0, 0))
    row_spec = pl.BlockSpec((1, tm, BRANCH_W), lambda i, j: (i, j, 0))
    vt_spec = pl.BlockSpec((1, tm // NA_VCHUNK, BRANCH_W, NA_VCHUNK), lambda i, j: (i, j, 0, 0))
    bf = jax.ShapeDtypeStruct((b, n, BRANCH_W), BF16)
    vt = jax.ShapeDtypeStruct((b, n // NA_VCHUNK, BRANCH_W, NA_VCHUNK), BF16)
    f32 = jax.ShapeDtypeStruct((b, n, BRANCH_W), F32)
    return pl.pallas_call(
        _proj_kernel,
        out_shape=(bf, bf, bf, vt, bf, bf, bf, bf, f32, bf, f32, bf),
        grid=(b, nt),
        in_specs=[pl.BlockSpec((1, tm, D_MODEL), lambda i, j: (i, j, 0)),
                  pl.BlockSpec((1, 1, D_MODEL), mod_map),
                  pl.BlockSpec((1, 1, D_MODEL), mod_map),
                  _resident((1, D_MODEL)),
                  _resident((D_MODEL, len(_PROJ_NAMES) * BRANCH_W)),
                  pl.BlockSpec((tm, LANES), lambda i, j: (j, 0)),
                  pl.BlockSpec((tm, LANES), lambda i, j: (j, 0)),
                  _resident((1, BRANCH_W)),
                  _resident((GMLP_GROUPS, GMLP_CHUNK, GMLP_CHUNK)),
                  _resident((GMLP_CHUNK, BRANCH_W)),
                  _resident((1, 2, BRANCH_W))],
        out_specs=(row_spec,) * 3 + (vt_spec,) + (row_spec,) * 8,
        compiler_params=_cparams(("arbitrary", "arbitrary")),
        name="branch_proj",
    )(x, sh, sc, gpre, w_proj, cos, sin_signed, gn, ws, bs_tab, lb)


def _head_mask(hh):
    lane = _lane_iota((1, LANES))
    return (lane < HEAD_DIM) if hh == 0 else (lane >= HEAD_DIM)


def _nattn_kernel(qr_ref, qp_ref, k_ref, vt_ref, kc_ref, vct_ref, bias_ref, o_ref):
    rows = k_ref.shape[1] // GRID_W
    r0 = pl.program_id(1) * NA_TILE_ROWS
    kb0 = jnp.clip(r0 - NA_WIN_H // 2, 0, rows - NA_BAND_ROWS)
    start = pl.multiple_of(kb0 * GRID_W, NA_VCHUNK)
    c0 = kb0 // (NA_VCHUNK // GRID_W)
    band = NA_BAND_ROWS * GRID_W
    nq = NA_TILE_ROWS * GRID_W

    def lanes(p):
        return slice(p * LANES, (p + 1) * LANES)

    def stack_heads(q):
        zero = jnp.zeros_like(q)
        return jnp.concatenate([jnp.where(_head_mask(0), q, zero), jnp.where(_head_mask(1), q, zero)], axis=0)

    def scores(p):
        s_ctx = _dot_nt(kc_ref[0, :, lanes(p)], stack_heads(qp_ref[0, :, lanes(p)]))
        s_band = _dot_nt(k_ref[0, pl.ds(start, band), lanes(p)], stack_heads(qr_ref[0, :, lanes(p)]))
        return s_ctx, s_band + bias_ref[0, p]

    def softmax(s_ctx, s_band):
        p_ctx, p_band = [], []
        for cb in range(2 * nq // LANES):
            cs = slice(cb * LANES, (cb + 1) * LANES)
            sc, sb = s_ctx[:, cs], s_band[:, cs]
            mx = jnp.maximum(jnp.max(sc, axis=0, keepdims=True), jnp.max(sb, axis=0, keepdims=True))
            p_ctx.append(jnp.exp2(sc - mx).astype(BF16))
            p_band.append(jnp.exp2(sb - mx).astype(BF16))
        return jnp.concatenate(p_ctx, axis=1), jnp.concatenate(p_band, axis=1)

    ones_rows = jnp.ones((NA_DEN_ROWS, NA_VCHUNK), BF16)

    def values(p, p_ctx, p_band):
        outs = []
        for hh in range(2):
            qs = slice(hh * nq, (hh + 1) * nq)
            ch = slice(p * LANES + hh * HEAD_DIM, p * LANES + (hh + 1) * HEAD_DIM)
            acc = None
            for j in range(vct_ref.shape[1]):
                lhs = jnp.concatenate([vct_ref[0, j, ch, :], ones_rows], axis=0)
                term = _dot(lhs, p_ctx[j * NA_VCHUNK:(j + 1) * NA_VCHUNK, qs])
                acc = term if acc is None else acc + term
            for j in range(band // NA_VCHUNK):
                lhs = jnp.concatenate([vt_ref[0, c0 + j, ch, :], ones_rows], axis=0)
                acc = acc + _dot(lhs, p_band[j * NA_VCHUNK:(j + 1) * NA_VCHUNK, qs])
            outs.append(acc[0:HEAD_DIM] * (1.0 / acc[HEAD_DIM:HEAD_DIM + 1]))
        o_ref[0, :, lanes(p)] = jnp.concatenate(outs, axis=0).T.astype(BF16)

    s_val, p_val = {}, {}
    for t in range(N_PAIRS + 2):
        if t < N_PAIRS:
            s_val[t] = scores(t)
        if 0 <= t - 1 < N_PAIRS:
            p_val[t - 1] = softmax(*s_val.pop(t - 1))
        if 0 <= t - 2 < N_PAIRS:
            values(t - 2, *p_val.pop(t - 2))


def _nattn_call(qr, qp, k, vt, kc, vct, bias):
    b, n, _ = qr.shape
    rows = n // GRID_W
    nt = rows // NA_TILE_ROWS
    lc = kc.shape[1]
    nq = NA_TILE_ROWS * GRID_W

    def bias_map(i, t):
        return (jnp.where(t == 0, 0, jnp.where(t == nt - 1, 2, 1)), 0, 0, 0)

    q_spec = pl.BlockSpec((1, nq, BRANCH_W), lambda i, t: (i, t, 0))
    full = pl.BlockSpec((1, n, BRANCH_W), lambda i, t: (i, 0, 0))
    full_t = pl.BlockSpec((1, n // NA_VCHUNK, BRANCH_W, NA_VCHUNK), lambda i, t: (i, 0, 0, 0))
    ctx = pl.BlockSpec((1, lc, BRANCH_W), lambda i, t: (i, 0, 0))
    ctx_t = pl.BlockSpec((1, lc // NA_VCHUNK, BRANCH_W, NA_VCHUNK), lambda i, t: (i, 0, 0, 0))
    return pl.pallas_call(
        _nattn_kernel,
        out_shape=jax.ShapeDtypeStruct((b, n, BRANCH_W), BF16),
        grid=(b, nt),
        in_specs=[q_spec, q_spec, full, full_t, ctx, ctx_t,
                  pl.BlockSpec((1, N_PAIRS, NA_BAND_ROWS * GRID_W, 2 * nq), bias_map)],
        out_specs=q_spec,
        compiler_params=_cparams(("arbitrary", "arbitrary")),
        name="neighbourhood_attention",
    )(qr, qp, k, vt, kc, vct, bias)


def _cattn_kernel(q_ref, k_ref, vt_ref, o_ref):
    lc = q_ref.shape[1]
    low_head = _lane_iota((lc, LANES)) < HEAD_DIM
    for p in range(N_PAIRS):
        ls = slice(p * LANES, (p + 1) * LANES)
        q = q_ref[0, :, ls]
        k = k_ref[0, :, ls]
        vt = jnp.concatenate([vt_ref[0, j, ls, :] for j in range(vt_ref.shape[1])], axis=1)
        outs = []
        for hh in range(2):
            s = _dot_nt(jnp.where(_head_mask(hh), q, jnp.zeros_like(q)), k)
            e = jnp.exp2(s - jnp.max(s, axis=-1, keepdims=True))
            outs.append(_dot_nt(e.astype(BF16), vt) * (1.0 / jnp.sum(e, axis=-1, keepdims=True)))
        o_ref[0, :, ls] = jnp.where(low_head, outs[0], outs[1]).astype(BF16)


def _cattn_call(q, k, vt):
    b, lc, _ = q.shape
    spec = pl.BlockSpec((1, lc, BRANCH_W), lambda i: (i, 0, 0))
    spec_t = pl.BlockSpec((1, lc // NA_VCHUNK, BRANCH_W, NA_VCHUNK), lambda i: (i, 0, 0, 0))
    return pl.pallas_call(
        _cattn_kernel,
        out_shape=jax.ShapeDtypeStruct((b, lc, BRANCH_W), BF16),
        grid=(b,),
        in_specs=[spec, spec, spec_t],
        out_specs=spec,
        compiler_params=_cparams(("arbitrary",)),
        name="context_attention",
    )(q, k, vt)


def _attention_bias(rpb, rows):
    col = jnp.arange(GRID_W)
    col_start = jnp.clip(col - NA_WIN_W // 2, 0, GRID_W - NA_WIN_W)
    valid = (col[None, :] >= col_start[:, None]) & (col[None, :] < col_start[:, None] + NA_WIN_W)
    col_idx = jnp.clip(col[None, :] - col[:, None] + NA_WIN_W - 1, 0, 2 * NA_WIN_W - 2)
    per_row = jnp.where(valid[None, None], rpb.astype(F32)[:, :, col_idx] * LOG2E, NEG_INF)
    tabs = []
    for r0 in (0, NA_TILE_ROWS, rows - NA_TILE_ROWS):
        kb0 = int(np.clip(r0 - NA_WIN_H // 2, 0, rows - NA_BAND_ROWS))
        r = r0 + np.arange(NA_TILE_ROWS)[:, None]
        kr = kb0 + np.arange(NA_BAND_ROWS)[None, :]
        rs = np.clip(r - NA_WIN_H // 2, 0, rows - NA_WIN_H)
        in_win = (kr >= rs) & (kr < rs + NA_WIN_H)
        idx = np.clip(kr - r + NA_WIN_H - 1, 0, 2 * NA_WIN_H - 2)
        t = jnp.where(in_win[None, :, :, None, None], per_row[:, idx], NEG_INF)
        t = t.reshape(N_PAIRS, 2, NA_TILE_ROWS, NA_BAND_ROWS, GRID_W, GRID_W).transpose(0, 3, 5, 1, 2, 4)
        tabs.append(t.reshape(N_PAIRS, NA_BAND_ROWS * GRID_W, 2 * NA_TILE_ROWS * GRID_W))
    return jnp.stack(tabs)


_KRON = 8


@functools.lru_cache(maxsize=None)
def _fourier_consts(n):
    rows = n // GRID_W
    k1 = np.arange(rows)[:, None, None, None]
    l1 = np.arange(_KRON)[None, :, None, None]
    n1 = np.arange(rows)[None, None, :, None]
    l2 = np.arange(_KRON)[None, None, None, :]
    a_cos, a_sin = [], []
    for j in range(GRID_W // _KRON):
        ang = 2.0 * np.pi * k1 * (GRID_W * n1 + _KRON * j + l1) / n
        same = (l1 == l2)
        a_cos.append((np.cos(ang) * same).reshape(rows * _KRON, rows * _KRON))
        a_sin.append((-np.sin(ang) * same).reshape(rows * _KRON, rows * _KRON))
    k2 = np.arange(GRID_W)[:, None, None, None]
    ang = 2.0 * np.pi * k2 * np.arange(GRID_W)[None, None, None, :] / GRID_W
    same = (np.arange(_KRON)[None, :, None, None] == np.arange(_KRON)[None, None, :, None])
    b_cos = (np.cos(ang) * same).reshape(GRID_W * _KRON, _KRON * GRID_W)
    b_sin = (np.sin(ang) * same).reshape(GRID_W * _KRON, _KRON * GRID_W)
    b_re = np.concatenate([b_cos, b_sin], axis=1)
    b_im = np.concatenate([-b_sin, b_cos], axis=1)
    return (np.stack(a_cos).astype(np.float32), np.stack(a_sin).astype(np.float32),
            b_re.astype(np.float32), b_im.astype(np.float32))


@functools.lru_cache(maxsize=None)
def _channel_dft():
    c = np.arange(FNET_GROUP_W)
    ang = 2.0 * np.pi * np.outer(c, c) / FNET_GROUP_W
    return np.concatenate([np.cos(ang), np.sin(ang)], axis=0).astype(np.float32)


@functools.lru_cache(maxsize=None)
def _dense_dft(n):
    t = np.arange(n)
    ang = 2.0 * np.pi * np.outer(t, t) / n
    return np.cos(ang).astype(np.float32), (-np.sin(ang)).astype(np.float32)


def _fold_channel_map(cs_ref, wf_ref, fold_ref, norm):
    c_hi, c_lo = _split2(cs_ref[...] * norm)
    for g in range(FNET_GROUPS):
        w_hi, w_lo = _split2(wf_ref[g])
        fold_ref[g] = (_dot(c_hi, w_hi) + _dot(c_hi, w_lo) + _dot(c_lo, w_hi)).astype(BF16)


def _channel_stage(xr, xi, fold_ref):
    outs = []
    for g in range(FNET_GROUPS):
        ls = slice(g * FNET_GROUP_W, (g + 1) * FNET_GROUP_W)
        xg = jnp.concatenate([xr[:, ls], xi[:, ls]], axis=1).astype(BF16)
        outs.append(_dot(xg, fold_ref[g]))
    return jnp.concatenate(outs, axis=1)


def _fourier_kernel(x_ref, ac_ref, as_ref, bre_ref, bim_ref, cs_ref, wf_ref, o_ref, s_ref, fold_ref, *, norm):
    @pl.when(pl.program_id(0) == 0)
    def _():
        _fold_channel_map(cs_ref, wf_ref, fold_ref, norm)

    rows = x_ref.shape[1]
    blk = rows * _KRON
    pair = 2 * _KRON
    for jj in range(GRID_W // pair):
        xt = x_ref[0, :, jj * pair:(jj + 1) * pair, :].astype(F32)
        re, im = [], []
        for half in range(2):
            xc = xt[:, half * _KRON:(half + 1) * _KRON, :].reshape(blk, BRANCH_W).astype(BF16)
            re.append(_dot(ac_ref[2 * jj + half], xc).reshape(rows, _KRON, BRANCH_W))
            im.append(_dot(as_ref[2 * jj + half], xc).reshape(rows, _KRON, BRANCH_W))
        s_ref[0, :, jj * pair:(jj + 1) * pair, :] = jnp.concatenate(re, axis=1).astype(BF16)
        s_ref[1, :, jj * pair:(jj + 1) * pair, :] = jnp.concatenate(im, axis=1).astype(BF16)
    sblk = _KRON * GRID_W
    for mm in range(rows // pair):
        ys = []
        for half in range(2):
            m0 = (2 * mm + half) * _KRON
            rhs = jnp.concatenate([s_ref[0, m0:m0 + _KRON].reshape(sblk, BRANCH_W),
                                   s_ref[1, m0:m0 + _KRON].reshape(sblk, BRANCH_W)], axis=0)
            xr = _dot(bre_ref[...], rhs)
            xi = _dot(bim_ref[...], rhs)
            ys.append(_channel_stage(xr, xi, fold_ref).reshape(GRID_W, _KRON, BRANCH_W))
        o_ref[0, :, mm * pair:(mm + 1) * pair, :] = jnp.concatenate(ys, axis=1).astype(BF16)


def _fourier_call(bx, wf):
    b, n, _ = bx.shape
    rows = n // GRID_W
    a_cos, a_sin, b_re, b_im = (jnp.asarray(t, BF16) for t in _fourier_consts(n))
    cs = jnp.asarray(_channel_dft(), F32)
    norm = float(1.0 / np.sqrt(n * FNET_GROUP_W))
    x4 = bx.reshape(b, rows, GRID_W, BRANCH_W)
    out = pl.pallas_call(
        functools.partial(_fourier_kernel, norm=norm),
        out_shape=jax.ShapeDtypeStruct((b, GRID_W, rows, BRANCH_W), BF16),
        grid=(b,),
        in_specs=[pl.BlockSpec((1, rows, GRID_W, BRANCH_W), lambda i: (i, 0, 0, 0)),
                  _resident(a_cos.shape), _resident(a_sin.shape),
                  _resident(b_re.shape), _resident(b_im.shape),
                  _resident(cs.shape), _resident(wf.shape)],
        out_specs=pl.BlockSpec((1, GRID_W, rows, BRANCH_W), lambda i: (i, 0, 0, 0)),
        scratch_shapes=[pltpu.VMEM((2, rows, GRID_W, BRANCH_W), BF16),
                        pltpu.VMEM((FNET_GROUPS, 2 * FNET_GROUP_W, FNET_GROUP_W), BF16)],
        compiler_params=_cparams(("arbitrary",)),
        name="fourier_mix",
    )(x4, a_cos, a_sin, b_re, b_im, cs, wf)
    return out.reshape(b, n, BRANCH_W)


def _fourier_ctx_kernel(x_ref, c_ref, s_ref, cs_ref, wf_ref, o_ref, fold_ref, *, norm):
    @pl.when(pl.program_id(0) == 0)
    def _():
        _fold_channel_map(cs_ref, wf_ref, fold_ref, norm)

    x = x_ref[0]
    xr = _dot(c_ref[...], x)
    xi = _dot(s_ref[...], x)
    o_ref[0] = _channel_stage(xr, xi, fold_ref).astype(BF16)


def _fourier_ctx_call(bx, wf):
    b, n, _ = bx.shape
    cn, sn = (jnp.asarray(t, BF16) for t in _dense_dft(n))
    cs = jnp.asarray(_channel_dft(), F32)
    norm = float(1.0 / np.sqrt(n * FNET_GROUP_W))
    spec = pl.BlockSpec((1, n, BRANCH_W), lambda i: (i, 0, 0))
    return pl.pallas_call(
        functools.partial(_fourier_ctx_kernel, norm=norm),
        out_shape=jax.ShapeDtypeStruct((b, n, BRANCH_W), BF16),
        grid=(b,),
        in_specs=[spec, _resident(cn.shape), _resident(sn.shape), _resident(cs.shape), _resident(wf.shape)],
        out_specs=spec,
        scratch_shapes=[pltpu.VMEM((FNET_GROUPS, 2 * FNET_GROUP_W, FNET_GROUP_W), BF16)],
        compiler_params=_cparams(("arbitrary",)),
        name="fourier_mix_context",
    )(bx, cn, sn, cs, wf)


def _block_diag(t):
    lo = _lane_iota(t.shape) < HEAD_DIM
    z = jnp.zeros_like(t)
    return jnp.concatenate([jnp.where(lo, t, z), jnp.where(lo, z, t)], axis=0)


def _hgrn_needed(j, reverse):
    nsub = HGRN_CHUNK // HGRN_SUB
    return list(range(0, j + 1)) if reverse else list(range(j, nsub))


def _hgrn_prepare(q, k, i, a, reverse):
    c = HGRN_CHUNK
    nsub = c // HGRN_SUB

    def level(r):
        return a[r:r + 1, :]

    zero_row = jnp.zeros((1, BRANCH_W), F32)
    if reverse:
        refs = [level((s + 1) * HGRN_SUB) if s + 1 < nsub else zero_row for s in range(nsub)]
        a_end = a[0:1, :]
    else:
        refs = [level(s * HGRN_SUB - 1) if s > 0 else zero_row for s in range(nsub)]
        a_end = a[c - 1:c, :]
    ref_rows = jnp.concatenate([jnp.broadcast_to(r, (HGRN_SUB, BRANCH_W)) for r in refs], axis=0)
    qf = q.astype(F32)
    kf = k.astype(F32)
    lift = ref_rows - a
    k_own = (kf * jnp.exp2(jnp.minimum(lift, HGRN_EXP_CLAMP))).astype(BF16)

    def q_variant(j):
        parts = []
        for s in _hgrn_needed(j, reverse):
            rs = slice(s * HGRN_SUB, (s + 1) * HGRN_SUB)
            parts.append((qf[rs] * jnp.exp2(a[rs] - refs[j])).astype(BF16))
        return jnp.concatenate(parts, axis=0)

    q_var = [q_variant(j) for j in range(nsub)]
    return dict(
        q_stack=jnp.concatenate(q_var, axis=0),
        q_in=q_var[nsub - 1] if reverse else q_var[0],
        k_own=k_own,
        k_out=(kf * jnp.exp2(a_end - a)).astype(BF16),
        decay_end=jnp.exp2(a_end),
        max_lift=jnp.max(lift, axis=0, keepdims=True),
        i=i, reverse=reverse)


def _hgrn_scores(ops):
    c = HGRN_CHUNK
    nsub = c // HGRN_SUB
    reverse = ops['reverse']
    src = _lane_iota((c, LANES)) % HEAD_DIM
    step = lax.broadcasted_iota(jnp.int32, (c, LANES), 0)
    seen = (src >= step) if reverse else (src <= step)
    src_sub = (_lane_iota((HGRN_SUB, LANES)) % HEAD_DIM) // HGRN_SUB
    where_blk, off = {}, 0
    for j in range(nsub):
        for s in _hgrn_needed(j, reverse):
            where_blk[(j, s)] = off
            off += HGRN_SUB
    out = []
    for p in range(N_PAIRS):
        ls = slice(p * LANES, (p + 1) * LANES)
        res = _dot_nt(ops['q_stack'][:, ls], _block_diag(ops['k_own'][:, ls]))
        rows = []
        for s in range(nsub):
            blk = None
            for j in range(nsub):
                if (j, s) in where_blk:
                    piece = res[where_blk[(j, s)]:where_blk[(j, s)] + HGRN_SUB]
                    blk = piece if blk is None else jnp.where(src_sub == j, piece, blk)
            rows.append(blk)
        out.append(jnp.where(seen, jnp.concatenate(rows, axis=0), 0.0).astype(BF16))
    return out


def _hgrn_local(ops, scores):
    low_rows = lax.broadcasted_iota(jnp.int32, (LANES, LANES), 0) < HEAD_DIM
    same_head = low_rows == (_lane_iota((LANES, LANES)) < HEAD_DIM)
    o_intra, upd = [], []
    for p in range(N_PAIRS):
        ls = slice(p * LANES, (p + 1) * LANES)
        ip = ops['i'][:, ls]
        o_intra.append(_dot(scores[p], _block_diag(ip)))
        upd.append(jnp.where(same_head, _dot_tn(ip, ops['k_out'][:, ls]), 0.0))
    return o_intra, upd


def _hgrn_carry(ops, o_intra, upd, state_ref, d):
    outs = []
    for p in range(N_PAIRS):
        ls = slice(p * LANES, (p + 1) * LANES)
        st = state_ref[d, p]
        outs.append(o_intra[p] + _dot_nt(ops['q_in'][:, ls], st.astype(BF16)))
        state_ref[d, p] = ops['decay_end'][:, ls] * st + upd[p]
    return jnp.concatenate(outs, axis=1)


def _hgrn_exact_tile(q_ref, k_ref, i_ref, a_ref, o_ref, state_ref, d, reverse, q_sc, k_sc, i_sc):
    c = HGRN_CHUNK
    nchunk = q_ref.shape[1] // c
    lane_head = _lane_iota((BRANCH_W, BRANCH_W)) // HEAD_DIM
    row_head = lax.broadcasted_iota(jnp.int32, (BRANCH_W, BRANCH_W), 0) // HEAD_DIM
    head_sum = (lane_head == row_head).astype(BF16)
    step = lax.broadcasted_iota(jnp.int32, (c, BRANCH_W), 0)
    low_rows = lax.broadcasted_iota(jnp.int32, (LANES, LANES), 0) < HEAD_DIM
    same_head = low_rows == (_lane_iota((LANES, LANES)) < HEAD_DIM)
    for cix in (range(nchunk - 1, -1, -1) if reverse else range(nchunk)):
        rs = slice(cix * c, (cix + 1) * c)
        q_sc[...] = q_ref[0, rs, :].astype(F32)
        k_sc[...] = k_ref[0, rs, :].astype(F32)
        i_sc[...] = i_ref[0, rs, :].astype(F32)
        qf = q_sc[...]
        kf = k_sc[...]
        a = a_ref[d, rs, :]

        def one_source(s, acc, a=a, qf=qf, cix=cix):
            a_s = a_ref[d, pl.ds(cix * c + s, 1), :]
            w = qf * (k_sc[pl.ds(s, 1), :] * jnp.exp2(jnp.minimum(a - a_s, 0.0)))
            w = jnp.where((step <= s) if reverse else (step >= s), w, 0.0)
            hi, lo = _split2(w)
            return acc + (_dot(hi, head_sum) + _dot(lo, head_sum)) * i_sc[pl.ds(s, 1), :]

        o_intra = lax.fori_loop(0, c, one_source, jnp.zeros((c, BRANCH_W), F32))
        a_end = a[0:1, :] if reverse else a[c - 1:c, :]
        q_in = (qf * jnp.exp2(a)).astype(BF16)
        k_out = (kf * jnp.exp2(a_end - a)).astype(BF16)
        decay_end = jnp.exp2(a_end)
        ib = i_ref[0, rs, :]
        outs = []
        for p in range(N_PAIRS):
            ls = slice(p * LANES, (p + 1) * LANES)
            st = state_ref[d, p]
            outs.append(o_intra[:, ls] + _dot_nt(q_in[:, ls], st.astype(BF16)))
            upd = jnp.where(same_head, _dot_tn(ib[:, ls], k_out[:, ls]), 0.0)
            state_ref[d, p] = decay_end[:, ls] * st + upd
        o_ref[0, rs, :] = jnp.concatenate(outs, axis=1).astype(BF16)


def _hgrn_kernel(qf_ref, if_ref, lff_ref, kf_ref, qb_ref, ib_ref, lfb_ref, kb_ref, s0f_ref, s0b_ref,
                 of_ref, ob_ref, sf_ref, sb_ref, state_ref, backup_ref, a_ref, q_sc, k_sc, i_sc):
    j = pl.program_id(1)
    nchunk = qf_ref.shape[1] // HGRN_CHUNK

    @pl.when(j == 0)
    def _():
        state_ref[0] = s0f_ref[0]
        state_ref[1] = s0b_ref[0]

    backup_ref[...] = state_ref[...]

    tm = qf_ref.shape[1]
    row = lax.broadcasted_iota(jnp.int32, (tm, tm), 0)
    col = lax.broadcasted_iota(jnp.int32, (tm, tm), 1)
    same_chunk = (row // HGRN_CHUNK) == (col // HGRN_CHUNK)

    def cum(lf, reverse):
        tri = (same_chunk & ((col >= row) if reverse else (col <= row))).astype(BF16)
        hi, lo = _split2(lf * LOG2E)
        return _dot(tri, hi) + _dot(tri, lo)

    a_f = cum(lff_ref[0], False)
    a_b = cum(lfb_ref[0], True)

    todo = []
    for cix in range(nchunk):
        bix = nchunk - 1 - cix
        todo.append((0, of_ref, slice(cix * HGRN_CHUNK, (cix + 1) * HGRN_CHUNK), qf_ref, kf_ref, if_ref, a_f, False))
        todo.append((1, ob_ref, slice(bix * HGRN_CHUNK, (bix + 1) * HGRN_CHUNK), qb_ref, kb_ref, ib_ref, a_b, True))
    ops, scores, local, lifts = {}, {}, {}, []
    for t in range(len(todo) + 3):
        if t < len(todo):
            _, _, rs, q_ref, k_ref, i_ref, a, reverse = todo[t]
            ops[t] = _hgrn_prepare(q_ref[0, rs, :], k_ref[0, rs, :], i_ref[0, rs, :], a[rs, :], reverse)
            lifts.append(ops[t]['max_lift'])
        if 0 <= t - 1 < len(todo):
            scores[t - 1] = _hgrn_scores(ops[t - 1])
        if 0 <= t - 2 < len(todo):
            local[t - 2] = _hgrn_local(ops[t - 2], scores.pop(t - 2))
        if 0 <= t - 3 < len(todo):
            d, o_ref, rs = todo[t - 3][:3]
            o_ref[0, rs, :] = _hgrn_carry(ops.pop(t - 3), *local.pop(t - 3), state_ref, d).astype(BF16)

    @pl.when(jnp.max(functools.reduce(jnp.maximum, lifts)) > HGRN_EXP_CLAMP)
    def _():
        state_ref[...] = backup_ref[...]
        a_ref[0] = a_f
        a_ref[1] = a_b
        _hgrn_exact_tile(qf_ref, kf_ref, if_ref, a_ref, of_ref, state_ref, 0, False, q_sc, k_sc, i_sc)
        _hgrn_exact_tile(qb_ref, kb_ref, ib_ref, a_ref, ob_ref, state_ref, 1, True, q_sc, k_sc, i_sc)

    @pl.when(j == pl.num_programs(1) - 1)
    def _():
        sf_ref[0] = state_ref[0]
        sb_ref[0] = state_ref[1]


def _hgrn_call(q, i, lff, kf, lfb, kb, s0f, s0b, tm):
    b, n, _ = q.shape
    nt = n // tm
    fwd = pl.BlockSpec((1, tm, BRANCH_W), lambda bi, j: (bi, j, 0))
    bwd = pl.BlockSpec((1, tm, BRANCH_W), lambda bi, j: (bi, nt - 1 - j, 0))
    st = pl.BlockSpec((1, N_PAIRS, LANES, LANES), lambda bi, j: (bi, 0, 0, 0))
    o_shape = jax.ShapeDtypeStruct((b, n, BRANCH_W), BF16)
    s_shape = jax.ShapeDtypeStruct((b, N_PAIRS, LANES, LANES), F32)
    return pl.pallas_call(
        _hgrn_kernel,
        out_shape=(o_shape, o_shape, s_shape, s_shape),
        grid=(b, nt),
        in_specs=[fwd, fwd, fwd, fwd, bwd, bwd, bwd, bwd, st, st],
        out_specs=(fwd, bwd, st, st),
        scratch_shapes=[pltpu.VMEM((2, N_PAIRS, LANES, LANES), F32),
                        pltpu.VMEM((2, N_PAIRS, LANES, LANES), F32),
                        pltpu.VMEM((2, tm, BRANCH_W), F32),
                        pltpu.VMEM((HGRN_CHUNK, BRANCH_W), F32),
                        pltpu.VMEM((HGRN_CHUNK, BRANCH_W), F32),
                        pltpu.VMEM((HGRN_CHUNK, BRANCH_W), F32)],
        compiler_params=_cparams(("arbitrary", "arbitrary")),
        name="hgrn_scan",
    )(q, i, lff, kf, q, i, lfb, kb, s0f, s0b)


def _merge_kernel(x_ref, sh_ref, sc_ref, gt_ref, gpre_ref, gpost_ref, w_ref, oa_ref, ob_ref, mx_ref,
                  of_ref, obk_ref, hn_ref, hm_ref, wb_ref, wo_ref, o_ref):
    x = x_ref[0]
    hb = _normed_input(x, gpre_ref[...], sc_ref[0], sh_ref[0]).astype(BF16)

    def proj(name):
        j = _MERGE_NAMES.index(name)
        return _dot(hb, w_ref[:, j * BRANCH_W:(j + 1) * BRANCH_W])

    ya = oa_ref[0].astype(F32) * _silu(proj('a_g'))
    yb = ob_ref[0].astype(F32) * _silu(proj('b_g'))
    yc = proj('c_u') * mx_ref[0].astype(F32) * _silu(proj('c_g'))
    o = of_ref[0].astype(F32) + obk_ref[0].astype(F32)
    ms = _dot((o * o).astype(BF16), hm_ref[...])
    yd = o * lax.rsqrt(ms + EPS) * hn_ref[...] * _silu(proj('d_g'))

    g0 = len(_MERGE_NAMES) * BRANCH_W
    merged = None
    for r, y in enumerate((ya, yb, yc, yd)):
        gate = _dot(hb, w_ref[:, g0 + r * D_MODEL:g0 + (r + 1) * D_MODEL])
        term = jax.nn.sigmoid(gate) * _dot(y.astype(BF16), wb_ref[r])
        merged = term if merged is None else merged + term
    out = _dot(merged.astype(BF16), wo_ref[...])
    post = out * lax.rsqrt(jnp.mean(out * out, axis=-1, keepdims=True) + EPS) * gpost_ref[...]
    o_ref[0] = x + gt_ref[0] * post


def _merge_call(x, sh, sc, gt, gpre, gpost, w_merge, oa, ob, mx, of, obk, hn, hmean, wb, wo, tm,
                per_batch_mod):
    b, n, _ = x.shape
    nt = n // tm
    mod_map = (lambda i, j: (i, 0, 0)) if per_batch_mod else (lambda i, j: (0, 0, 0))
    x_spec = pl.BlockSpec((1, tm, D_MODEL), lambda i, j: (i, j, 0))
    br_spec = pl.BlockSpec((1, tm, BRANCH_W), lambda i, j: (i, j, 0))
    mod_spec = pl.BlockSpec((1, 1, D_MODEL), mod_map)
    return pl.pallas_call(
        _merge_kernel,
        out_shape=jax.ShapeDtypeStruct((b, n, D_MODEL), F32),
        grid=(b, nt),
        in_specs=[x_spec, mod_spec, mod_spec, mod_spec,
                  _resident((1, D_MODEL)), _resident((1, D_MODEL)),
                  _resident(w_merge.shape),
                  br_spec, br_spec, br_spec, br_spec, br_spec,
                  _resident((1, BRANCH_W)), _resident((BRANCH_W, BRANCH_W)),
                  _resident(wb.shape), _resident(wo.shape)],
        out_specs=x_spec,
        compiler_params=_cparams(("arbitrary", "arbitrary")),
        name="branch_merge",
    )(x, sh, sc, gt, gpre, gpost, w_merge, oa, ob, mx, of, obk, hn, hmean, wb, wo)


def _rope_tables(n_tok, rotate):
    if not rotate:
        return jnp.ones((n_tok, LANES), F32), jnp.zeros((n_tok, LANES), F32)
    t = jnp.arange(n_tok, dtype=jnp.int32)
    pos = jnp.stack([t // GRID_W, t % GRID_W], axis=-1).astype(F32)
    inv = ROPE_THETA ** (-jnp.arange(ROPE_FREQS, dtype=F32) * 2.0 / (2 * ROPE_FREQS))
    ang = pos[:, :, None] * inv
    cos = jnp.repeat(jnp.cos(ang)[:, :, None, :], 2, axis=2).reshape(n_tok, HEAD_DIM)
    sin = jnp.sin(ang)
    sin_signed = jnp.stack([-sin, sin], axis=2).reshape(n_tok, HEAD_DIM)
    return jnp.tile(cos, (1, 2)), jnp.tile(sin_signed, (1, 2))


def _gather_cols(w_in_l, names):
    return jnp.concatenate([w_in_l[:, _IN_COL[nm] * BRANCH_W:(_IN_COL[nm] + 1) * BRANCH_W] for nm in names],
                           axis=1)


def _row_tile(n):
    return 512 if n % 512 == 0 else 256


def kernel(x, c, ctx, c_ctx, w_ada, b_ada, g_pre, g_post, w_in, na_rpb, fnet_w, gmlp_norm_g, gmlp_ws,
           gmlp_bs, hgrn_lb_logits, hgrn_norm_g, w_branch, w_out):
    batch, n_tok, _ = x.shape
    n_ctx = ctx.shape[1]
    depth = w_in.shape[0]

    w_in_b = w_in.astype(BF16)
    w_proj = [_gather_cols(w_in_b[l], _PROJ_NAMES) for l in range(depth)]
    w_merge = [jnp.concatenate([_gather_cols(w_in_b[l], _MERGE_NAMES), w_in_b[l][:, _GATE_COL0:]], axis=1)
               for l in range(depth)]
    w_branch_b = w_branch.astype(BF16)
    w_out_b = w_out.astype(BF16)
    gmlp_ws_b = gmlp_ws.astype(BF16)
    bs_tab = jnp.repeat(jnp.swapaxes(gmlp_bs, 1, 2), BRANCH_W // GMLP_GROUPS, axis=2)
    head_mean = jnp.asarray(np.kron(np.eye(N_HEADS), np.ones((HEAD_DIM, HEAD_DIM)) / HEAD_DIM), BF16)
    cos_x, sin_x = _rope_tables(n_tok, True)
    cos_c, sin_c = _rope_tables(n_ctx, False)

    c_all = jnp.concatenate([c, jnp.broadcast_to(c_ctx[None, :], (8, D_MODEL))], axis=0)
    mod = _modulation(c_all, w_ada, b_ada)
    lower = _lower_bounds(hgrn_lb_logits)

    zero_state = jnp.zeros((batch, N_PAIRS, LANES, LANES), F32)
    tm_x = _row_tile(n_tok)
    tm_c = _row_tile(n_ctx)

    for l in range(depth):
        with_ctx = l < depth - 1
        mod_x = [mod[l, :batch, i * D_MODEL:(i + 1) * D_MODEL].reshape(batch, 1, D_MODEL) for i in range(3)]
        mod_c = [mod[l, batch:batch + 1, i * D_MODEL:(i + 1) * D_MODEL].reshape(1, 1, D_MODEL) for i in range(3)]
        gpre = g_pre[l].reshape(1, D_MODEL)
        gpost = g_post[l].reshape(1, D_MODEL)
        gn = gmlp_norm_g[l].reshape(1, BRANCH_W)
        hn = hgrn_norm_g[l].reshape(1, BRANCH_W)
        lb = lower[l].reshape(1, 2, BRANCH_W)
        bias = _attention_bias(na_rpb[l], n_tok // GRID_W)

        (_, qp_c, k_c, v_c, bx_c, mx_c, dq_c, di_c, lff_c, kf_c, lfb_c, kb_c) = _proj_call(
            ctx, mod_c[0], mod_c[1], gpre, w_proj[l], cos_c, sin_c, gn, gmlp_ws_b[l], bs_tab[l], lb,
            tm_c, False)
        of_c, ob_c, st_f, st_b = _hgrn_call(dq_c, di_c, lff_c, kf_c, lfb_c, kb_c, zero_state, zero_state,
                                            min(HGRN_ROWS, n_ctx))

        (qr, qp, k, v, bx, mx, dq, di, lff, kf, lfb, kb) = _proj_call(
            x, mod_x[0], mod_x[1], gpre, w_proj[l], cos_x, sin_x, gn, gmlp_ws_b[l], bs_tab[l], lb,
            tm_x, True)
        oa = _nattn_call(qr, qp, k, v, k_c, v_c, bias)
        ob = _fourier_call(bx, fnet_w[l])
        of, obk, _, _ = _hgrn_call(dq, di, lff, kf, lfb, kb, st_f, st_b, min(HGRN_ROWS, n_tok))
        x = _merge_call(x, mod_x[0], mod_x[1], mod_x[2], gpre, gpost, w_merge[l], oa, ob, mx, of, obk,
                        hn, head_mean, w_branch_b[l], w_out_b[l], tm_x, True)

        if with_ctx:
            oa_c = _cattn_call(qp_c, k_c, v_c)
            ob_c2 = _fourier_ctx_call(bx_c, fnet_w[l])
            ctx = _merge_call(ctx, mod_c[0], mod_c[1], mod_c[2], gpre, gpost, w_merge[l], oa_c, ob_c2, mx_c,
                              of_c, ob_c, hn, head_mean, w_branch_b[l], w_out_b[l], tm_c, False)
    return x
```

```python
import functools

import numpy as np
import jax
import jax.numpy as jnp
from jax import lax
from jax.experimental import pallas as pl
from jax.experimental.pallas import tpu as pltpu

F32 = jnp.float32
BF16 = jnp.bfloat16

D_MODEL = 1024
BRANCH_W = 512
N_BRANCH = 4
GRID_W = 64
HEAD_DIM = 64
N_HEADS = 8
LANES = 128
N_PAIRS = BRANCH_W // LANES
NA_WIN_H = 8
NA_WIN_W = 16
NA_TILE_ROWS = 4
NA_BAND_ROWS = 12
NA_VCHUNK = 256
NA_DEN_ROWS = 16
LOG2E = 1.4426950408889634
ROPE_THETA = 10000.0
ROPE_FREQS = 16
FNET_GROUPS = 4
FNET_GROUP_W = 128
GMLP_CHUNK = 128
GMLP_GROUPS = 8
HGRN_CHUNK = 64
HGRN_SUB = 16
HGRN_EXP_CLAMP = 115.0
HGRN_ROWS = 256
EPS = 1e-6
F_FLOOR = 1e-30
NEG_INF = -1e30

VMEM_LIMIT = 56 * 2**20

_IN_COL = {'a_q': 0, 'a_k': 1, 'a_v': 2, 'a_g': 3, 'b_x': 4, 'b_g': 5, 'c_u': 6, 'c_v': 7, 'c_g': 8,
           'd_q': 9, 'd_f_fwd': 10, 'd_f_bwd': 11, 'd_i': 12, 'd_g': 13}
_PROJ_NAMES = ('a_q', 'a_k', 'a_v', 'b_x', 'c_v', 'd_q', 'd_f_fwd', 'd_f_bwd', 'd_i')
_MERGE_NAMES = ('c_u', 'a_g', 'b_g', 'c_g', 'd_g')
_GATE_COL0 = 14 * BRANCH_W


def _cparams(sem):
    return pltpu.CompilerParams(dimension_semantics=sem, vmem_limit_bytes=VMEM_LIMIT)


def _resident(shape):
    nd = len(shape)
    return pl.BlockSpec(shape, lambda *_: (0,) * nd, pipeline_mode=pl.Buffered(1))


def _silu(t):
    return t * jax.nn.sigmoid(t)


def _lane_iota(shape):
    return lax.broadcasted_iota(jnp.int32, shape, len(shape) - 1)


def _dot(a, b):
    return jnp.dot(a, b, preferred_element_type=F32)


def _dot_nt(a, b):
    return lax.dot_general(a, b, (((1,), (1,)), ((), ())), preferred_element_type=F32)


def _dot_tn(a, b):
    return lax.dot_general(a, b, (((0,), (0,)), ((), ())), preferred_element_type=F32)


def _split2(t):
    hi = t.astype(BF16)
    return hi, (t - hi.astype(F32)).astype(BF16)


def _normed_input(x, gpre, sc, sh):
    ms = jnp.mean(x * x, axis=-1, keepdims=True)
    h = x * lax.rsqrt(ms + EPS) * gpre
    return h * (1.0 + sc) + sh


def _mod_kernel(c_ref, w_ref, b_ref, o_ref):
    s = _silu(c_ref[...]).astype(BF16)
    o_ref[0] = _dot(s, w_ref[0].astype(BF16)) + b_ref[0]


def _modulation(c_all, w_ada, b_ada):
    depth = w_ada.shape[0]
    rows = c_all.shape[0]
    tn = 1024
    return pl.pallas_call(
        _mod_kernel,
        out_shape=jax.ShapeDtypeStruct((depth, rows, 3 * D_MODEL), F32),
        grid=(depth, 3 * D_MODEL // tn),
        in_specs=[pl.BlockSpec((rows, D_MODEL), lambda l, j: (0, 0)),
                  pl.BlockSpec((1, D_MODEL, tn), lambda l, j: (l, 0, j)),
                  pl.BlockSpec((1, 1, tn), lambda l, j: (l, 0, j))],
        out_specs=pl.BlockSpec((1, rows, tn), lambda l, j: (l, 0, j)),
        compiler_params=_cparams(("arbitrary", "arbitrary")),
        name="adaln_modulation",
    )(c_all, w_ada, b_ada.reshape(depth, 1, 3 * D_MODEL))


def _lb_kernel(lg_ref, o_ref):
    depth = lg_ref.shape[0]
    lg = [lg_ref[l] for l in range(depth)]
    m = functools.reduce(jnp.maximum, lg)
    e = [jnp.exp(t - m) for t in lg]
    tot = functools.reduce(lambda a, b: a + b, e)
    sm = [t / tot for t in e]
    run = jnp.zeros_like(sm[0])
    for l in range(depth):
        run = run + sm[l]
        o_ref[l] = jnp.maximum(run - sm[0], 0.0)


def _lower_bounds(lb_logits):
    return pl.pallas_call(
        _lb_kernel,
        out_shape=jax.ShapeDtypeStruct(lb_logits.shape, F32),
        name="hgrn_lower_bounds",
    )(lb_logits)


def _rope(t, cos, sin_signed, first_half):
    up = pltpu.roll(t, LANES - ROPE_FREQS, 1)
    down = pltpu.roll(t, ROPE_FREQS, 1)
    return t * cos + jnp.where(first_half, up, down) * sin_signed


def _proj_kernel(x_ref, sh_ref, sc_ref, gpre_ref, w_ref, cos_ref, sin_ref, gn_ref, ws_ref, bs_ref,
                 lb_ref, qr_ref, qp_ref, kr_ref, v_ref, bx_ref, mx_ref, dq_ref, di_ref,
                 lff_ref, kf_ref, lfb_ref, kb_ref):
    tm = x_ref.shape[1]
    hb = _normed_input(x_ref[0], gpre_ref[...], sc_ref[0], sh_ref[0]).astype(BF16)

    def proj(name):
        j = _PROJ_NAMES.index(name)
        return _dot(hb, w_ref[:, j * BRANCH_W:(j + 1) * BRANCH_W])

    cos = cos_ref[...]
    sin_signed = sin_ref[...]
    first_half = (_lane_iota((tm, LANES)) % (2 * ROPE_FREQS)) < ROPE_FREQS

    def rope_all(t):
        return jnp.concatenate(
            [_rope(t[:, p * LANES:(p + 1) * LANES], cos, sin_signed, first_half) for p in range(N_PAIRS)],
            axis=1)

    for name, d, lf_ref, k_ref in (('d_f_fwd', 0, lff_ref, kf_ref), ('d_f_bwd', 1, lfb_ref, kb_ref)):
        z = proj(name)
        lb = lb_ref[0, d:d + 1, :]
        sg = jax.nn.sigmoid(z)
        lf_ref[0] = jnp.log(jnp.maximum(lb + (1.0 - lb) * sg, F_FLOOR))
        k_ref[0] = ((1.0 - lb) * (1.0 - sg)).astype(BF16)

    cv = proj('c_v')
    vn = (cv * lax.rsqrt(jnp.mean(cv * cv, axis=-1, keepdims=True) + EPS) * gn_ref[...]).astype(BF16)
    low_group = _lane_iota((GMLP_CHUNK, LANES)) < (LANES // 2)
    for ch in range(tm // GMLP_CHUNK):
        r0 = ch * GMLP_CHUNK
        for p in range(N_PAIRS):
            slab = vn[r0:r0 + GMLP_CHUNK, p * LANES:(p + 1) * LANES]
            mixed = jnp.where(low_group, _dot(ws_ref[2 * p], slab), _dot(ws_ref[2 * p + 1], slab))
            mixed = mixed + bs_ref[:, p * LANES:(p + 1) * LANES]
            mx_ref[0, r0:r0 + GMLP_CHUNK, p * LANES:(p + 1) * LANES] = mixed.astype(BF16)

    q = proj('a_q') * (HEAD_DIM ** -0.5 * LOG2E)
    q_rot = rope_all(q)
    kr_ref[0] = rope_all(proj('a_k')).astype(BF16)
    v = proj('a_v')
    for ch in range(tm // NA_VCHUNK):
        rs = slice(ch * NA_VCHUNK, (ch + 1) * NA_VCHUNK)
        qp_ref[0, ch] = q[rs].T.astype(BF16)
        qr_ref[0, ch] = q_rot[rs].T.astype(BF16)
        v_ref[0, ch] = v[rs].T.astype(BF16)
    bx_ref[0] = proj('b_x').astype(BF16)
    dq_ref[0] = proj('d_q').astype(BF16)
    di_ref[0] = proj('d_i').astype(BF16)


def _proj_call(x, sh, sc, gpre, w_proj, cos, sin_signed, gn, ws, bs_tab, lb, tm, per_batch_mod):
    b, n, _ = x.shape
    nt = n // tm
    mod_map = (lambda i, j: (i, 0, 0)) if per_batch_mod else (lambda i, j: (0, 0, 0))
    row_spec = pl.BlockSpec((1, tm, BRANCH_W), lambda i, j: (i, j, 0))
    vt_spec = pl.BlockSpec((1, tm // NA_VCHUNK, BRANCH_W, NA_VCHUNK), lambda i, j: (i, j, 0, 0))
    bf = jax.ShapeDtypeStruct((b, n, BRANCH_W), BF16)
    vt = jax.ShapeDtypeStruct((b, n // NA_VCHUNK, BRANCH_W, NA_VCHUNK), BF16)
    f32 = jax.ShapeDtypeStruct((b, n, BRANCH_W), F32)
    return pl.pallas_call(
        _proj_kernel,
        out_shape=(vt, vt, bf, vt, bf, bf, bf, bf, f32, bf, f32, bf),
        grid=(b, nt),
        in_specs=[pl.BlockSpec((1, tm, D_MODEL), lambda i, j: (i, j, 0)),
                  pl.BlockSpec((1, 1, D_MODEL), mod_map),
                  pl.BlockSpec((1, 1, D_MODEL), mod_map),
                  _resident((1, D_MODEL)),
                  _resident((D_MODEL, len(_PROJ_NAMES) * BRANCH_W)),
                  pl.BlockSpec((tm, LANES), lambda i, j: (j, 0)),
                  pl.BlockSpec((tm, LANES), lambda i, j: (j, 0)),
                  _resident((1, BRANCH_W)),
                  _resident((GMLP_GROUPS, GMLP_CHUNK, GMLP_CHUNK)),
                  _resident((GMLP_CHUNK, BRANCH_W)),
                  _resident((1, 2, BRANCH_W))],
        out_specs=(vt_spec, vt_spec, row_spec, vt_spec) + (row_spec,) * 8,
        compiler_params=_cparams(("arbitrary", "arbitrary")),
        name="branch_proj",
    )(x, sh, sc, gpre, w_proj, cos, sin_signed, gn, ws, bs_tab, lb)


def _nattn_kernel(qr_ref, qp_ref, k_ref, vt_ref, kc_ref, vct_ref, bias_ref, o_ref):
    rows = k_ref.shape[1] // GRID_W
    r0 = pl.program_id(1) * NA_TILE_ROWS
    kb0 = jnp.clip(r0 - NA_WIN_H // 2, 0, rows - NA_BAND_ROWS)
    start = pl.multiple_of(kb0 * GRID_W, NA_VCHUNK)
    c0 = kb0 // (NA_VCHUNK // GRID_W)
    band = NA_BAND_ROWS * GRID_W
    nq = NA_TILE_ROWS * GRID_W

    def lanes(p):
        return slice(p * LANES, (p + 1) * LANES)

    head0_rows = lax.broadcasted_iota(jnp.int32, (LANES, nq), 0) < HEAD_DIM

    def stack_heads(qt):
        zero = jnp.zeros_like(qt)
        return jnp.concatenate([jnp.where(head0_rows, qt, zero), jnp.where(head0_rows, zero, qt)], axis=1)

    def scores(p):
        s_ctx = _dot(kc_ref[0, :, lanes(p)], stack_heads(qp_ref[0, 0, lanes(p), :]))
        s_band = _dot(k_ref[0, pl.ds(start, band), lanes(p)], stack_heads(qr_ref[0, 0, lanes(p), :]))
        return s_ctx, s_band + bias_ref[0, p]

    def softmax(s_ctx, s_band):
        p_ctx, p_band = [], []
        for cb in range(2 * nq // LANES):
            cs = slice(cb * LANES, (cb + 1) * LANES)
            sc, sb = s_ctx[:, cs], s_band[:, cs]
            mx = jnp.maximum(jnp.max(sc, axis=0, keepdims=True), jnp.max(sb, axis=0, keepdims=True))
            p_ctx.append(jnp.exp2(sc - mx).astype(BF16))
            p_band.append(jnp.exp2(sb - mx).astype(BF16))
        return jnp.concatenate(p_ctx, axis=1), jnp.concatenate(p_band, axis=1)

    ones_rows = jnp.ones((NA_DEN_ROWS, NA_VCHUNK), BF16)

    def values(p, p_ctx, p_band):
        outs = []
        for hh in range(2):
            qs = slice(hh * nq, (hh + 1) * nq)
            ch = slice(p * LANES + hh * HEAD_DIM, p * LANES + (hh + 1) * HEAD_DIM)
            acc = None
            for j in range(vct_ref.shape[1]):
                lhs = jnp.concatenate([vct_ref[0, j, ch, :], ones_rows], axis=0)
                term = _dot(lhs, p_ctx[j * NA_VCHUNK:(j + 1) * NA_VCHUNK, qs])
                acc = term if acc is None else acc + term
            for j in range(band // NA_VCHUNK):
                lhs = jnp.concatenate([vt_ref[0, c0 + j, ch, :], ones_rows], axis=0)
                acc = acc + _dot(lhs, p_band[j * NA_VCHUNK:(j + 1) * NA_VCHUNK, qs])
            outs.append(acc[0:HEAD_DIM] * (1.0 / acc[HEAD_DIM:HEAD_DIM + 1]))
        o_ref[0, :, lanes(p)] = jnp.concatenate(outs, axis=0).T.astype(BF16)

    s_val, p_val = {}, {}
    for t in range(N_PAIRS + 2):
        if t < N_PAIRS:
            s_val[t] = scores(t)
        if 0 <= t - 1 < N_PAIRS:
            p_val[t - 1] = softmax(*s_val.pop(t - 1))
        if 0 <= t - 2 < N_PAIRS:
            values(t - 2, *p_val.pop(t - 2))


def _nattn_call(qrt, qpt, k, vt, kc, vct, bias):
    b, n, _ = k.shape
    rows = n // GRID_W
    nt = rows // NA_TILE_ROWS
    lc = kc.shape[1]
    nq = NA_TILE_ROWS * GRID_W

    def bias_map(i, t):
        return (jnp.where(t == 0, 0, jnp.where(t == nt - 1, 2, 1)), 0, 0, 0)

    assert nq == NA_VCHUNK
    q_spec = pl.BlockSpec((1, 1, BRANCH_W, NA_VCHUNK), lambda i, t: (i, t, 0, 0))
    o_spec = pl.BlockSpec((1, nq, BRANCH_W), lambda i, t: (i, t, 0))
    full = pl.BlockSpec((1, n, BRANCH_W), lambda i, t: (i, 0, 0))
    full_t = pl.BlockSpec((1, n // NA_VCHUNK, BRANCH_W, NA_VCHUNK), lambda i, t: (i, 0, 0, 0))
    ctx = pl.BlockSpec((1, lc, BRANCH_W), lambda i, t: (i, 0, 0))
    ctx_t = pl.BlockSpec((1, lc // NA_VCHUNK, BRANCH_W, NA_VCHUNK), lambda i, t: (i, 0, 0, 0))
    return pl.pallas_call(
        _nattn_kernel,
        out_shape=jax.ShapeDtypeStruct((b, n, BRANCH_W), BF16),
        grid=(b, nt),
        in_specs=[q_spec, q_spec, full, full_t, ctx, ctx_t,
                  pl.BlockSpec((1, N_PAIRS, NA_BAND_ROWS * GRID_W, 2 * nq), bias_map)],
        out_specs=o_spec,
        compiler_params=_cparams(("arbitrary", "arbitrary")),
        name="neighbourhood_attention",
    )(qrt, qpt, k, vt, kc, vct, bias)


def _cattn_kernel(qt_ref, k_ref, vt_ref, o_ref):
    lc = k_ref.shape[1]
    head0_rows = lax.broadcasted_iota(jnp.int32, (LANES, lc), 0) < HEAD_DIM
    for p in range(N_PAIRS):
        ls = slice(p * LANES, (p + 1) * LANES)
        qt = jnp.concatenate([qt_ref[0, j, ls, :] for j in range(qt_ref.shape[1])], axis=1)
        vt = jnp.concatenate([vt_ref[0, j, ls, :] for j in range(vt_ref.shape[1])], axis=1)
        k = k_ref[0, :, ls]
        zero = jnp.zeros_like(qt)
        outs = []
        for hh in range(2):
            s = _dot(k, jnp.where(head0_rows, qt, zero) if hh == 0 else jnp.where(head0_rows, zero, qt))
            e = jnp.exp2(s - jnp.max(s, axis=0, keepdims=True))
            acc = _dot(vt[hh * HEAD_DIM:(hh + 1) * HEAD_DIM], e.astype(BF16))
            outs.append(acc * (1.0 / jnp.sum(e, axis=0, keepdims=True)))
        o_ref[0, :, ls] = jnp.concatenate(outs, axis=0).T.astype(BF16)


def _cattn_call(qt, k, vt):
    b, lc, _ = k.shape
    spec = pl.BlockSpec((1, lc, BRANCH_W), lambda i: (i, 0, 0))
    spec_t = pl.BlockSpec((1, lc // NA_VCHUNK, BRANCH_W, NA_VCHUNK), lambda i: (i, 0, 0, 0))
    return pl.pallas_call(
        _cattn_kernel,
        out_shape=jax.ShapeDtypeStruct((b, lc, BRANCH_W), BF16),
        grid=(b,),
        in_specs=[spec_t, spec, spec_t],
        out_specs=spec,
        compiler_params=_cparams(("arbitrary",)),
        name="context_attention",
    )(qt, k, vt)


def _attention_bias(rpb, rows):
    col = jnp.arange(GRID_W)
    col_start = jnp.clip(col - NA_WIN_W // 2, 0, GRID_W - NA_WIN_W)
    valid = (col[None, :] >= col_start[:, None]) & (col[None, :] < col_start[:, None] + NA_WIN_W)
    col_idx = jnp.clip(col[None, :] - col[:, None] + NA_WIN_W - 1, 0, 2 * NA_WIN_W - 2)
    per_row = jnp.where(valid[None, None], rpb.astype(F32)[:, :, col_idx] * LOG2E, NEG_INF)
    tabs = []
    for r0 in (0, NA_TILE_ROWS, rows - NA_TILE_ROWS):
        kb0 = int(np.clip(r0 - NA_WIN_H // 2, 0, rows - NA_BAND_ROWS))
        r = r0 + np.arange(NA_TILE_ROWS)[:, None]
        kr = kb0 + np.arange(NA_BAND_ROWS)[None, :]
        rs = np.clip(r - NA_WIN_H // 2, 0, rows - NA_WIN_H)
        in_win = (kr >= rs) & (kr < rs + NA_WIN_H)
        idx = np.clip(kr - r + NA_WIN_H - 1, 0, 2 * NA_WIN_H - 2)
        t = jnp.where(in_win[None, :, :, None, None], per_row[:, idx], NEG_INF)
        t = t.reshape(N_PAIRS, 2, NA_TILE_ROWS, NA_BAND_ROWS, GRID_W, GRID_W).transpose(0, 3, 5, 1, 2, 4)
        tabs.append(t.reshape(N_PAIRS, NA_BAND_ROWS * GRID_W, 2 * NA_TILE_ROWS * GRID_W))
    return jnp.stack(tabs)


_KRON = 8


@functools.lru_cache(maxsize=None)
def _fourier_consts(n):
    rows = n // GRID_W
    k1 = np.arange(rows)[:, None, None, None]
    l1 = np.arange(_KRON)[None, :, None, None]
    n1 = np.arange(rows)[None, None, :, None]
    l2 = np.arange(_KRON)[None, None, None, :]
    a_cos, a_sin = [], []
    for j in range(GRID_W // _KRON):
        ang = 2.0 * np.pi * k1 * (GRID_W * n1 + _KRON * j + l1) / n
        same = (l1 == l2)
        a_cos.append((np.cos(ang) * same).reshape(rows * _KRON, rows * _KRON))
        a_sin.append((-np.sin(ang) * same).reshape(rows * _KRON, rows * _KRON))
    k2 = np.arange(GRID_W)[:, None, None, None]
    ang = 2.0 * np.pi * k2 * np.arange(GRID_W)[None, None, None, :] / GRID_W
    same = (np.arange(_KRON)[None, :, None, None] == np.arange(_KRON)[None, None, :, None])
    b_cos = (np.cos(ang) * same).reshape(GRID_W * _KRON, _KRON * GRID_W)
    b_sin = (np.sin(ang) * same).reshape(GRID_W * _KRON, _KRON * GRID_W)
    b_re = np.concatenate([b_cos, b_sin], axis=1)
    b_im = np.concatenate([-b_sin, b_cos], axis=1)
    return (np.stack(a_cos).astype(np.float32), np.stack(a_sin).astype(np.float32),
            b_re.astype(np.float32), b_im.astype(np.float32))


@functools.lru_cache(maxsize=None)
def _channel_dft():
    c = np.arange(FNET_GROUP_W)
    ang = 2.0 * np.pi * np.outer(c, c) / FNET_GROUP_W
    return np.concatenate([np.cos(ang), np.sin(ang)], axis=0).astype(np.float32)


@functools.lru_cache(maxsize=None)
def _dense_dft(n):
    t = np.arange(n)
    ang = 2.0 * np.pi * np.outer(t, t) / n
    return np.cos(ang).astype(np.float32), (-np.sin(ang)).astype(np.float32)


def _fold_channel_map(cs_ref, wf_ref, fold_ref, norm):
    c_hi, c_lo = _split2(cs_ref[...] * norm)
    for g in range(FNET_GROUPS):
        w_hi, w_lo = _split2(wf_ref[g])
        fold_ref[g] = (_dot(c_hi, w_hi) + _dot(c_hi, w_lo) + _dot(c_lo, w_hi)).astype(BF16)


def _channel_stage(xr, xi, fold_ref):
    outs = []
    for g in range(FNET_GROUPS):
        ls = slice(g * FNET_GROUP_W, (g + 1) * FNET_GROUP_W)
        xg = jnp.concatenate([xr[:, ls], xi[:, ls]], axis=1).astype(BF16)
        outs.append(_dot(xg, fold_ref[g]))
    return jnp.concatenate(outs, axis=1)


def _fourier_kernel(x_ref, ac_ref, as_ref, bre_ref, bim_ref, cs_ref, wf_ref, o_ref, s_ref, fold_ref, *, norm):
    @pl.when(pl.program_id(0) == 0)
    def _():
        _fold_channel_map(cs_ref, wf_ref, fold_ref, norm)

    rows = x_ref.shape[1]
    blk = rows * _KRON
    pair = 2 * _KRON
    for jj in range(GRID_W // pair):
        xt = x_ref[0, :, jj * pair:(jj + 1) * pair, :].astype(F32)
        re, im = [], []
        for half in range(2):
            xc = xt[:, half * _KRON:(half + 1) * _KRON, :].reshape(blk, BRANCH_W).astype(BF16)
            re.append(_dot(ac_ref[2 * jj + half], xc).reshape(rows, _KRON, BRANCH_W))
            im.append(_dot(as_ref[2 * jj + half], xc).reshape(rows, _KRON, BRANCH_W))
        s_ref[0, :, jj * pair:(jj + 1) * pair, :] = jnp.concatenate(re, axis=1).astype(BF16)
        s_ref[1, :, jj * pair:(jj + 1) * pair, :] = jnp.concatenate(im, axis=1).astype(BF16)
    sblk = _KRON * GRID_W
    for mm in range(rows // pair):
        ys = []
        for half in range(2):
            m0 = (2 * mm + half) * _KRON
            rhs = jnp.concatenate([s_ref[0, m0:m0 + _KRON].reshape(sblk, BRANCH_W),
                                   s_ref[1, m0:m0 + _KRON].reshape(sblk, BRANCH_W)], axis=0)
            xr = _dot(bre_ref[...], rhs)
            xi = _dot(bim_ref[...], rhs)
            ys.append(_channel_stage(xr, xi, fold_ref).reshape(GRID_W, _KRON, BRANCH_W))
        o_ref[0, :, mm * pair:(mm + 1) * pair, :] = jnp.concatenate(ys, axis=1).astype(BF16)


def _fourier_call(bx, wf):
    b, n, _ = bx.shape
    rows = n // GRID_W
    a_cos, a_sin, b_re, b_im = (jnp.asarray(t, BF16) for t in _fourier_consts(n))
    cs = jnp.asarray(_channel_dft(), F32)
    norm = float(1.0 / np.sqrt(n * FNET_GROUP_W))
    x4 = bx.reshape(b, rows, GRID_W, BRANCH_W)
    out = pl.pallas_call(
        functools.partial(_fourier_kernel, norm=norm),
        out_shape=jax.ShapeDtypeStruct((b, GRID_W, rows, BRANCH_W), BF16),
        grid=(b,),
        in_specs=[pl.BlockSpec((1, rows, GRID_W, BRANCH_W), lambda i: (i, 0, 0, 0)),
                  _resident(a_cos.shape), _resident(a_sin.shape),
                  _resident(b_re.shape), _resident(b_im.shape),
                  _resident(cs.shape), _resident(wf.shape)],
        out_specs=pl.BlockSpec((1, GRID_W, rows, BRANCH_W), lambda i: (i, 0, 0, 0)),
        scratch_shapes=[pltpu.VMEM((2, rows, GRID_W, BRANCH_W), BF16),
                        pltpu.VMEM((FNET_GROUPS, 2 * FNET_GROUP_W, FNET_GROUP_W), BF16)],
        compiler_params=_cparams(("arbitrary",)),
        name="fourier_mix",
    )(x4, a_cos, a_sin, b_re, b_im, cs, wf)
    return out.reshape(b, n, BRANCH_W)


def _fourier_ctx_kernel(x_ref, c_ref, s_ref, cs_ref, wf_ref, o_ref, fold_ref, *, norm):
    @pl.when(pl.program_id(0) == 0)
    def _():
        _fold_channel_map(cs_ref, wf_ref, fold_ref, norm)

    x = x_ref[0]
    xr = _dot(c_ref[...], x)
    xi = _dot(s_ref[...], x)
    o_ref[0] = _channel_stage(xr, xi, fold_ref).astype(BF16)


def _fourier_ctx_call(bx, wf):
    b, n, _ = bx.shape
    cn, sn = (jnp.asarray(t, BF16) for t in _dense_dft(n))
    cs = jnp.asarray(_channel_dft(), F32)
    norm = float(1.0 / np.sqrt(n * FNET_GROUP_W))
    spec = pl.BlockSpec((1, n, BRANCH_W), lambda i: (i, 0, 0))
    return pl.pallas_call(
        functools.partial(_fourier_ctx_kernel, norm=norm),
        out_shape=jax.ShapeDtypeStruct((b, n, BRANCH_W), BF16),
        grid=(b,),
        in_specs=[spec, _resident(cn.shape), _resident(sn.shape), _resident(cs.shape), _resident(wf.shape)],
        out_specs=spec,
        scratch_shapes=[pltpu.VMEM((FNET_GROUPS, 2 * FNET_GROUP_W, FNET_GROUP_W), BF16)],
        compiler_params=_cparams(("arbitrary",)),
        name="fourier_mix_context",
    )(bx, cn, sn, cs, wf)


def _block_diag(t):
    lo = _lane_iota(t.shape) < HEAD_DIM
    z = jnp.zeros_like(t)
    return jnp.concatenate([jnp.where(lo, t, z), jnp.where(lo, z, t)], axis=0)


def _hgrn_needed(j, reverse):
    nsub = HGRN_CHUNK // HGRN_SUB
    return list(range(0, j + 1)) if reverse else list(range(j, nsub))


def _hgrn_prepare(q, k, i, a, reverse):
    c = HGRN_CHUNK
    nsub = c // HGRN_SUB

    def level(r):
        return a[r:r + 1, :]

    zero_row = jnp.zeros((1, BRANCH_W), F32)
    if reverse:
        refs = [level((s + 1) * HGRN_SUB) if s + 1 < nsub else zero_row for s in range(nsub)]
        a_end = a[0:1, :]
    else:
        refs = [level(s * HGRN_SUB - 1) if s > 0 else zero_row for s in range(nsub)]
        a_end = a[c - 1:c, :]
    ref_rows = jnp.concatenate([jnp.broadcast_to(r, (HGRN_SUB, BRANCH_W)) for r in refs], axis=0)
    qf = q.astype(F32)
    kf = k.astype(F32)
    lift = ref_rows - a
    k_own = (kf * jnp.exp2(jnp.minimum(lift, HGRN_EXP_CLAMP))).astype(BF16)

    def q_variant(j):
        parts = []
        for s in _hgrn_needed(j, reverse):
            rs = slice(s * HGRN_SUB, (s + 1) * HGRN_SUB)
            parts.append((qf[rs] * jnp.exp2(a[rs] - refs[j])).astype(BF16))
        return jnp.concatenate(parts, axis=0)

    q_var = [q_variant(j) for j in range(nsub)]
    return dict(
        q_stack=jnp.concatenate(q_var, axis=0),
        q_in=q_var[nsub - 1] if reverse else q_var[0],
        k_own=k_own,
        k_out=(kf * jnp.exp2(a_end - a)).astype(BF16),
        decay_end=jnp.exp2(a_end),
        max_lift=jnp.max(lift, axis=0, keepdims=True),
        i=i, reverse=reverse)


def _hgrn_scores(ops):
    c = HGRN_CHUNK
    nsub = c // HGRN_SUB
    reverse = ops['reverse']
    src = _lane_iota((c, LANES)) % HEAD_DIM
    step = lax.broadcasted_iota(jnp.int32, (c, LANES), 0)
    seen = (src >= step) if reverse else (src <= step)
    src_sub = (_lane_iota((HGRN_SUB, LANES)) % HEAD_DIM) // HGRN_SUB
    where_blk, off = {}, 0
    for j in range(nsub):
        for s in _hgrn_needed(j, reverse):
            where_blk[(j, s)] = off
            off += HGRN_SUB
    out = []
    for p in range(N_PAIRS):
        ls = slice(p * LANES, (p + 1) * LANES)
        res = _dot_nt(ops['q_stack'][:, ls], _block_diag(ops['k_own'][:, ls]))
        rows = []
        for s in range(nsub):
            blk = None
            for j in range(nsub):
                if (j, s) in where_blk:
                    piece = res[where_blk[(j, s)]:where_blk[(j, s)] + HGRN_SUB]
                    blk = piece if blk is None else jnp.where(src_sub == j, piece, blk)
            rows.append(blk)
        out.append(jnp.where(seen, jnp.concatenate(rows, axis=0), 0.0).astype(BF16))
    return out


def _hgrn_local(ops, scores):
    low_rows = lax.broadcasted_iota(jnp.int32, (LANES, LANES), 0) < HEAD_DIM
    same_head = low_rows == (_lane_iota((LANES, LANES)) < HEAD_DIM)
    o_intra, upd = [], []
    for p in range(N_PAIRS):
        ls = slice(p * LANES, (p + 1) * LANES)
        ip = ops['i'][:, ls]
        o_intra.append(_dot(scores[p], _block_diag(ip)))
        upd.append(jnp.where(same_head, _dot_tn(ip, ops['k_out'][:, ls]), 0.0))
    return o_intra, upd


def _hgrn_carry(ops, o_intra, upd, state_ref, d):
    outs = []
    for p in range(N_PAIRS):
        ls = slice(p * LANES, (p + 1) * LANES)
        st = state_ref[d, p]
        outs.append(o_intra[p] + _dot_nt(ops['q_in'][:, ls], st.astype(BF16)))
        state_ref[d, p] = ops['decay_end'][:, ls] * st + upd[p]
    return jnp.concatenate(outs, axis=1)


def _hgrn_exact_tile(q_ref, k_ref, i_ref, a_ref, o_ref, state_ref, d, reverse, q_sc, k_sc, i_sc):
    c = HGRN_CHUNK
    nchunk = q_ref.shape[1] // c
    lane_head = _lane_iota((BRANCH_W, BRANCH_W)) // HEAD_DIM
    row_head = lax.broadcasted_iota(jnp.int32, (BRANCH_W, BRANCH_W), 0) // HEAD_DIM
    head_sum = (lane_head == row_head).astype(BF16)
    step = lax.broadcasted_iota(jnp.int32, (c, BRANCH_W), 0)
    low_rows = lax.broadcasted_iota(jnp.int32, (LANES, LANES), 0) < HEAD_DIM
    same_head = low_rows == (_lane_iota((LANES, LANES)) < HEAD_DIM)
    for cix in (range(nchunk - 1, -1, -1) if reverse else range(nchunk)):
        rs = slice(cix * c, (cix + 1) * c)
        q_sc[...] = q_ref[0, rs, :].astype(F32)
        k_sc[...] = k_ref[0, rs, :].astype(F32)
        i_sc[...] = i_ref[0, rs, :].astype(F32)
        qf = q_sc[...]
        kf = k_sc[...]
        a = a_ref[d, rs, :]

        def one_source(s, acc, a=a, qf=qf, cix=cix):
            a_s = a_ref[d, pl.ds(cix * c + s, 1), :]
            w = qf * (k_sc[pl.ds(s, 1), :] * jnp.exp2(jnp.minimum(a - a_s, 0.0)))
            w = jnp.where((step <= s) if reverse else (step >= s), w, 0.0)
            hi, lo = _split2(w)
            return acc + (_dot(hi, head_sum) + _dot(lo, head_sum)) * i_sc[pl.ds(s, 1), :]

        o_intra = lax.fori_loop(0, c, one_source, jnp.zeros((c, BRANCH_W), F32))
        a_end = a[0:1, :] if reverse else a[c - 1:c, :]
        q_in = (qf * jnp.exp2(a)).astype(BF16)
        k_out = (kf * jnp.exp2(a_end - a)).astype(BF16)
        decay_end = jnp.exp2(a_end)
        ib = i_ref[0, rs, :]
        outs = []
        for p in range(N_PAIRS):
            ls = slice(p * LANES, (p + 1) * LANES)
            st = state_ref[d, p]
            outs.append(o_intra[:, ls] + _dot_nt(q_in[:, ls], st.astype(BF16)))
            upd = jnp.where(same_head, _dot_tn(ib[:, ls], k_out[:, ls]), 0.0)
            state_ref[d, p] = decay_end[:, ls] * st + upd
        o_ref[0, rs, :] = jnp.concatenate(outs, axis=1).astype(BF16)


def _hgrn_kernel(qf_ref, if_ref, lff_ref, kf_ref, qb_ref, ib_ref, lfb_ref, kb_ref, s0f_ref, s0b_ref,
                 of_ref, ob_ref, sf_ref, sb_ref, state_ref, backup_ref, a_ref, q_sc, k_sc, i_sc):
    j = pl.program_id(1)
    nchunk = qf_ref.shape[1] // HGRN_CHUNK

    @pl.when(j == 0)
    def _():
        state_ref[0] = s0f_ref[0]
        state_ref[1] = s0b_ref[0]

    backup_ref[...] = state_ref[...]

    tm = qf_ref.shape[1]
    row = lax.broadcasted_iota(jnp.int32, (tm, tm), 0)
    col = lax.broadcasted_iota(jnp.int32, (tm, tm), 1)
    same_chunk = (row // HGRN_CHUNK) == (col // HGRN_CHUNK)

    def cum(lf, reverse):
        tri = (same_chunk & ((col >= row) if reverse else (col <= row))).astype(BF16)
        hi, lo = _split2(lf * LOG2E)
        return _dot(tri, hi) + _dot(tri, lo)

    a_f = cum(lff_ref[0], False)
    a_b = cum(lfb_ref[0], True)

    todo = []
    for cix in range(nchunk):
        bix = nchunk - 1 - cix
        todo.append((0, of_ref, slice(cix * HGRN_CHUNK, (cix + 1) * HGRN_CHUNK), qf_ref, kf_ref, if_ref, a_f, False))
        todo.append((1, ob_ref, slice(bix * HGRN_CHUNK, (bix + 1) * HGRN_CHUNK), qb_ref, kb_ref, ib_ref, a_b, True))
    ops, scores, local, lifts = {}, {}, {}, []
    for t in range(len(todo) + 3):
        if t < len(todo):
            _, _, rs, q_ref, k_ref, i_ref, a, reverse = todo[t]
            ops[t] = _hgrn_prepare(q_ref[0, rs, :], k_ref[0, rs, :], i_ref[0, rs, :], a[rs, :], reverse)
            lifts.append(ops[t]['max_lift'])
        if 0 <= t - 1 < len(todo):
            scores[t - 1] = _hgrn_scores(ops[t - 1])
        if 0 <= t - 2 < len(todo):
            local[t - 2] = _hgrn_local(ops[t - 2], scores.pop(t - 2))
        if 0 <= t - 3 < len(todo):
            d, o_ref, rs = todo[t - 3][:3]
            o_ref[0, rs, :] = _hgrn_carry(ops.pop(t - 3), *local.pop(t - 3), state_ref, d).astype(BF16)

    @pl.when(jnp.max(functools.reduce(jnp.maximum, lifts)) > HGRN_EXP_CLAMP)
    def _():
        state_ref[...] = backup_ref[...]
        a_ref[0] = a_f
        a_ref[1] = a_b
        _hgrn_exact_tile(qf_ref, kf_ref, if_ref, a_ref, of_ref, state_ref, 0, False, q_sc, k_sc, i_sc)
        _hgrn_exact_tile(qb_ref, kb_ref, ib_ref, a_ref, ob_ref, state_ref, 1, True, q_sc, k_sc, i_sc)

    @pl.when(j == pl.num_programs(1) - 1)
    def _():
        sf_ref[0] = state_ref[0]
        sb_ref[0] = state_ref[1]


def _hgrn_call(q, i, lff, kf, lfb, kb, s0f, s0b, tm):
    b, n, _ = q.shape
    nt = n // tm
    fwd = pl.BlockSpec((1, tm, BRANCH_W), lambda bi, j: (bi, j, 0))
    bwd = pl.BlockSpec((1, tm, BRANCH_W), lambda bi, j: (bi, nt - 1 - j, 0))
    st = pl.BlockSpec((1, N_PAIRS, LANES, LANES), lambda bi, j: (bi, 0, 0, 0))
    o_shape = jax.ShapeDtypeStruct((b, n, BRANCH_W), BF16)
    s_shape = jax.ShapeDtypeStruct((b, N_PAIRS, LANES, LANES), F32)
    return pl.pallas_call(
        _hgrn_kernel,
        out_shape=(o_shape, o_shape, s_shape, s_shape),
        grid=(b, nt),
        in_specs=[fwd, fwd, fwd, fwd, bwd, bwd, bwd, bwd, st, st],
        out_specs=(fwd, bwd, st, st),
        scratch_shapes=[pltpu.VMEM((2, N_PAIRS, LANES, LANES), F32),
                        pltpu.VMEM((2, N_PAIRS, LANES, LANES), F32),
                        pltpu.VMEM((2, tm, BRANCH_W), F32),
                        pltpu.VMEM((HGRN_CHUNK, BRANCH_W), F32),
                        pltpu.VMEM((HGRN_CHUNK, BRANCH_W), F32),
                        pltpu.VMEM((HGRN_CHUNK, BRANCH_W), F32)],
        compiler_params=_cparams(("arbitrary", "arbitrary")),
        name="hgrn_scan",
    )(q, i, lff, kf, q, i, lfb, kb, s0f, s0b)


def _merge_kernel(x_ref, sh_ref, sc_ref, gt_ref, gpre_ref, gpost_ref, w_ref, oa_ref, ob_ref, mx_ref,
                  of_ref, obk_ref, hn_ref, hm_ref, wb_ref, wo_ref, o_ref):
    x = x_ref[0]
    hb = _normed_input(x, gpre_ref[...], sc_ref[0], sh_ref[0]).astype(BF16)

    def proj(name):
        j = _MERGE_NAMES.index(name)
        return _dot(hb, w_ref[:, j * BRANCH_W:(j + 1) * BRANCH_W])

    ya = oa_ref[0].astype(F32) * _silu(proj('a_g'))
    yb = ob_ref[0].astype(F32) * _silu(proj('b_g'))
    yc = proj('c_u') * mx_ref[0].astype(F32) * _silu(proj('c_g'))
    o = of_ref[0].astype(F32) + obk_ref[0].astype(F32)
    ms = _dot((o * o).astype(BF16), hm_ref[...])
    yd = o * lax.rsqrt(ms + EPS) * hn_ref[...] * _silu(proj('d_g'))

    g0 = len(_MERGE_NAMES) * BRANCH_W
    merged = None
    for r, y in enumerate((ya, yb, yc, yd)):
        gate = _dot(hb, w_ref[:, g0 + r * D_MODEL:g0 + (r + 1) * D_MODEL])
        term = jax.nn.sigmoid(gate) * _dot(y.astype(BF16), wb_ref[r])
        merged = term if merged is None else merged + term
    out = _dot(merged.astype(BF16), wo_ref[...])
    post = out * lax.rsqrt(jnp.mean(out * out, axis=-1, keepdims=True) + EPS) * gpost_ref[...]
    o_ref[0] = x + gt_ref[0] * post


def _merge_call(x, sh, sc, gt, gpre, gpost, w_merge, oa, ob, mx, of, obk, hn, hmean, wb, wo, tm,
                per_batch_mod):
    b, n, _ = x.shape
    nt = n // tm
    mod_map = (lambda i, j: (i, 0, 0)) if per_batch_mod else (lambda i, j: (0, 0, 0))
    x_spec = pl.BlockSpec((1, tm, D_MODEL), lambda i, j: (i, j, 0))
    br_spec = pl.BlockSpec((1, tm, BRANCH_W), lambda i, j: (i, j, 0))
    mod_spec = pl.BlockSpec((1, 1, D_MODEL), mod_map)
    return pl.pallas_call(
        _merge_kernel,
        out_shape=jax.ShapeDtypeStruct((b, n, D_MODEL), F32),
        grid=(b, nt),
        in_specs=[x_spec, mod_spec, mod_spec, mod_spec,
                  _resident((1, D_MODEL)), _resident((1, D_MODEL)),
                  _resident(w_merge.shape),
                  br_spec, br_spec, br_spec, br_spec, br_spec,
                  _resident((1, BRANCH_W)), _resident((BRANCH_W, BRANCH_W)),
                  _resident(wb.shape), _resident(wo.shape)],
        out_specs=x_spec,
        compiler_params=_cparams(("arbitrary", "arbitrary")),
        name="branch_merge",
    )(x, sh, sc, gt, gpre, gpost, w_merge, oa, ob, mx, of, obk, hn, hmean, wb, wo)


def _rope_tables(n_tok, rotate):
    if not rotate:
        return jnp.ones((n_tok, LANES), F32), jnp.zeros((n_tok, LANES), F32)
    t = jnp.arange(n_tok, dtype=jnp.int32)
    pos = jnp.stack([t // GRID_W, t % GRID_W], axis=-1).astype(F32)
    inv = ROPE_THETA ** (-jnp.arange(ROPE_FREQS, dtype=F32) * 2.0 / (2 * ROPE_FREQS))
    ang = pos[:, :, None] * inv
    cos = jnp.repeat(jnp.cos(ang)[:, :, None, :], 2, axis=2).reshape(n_tok, HEAD_DIM)
    sin = jnp.sin(ang)
    sin_signed = jnp.stack([-sin, sin], axis=2).reshape(n_tok, HEAD_DIM)
    return jnp.tile(cos, (1, 2)), jnp.tile(sin_signed, (1, 2))


def _gather_cols(w_in_l, names):
    return jnp.concatenate([w_in_l[:, _IN_COL[nm] * BRANCH_W:(_IN_COL[nm] + 1) * BRANCH_W] for nm in names],
                           axis=1)


def _row_tile(n):
    return 512 if n % 512 == 0 else 256


def kernel(x, c, ctx, c_ctx, w_ada, b_ada, g_pre, g_post, w_in, na_rpb, fnet_w, gmlp_norm_g, gmlp_ws,
           gmlp_bs, hgrn_lb_logits, hgrn_norm_g, w_branch, w_out):
    batch, n_tok, _ = x.shape
    n_ctx = ctx.shape[1]
    depth = w_in.shape[0]

    w_in_b = w_in.astype(BF16)
    w_proj = [_gather_cols(w_in_b[l], _PROJ_NAMES) for l in range(depth)]
    w_merge = [jnp.concatenate([_gather_cols(w_in_b[l], _MERGE_NAMES), w_in_b[l][:, _GATE_COL0:]], axis=1)
               for l in range(depth)]
    w_branch_b = w_branch.astype(BF16)
    w_out_b = w_out.astype(BF16)
    gmlp_ws_b = gmlp_ws.astype(BF16)
    bs_tab = jnp.repeat(jnp.swapaxes(gmlp_bs, 1, 2), BRANCH_W // GMLP_GROUPS, axis=2)
    head_mean = jnp.asarray(np.kron(np.eye(N_HEADS), np.ones((HEAD_DIM, HEAD_DIM)) / HEAD_DIM), BF16)
    cos_x, sin_x = _rope_tables(n_tok, True)
    cos_c, sin_c = _rope_tables(n_ctx, False)

    c_all = jnp.concatenate([c, jnp.broadcast_to(c_ctx[None, :], (8, D_MODEL))], axis=0)
    mod = _modulation(c_all, w_ada, b_ada)
    lower = _lower_bounds(hgrn_lb_logits)

    zero_state = jnp.zeros((batch, N_PAIRS, LANES, LANES), F32)
    tm_x = _row_tile(n_tok)
    tm_c = _row_tile(n_ctx)

    for l in range(depth):
        with_ctx = l < depth - 1
        mod_x = [mod[l, :batch, i * D_MODEL:(i + 1) * D_MODEL].reshape(batch, 1, D_MODEL) for i in range(3)]
        mod_c = [mod[l, batch:batch + 1, i * D_MODEL:(i + 1) * D_MODEL].reshape(1, 1, D_MODEL) for i in range(3)]
        gpre = g_pre[l].reshape(1, D_MODEL)
        gpost = g_post[l].reshape(1, D_MODEL)
        gn = gmlp_norm_g[l].reshape(1, BRANCH_W)
        hn = hgrn_norm_g[l].reshape(1, BRANCH_W)
        lb = lower[l].reshape(1, 2, BRANCH_W)
        bias = _attention_bias(na_rpb[l], n_tok // GRID_W)

        (_, qp_c, k_c, v_c, bx_c, mx_c, dq_c, di_c, lff_c, kf_c, lfb_c, kb_c) = _proj_call(
            ctx, mod_c[0], mod_c[1], gpre, w_proj[l], cos_c, sin_c, gn, gmlp_ws_b[l], bs_tab[l], lb,
            tm_c, False)
        of_c, ob_c, st_f, st_b = _hgrn_call(dq_c, di_c, lff_c, kf_c, lfb_c, kb_c, zero_state, zero_state,
                                            min(HGRN_ROWS, n_ctx))

        (qr, qp, k, v, bx, mx, dq, di, lff, kf, lfb, kb) = _proj_call(
            x, mod_x[0], mod_x[1], gpre, w_proj[l], cos_x, sin_x, gn, gmlp_ws_b[l], bs_tab[l], lb,
            tm_x, True)
        oa = _nattn_call(qr, qp, k, v, k_c, v_c, bias)
        ob = _fourier_call(bx, fnet_w[l])
        of, obk, _, _ = _hgrn_call(dq, di, lff, kf, lfb, kb, st_f, st_b, min(HGRN_ROWS, n_tok))
        x = _merge_call(x, mod_x[0], mod_x[1], mod_x[2], gpre, gpost, w_merge[l], oa, ob, mx, of, obk,
                        hn, head_mean, w_branch_b[l], w_out_b[l], tm_x, True)

        if with_ctx:
            oa_c = _cattn_call(qp_c, k_c, v_c)
            ob_c2 = _fourier_ctx_call(bx_c, fnet_w[l])
            ctx = _merge_call(ctx, mod_c[0], mod_c[1], mod_c[2], gpre, gpost, w_merge[l], oa_c, ob_c2, mx_c,
                              of_c, ob_c, hn, head_mean, w_branch_b[l], w_out_b[l], tm_c, False)
    return x
```

```python
import functools

import numpy as np
import jax
import jax.numpy as jnp
from jax import lax
from jax.experimental import pallas as pl
from jax.experimental.pallas import tpu as pltpu

F32 = jnp.float32
BF16 = jnp.bfloat16

D_MODEL = 1024
BRANCH_W = 512
N_BRANCH = 4
GRID_W = 64
HEAD_DIM = 64
N_HEADS = 8
LANES = 128
N_PAIRS = BRANCH_W // LANES
NA_WIN_H = 8
NA_WIN_W = 16
NA_TILE_ROWS = 4
NA_BAND_ROWS = 12
NA_VCHUNK = 256
NA_DEN_ROWS = 16
LOG2E = 1.4426950408889634
ROPE_THETA = 10000.0
ROPE_FREQS = 16
FNET_GROUPS = 4
FNET_GROUP_W = 128
GMLP_CHUNK = 128
GMLP_GROUPS = 8
HGRN_CHUNK = 64
HGRN_SUB = 16
HGRN_EXP_CLAMP = 115.0
HGRN_ROWS = 256
HGRN_BATCH_ROWS = 2
EPS = 1e-6
F_FLOOR = 1e-30
NEG_INF = -1e30

VMEM_LIMIT = 56 * 2**20

_IN_COL = {'a_q': 0, 'a_k': 1, 'a_v': 2, 'a_g': 3, 'b_x': 4, 'b_g': 5, 'c_u': 6, 'c_v': 7, 'c_g': 8,
           'd_q': 9, 'd_f_fwd': 10, 'd_f_bwd': 11, 'd_i': 12, 'd_g': 13}
_PROJ_NAMES = ('a_q', 'a_k', 'a_v', 'b_x', 'c_v', 'd_q', 'd_f_fwd', 'd_f_bwd', 'd_i')
_MERGE_NAMES = ('c_u', 'a_g', 'b_g', 'c_g', 'd_g')
_GATE_COL0 = 14 * BRANCH_W


def _cparams(sem):
    return pltpu.CompilerParams(dimension_semantics=sem, vmem_limit_bytes=VMEM_LIMIT)


def _resident(shape):
    nd = len(shape)
    return pl.BlockSpec(shape, lambda *_: (0,) * nd, pipeline_mode=pl.Buffered(1))


def _silu(t):
    return t * jax.nn.sigmoid(t)


def _lane_iota(shape):
    return lax.broadcasted_iota(jnp.int32, shape, len(shape) - 1)


def _dot(a, b):
    return jnp.dot(a, b, preferred_element_type=F32)


def _dot_nt(a, b):
    return lax.dot_general(a, b, (((1,), (1,)), ((), ())), preferred_element_type=F32)


def _dot_tn(a, b):
    return lax.dot_general(a, b, (((0,), (0,)), ((), ())), preferred_element_type=F32)


def _split2(t):
    hi = t.astype(BF16)
    return hi, (t - hi.astype(F32)).astype(BF16)


def _normed_input(x, gpre, sc, sh):
    ms = jnp.mean(x * x, axis=-1, keepdims=True)
    h = x * lax.rsqrt(ms + EPS) * gpre
    return h * (1.0 + sc) + sh


def _mod_kernel(c_ref, w_ref, b_ref, o_ref):
    s = _silu(c_ref[...]).astype(BF16)
    o_ref[0] = _dot(s, w_ref[0].astype(BF16)) + b_ref[0]


def _modulation(c_all, w_ada, b_ada):
    depth = w_ada.shape[0]
    rows = c_all.shape[0]
    tn = 1024
    return pl.pallas_call(
        _mod_kernel,
        out_shape=jax.ShapeDtypeStruct((depth, rows, 3 * D_MODEL), F32),
        grid=(depth, 3 * D_MODEL // tn),
        in_specs=[pl.BlockSpec((rows, D_MODEL), lambda l, j: (0, 0)),
                  pl.BlockSpec((1, D_MODEL, tn), lambda l, j: (l, 0, j)),
                  pl.BlockSpec((1, 1, tn), lambda l, j: (l, 0, j))],
        out_specs=pl.BlockSpec((1, rows, tn), lambda l, j: (l, 0, j)),
        compiler_params=_cparams(("arbitrary", "arbitrary")),
        name="adaln_modulation",
    )(c_all, w_ada, b_ada.reshape(depth, 1, 3 * D_MODEL))


def _lb_kernel(lg_ref, o_ref):
    depth = lg_ref.shape[0]
    lg = [lg_ref[l] for l in range(depth)]
    m = functools.reduce(jnp.maximum, lg)
    e = [jnp.exp(t - m) for t in lg]
    tot = functools.reduce(lambda a, b: a + b, e)
    sm = [t / tot for t in e]
    run = jnp.zeros_like(sm[0])
    for l in range(depth):
        run = run + sm[l]
        o_ref[l] = jnp.maximum(run - sm[0], 0.0)


def _lower_bounds(lb_logits):
    return pl.pallas_call(
        _lb_kernel,
        out_shape=jax.ShapeDtypeStruct(lb_logits.shape, F32),
        name="hgrn_lower_bounds",
    )(lb_logits)


def _rope(t, cos, sin_signed, first_half):
    up = pltpu.roll(t, LANES - ROPE_FREQS, 1)
    down = pltpu.roll(t, ROPE_FREQS, 1)
    return t * cos + jnp.where(first_half, up, down) * sin_signed


def _proj_kernel(x_ref, sh_ref, sc_ref, gpre_ref, w_ref, cos_ref, sin_ref, gn_ref, ws_ref, bs_ref,
                 lb_ref, qr_ref, qp_ref, kr_ref, v_ref, bx_ref, mx_ref, dq_ref, di_ref,
                 lff_ref, kf_ref, lfb_ref, kb_ref):
    tm = x_ref.shape[1]
    hb = _normed_input(x_ref[0], gpre_ref[...], sc_ref[0], sh_ref[0]).astype(BF16)

    def proj(name):
        j = _PROJ_NAMES.index(name)
        return _dot(hb, w_ref[:, j * BRANCH_W:(j + 1) * BRANCH_W])

    cos = cos_ref[...]
    sin_signed = sin_ref[...]
    first_half = (_lane_iota((tm, LANES)) % (2 * ROPE_FREQS)) < ROPE_FREQS

    def rope_all(t):
        return jnp.concatenate(
            [_rope(t[:, p * LANES:(p + 1) * LANES], cos, sin_signed, first_half) for p in range(N_PAIRS)],
            axis=1)

    for name, d, lf_ref, k_ref in (('d_f_fwd', 0, lff_ref, kf_ref), ('d_f_bwd', 1, lfb_ref, kb_ref)):
        z = proj(name)
        lb = lb_ref[0, d:d + 1, :]
        sg = jax.nn.sigmoid(z)
        lf_ref[0] = jnp.log(jnp.maximum(lb + (1.0 - lb) * sg, F_FLOOR))
        k_ref[0] = ((1.0 - lb) * (1.0 - sg)).astype(BF16)

    cv = proj('c_v')
    vn = (cv * lax.rsqrt(jnp.mean(cv * cv, axis=-1, keepdims=True) + EPS) * gn_ref[...]).astype(BF16)
    low_group = _lane_iota((GMLP_CHUNK, LANES)) < (LANES // 2)
    for ch in range(tm // GMLP_CHUNK):
        r0 = ch * GMLP_CHUNK
        for p in range(N_PAIRS):
            slab = vn[r0:r0 + GMLP_CHUNK, p * LANES:(p + 1) * LANES]
            zero = jnp.zeros_like(slab)
            stacked = jnp.concatenate([jnp.where(low_group, slab, zero), jnp.where(low_group, zero, slab)], axis=0)
            mixed = _dot(ws_ref[p], stacked) + bs_ref[:, p * LANES:(p + 1) * LANES]
            mx_ref[0, r0:r0 + GMLP_CHUNK, p * LANES:(p + 1) * LANES] = mixed.astype(BF16)

    q = proj('a_q') * (HEAD_DIM ** -0.5 * LOG2E)
    q_rot = rope_all(q)
    kr_ref[0] = rope_all(proj('a_k')).astype(BF16)
    v = proj('a_v')
    for ch in range(tm // NA_VCHUNK):
        rs = slice(ch * NA_VCHUNK, (ch + 1) * NA_VCHUNK)
        qp_ref[0, ch] = q[rs].T.astype(BF16)
        qr_ref[0, ch] = q_rot[rs].T.astype(BF16)
        v_ref[0, ch] = v[rs].T.astype(BF16)
    bx_ref[0] = proj('b_x').astype(BF16)
    dq_ref[0] = proj('d_q').astype(BF16)
    di_ref[0] = proj('d_i').astype(BF16)


def _proj_call(x, sh, sc, gpre, w_proj, cos, sin_signed, gn, ws, bs_tab, lb, tm, per_batch_mod):
    b, n, _ = x.shape
    nt = n // tm
    mod_map = (lambda i, j: (i, 0, 0)) if per_batch_mod else (lambda i, j: (0, 0, 0))
    row_spec = pl.BlockSpec((1, tm, BRANCH_W), lambda i, j: (i, j, 0))
    vt_spec = pl.BlockSpec((1, tm // NA_VCHUNK, BRANCH_W, NA_VCHUNK), lambda i, j: (i, j, 0, 0))
    bf = jax.ShapeDtypeStruct((b, n, BRANCH_W), BF16)
    vt = jax.ShapeDtypeStruct((b, n // NA_VCHUNK, BRANCH_W, NA_VCHUNK), BF16)
    f32 = jax.ShapeDtypeStruct((b, n, BRANCH_W), F32)
    return pl.pallas_call(
        _proj_kernel,
        out_shape=(vt, vt, bf, vt, bf, bf, bf, bf, f32, bf, f32, bf),
        grid=(b, nt),
        in_specs=[pl.BlockSpec((1, tm, D_MODEL), lambda i, j: (i, j, 0)),
                  pl.BlockSpec((1, 1, D_MODEL), mod_map),
                  pl.BlockSpec((1, 1, D_MODEL), mod_map),
                  _resident((1, D_MODEL)),
                  _resident((D_MODEL, len(_PROJ_NAMES) * BRANCH_W)),
                  pl.BlockSpec((tm, LANES), lambda i, j: (j, 0)),
                  pl.BlockSpec((tm, LANES), lambda i, j: (j, 0)),
                  _resident((1, BRANCH_W)),
                  _resident((GMLP_GROUPS // 2, GMLP_CHUNK, 2 * GMLP_CHUNK)),
                  _resident((GMLP_CHUNK, BRANCH_W)),
                  _resident((1, 2, BRANCH_W))],
        out_specs=(vt_spec, vt_spec, row_spec, vt_spec) + (row_spec,) * 8,
        compiler_params=_cparams(("arbitrary", "arbitrary")),
        name="branch_proj",
    )(x, sh, sc, gpre, w_proj, cos, sin_signed, gn, ws, bs_tab, lb)


def _nattn_kernel(qr_ref, qp_ref, k_ref, vt_ref, kc_ref, vct_ref, bias_ref, o_ref):
    rows = k_ref.shape[1] // GRID_W
    r0 = pl.program_id(1) * NA_TILE_ROWS
    kb0 = jnp.clip(r0 - NA_WIN_H // 2, 0, rows - NA_BAND_ROWS)
    start = pl.multiple_of(kb0 * GRID_W, NA_VCHUNK)
    c0 = kb0 // (NA_VCHUNK // GRID_W)
    band = NA_BAND_ROWS * GRID_W
    nq = NA_TILE_ROWS * GRID_W

    def lanes(p):
        return slice(p * LANES, (p + 1) * LANES)

    head0_rows = lax.broadcasted_iota(jnp.int32, (LANES, nq), 0) < HEAD_DIM

    def stack_heads(qt):
        zero = jnp.zeros_like(qt)
        return jnp.concatenate([jnp.where(head0_rows, qt, zero), jnp.where(head0_rows, zero, qt)], axis=1)

    def scores(p):
        s_ctx = _dot(kc_ref[0, :, lanes(p)], stack_heads(qp_ref[0, 0, lanes(p), :]))
        s_band = _dot(k_ref[0, pl.ds(start, band), lanes(p)], stack_heads(qr_ref[0, 0, lanes(p), :]))
        return s_ctx, s_band + bias_ref[0, p]

    def softmax(s_ctx, s_band):
        p_ctx, p_band = [], []
        for cb in range(2 * nq // LANES):
            cs = slice(cb * LANES, (cb + 1) * LANES)
            sc, sb = s_ctx[:, cs], s_band[:, cs]
            mx = jnp.maximum(jnp.max(sc, axis=0, keepdims=True), jnp.max(sb, axis=0, keepdims=True))
            p_ctx.append(jnp.exp2(sc - mx).astype(BF16))
            p_band.append(jnp.exp2(sb - mx).astype(BF16))
        return jnp.concatenate(p_ctx, axis=1), jnp.concatenate(p_band, axis=1)

    ones_rows = jnp.ones((NA_DEN_ROWS, NA_VCHUNK), BF16)

    def values(p, p_ctx, p_band):
        outs = []
        for hh in range(2):
            qs = slice(hh * nq, (hh + 1) * nq)
            ch = slice(p * LANES + hh * HEAD_DIM, p * LANES + (hh + 1) * HEAD_DIM)
            acc = None
            for j in range(vct_ref.shape[1]):
                lhs = jnp.concatenate([vct_ref[0, j, ch, :], ones_rows], axis=0)
                term = _dot(lhs, p_ctx[j * NA_VCHUNK:(j + 1) * NA_VCHUNK, qs])
                acc = term if acc is None else acc + term
            for j in range(band // NA_VCHUNK):
                lhs = jnp.concatenate([vt_ref[0, c0 + j, ch, :], ones_rows], axis=0)
                acc = acc + _dot(lhs, p_band[j * NA_VCHUNK:(j + 1) * NA_VCHUNK, qs])
            outs.append(acc[0:HEAD_DIM] * (1.0 / acc[HEAD_DIM:HEAD_DIM + 1]))
        o_ref[0, :, lanes(p)] = jnp.concatenate(outs, axis=0).T.astype(BF16)

    s_val, p_val = {}, {}
    for t in range(N_PAIRS + 2):
        if t < N_PAIRS:
            s_val[t] = scores(t)
        if 0 <= t - 1 < N_PAIRS:
            p_val[t - 1] = softmax(*s_val.pop(t - 1))
        if 0 <= t - 2 < N_PAIRS:
            values(t - 2, *p_val.pop(t - 2))


def _nattn_call(qrt, qpt, k, vt, kc, vct, bias):
    b, n, _ = k.shape
    rows = n // GRID_W
    nt = rows // NA_TILE_ROWS
    lc = kc.shape[1]
    nq = NA_TILE_ROWS * GRID_W

    def bias_map(i, t):
        return (jnp.where(t == 0, 0, jnp.where(t == nt - 1, 2, 1)), 0, 0, 0)

    assert nq == NA_VCHUNK
    q_spec = pl.BlockSpec((1, 1, BRANCH_W, NA_VCHUNK), lambda i, t: (i, t, 0, 0))
    o_spec = pl.BlockSpec((1, nq, BRANCH_W), lambda i, t: (i, t, 0))
    full = pl.BlockSpec((1, n, BRANCH_W), lambda i, t: (i, 0, 0))
    full_t = pl.BlockSpec((1, n // NA_VCHUNK, BRANCH_W, NA_VCHUNK), lambda i, t: (i, 0, 0, 0))
    ctx = pl.BlockSpec((1, lc, BRANCH_W), lambda i, t: (i, 0, 0))
    ctx_t = pl.BlockSpec((1, lc // NA_VCHUNK, BRANCH_W, NA_VCHUNK), lambda i, t: (i, 0, 0, 0))
    return pl.pallas_call(
        _nattn_kernel,
        out_shape=jax.ShapeDtypeStruct((b, n, BRANCH_W), BF16),
        grid=(b, nt),
        in_specs=[q_spec, q_spec, full, full_t, ctx, ctx_t,
                  pl.BlockSpec((1, N_PAIRS, NA_BAND_ROWS * GRID_W, 2 * nq), bias_map)],
        out_specs=o_spec,
        compiler_params=_cparams(("arbitrary", "arbitrary")),
        name="neighbourhood_attention",
    )(qrt, qpt, k, vt, kc, vct, bias)


def _cattn_kernel(qt_ref, k_ref, vt_ref, o_ref):
    lc = k_ref.shape[1]
    head0_rows = lax.broadcasted_iota(jnp.int32, (LANES, lc), 0) < HEAD_DIM
    for p in range(N_PAIRS):
        ls = slice(p * LANES, (p + 1) * LANES)
        qt = jnp.concatenate([qt_ref[0, j, ls, :] for j in range(qt_ref.shape[1])], axis=1)
        vt = jnp.concatenate([vt_ref[0, j, ls, :] for j in range(vt_ref.shape[1])], axis=1)
        k = k_ref[0, :, ls]
        zero = jnp.zeros_like(qt)
        outs = []
        for hh in range(2):
            s = _dot(k, jnp.where(head0_rows, qt, zero) if hh == 0 else jnp.where(head0_rows, zero, qt))
            e = jnp.exp2(s - jnp.max(s, axis=0, keepdims=True))
            acc = _dot(vt[hh * HEAD_DIM:(hh + 1) * HEAD_DIM], e.astype(BF16))
            outs.append(acc * (1.0 / jnp.sum(e, axis=0, keepdims=True)))
        o_ref[0, :, ls] = jnp.concatenate(outs, axis=0).T.astype(BF16)


def _cattn_call(qt, k, vt):
    b, lc, _ = k.shape
    spec = pl.BlockSpec((1, lc, BRANCH_W), lambda i: (i, 0, 0))
    spec_t = pl.BlockSpec((1, lc // NA_VCHUNK, BRANCH_W, NA_VCHUNK), lambda i: (i, 0, 0, 0))
    return pl.pallas_call(
        _cattn_kernel,
        out_shape=jax.ShapeDtypeStruct((b, lc, BRANCH_W), BF16),
        grid=(b,),
        in_specs=[spec_t, spec, spec_t],
        out_specs=spec,
        compiler_params=_cparams(("arbitrary",)),
        name="context_attention",
    )(qt, k, vt)


def _attention_bias(rpb, rows):
    col = jnp.arange(GRID_W)
    col_start = jnp.clip(col - NA_WIN_W // 2, 0, GRID_W - NA_WIN_W)
    valid = (col[None, :] >= col_start[:, None]) & (col[None, :] < col_start[:, None] + NA_WIN_W)
    col_idx = jnp.clip(col[None, :] - col[:, None] + NA_WIN_W - 1, 0, 2 * NA_WIN_W - 2)
    per_row = jnp.where(valid[None, None], rpb.astype(F32)[:, :, col_idx] * LOG2E, NEG_INF)
    tabs = []
    for r0 in (0, NA_TILE_ROWS, rows - NA_TILE_ROWS):
        kb0 = int(np.clip(r0 - NA_WIN_H // 2, 0, rows - NA_BAND_ROWS))
        r = r0 + np.arange(NA_TILE_ROWS)[:, None]
        kr = kb0 + np.arange(NA_BAND_ROWS)[None, :]
        rs = np.clip(r - NA_WIN_H // 2, 0, rows - NA_WIN_H)
        in_win = (kr >= rs) & (kr < rs + NA_WIN_H)
        idx = np.clip(kr - r + NA_WIN_H - 1, 0, 2 * NA_WIN_H - 2)
        t = jnp.where(in_win[None, :, :, None, None], per_row[:, idx], NEG_INF)
        t = t.reshape(N_PAIRS, 2, NA_TILE_ROWS, NA_BAND_ROWS, GRID_W, GRID_W).transpose(0, 3, 5, 1, 2, 4)
        tabs.append(t.reshape(N_PAIRS, NA_BAND_ROWS * GRID_W, 2 * NA_TILE_ROWS * GRID_W))
    return jnp.stack(tabs)


_KRON = 8


@functools.lru_cache(maxsize=None)
def _fourier_consts(n):
    rows = n // GRID_W
    k1 = np.arange(rows)[:, None, None, None]
    l1 = np.arange(_KRON)[None, :, None, None]
    n1 = np.arange(rows)[None, None, :, None]
    l2 = np.arange(_KRON)[None, None, None, :]
    a_cos, a_sin = [], []
    for j in range(GRID_W // _KRON):
        ang = 2.0 * np.pi * k1 * (GRID_W * n1 + _KRON * j + l1) / n
        same = (l1 == l2)
        a_cos.append((np.cos(ang) * same).reshape(rows * _KRON, rows * _KRON))
        a_sin.append((-np.sin(ang) * same).reshape(rows * _KRON, rows * _KRON))
    k2 = np.arange(GRID_W)[:, None, None, None]
    ang = 2.0 * np.pi * k2 * np.arange(GRID_W)[None, None, None, :] / GRID_W
    same = (np.arange(_KRON)[None, :, None, None] == np.arange(_KRON)[None, None, :, None])
    b_cos = (np.cos(ang) * same).reshape(GRID_W * _KRON, _KRON * GRID_W)
    b_sin = (np.sin(ang) * same).reshape(GRID_W * _KRON, _KRON * GRID_W)
    b_re = np.concatenate([b_cos, b_sin], axis=1)
    b_im = np.concatenate([-b_sin, b_cos], axis=1)
    return (np.stack(a_cos).astype(np.float32), np.stack(a_sin).astype(np.float32),
            b_re.astype(np.float32), b_im.astype(np.float32))


@functools.lru_cache(maxsize=None)
def _channel_dft():
    c = np.arange(FNET_GROUP_W)
    ang = 2.0 * np.pi * np.outer(c, c) / FNET_GROUP_W
    return np.concatenate([np.cos(ang), np.sin(ang)], axis=0).astype(np.float32)


@functools.lru_cache(maxsize=None)
def _dense_dft(n):
    t = np.arange(n)
    ang = 2.0 * np.pi * np.outer(t, t) / n
    return np.cos(ang).astype(np.float32), (-np.sin(ang)).astype(np.float32)


def _fold_channel_map(cs_ref, wf_ref, fold_ref, norm):
    c_hi, c_lo = _split2(cs_ref[...] * norm)
    for g in range(FNET_GROUPS):
        w_hi, w_lo = _split2(wf_ref[g])
        fold_ref[g] = (_dot(c_hi, w_hi) + _dot(c_hi, w_lo) + _dot(c_lo, w_hi)).astype(BF16)


def _channel_stage(xr, xi, fold_ref):
    outs = []
    for g in range(FNET_GROUPS):
        ls = slice(g * FNET_GROUP_W, (g + 1) * FNET_GROUP_W)
        xg = jnp.concatenate([xr[:, ls], xi[:, ls]], axis=1).astype(BF16)
        outs.append(_dot(xg, fold_ref[g]))
    return jnp.concatenate(outs, axis=1)


def _fourier_kernel(x_ref, ac_ref, as_ref, bre_ref, bim_ref, cs_ref, wf_ref, o_ref, s_ref, fold_ref, *, norm):
    @pl.when(pl.program_id(0) == 0)
    def _():
        _fold_channel_map(cs_ref, wf_ref, fold_ref, norm)

    rows = x_ref.shape[1]
    blk = rows * _KRON
    pair = 2 * _KRON
    for jj in range(GRID_W // pair):
        xt = x_ref[0, :, jj * pair:(jj + 1) * pair, :].astype(F32)
        re, im = [], []
        for half in range(2):
            xc = xt[:, half * _KRON:(half + 1) * _KRON, :].reshape(blk, BRANCH_W).astype(BF16)
            re.append(_dot(ac_ref[2 * jj + half], xc).reshape(rows, _KRON, BRANCH_W))
            im.append(_dot(as_ref[2 * jj + half], xc).reshape(rows, _KRON, BRANCH_W))
        s_ref[0, :, jj * pair:(jj + 1) * pair, :] = jnp.concatenate(re, axis=1).astype(BF16)
        s_ref[1, :, jj * pair:(jj + 1) * pair, :] = jnp.concatenate(im, axis=1).astype(BF16)
    sblk = _KRON * GRID_W
    for mm in range(rows // pair):
        ys = []
        for half in range(2):
            m0 = (2 * mm + half) * _KRON
            rhs = jnp.concatenate([s_ref[0, m0:m0 + _KRON].reshape(sblk, BRANCH_W),
                                   s_ref[1, m0:m0 + _KRON].reshape(sblk, BRANCH_W)], axis=0)
            xr = _dot(bre_ref[...], rhs)
            xi = _dot(bim_ref[...], rhs)
            ys.append(_channel_stage(xr, xi, fold_ref).reshape(GRID_W, _KRON, BRANCH_W))
        o_ref[0, :, mm * pair:(mm + 1) * pair, :] = jnp.concatenate(ys, axis=1).astype(BF16)


def _fourier_call(bx, wf):
    b, n, _ = bx.shape
    rows = n // GRID_W
    a_cos, a_sin, b_re, b_im = (jnp.asarray(t, BF16) for t in _fourier_consts(n))
    cs = jnp.asarray(_channel_dft(), F32)
    norm = float(1.0 / np.sqrt(n * FNET_GROUP_W))
    x4 = bx.reshape(b, rows, GRID_W, BRANCH_W)
    out = pl.pallas_call(
        functools.partial(_fourier_kernel, norm=norm),
        out_shape=jax.ShapeDtypeStruct((b, GRID_W, rows, BRANCH_W), BF16),
        grid=(b,),
        in_specs=[pl.BlockSpec((1, rows, GRID_W, BRANCH_W), lambda i: (i, 0, 0, 0)),
                  _resident(a_cos.shape), _resident(a_sin.shape),
                  _resident(b_re.shape), _resident(b_im.shape),
                  _resident(cs.shape), _resident(wf.shape)],
        out_specs=pl.BlockSpec((1, GRID_W, rows, BRANCH_W), lambda i: (i, 0, 0, 0)),
        scratch_shapes=[pltpu.VMEM((2, rows, GRID_W, BRANCH_W), BF16),
                        pltpu.VMEM((FNET_GROUPS, 2 * FNET_GROUP_W, FNET_GROUP_W), BF16)],
        compiler_params=_cparams(("arbitrary",)),
        name="fourier_mix",
    )(x4, a_cos, a_sin, b_re, b_im, cs, wf)
    return out.reshape(b, n, BRANCH_W)


def _fourier_ctx_kernel(x_ref, c_ref, s_ref, cs_ref, wf_ref, o_ref, fold_ref, *, norm):
    @pl.when(pl.program_id(0) == 0)
    def _():
        _fold_channel_map(cs_ref, wf_ref, fold_ref, norm)

    x = x_ref[0]
    xr = _dot(c_ref[...], x)
    xi = _dot(s_ref[...], x)
    o_ref[0] = _channel_stage(xr, xi, fold_ref).astype(BF16)


def _fourier_ctx_call(bx, wf):
    b, n, _ = bx.shape
    cn, sn = (jnp.asarray(t, BF16) for t in _dense_dft(n))
    cs = jnp.asarray(_channel_dft(), F32)
    norm = float(1.0 / np.sqrt(n * FNET_GROUP_W))
    spec = pl.BlockSpec((1, n, BRANCH_W), lambda i: (i, 0, 0))
    return pl.pallas_call(
        functools.partial(_fourier_ctx_kernel, norm=norm),
        out_shape=jax.ShapeDtypeStruct((b, n, BRANCH_W), BF16),
        grid=(b,),
        in_specs=[spec, _resident(cn.shape), _resident(sn.shape), _resident(cs.shape), _resident(wf.shape)],
        out_specs=spec,
        scratch_shapes=[pltpu.VMEM((FNET_GROUPS, 2 * FNET_GROUP_W, FNET_GROUP_W), BF16)],
        compiler_params=_cparams(("arbitrary",)),
        name="fourier_mix_context",
    )(bx, cn, sn, cs, wf)


def _block_diag(t):
    lo = _lane_iota(t.shape) < HEAD_DIM
    z = jnp.zeros_like(t)
    return jnp.concatenate([jnp.where(lo, t, z), jnp.where(lo, z, t)], axis=0)


def _hgrn_needed(j, reverse):
    nsub = HGRN_CHUNK // HGRN_SUB
    return list(range(0, j + 1)) if reverse else list(range(j, nsub))


def _hgrn_prepare(q, k, i, a, reverse):
    c = HGRN_CHUNK
    nsub = c // HGRN_SUB

    def level(r):
        return a[r:r + 1, :]

    zero_row = jnp.zeros((1, BRANCH_W), F32)
    if reverse:
        refs = [level((s + 1) * HGRN_SUB) if s + 1 < nsub else zero_row for s in range(nsub)]
        a_end = a[0:1, :]
    else:
        refs = [level(s * HGRN_SUB - 1) if s > 0 else zero_row for s in range(nsub)]
        a_end = a[c - 1:c, :]
    ref_rows = jnp.concatenate([jnp.broadcast_to(r, (HGRN_SUB, BRANCH_W)) for r in refs], axis=0)
    qf = q.astype(F32)
    kf = k.astype(F32)
    lift = ref_rows - a
    k_own = (kf * jnp.exp2(jnp.minimum(lift, HGRN_EXP_CLAMP))).astype(BF16)

    def q_variant(j):
        parts = []
        for s in _hgrn_needed(j, reverse):
            rs = slice(s * HGRN_SUB, (s + 1) * HGRN_SUB)
            parts.append((qf[rs] * jnp.exp2(a[rs] - refs[j])).astype(BF16))
        return jnp.concatenate(parts, axis=0)

    q_var = [q_variant(j) for j in range(nsub)]
    return dict(
        q_stack=jnp.concatenate(q_var, axis=0),
        q_in=q_var[nsub - 1] if reverse else q_var[0],
        k_own=k_own,
        k_out=(kf * jnp.exp2(a_end - a)).astype(BF16),
        decay_end=jnp.exp2(a_end),
        max_lift=jnp.max(lift, axis=0, keepdims=True),
        i=i, reverse=reverse)


def _hgrn_scores(ops):
    c = HGRN_CHUNK
    nsub = c // HGRN_SUB
    reverse = ops['reverse']
    src = _lane_iota((c, LANES)) % HEAD_DIM
    step = lax.broadcasted_iota(jnp.int32, (c, LANES), 0)
    seen = (src >= step) if reverse else (src <= step)
    src_sub = (_lane_iota((HGRN_SUB, LANES)) % HEAD_DIM) // HGRN_SUB
    where_blk, off = {}, 0
    for j in range(nsub):
        for s in _hgrn_needed(j, reverse):
            where_blk[(j, s)] = off
            off += HGRN_SUB
    out = []
    for p in range(N_PAIRS):
        ls = slice(p * LANES, (p + 1) * LANES)
        res = _dot_nt(ops['q_stack'][:, ls], _block_diag(ops['k_own'][:, ls]))
        rows = []
        for s in range(nsub):
            blk = None
            for j in range(nsub):
                if (j, s) in where_blk:
                    piece = res[where_blk[(j, s)]:where_blk[(j, s)] + HGRN_SUB]
                    blk = piece if blk is None else jnp.where(src_sub == j, piece, blk)
            rows.append(blk)
        out.append(jnp.where(seen, jnp.concatenate(rows, axis=0), 0.0).astype(BF16))
    return out


def _hgrn_local(ops, scores):
    low_rows = lax.broadcasted_iota(jnp.int32, (LANES, LANES), 0) < HEAD_DIM
    same_head = low_rows == (_lane_iota((LANES, LANES)) < HEAD_DIM)
    o_intra, upd = [], []
    for p in range(N_PAIRS):
        ls = slice(p * LANES, (p + 1) * LANES)
        ip = ops['i'][:, ls]
        o_intra.append(_dot(scores[p], _block_diag(ip)))
        upd.append(jnp.where(same_head, _dot_tn(ip, ops['k_out'][:, ls]), 0.0))
    return o_intra, upd


def _hgrn_carry(ops, o_intra, upd, state_ref, d):
    outs = []
    for p in range(N_PAIRS):
        ls = slice(p * LANES, (p + 1) * LANES)
        st = state_ref[d, p]
        outs.append(o_intra[p] + _dot_nt(ops['q_in'][:, ls], st.astype(BF16)))
        state_ref[d, p] = ops['decay_end'][:, ls] * st + upd[p]
    return jnp.concatenate(outs, axis=1)


def _hgrn_exact_tile(q_ref, k_ref, i_ref, a_ref, o_ref, state_ref, bb, d, reverse, q_sc, k_sc, i_sc):
    c = HGRN_CHUNK
    nchunk = q_ref.shape[1] // c
    lane_head = _lane_iota((BRANCH_W, BRANCH_W)) // HEAD_DIM
    row_head = lax.broadcasted_iota(jnp.int32, (BRANCH_W, BRANCH_W), 0) // HEAD_DIM
    head_sum = (lane_head == row_head).astype(BF16)
    step = lax.broadcasted_iota(jnp.int32, (c, BRANCH_W), 0)
    low_rows = lax.broadcasted_iota(jnp.int32, (LANES, LANES), 0) < HEAD_DIM
    same_head = low_rows == (_lane_iota((LANES, LANES)) < HEAD_DIM)
    for cix in (range(nchunk - 1, -1, -1) if reverse else range(nchunk)):
        rs = slice(cix * c, (cix + 1) * c)
        q_sc[...] = q_ref[bb, rs, :].astype(F32)
        k_sc[...] = k_ref[bb, rs, :].astype(F32)
        i_sc[...] = i_ref[bb, rs, :].astype(F32)
        qf = q_sc[...]
        kf = k_sc[...]
        a = a_ref[d, rs, :]

        def one_source(s, acc, a=a, qf=qf, cix=cix):
            a_s = a_ref[d, pl.ds(cix * c + s, 1), :]
            w = qf * (k_sc[pl.ds(s, 1), :] * jnp.exp2(jnp.minimum(a - a_s, 0.0)))
            w = jnp.where((step <= s) if reverse else (step >= s), w, 0.0)
            hi, lo = _split2(w)
            return acc + (_dot(hi, head_sum) + _dot(lo, head_sum)) * i_sc[pl.ds(s, 1), :]

        o_intra = lax.fori_loop(0, c, one_source, jnp.zeros((c, BRANCH_W), F32))
        a_end = a[0:1, :] if reverse else a[c - 1:c, :]
        q_in = (qf * jnp.exp2(a)).astype(BF16)
        k_out = (kf * jnp.exp2(a_end - a)).astype(BF16)
        decay_end = jnp.exp2(a_end)
        ib = i_ref[bb, rs, :]
        outs = []
        for p in range(N_PAIRS):
            ls = slice(p * LANES, (p + 1) * LANES)
            st = state_ref[d, p]
            outs.append(o_intra[:, ls] + _dot_nt(q_in[:, ls], st.astype(BF16)))
            upd = jnp.where(same_head, _dot_tn(ib[:, ls], k_out[:, ls]), 0.0)
            state_ref[d, p] = decay_end[:, ls] * st + upd
        o_ref[bb, rs, :] = jnp.concatenate(outs, axis=1).astype(BF16)


def _hgrn_kernel(qf_ref, if_ref, lff_ref, kf_ref, qb_ref, ib_ref, lfb_ref, kb_ref, s0f_ref, s0b_ref,
                 of_ref, ob_ref, sf_ref, sb_ref, state_ref, backup_ref, a_ref, q_sc, k_sc, i_sc):
    j = pl.program_id(1)
    nb = qf_ref.shape[0]
    nchunk = qf_ref.shape[1] // HGRN_CHUNK

    @pl.when(j == 0)
    def _():
        for bb in range(nb):
            state_ref[2 * bb] = s0f_ref[bb]
            state_ref[2 * bb + 1] = s0b_ref[bb]

    backup_ref[...] = state_ref[...]

    tm = qf_ref.shape[1]
    row = lax.broadcasted_iota(jnp.int32, (tm, tm), 0)
    col = lax.broadcasted_iota(jnp.int32, (tm, tm), 1)
    same_chunk = (row // HGRN_CHUNK) == (col // HGRN_CHUNK)

    def cum(lf, reverse):
        tri = (same_chunk & ((col >= row) if reverse else (col <= row))).astype(BF16)
        hi, lo = _split2(lf * LOG2E)
        return _dot(tri, hi) + _dot(tri, lo)

    a_f = [cum(lff_ref[bb], False) for bb in range(nb)]
    a_b = [cum(lfb_ref[bb], True) for bb in range(nb)]

    todo = []
    for cix in range(nchunk):
        bix = nchunk - 1 - cix
        for bb in range(nb):
            todo.append((bb, 2 * bb, of_ref, slice(cix * HGRN_CHUNK, (cix + 1) * HGRN_CHUNK),
                         qf_ref, kf_ref, if_ref, a_f[bb], False))
            todo.append((bb, 2 * bb + 1, ob_ref, slice(bix * HGRN_CHUNK, (bix + 1) * HGRN_CHUNK),
                         qb_ref, kb_ref, ib_ref, a_b[bb], True))
    ops, scores, local, lifts = {}, {}, {}, []
    for t in range(len(todo) + 3):
        if t < len(todo):
            bb, _, _, rs, q_ref, k_ref, i_ref, a, reverse = todo[t]
            ops[t] = _hgrn_prepare(q_ref[bb, rs, :], k_ref[bb, rs, :], i_ref[bb, rs, :], a[rs, :], reverse)
            lifts.append(ops[t]['max_lift'])
        if 0 <= t - 1 < len(todo):
            scores[t - 1] = _hgrn_scores(ops[t - 1])
        if 0 <= t - 2 < len(todo):
            local[t - 2] = _hgrn_local(ops[t - 2], scores.pop(t - 2))
        if 0 <= t - 3 < len(todo):
            bb, slot, o_ref, rs = todo[t - 3][:4]
            o_ref[bb, rs, :] = _hgrn_carry(ops.pop(t - 3), *local.pop(t - 3), state_ref, slot).astype(BF16)

    @pl.when(jnp.max(functools.reduce(jnp.maximum, lifts)) > HGRN_EXP_CLAMP)
    def _():
        state_ref[...] = backup_ref[...]
        for bb in range(nb):
            a_ref[2 * bb] = a_f[bb]
            a_ref[2 * bb + 1] = a_b[bb]
            _hgrn_exact_tile(qf_ref, kf_ref, if_ref, a_ref, of_ref, state_ref, bb, 2 * bb, False,
                             q_sc, k_sc, i_sc)
            _hgrn_exact_tile(qb_ref, kb_ref, ib_ref, a_ref, ob_ref, state_ref, bb, 2 * bb + 1, True,
                             q_sc, k_sc, i_sc)

    @pl.when(j == pl.num_programs(1) - 1)
    def _():
        for bb in range(nb):
            sf_ref[bb] = state_ref[2 * bb]
            sb_ref[bb] = state_ref[2 * bb + 1]


def _hgrn_call(q, i, lff, kf, lfb, kb, s0f, s0b, tm):
    b, n, _ = q.shape
    nt = n // tm
    nb = HGRN_BATCH_ROWS if b % HGRN_BATCH_ROWS == 0 else 1
    fwd = pl.BlockSpec((nb, tm, BRANCH_W), lambda bi, j: (bi, j, 0))
    bwd = pl.BlockSpec((nb, tm, BRANCH_W), lambda bi, j: (bi, nt - 1 - j, 0))
    st = pl.BlockSpec((nb, N_PAIRS, LANES, LANES), lambda bi, j: (bi, 0, 0, 0))
    o_shape = jax.ShapeDtypeStruct((b, n, BRANCH_W), BF16)
    s_shape = jax.ShapeDtypeStruct((b, N_PAIRS, LANES, LANES), F32)
    return pl.pallas_call(
        _hgrn_kernel,
        out_shape=(o_shape, o_shape, s_shape, s_shape),
        grid=(b // nb, nt),
        in_specs=[fwd, fwd, fwd, fwd, bwd, bwd, bwd, bwd, st, st],
        out_specs=(fwd, bwd, st, st),
        scratch_shapes=[pltpu.VMEM((2 * nb, N_PAIRS, LANES, LANES), F32),
                        pltpu.VMEM((2 * nb, N_PAIRS, LANES, LANES), F32),
                        pltpu.VMEM((2 * nb, tm, BRANCH_W), F32),
                        pltpu.VMEM((HGRN_CHUNK, BRANCH_W), F32),
                        pltpu.VMEM((HGRN_CHUNK, BRANCH_W), F32),
                        pltpu.VMEM((HGRN_CHUNK, BRANCH_W), F32)],
        compiler_params=_cparams(("arbitrary", "arbitrary")),
        name="hgrn_scan",
    )(q, i, lff, kf, q, i, lfb, kb, s0f, s0b)


def _merge_kernel(x_ref, sh_ref, sc_ref, gt_ref, gpre_ref, gpost_ref, w_ref, oa_ref, ob_ref, mx_ref,
                  of_ref, obk_ref, hn_ref, hm_ref, wb_ref, wo_ref, o_ref):
    x = x_ref[0]
    hb = _normed_input(x, gpre_ref[...], sc_ref[0], sh_ref[0]).astype(BF16)

    def proj(name):
        j = _MERGE_NAMES.index(name)
        return _dot(hb, w_ref[:, j * BRANCH_W:(j + 1) * BRANCH_W])

    ya = oa_ref[0].astype(F32) * _silu(proj('a_g'))
    yb = ob_ref[0].astype(F32) * _silu(proj('b_g'))
    yc = proj('c_u') * mx_ref[0].astype(F32) * _silu(proj('c_g'))
    o = of_ref[0].astype(F32) + obk_ref[0].astype(F32)
    ms = _dot((o * o).astype(BF16), hm_ref[...])
    yd = o * lax.rsqrt(ms + EPS) * hn_ref[...] * _silu(proj('d_g'))

    g0 = len(_MERGE_NAMES) * BRANCH_W
    merged = None
    for r, y in enumerate((ya, yb, yc, yd)):
        gate = _dot(hb, w_ref[:, g0 + r * D_MODEL:g0 + (r + 1) * D_MODEL])
        term = jax.nn.sigmoid(gate) * _dot(y.astype(BF16), wb_ref[r])
        merged = term if merged is None else merged + term
    out = _dot(merged.astype(BF16), wo_ref[...])
    post = out * lax.rsqrt(jnp.mean(out * out, axis=-1, keepdims=True) + EPS) * gpost_ref[...]
    o_ref[0] = x + gt_ref[0] * post


def _merge_call(x, sh, sc, gt, gpre, gpost, w_merge, oa, ob, mx, of, obk, hn, hmean, wb, wo, tm,
                per_batch_mod):
    b, n, _ = x.shape
    nt = n // tm
    mod_map = (lambda i, j: (i, 0, 0)) if per_batch_mod else (lambda i, j: (0, 0, 0))
    x_spec = pl.BlockSpec((1, tm, D_MODEL), lambda i, j: (i, j, 0))
    br_spec = pl.BlockSpec((1, tm, BRANCH_W), lambda i, j: (i, j, 0))
    mod_spec = pl.BlockSpec((1, 1, D_MODEL), mod_map)
    return pl.pallas_call(
        _merge_kernel,
        out_shape=jax.ShapeDtypeStruct((b, n, D_MODEL), F32),
        grid=(b, nt),
        in_specs=[x_spec, mod_spec, mod_spec, mod_spec,
                  _resident((1, D_MODEL)), _resident((1, D_MODEL)),
                  _resident(w_merge.shape),
                  br_spec, br_spec, br_spec, br_spec, br_spec,
                  _resident((1, BRANCH_W)), _resident((BRANCH_W, BRANCH_W)),
                  _resident(wb.shape), _resident(wo.shape)],
        out_specs=x_spec,
        compiler_params=_cparams(("arbitrary", "arbitrary")),
        name="branch_merge",
    )(x, sh, sc, gt, gpre, gpost, w_merge, oa, ob, mx, of, obk, hn, hmean, wb, wo)


def _rope_tables(n_tok, rotate):
    if not rotate:
        return jnp.ones((n_tok, LANES), F32), jnp.zeros((n_tok, LANES), F32)
    t = jnp.arange(n_tok, dtype=jnp.int32)
    pos = jnp.stack([t // GRID_W, t % GRID_W], axis=-1).astype(F32)
    inv = ROPE_THETA ** (-jnp.arange(ROPE_FREQS, dtype=F32) * 2.0 / (2 * ROPE_FREQS))
    ang = pos[:, :, None] * inv
    cos = jnp.repeat(jnp.cos(ang)[:, :, None, :], 2, axis=2).reshape(n_tok, HEAD_DIM)
    sin = jnp.sin(ang)
    sin_signed = jnp.stack([-sin, sin], axis=2).reshape(n_tok, HEAD_DIM)
    return jnp.tile(cos, (1, 2)), jnp.tile(sin_signed, (1, 2))


def _gather_cols(w_in_l, names):
    return jnp.concatenate([w_in_l[:, _IN_COL[nm] * BRANCH_W:(_IN_COL[nm] + 1) * BRANCH_W] for nm in names],
                           axis=1)


def _row_tile(n):
    return 512 if n % 512 == 0 else 256


def kernel(x, c, ctx, c_ctx, w_ada, b_ada, g_pre, g_post, w_in, na_rpb, fnet_w, gmlp_norm_g, gmlp_ws,
           gmlp_bs, hgrn_lb_logits, hgrn_norm_g, w_branch, w_out):
    batch, n_tok, _ = x.shape
    n_ctx = ctx.shape[1]
    depth = w_in.shape[0]

    w_in_b = w_in.astype(BF16)
    w_proj = [_gather_cols(w_in_b[l], _PROJ_NAMES) for l in range(depth)]
    w_merge = [jnp.concatenate([_gather_cols(w_in_b[l], _MERGE_NAMES), w_in_b[l][:, _GATE_COL0:]], axis=1)
               for l in range(depth)]
    w_branch_b = w_branch.astype(BF16)
    w_out_b = w_out.astype(BF16)
    gmlp_ws_b = gmlp_ws.astype(BF16).reshape(depth, GMLP_GROUPS // 2, 2, GMLP_CHUNK, GMLP_CHUNK)
    gmlp_ws_b = gmlp_ws_b.transpose(0, 1, 3, 2, 4).reshape(depth, GMLP_GROUPS // 2, GMLP_CHUNK, 2 * GMLP_CHUNK)
    bs_tab = jnp.repeat(jnp.swapaxes(gmlp_bs, 1, 2), BRANCH_W // GMLP_GROUPS, axis=2)
    head_mean = jnp.asarray(np.kron(np.eye(N_HEADS), np.ones((HEAD_DIM, HEAD_DIM)) / HEAD_DIM), BF16)
    cos_x, sin_x = _rope_tables(n_tok, True)
    cos_c, sin_c = _rope_tables(n_ctx, False)

    c_all = jnp.concatenate([c, jnp.broadcast_to(c_ctx[None, :], (8, D_MODEL))], axis=0)
    mod = _modulation(c_all, w_ada, b_ada)
    lower = _lower_bounds(hgrn_lb_logits)

    zero_state = jnp.zeros((batch, N_PAIRS, LANES, LANES), F32)
    tm_x = _row_tile(n_tok)
    tm_c = _row_tile(n_ctx)

    for l in range(depth):
        with_ctx = l < depth - 1
        mod_x = [mod[l, :batch, i * D_MODEL:(i + 1) * D_MODEL].reshape(batch, 1, D_MODEL) for i in range(3)]
        mod_c = [mod[l, batch:batch + 1, i * D_MODEL:(i + 1) * D_MODEL].reshape(1, 1, D_MODEL) for i in range(3)]
        gpre = g_pre[l].reshape(1, D_MODEL)
        gpost = g_post[l].reshape(1, D_MODEL)
        gn = gmlp_norm_g[l].reshape(1, BRANCH_W)
        hn = hgrn_norm_g[l].reshape(1, BRANCH_W)
        lb = lower[l].reshape(1, 2, BRANCH_W)
        bias = _attention_bias(na_rpb[l], n_tok // GRID_W)

        (_, qp_c, k_c, v_c, bx_c, mx_c, dq_c, di_c, lff_c, kf_c, lfb_c, kb_c) = _proj_call(
            ctx, mod_c[0], mod_c[1], gpre, w_proj[l], cos_c, sin_c, gn, gmlp_ws_b[l], bs_tab[l], lb,
            tm_c, False)
        of_c, ob_c, st_f, st_b = _hgrn_call(dq_c, di_c, lff_c, kf_c, lfb_c, kb_c, zero_state, zero_state,
                                            min(HGRN_ROWS, n_ctx))

        (qr, qp, k, v, bx, mx, dq, di, lff, kf, lfb, kb) = _proj_call(
            x, mod_x[0], mod_x[1], gpre, w_proj[l], cos_x, sin_x, gn, gmlp_ws_b[l], bs_tab[l], lb,
            tm_x, True)
        oa = _nattn_call(qr, qp, k, v, k_c, v_c, bias)
        ob = _fourier_call(bx, fnet_w[l])
        of, obk, _, _ = _hgrn_call(dq, di, lff, kf, lfb, kb, st_f, st_b, min(HGRN_ROWS, n_tok))
        x = _merge_call(x, mod_x[0], mod_x[1], mod_x[2], gpre, gpost, w_merge[l], oa, ob, mx, of, obk,
                        hn, head_mean, w_branch_b[l], w_out_b[l], tm_x, True)

        if with_ctx:
            oa_c = _cattn_call(qp_c, k_c, v_c)
            ob_c2 = _fourier_ctx_call(bx_c, fnet_w[l])
            ctx = _merge_call(ctx, mod_c[0], mod_c[1], mod_c[2], gpre, gpost, w_merge[l], oa_c, ob_c2, mx_c,
                              of_c, ob_c, hn, head_mean, w_branch_b[l], w_out_b[l], tm_c, False)
    return x
```

```python
import functools

import numpy as np
import jax
import jax.numpy as jnp
from jax import lax
from jax.experimental import pallas as pl
from jax.experimental.pallas import tpu as pltpu

F32 = jnp.float32
BF16 = jnp.bfloat16

D_MODEL = 1024
BRANCH_W = 512
N_BRANCH = 4
GRID_W = 64
HEAD_DIM = 64
N_HEADS = 8
LANES = 128
N_PAIRS = BRANCH_W // LANES
NA_WIN_H = 8
NA_WIN_W = 16
NA_TILE_ROWS = 4
NA_BAND_ROWS = 12
NA_VCHUNK = 256
NA_DEN_ROWS = 16
LOG2E = 1.4426950408889634
ROPE_THETA = 10000.0
ROPE_FREQS = 16
FNET_GROUPS = 4
FNET_GROUP_W = 128
GMLP_CHUNK = 128
GMLP_GROUPS = 8
HGRN_CHUNK = 64
HGRN_SUB = 16
HGRN_EXP_CLAMP = 115.0
HGRN_ROWS = 256
HGRN_BATCH_ROWS = 4
EPS = 1e-6
F_FLOOR = 1e-30
NEG_INF = -1e30

VMEM_LIMIT = 56 * 2**20

_IN_COL = {'a_q': 0, 'a_k': 1, 'a_v': 2, 'a_g': 3, 'b_x': 4, 'b_g': 5, 'c_u': 6, 'c_v': 7, 'c_g': 8,
           'd_q': 9, 'd_f_fwd': 10, 'd_f_bwd': 11, 'd_i': 12, 'd_g': 13}
_PROJ_NAMES = ('a_q', 'a_k', 'a_v', 'b_x', 'c_v', 'd_q', 'd_f_fwd', 'd_f_bwd', 'd_i')
_MERGE_NAMES = ('c_u', 'a_g', 'b_g', 'c_g', 'd_g')
_GATE_COL0 = 14 * BRANCH_W


def _cparams(sem):
    return pltpu.CompilerParams(dimension_semantics=sem, vmem_limit_bytes=VMEM_LIMIT)


def _resident(shape):
    nd = len(shape)
    return pl.BlockSpec(shape, lambda *_: (0,) * nd, pipeline_mode=pl.Buffered(1))


def _silu(t):
    return t * jax.nn.sigmoid(t)


def _lane_iota(shape):
    return lax.broadcasted_iota(jnp.int32, shape, len(shape) - 1)


def _dot(a, b):
    return jnp.dot(a, b, preferred_element_type=F32)


def _dot_nt(a, b):
    return lax.dot_general(a, b, (((1,), (1,)), ((), ())), preferred_element_type=F32)


def _dot_tn(a, b):
    return lax.dot_general(a, b, (((0,), (0,)), ((), ())), preferred_element_type=F32)


def _split2(t):
    hi = t.astype(BF16)
    return hi, (t - hi.astype(F32)).astype(BF16)


def _normed_input(x, gpre, sc, sh):
    ms = jnp.mean(x * x, axis=-1, keepdims=True)
    h = x * lax.rsqrt(ms + EPS) * gpre
    return h * (1.0 + sc) + sh


def _mod_kernel(c_ref, w_ref, b_ref, o_ref):
    s = _silu(c_ref[...]).astype(BF16)
    o_ref[0] = _dot(s, w_ref[0].astype(BF16)) + b_ref[0]


def _modulation(c_all, w_ada, b_ada):
    depth = w_ada.shape[0]
    rows = c_all.shape[0]
    tn = 1024
    return pl.pallas_call(
        _mod_kernel,
        out_shape=jax.ShapeDtypeStruct((depth, rows, 3 * D_MODEL), F32),
        grid=(depth, 3 * D_MODEL // tn),
        in_specs=[pl.BlockSpec((rows, D_MODEL), lambda l, j: (0, 0)),
                  pl.BlockSpec((1, D_MODEL, tn), lambda l, j: (l, 0, j)),
                  pl.BlockSpec((1, 1, tn), lambda l, j: (l, 0, j))],
        out_specs=pl.BlockSpec((1, rows, tn), lambda l, j: (l, 0, j)),
        compiler_params=_cparams(("arbitrary", "arbitrary")),
        name="adaln_modulation",
    )(c_all, w_ada, b_ada.reshape(depth, 1, 3 * D_MODEL))


def _lb_kernel(lg_ref, o_ref):
    depth = lg_ref.shape[0]
    lg = [lg_ref[l] for l in range(depth)]
    m = functools.reduce(jnp.maximum, lg)
    e = [jnp.exp(t - m) for t in lg]
    tot = functools.reduce(lambda a, b: a + b, e)
    sm = [t / tot for t in e]
    run = jnp.zeros_like(sm[0])
    for l in range(depth):
        run = run + sm[l]
        o_ref[l] = jnp.maximum(run - sm[0], 0.0)


def _lower_bounds(lb_logits):
    return pl.pallas_call(
        _lb_kernel,
        out_shape=jax.ShapeDtypeStruct(lb_logits.shape, F32),
        name="hgrn_lower_bounds",
    )(lb_logits)


def _rope(t, cos, sin_signed, first_half):
    up = pltpu.roll(t, LANES - ROPE_FREQS, 1)
    down = pltpu.roll(t, ROPE_FREQS, 1)
    return t * cos + jnp.where(first_half, up, down) * sin_signed


def _proj_kernel(x_ref, sh_ref, sc_ref, gpre_ref, w_ref, cos_ref, sin_ref, gn_ref, ws_ref, bs_ref,
                 lb_ref, qr_ref, qp_ref, kr_ref, v_ref, bx_ref, mx_ref, dq_ref, di_ref,
                 lff_ref, kf_ref, lfb_ref, kb_ref):
    tm = x_ref.shape[1]
    hb = _normed_input(x_ref[0], gpre_ref[...], sc_ref[0], sh_ref[0]).astype(BF16)

    def proj(name):
        j = _PROJ_NAMES.index(name)
        return _dot(hb, w_ref[:, j * BRANCH_W:(j + 1) * BRANCH_W])

    cos = cos_ref[...]
    sin_signed = sin_ref[...]
    first_half = (_lane_iota((tm, LANES)) % (2 * ROPE_FREQS)) < ROPE_FREQS

    def rope_all(t):
        return jnp.concatenate(
            [_rope(t[:, p * LANES:(p + 1) * LANES], cos, sin_signed, first_half) for p in range(N_PAIRS)],
            axis=1)

    for name, d, lf_ref, k_ref in (('d_f_fwd', 0, lff_ref, kf_ref), ('d_f_bwd', 1, lfb_ref, kb_ref)):
        z = proj(name)
        lb = lb_ref[0, d:d + 1, :]
        sg = jax.nn.sigmoid(z)
        lf_ref[0] = jnp.log(jnp.maximum(lb + (1.0 - lb) * sg, F_FLOOR))
        k_ref[0] = ((1.0 - lb) * (1.0 - sg)).astype(BF16)

    cv = proj('c_v')
    vn = (cv * lax.rsqrt(jnp.mean(cv * cv, axis=-1, keepdims=True) + EPS) * gn_ref[...]).astype(BF16)
    low_group = _lane_iota((GMLP_CHUNK, LANES)) < (LANES // 2)
    for ch in range(tm // GMLP_CHUNK):
        r0 = ch * GMLP_CHUNK
        for p in range(N_PAIRS):
            slab = vn[r0:r0 + GMLP_CHUNK, p * LANES:(p + 1) * LANES]
            zero = jnp.zeros_like(slab)
            stacked = jnp.concatenate([jnp.where(low_group, slab, zero), jnp.where(low_group, zero, slab)], axis=0)
            mixed = _dot(ws_ref[p], stacked) + bs_ref[:, p * LANES:(p + 1) * LANES]
            mx_ref[0, r0:r0 + GMLP_CHUNK, p * LANES:(p + 1) * LANES] = mixed.astype(BF16)

    q = proj('a_q') * (HEAD_DIM ** -0.5 * LOG2E)
    q_rot = rope_all(q)
    kr_ref[0] = rope_all(proj('a_k')).astype(BF16)
    v = proj('a_v')
    for ch in range(tm // NA_VCHUNK):
        rs = slice(ch * NA_VCHUNK, (ch + 1) * NA_VCHUNK)
        qp_ref[0, ch] = q[rs].T.astype(BF16)
        qr_ref[0, ch] = q_rot[rs].T.astype(BF16)
        v_ref[0, ch] = v[rs].T.astype(BF16)
    bx_ref[0] = proj('b_x').astype(BF16)
    dq_ref[0] = proj('d_q').astype(BF16)
    di_ref[0] = proj('d_i').astype(BF16)


def _proj_call(x, sh, sc, gpre, w_proj, cos, sin_signed, gn, ws, bs_tab, lb, tm, per_batch_mod):
    b, n, _ = x.shape
    nt = n // tm
    mod_map = (lambda i, j: (i, 0, 0)) if per_batch_mod else (lambda i, j: (0, 0, 0))
    row_spec = pl.BlockSpec((1, tm, BRANCH_W), lambda i, j: (i, j, 0))
    vt_spec = pl.BlockSpec((1, tm // NA_VCHUNK, BRANCH_W, NA_VCHUNK), lambda i, j: (i, j, 0, 0))
    bf = jax.ShapeDtypeStruct((b, n, BRANCH_W), BF16)
    vt = jax.ShapeDtypeStruct((b, n // NA_VCHUNK, BRANCH_W, NA_VCHUNK), BF16)
    f32 = jax.ShapeDtypeStruct((b, n, BRANCH_W), F32)
    return pl.pallas_call(
        _proj_kernel,
        out_shape=(vt, vt, bf, vt, bf, bf, bf, bf, f32, bf, f32, bf),
        grid=(b, nt),
        in_specs=[pl.BlockSpec((1, tm, D_MODEL), lambda i, j: (i, j, 0)),
                  pl.BlockSpec((1, 1, D_MODEL), mod_map),
                  pl.BlockSpec((1, 1, D_MODEL), mod_map),
                  _resident((1, D_MODEL)),
                  _resident((D_MODEL, len(_PROJ_NAMES) * BRANCH_W)),
                  pl.BlockSpec((tm, LANES), lambda i, j: (j, 0)),
                  pl.BlockSpec((tm, LANES), lambda i, j: (j, 0)),
                  _resident((1, BRANCH_W)),
                  _resident((GMLP_GROUPS // 2, GMLP_CHUNK, 2 * GMLP_CHUNK)),
                  _resident((GMLP_CHUNK, BRANCH_W)),
                  _resident((1, 2, BRANCH_W))],
        out_specs=(vt_spec, vt_spec, row_spec, vt_spec) + (row_spec,) * 8,
        compiler_params=_cparams(("arbitrary", "arbitrary")),
        name="branch_proj",
    )(x, sh, sc, gpre, w_proj, cos, sin_signed, gn, ws, bs_tab, lb)


def _nattn_kernel(qr_ref, qp_ref, k_ref, vt_ref, kc_ref, vct_ref, bias_ref, o_ref):
    rows = k_ref.shape[1] // GRID_W
    r0 = pl.program_id(1) * NA_TILE_ROWS
    kb0 = jnp.clip(r0 - NA_WIN_H // 2, 0, rows - NA_BAND_ROWS)
    start = pl.multiple_of(kb0 * GRID_W, NA_VCHUNK)
    c0 = kb0 // (NA_VCHUNK // GRID_W)
    band = NA_BAND_ROWS * GRID_W
    nq = NA_TILE_ROWS * GRID_W

    def lanes(p):
        return slice(p * LANES, (p + 1) * LANES)

    head0_rows = lax.broadcasted_iota(jnp.int32, (LANES, nq), 0) < HEAD_DIM

    def stack_heads(qt):
        zero = jnp.zeros_like(qt)
        return jnp.concatenate([jnp.where(head0_rows, qt, zero), jnp.where(head0_rows, zero, qt)], axis=1)

    def scores(p):
        s_ctx = _dot(kc_ref[0, :, lanes(p)], stack_heads(qp_ref[0, 0, lanes(p), :]))
        s_band = _dot(k_ref[0, pl.ds(start, band), lanes(p)], stack_heads(qr_ref[0, 0, lanes(p), :]))
        return s_ctx, s_band + bias_ref[0, :, 2 * p * nq:2 * (p + 1) * nq]

    def softmax(s_ctx, s_band):
        p_ctx, p_band = [], []
        for cb in range(2 * nq // LANES):
            cs = slice(cb * LANES, (cb + 1) * LANES)
            sc, sb = s_ctx[:, cs], s_band[:, cs]
            mx = jnp.maximum(jnp.max(sc, axis=0, keepdims=True), jnp.max(sb, axis=0, keepdims=True))
            p_ctx.append(jnp.exp2(sc - mx).astype(BF16))
            p_band.append(jnp.exp2(sb - mx).astype(BF16))
        return jnp.concatenate(p_ctx, axis=1), jnp.concatenate(p_band, axis=1)

    ones_rows = jnp.ones((NA_DEN_ROWS, NA_VCHUNK), BF16)

    def values(p, p_ctx, p_band):
        outs = []
        for hh in range(2):
            qs = slice(hh * nq, (hh + 1) * nq)
            ch = slice(p * LANES + hh * HEAD_DIM, p * LANES + (hh + 1) * HEAD_DIM)
            acc = None
            for j in range(vct_ref.shape[1]):
                lhs = jnp.concatenate([vct_ref[0, j, ch, :], ones_rows], axis=0)
                term = _dot(lhs, p_ctx[j * NA_VCHUNK:(j + 1) * NA_VCHUNK, qs])
                acc = term if acc is None else acc + term
            for j in range(band // NA_VCHUNK):
                lhs = jnp.concatenate([vt_ref[0, c0 + j, ch, :], ones_rows], axis=0)
                acc = acc + _dot(lhs, p_band[j * NA_VCHUNK:(j + 1) * NA_VCHUNK, qs])
            outs.append(acc[0:HEAD_DIM] * (1.0 / acc[HEAD_DIM:HEAD_DIM + 1]))
        o_ref[0, :, lanes(p)] = jnp.concatenate(outs, axis=0).T.astype(BF16)

    s_val, p_val = {}, {}
    for t in range(N_PAIRS + 2):
        if t < N_PAIRS:
            s_val[t] = scores(t)
        if 0 <= t - 1 < N_PAIRS:
            p_val[t - 1] = softmax(*s_val.pop(t - 1))
        if 0 <= t - 2 < N_PAIRS:
            values(t - 2, *p_val.pop(t - 2))


def _nattn_call(qrt, qpt, k, vt, kc, vct, bias):
    b, n, _ = k.shape
    rows = n // GRID_W
    nt = rows // NA_TILE_ROWS
    lc = kc.shape[1]
    nq = NA_TILE_ROWS * GRID_W

    def bias_map(i, t):
        return (jnp.where(t == 0, 0, jnp.where(t == nt - 1, 2, 1)), 0, 0)

    assert nq == NA_VCHUNK
    q_spec = pl.BlockSpec((1, 1, BRANCH_W, NA_VCHUNK), lambda i, t: (i, t, 0, 0))
    o_spec = pl.BlockSpec((1, nq, BRANCH_W), lambda i, t: (i, t, 0))
    full = pl.BlockSpec((1, n, BRANCH_W), lambda i, t: (i, 0, 0))
    full_t = pl.BlockSpec((1, n // NA_VCHUNK, BRANCH_W, NA_VCHUNK), lambda i, t: (i, 0, 0, 0))
    ctx = pl.BlockSpec((1, lc, BRANCH_W), lambda i, t: (i, 0, 0))
    ctx_t = pl.BlockSpec((1, lc // NA_VCHUNK, BRANCH_W, NA_VCHUNK), lambda i, t: (i, 0, 0, 0))
    return pl.pallas_call(
        _nattn_kernel,
        out_shape=jax.ShapeDtypeStruct((b, n, BRANCH_W), BF16),
        grid=(b, nt),
        in_specs=[q_spec, q_spec, full, full_t, ctx, ctx_t,
                  pl.BlockSpec((1, NA_BAND_ROWS * GRID_W, N_HEADS * nq), bias_map)],
        out_specs=o_spec,
        compiler_params=_cparams(("arbitrary", "arbitrary")),
        name="neighbourhood_attention",
    )(qrt, qpt, k, vt, kc, vct, bias)


def _cattn_kernel(qt_ref, k_ref, vt_ref, o_ref):
    lc = k_ref.shape[1]
    head0_rows = lax.broadcasted_iota(jnp.int32, (LANES, lc), 0) < HEAD_DIM
    for p in range(N_PAIRS):
        ls = slice(p * LANES, (p + 1) * LANES)
        qt = jnp.concatenate([qt_ref[0, j, ls, :] for j in range(qt_ref.shape[1])], axis=1)
        vt = jnp.concatenate([vt_ref[0, j, ls, :] for j in range(vt_ref.shape[1])], axis=1)
        k = k_ref[0, :, ls]
        zero = jnp.zeros_like(qt)
        outs = []
        for hh in range(2):
            s = _dot(k, jnp.where(head0_rows, qt, zero) if hh == 0 else jnp.where(head0_rows, zero, qt))
            e = jnp.exp2(s - jnp.max(s, axis=0, keepdims=True))
            acc = _dot(vt[hh * HEAD_DIM:(hh + 1) * HEAD_DIM], e.astype(BF16))
            outs.append(acc * (1.0 / jnp.sum(e, axis=0, keepdims=True)))
        o_ref[0, :, ls] = jnp.concatenate(outs, axis=0).T.astype(BF16)


def _cattn_call(qt, k, vt):
    b, lc, _ = k.shape
    spec = pl.BlockSpec((1, lc, BRANCH_W), lambda i: (i, 0, 0))
    spec_t = pl.BlockSpec((1, lc // NA_VCHUNK, BRANCH_W, NA_VCHUNK), lambda i: (i, 0, 0, 0))
    return pl.pallas_call(
        _cattn_kernel,
        out_shape=jax.ShapeDtypeStruct((b, lc, BRANCH_W), BF16),
        grid=(b,),
        in_specs=[spec_t, spec, spec_t],
        out_specs=spec,
        compiler_params=_cparams(("arbitrary",)),
        name="context_attention",
    )(qt, k, vt)


def _attention_bias(rpb, rows):
    col = jnp.arange(GRID_W)
    col_start = jnp.clip(col - NA_WIN_W // 2, 0, GRID_W - NA_WIN_W)
    valid = (col[:, None] >= col_start[None, :]) & (col[:, None] < col_start[None, :] + NA_WIN_W)
    col_idx = jnp.clip(col[:, None] - col[None, :] + NA_WIN_W - 1, 0, 2 * NA_WIN_W - 2)
    tab = jnp.where(valid[None, None], rpb.astype(F32)[:, :, col_idx] * LOG2E, NEG_INF)
    pad = NA_BAND_ROWS - NA_WIN_H
    tab = jnp.pad(tab.transpose(1, 2, 0, 3), ((pad, pad), (0, 0), (0, 0), (0, 0)), constant_values=NEG_INF)
    kinds = []
    for r0 in (0, NA_TILE_ROWS, rows - NA_TILE_ROWS):
        kb0 = int(np.clip(r0 - NA_WIN_H // 2, 0, rows - NA_BAND_ROWS))
        per_query_row = []
        for r in range(r0, r0 + NA_TILE_ROWS):
            lo = kb0 - r + NA_WIN_H - 1 + pad
            kr = kb0 + np.arange(NA_BAND_ROWS)
            rs = int(np.clip(r - NA_WIN_H // 2, 0, rows - NA_WIN_H))
            in_win = (kr >= rs) & (kr < rs + NA_WIN_H)
            per_query_row.append(jnp.where(in_win[:, None, None, None], tab[lo:lo + NA_BAND_ROWS], NEG_INF))
        kinds.append(jnp.stack(per_query_row, axis=3))
    return jnp.stack(kinds).reshape(3, NA_BAND_ROWS * GRID_W, N_HEADS * NA_TILE_ROWS * GRID_W)


_KRON = 8


@functools.lru_cache(maxsize=None)
def _fourier_consts(n):
    rows = n // GRID_W
    k1 = np.arange(rows)[:, None, None, None]
    l1 = np.arange(_KRON)[None, :, None, None]
    n1 = np.arange(rows)[None, None, :, None]
    l2 = np.arange(_KRON)[None, None, None, :]
    a_cos, a_sin = [], []
    for j in range(GRID_W // _KRON):
        ang = 2.0 * np.pi * k1 * (GRID_W * n1 + _KRON * j + l1) / n
        same = (l1 == l2)
        a_cos.append((np.cos(ang) * same).reshape(rows * _KRON, rows * _KRON))
        a_sin.append((-np.sin(ang) * same).reshape(rows * _KRON, rows * _KRON))
    k2 = np.arange(GRID_W)[:, None, None, None]
    ang = 2.0 * np.pi * k2 * np.arange(GRID_W)[None, None, None, :] / GRID_W
    same = (np.arange(_KRON)[None, :, None, None] == np.arange(_KRON)[None, None, :, None])
    b_cos = (np.cos(ang) * same).reshape(GRID_W * _KRON, _KRON * GRID_W)
    b_sin = (np.sin(ang) * same).reshape(GRID_W * _KRON, _KRON * GRID_W)
    b_re = np.concatenate([b_cos, b_sin], axis=1)
    b_im = np.concatenate([-b_sin, b_cos], axis=1)
    return (np.stack(a_cos).astype(np.float32), np.stack(a_sin).astype(np.float32),
            b_re.astype(np.float32), b_im.astype(np.float32))


@functools.lru_cache(maxsize=None)
def _channel_dft():
    c = np.arange(FNET_GROUP_W)
    ang = 2.0 * np.pi * np.outer(c, c) / FNET_GROUP_W
    return np.concatenate([np.cos(ang), np.sin(ang)], axis=0).astype(np.float32)


@functools.lru_cache(maxsize=None)
def _dense_dft(n):
    t = np.arange(n)
    ang = 2.0 * np.pi * np.outer(t, t) / n
    return np.cos(ang).astype(np.float32), (-np.sin(ang)).astype(np.float32)


def _fold_channel_map(cs_ref, wf_ref, fold_ref, norm):
    c_hi, c_lo = _split2(cs_ref[...] * norm)
    for g in range(FNET_GROUPS):
        w_hi, w_lo = _split2(wf_ref[g])
        fold_ref[g] = (_dot(c_hi, w_hi) + _dot(c_hi, w_lo) + _dot(c_lo, w_hi)).astype(BF16)


def _channel_stage(xr, xi, fold_ref):
    outs = []
    for g in range(FNET_GROUPS):
        ls = slice(g * FNET_GROUP_W, (g + 1) * FNET_GROUP_W)
        xg = jnp.concatenate([xr[:, ls], xi[:, ls]], axis=1).astype(BF16)
        outs.append(_dot(xg, fold_ref[g]))
    return jnp.concatenate(outs, axis=1)


def _fourier_kernel(x_ref, ac_ref, as_ref, bre_ref, bim_ref, cs_ref, wf_ref, o_ref, s_ref, fold_ref, *, norm):
    @pl.when(pl.program_id(0) == 0)
    def _():
        _fold_channel_map(cs_ref, wf_ref, fold_ref, norm)

    rows = x_ref.shape[1]
    blk = rows * _KRON
    pair = 2 * _KRON
    for jj in range(GRID_W // pair):
        xt = x_ref[0, :, jj * pair:(jj + 1) * pair, :].astype(F32)
        re, im = [], []
        for half in range(2):
            xc = xt[:, half * _KRON:(half + 1) * _KRON, :].reshape(blk, BRANCH_W).astype(BF16)
            re.append(_dot(ac_ref[2 * jj + half], xc).reshape(rows, _KRON, BRANCH_W))
            im.append(_dot(as_ref[2 * jj + half], xc).reshape(rows, _KRON, BRANCH_W))
        s_ref[0, :, jj * pair:(jj + 1) * pair, :] = jnp.concatenate(re, axis=1).astype(BF16)
        s_ref[1, :, jj * pair:(jj + 1) * pair, :] = jnp.concatenate(im, axis=1).astype(BF16)
    sblk = _KRON * GRID_W
    for mm in range(rows // pair):
        ys = []
        for half in range(2):
            m0 = (2 * mm + half) * _KRON
            rhs = jnp.concatenate([s_ref[0, m0:m0 + _KRON].reshape(sblk, BRANCH_W),
                                   s_ref[1, m0:m0 + _KRON].reshape(sblk, BRANCH_W)], axis=0)
            xr = _dot(bre_ref[...], rhs)
            xi = _dot(bim_ref[...], rhs)
            ys.append(_channel_stage(xr, xi, fold_ref).reshape(GRID_W, _KRON, BRANCH_W))
        o_ref[0, :, mm * pair:(mm + 1) * pair, :] = jnp.concatenate(ys, axis=1).astype(BF16)


def _fourier_call(bx, wf):
    b, n, _ = bx.shape
    rows = n // GRID_W
    a_cos, a_sin, b_re, b_im = (jnp.asarray(t, BF16) for t in _fourier_consts(n))
    cs = jnp.asarray(_channel_dft(), F32)
    norm = float(1.0 / np.sqrt(n * FNET_GROUP_W))
    x4 = bx.reshape(b, rows, GRID_W, BRANCH_W)
    out = pl.pallas_call(
        functools.partial(_fourier_kernel, norm=norm),
        out_shape=jax.ShapeDtypeStruct((b, GRID_W, rows, BRANCH_W), BF16),
        grid=(b,),
        in_specs=[pl.BlockSpec((1, rows, GRID_W, BRANCH_W), lambda i: (i, 0, 0, 0)),
                  _resident(a_cos.shape), _resident(a_sin.shape),
                  _resident(b_re.shape), _resident(b_im.shape),
                  _resident(cs.shape), _resident(wf.shape)],
        out_specs=pl.BlockSpec((1, GRID_W, rows, BRANCH_W), lambda i: (i, 0, 0, 0)),
        scratch_shapes=[pltpu.VMEM((2, rows, GRID_W, BRANCH_W), BF16),
                        pltpu.VMEM((FNET_GROUPS, 2 * FNET_GROUP_W, FNET_GROUP_W), BF16)],
        compiler_params=_cparams(("arbitrary",)),
        name="fourier_mix",
    )(x4, a_cos, a_sin, b_re, b_im, cs, wf)
    return out.reshape(b, n, BRANCH_W)


def _fourier_ctx_kernel(x_ref, c_ref, s_ref, cs_ref, wf_ref, o_ref, fold_ref, *, norm):
    @pl.when(pl.program_id(0) == 0)
    def _():
        _fold_channel_map(cs_ref, wf_ref, fold_ref, norm)

    x = x_ref[0]
    xr = _dot(c_ref[...], x)
    xi = _dot(s_ref[...], x)
    o_ref[0] = _channel_stage(xr, xi, fold_ref).astype(BF16)


def _fourier_ctx_call(bx, wf):
    b, n, _ = bx.shape
    cn, sn = (jnp.asarray(t, BF16) for t in _dense_dft(n))
    cs = jnp.asarray(_channel_dft(), F32)
    norm = float(1.0 / np.sqrt(n * FNET_GROUP_W))
    spec = pl.BlockSpec((1, n, BRANCH_W), lambda i: (i, 0, 0))
    return pl.pallas_call(
        functools.partial(_fourier_ctx_kernel, norm=norm),
        out_shape=jax.ShapeDtypeStruct((b, n, BRANCH_W), BF16),
        grid=(b,),
        in_specs=[spec, _resident(cn.shape), _resident(sn.shape), _resident(cs.shape), _resident(wf.shape)],
        out_specs=spec,
        scratch_shapes=[pltpu.VMEM((FNET_GROUPS, 2 * FNET_GROUP_W, FNET_GROUP_W), BF16)],
        compiler_params=_cparams(("arbitrary",)),
        name="fourier_mix_context",
    )(bx, cn, sn, cs, wf)


def _block_diag(t):
    lo = _lane_iota(t.shape) < HEAD_DIM
    z = jnp.zeros_like(t)
    return jnp.concatenate([jnp.where(lo, t, z), jnp.where(lo, z, t)], axis=0)


def _hgrn_needed(j, reverse):
    nsub = HGRN_CHUNK // HGRN_SUB
    return list(range(0, j + 1)) if reverse else list(range(j, nsub))


def _hgrn_prepare(q, k, i, a, reverse):
    c = HGRN_CHUNK
    nsub = c // HGRN_SUB

    def level(r):
        return a[r:r + 1, :]

    zero_row = jnp.zeros((1, BRANCH_W), F32)
    if reverse:
        refs = [level((s + 1) * HGRN_SUB) if s + 1 < nsub else zero_row for s in range(nsub)]
        a_end = a[0:1, :]
    else:
        refs = [level(s * HGRN_SUB - 1) if s > 0 else zero_row for s in range(nsub)]
        a_end = a[c - 1:c, :]
    ref_rows = jnp.concatenate([jnp.broadcast_to(r, (HGRN_SUB, BRANCH_W)) for r in refs], axis=0)
    qf = q.astype(F32)
    kf = k.astype(F32)
    lift = ref_rows - a
    k_own = (kf * jnp.exp2(jnp.minimum(lift, HGRN_EXP_CLAMP))).astype(BF16)

    def q_variant(j):
        parts = []
        for s in _hgrn_needed(j, reverse):
            rs = slice(s * HGRN_SUB, (s + 1) * HGRN_SUB)
            parts.append((qf[rs] * jnp.exp2(a[rs] - refs[j])).astype(BF16))
        return jnp.concatenate(parts, axis=0)

    q_var = [q_variant(j) for j in range(nsub)]
    return dict(
        q_stack=jnp.concatenate(q_var, axis=0),
        q_in=q_var[nsub - 1] if reverse else q_var[0],
        k_own=k_own,
        k_out=(kf * jnp.exp2(a_end - a)).astype(BF16),
        decay_end=jnp.exp2(a_end),
        max_lift=jnp.max(lift, axis=0, keepdims=True),
        i=i, reverse=reverse)


def _hgrn_scores(ops):
    c = HGRN_CHUNK
    nsub = c // HGRN_SUB
    reverse = ops['reverse']
    src = _lane_iota((c, LANES)) % HEAD_DIM
    step = lax.broadcasted_iota(jnp.int32, (c, LANES), 0)
    seen = (src >= step) if reverse else (src <= step)
    src_sub = (_lane_iota((HGRN_SUB, LANES)) % HEAD_DIM) // HGRN_SUB
    where_blk, off = {}, 0
    for j in range(nsub):
        for s in _hgrn_needed(j, reverse):
            where_blk[(j, s)] = off
            off += HGRN_SUB
    out = []
    for p in range(N_PAIRS):
        ls = slice(p * LANES, (p + 1) * LANES)
        res = _dot_nt(ops['q_stack'][:, ls], _block_diag(ops['k_own'][:, ls]))
        rows = []
        for s in range(nsub):
            blk = None
            for j in range(nsub):
                if (j, s) in where_blk:
                    piece = res[where_blk[(j, s)]:where_blk[(j, s)] + HGRN_SUB]
                    blk = piece if blk is None else jnp.where(src_sub == j, piece, blk)
            rows.append(blk)
        out.append(jnp.where(seen, jnp.concatenate(rows, axis=0), 0.0).astype(BF16))
    return out


def _hgrn_local(ops, scores):
    low_rows = lax.broadcasted_iota(jnp.int32, (LANES, LANES), 0) < HEAD_DIM
    same_head = low_rows == (_lane_iota((LANES, LANES)) < HEAD_DIM)
    o_intra, upd = [], []
    for p in range(N_PAIRS):
        ls = slice(p * LANES, (p + 1) * LANES)
        ip = ops['i'][:, ls]
        o_intra.append(_dot(scores[p], _block_diag(ip)))
        upd.append(jnp.where(same_head, _dot_tn(ip, ops['k_out'][:, ls]), 0.0))
    return o_intra, upd


def _hgrn_carry(ops, o_intra, upd, state_ref, d):
    outs = []
    for p in range(N_PAIRS):
        ls = slice(p * LANES, (p + 1) * LANES)
        st = state_ref[d, p]
        outs.append(o_intra[p] + _dot_nt(ops['q_in'][:, ls], st.astype(BF16)))
        state_ref[d, p] = ops['decay_end'][:, ls] * st + upd[p]
    return jnp.concatenate(outs, axis=1)


def _hgrn_exact_tile(q_ref, k_ref, i_ref, a_ref, o_ref, state_ref, bb, d, reverse, q_sc, k_sc, i_sc):
    c = HGRN_CHUNK
    nchunk = q_ref.shape[1] // c
    lane_head = _lane_iota((BRANCH_W, BRANCH_W)) // HEAD_DIM
    row_head = lax.broadcasted_iota(jnp.int32, (BRANCH_W, BRANCH_W), 0) // HEAD_DIM
    head_sum = (lane_head == row_head).astype(BF16)
    step = lax.broadcasted_iota(jnp.int32, (c, BRANCH_W), 0)
    low_rows = lax.broadcasted_iota(jnp.int32, (LANES, LANES), 0) < HEAD_DIM
    same_head = low_rows == (_lane_iota((LANES, LANES)) < HEAD_DIM)
    for cix in (range(nchunk - 1, -1, -1) if reverse else range(nchunk)):
        rs = slice(cix * c, (cix + 1) * c)
        q_sc[...] = q_ref[bb, rs, :].astype(F32)
        k_sc[...] = k_ref[bb, rs, :].astype(F32)
        i_sc[...] = i_ref[bb, rs, :].astype(F32)
        qf = q_sc[...]
        kf = k_sc[...]
        a = a_ref[d, rs, :]

        def one_source(s, acc, a=a, qf=qf, cix=cix):
            a_s = a_ref[d, pl.ds(cix * c + s, 1), :]
            w = qf * (k_sc[pl.ds(s, 1), :] * jnp.exp2(jnp.minimum(a - a_s, 0.0)))
            w = jnp.where((step <= s) if reverse else (step >= s), w, 0.0)
            hi, lo = _split2(w)
            return acc + (_dot(hi, head_sum) + _dot(lo, head_sum)) * i_sc[pl.ds(s, 1), :]

        o_intra = lax.fori_loop(0, c, one_source, jnp.zeros((c, BRANCH_W), F32))
        a_end = a[0:1, :] if reverse else a[c - 1:c, :]
        q_in = (qf * jnp.exp2(a)).astype(BF16)
        k_out = (kf * jnp.exp2(a_end - a)).astype(BF16)
        decay_end = jnp.exp2(a_end)
        ib = i_ref[bb, rs, :]
        outs = []
        for p in range(N_PAIRS):
            ls = slice(p * LANES, (p + 1) * LANES)
            st = state_ref[d, p]
            outs.append(o_intra[:, ls] + _dot_nt(q_in[:, ls], st.astype(BF16)))
            upd = jnp.where(same_head, _dot_tn(ib[:, ls], k_out[:, ls]), 0.0)
            state_ref[d, p] = decay_end[:, ls] * st + upd
        o_ref[bb, rs, :] = jnp.concatenate(outs, axis=1).astype(BF16)


def _hgrn_kernel(qf_ref, if_ref, lff_ref, kf_ref, qb_ref, ib_ref, lfb_ref, kb_ref, s0f_ref, s0b_ref,
                 of_ref, ob_ref, sf_ref, sb_ref, state_ref, backup_ref, a_ref, q_sc, k_sc, i_sc):
    j = pl.program_id(1)
    nb = qf_ref.shape[0]
    nchunk = qf_ref.shape[1] // HGRN_CHUNK

    @pl.when(j == 0)
    def _():
        for bb in range(nb):
            state_ref[2 * bb] = s0f_ref[bb]
            state_ref[2 * bb + 1] = s0b_ref[bb]

    backup_ref[...] = state_ref[...]

    tm = qf_ref.shape[1]
    row = lax.broadcasted_iota(jnp.int32, (tm, tm), 0)
    col = lax.broadcasted_iota(jnp.int32, (tm, tm), 1)
    same_chunk = (row // HGRN_CHUNK) == (col // HGRN_CHUNK)

    def cum(lf, reverse):
        tri = (same_chunk & ((col >= row) if reverse else (col <= row))).astype(BF16)
        hi, lo = _split2(lf * LOG2E)
        return _dot(tri, hi) + _dot(tri, lo)

    a_f = [cum(lff_ref[bb], False) for bb in range(nb)]
    a_b = [cum(lfb_ref[bb], True) for bb in range(nb)]

    todo = []
    for cix in range(nchunk):
        bix = nchunk - 1 - cix
        for bb in range(nb):
            todo.append((bb, 2 * bb, of_ref, slice(cix * HGRN_CHUNK, (cix + 1) * HGRN_CHUNK),
                         qf_ref, kf_ref, if_ref, a_f[bb], False))
            todo.append((bb, 2 * bb + 1, ob_ref, slice(bix * HGRN_CHUNK, (bix + 1) * HGRN_CHUNK),
                         qb_ref, kb_ref, ib_ref, a_b[bb], True))
    ops, scores, local, lifts = {}, {}, {}, []
    for t in range(len(todo) + 3):
        if t < len(todo):
            bb, _, _, rs, q_ref, k_ref, i_ref, a, reverse = todo[t]
            ops[t] = _hgrn_prepare(q_ref[bb, rs, :], k_ref[bb, rs, :], i_ref[bb, rs, :], a[rs, :], reverse)
            lifts.append(ops[t]['max_lift'])
        if 0 <= t - 1 < len(todo):
            scores[t - 1] = _hgrn_scores(ops[t - 1])
        if 0 <= t - 2 < len(todo):
            local[t - 2] = _hgrn_local(ops[t - 2], scores.pop(t - 2))
        if 0 <= t - 3 < len(todo):
            bb, slot, o_ref, rs = todo[t - 3][:4]
            o_ref[bb, rs, :] = _hgrn_carry(ops.pop(t - 3), *local.pop(t - 3), state_ref, slot).astype(BF16)

    @pl.when(jnp.max(functools.reduce(jnp.maximum, lifts)) > HGRN_EXP_CLAMP)
    def _():
        state_ref[...] = backup_ref[...]
        for bb in range(nb):
            a_ref[2 * bb] = a_f[bb]
            a_ref[2 * bb + 1] = a_b[bb]
            _hgrn_exact_tile(qf_ref, kf_ref, if_ref, a_ref, of_ref, state_ref, bb, 2 * bb, False,
                             q_sc, k_sc, i_sc)
            _hgrn_exact_tile(qb_ref, kb_ref, ib_ref, a_ref, ob_ref, state_ref, bb, 2 * bb + 1, True,
                             q_sc, k_sc, i_sc)

    @pl.when(j == pl.num_programs(1) - 1)
    def _():
        for bb in range(nb):
            sf_ref[bb] = state_ref[2 * bb]
            sb_ref[bb] = state_ref[2 * bb + 1]


def _hgrn_call(q, i, lff, kf, lfb, kb, s0f, s0b, tm):
    b, n, _ = q.shape
    nt = n // tm
    nb = HGRN_BATCH_ROWS if b % HGRN_BATCH_ROWS == 0 else 1
    fwd = pl.BlockSpec((nb, tm, BRANCH_W), lambda bi, j: (bi, j, 0))
    bwd = pl.BlockSpec((nb, tm, BRANCH_W), lambda bi, j: (bi, nt - 1 - j, 0))
    st = pl.BlockSpec((nb, N_PAIRS, LANES, LANES), lambda bi, j: (bi, 0, 0, 0))
    o_shape = jax.ShapeDtypeStruct((b, n, BRANCH_W), BF16)
    s_shape = jax.ShapeDtypeStruct((b, N_PAIRS, LANES, LANES), F32)
    return pl.pallas_call(
        _hgrn_kernel,
        out_shape=(o_shape, o_shape, s_shape, s_shape),
        grid=(b // nb, nt),
        in_specs=[fwd, fwd, fwd, fwd, bwd, bwd, bwd, bwd, st, st],
        out_specs=(fwd, bwd, st, st),
        scratch_shapes=[pltpu.VMEM((2 * nb, N_PAIRS, LANES, LANES), F32),
                        pltpu.VMEM((2 * nb, N_PAIRS, LANES, LANES), F32),
                        pltpu.VMEM((2 * nb, tm, BRANCH_W), F32),
                        pltpu.VMEM((HGRN_CHUNK, BRANCH_W), F32),
                        pltpu.VMEM((HGRN_CHUNK, BRANCH_W), F32),
                        pltpu.VMEM((HGRN_CHUNK, BRANCH_W), F32)],
        compiler_params=_cparams(("arbitrary", "arbitrary")),
        name="hgrn_scan",
    )(q, i, lff, kf, q, i, lfb, kb, s0f, s0b)


def _merge_kernel(x_ref, sh_ref, sc_ref, gt_ref, gpre_ref, gpost_ref, w_ref, oa_ref, ob_ref, mx_ref,
                  of_ref, obk_ref, hn_ref, hm_ref, wb_ref, wo_ref, o_ref):
    x = x_ref[0]
    hb = _normed_input(x, gpre_ref[...], sc_ref[0], sh_ref[0]).astype(BF16)

    def proj(name):
        j = _MERGE_NAMES.index(name)
        return _dot(hb, w_ref[:, j * BRANCH_W:(j + 1) * BRANCH_W])

    ya = oa_ref[0].astype(F32) * _silu(proj('a_g'))
    yb = ob_ref[0].astype(F32) * _silu(proj('b_g'))
    yc = proj('c_u') * mx_ref[0].astype(F32) * _silu(proj('c_g'))
    o = of_ref[0].astype(F32) + obk_ref[0].astype(F32)
    ms = _dot((o * o).astype(BF16), hm_ref[...])
    yd = o * lax.rsqrt(ms + EPS) * hn_ref[...] * _silu(proj('d_g'))

    g0 = len(_MERGE_NAMES) * BRANCH_W
    merged = None
    for r, y in enumerate((ya, yb, yc, yd)):
        gate = _dot(hb, w_ref[:, g0 + r * D_MODEL:g0 + (r + 1) * D_MODEL])
        term = jax.nn.sigmoid(gate) * _dot(y.astype(BF16), wb_ref[r])
        merged = term if merged is None else merged + term
    out = _dot(merged.astype(BF16), wo_ref[...])
    post = out * lax.rsqrt(jnp.mean(out * out, axis=-1, keepdims=True) + EPS) * gpost_ref[...]
    o_ref[0] = x + gt_ref[0] * post


def _merge_call(x, sh, sc, gt, gpre, gpost, w_merge, oa, ob, mx, of, obk, hn, hmean, wb, wo, tm,
                per_batch_mod):
    b, n, _ = x.shape
    nt = n // tm
    mod_map = (lambda i, j: (i, 0, 0)) if per_batch_mod else (lambda i, j: (0, 0, 0))
    x_spec = pl.BlockSpec((1, tm, D_MODEL), lambda i, j: (i, j, 0))
    br_spec = pl.BlockSpec((1, tm, BRANCH_W), lambda i, j: (i, j, 0))
    mod_spec = pl.BlockSpec((1, 1, D_MODEL), mod_map)
    return pl.pallas_call(
        _merge_kernel,
        out_shape=jax.ShapeDtypeStruct((b, n, D_MODEL), F32),
        grid=(b, nt),
        in_specs=[x_spec, mod_spec, mod_spec, mod_spec,
                  _resident((1, D_MODEL)), _resident((1, D_MODEL)),
                  _resident(w_merge.shape),
                  br_spec, br_spec, br_spec, br_spec, br_spec,
                  _resident((1, BRANCH_W)), _resident((BRANCH_W, BRANCH_W)),
                  _resident(wb.shape), _resident(wo.shape)],
        out_specs=x_spec,
        compiler_params=_cparams(("arbitrary", "arbitrary")),
        name="branch_merge",
    )(x, sh, sc, gt, gpre, gpost, w_merge, oa, ob, mx, of, obk, hn, hmean, wb, wo)


def _rope_tables(n_tok, rotate):
    if not rotate:
        return jnp.ones((n_tok, LANES), F32), jnp.zeros((n_tok, LANES), F32)
    t = jnp.arange(n_tok, dtype=jnp.int32)
    pos = jnp.stack([t // GRID_W, t % GRID_W], axis=-1).astype(F32)
    inv = ROPE_THETA ** (-jnp.arange(ROPE_FREQS, dtype=F32) * 2.0 / (2 * ROPE_FREQS))
    ang = pos[:, :, None] * inv
    cos = jnp.repeat(jnp.cos(ang)[:, :, None, :], 2, axis=2).reshape(n_tok, HEAD_DIM)
    sin = jnp.sin(ang)
    sin_signed = jnp.stack([-sin, sin], axis=2).reshape(n_tok, HEAD_DIM)
    return jnp.tile(cos, (1, 2)), jnp.tile(sin_signed, (1, 2))


def _gather_cols(w_in_l, names):
    return jnp.concatenate([w_in_l[:, _IN_COL[nm] * BRANCH_W:(_IN_COL[nm] + 1) * BRANCH_W] for nm in names],
                           axis=1)


def _row_tile(n):
    return 512 if n % 512 == 0 else 256


def kernel(x, c, ctx, c_ctx, w_ada, b_ada, g_pre, g_post, w_in, na_rpb, fnet_w, gmlp_norm_g, gmlp_ws,
           gmlp_bs, hgrn_lb_logits, hgrn_norm_g, w_branch, w_out):
    batch, n_tok, _ = x.shape
    n_ctx = ctx.shape[1]
    depth = w_in.shape[0]

    w_in_b = w_in.astype(BF16)
    w_proj = [_gather_cols(w_in_b[l], _PROJ_NAMES) for l in range(depth)]
    w_merge = [jnp.concatenate([_gather_cols(w_in_b[l], _MERGE_NAMES), w_in_b[l][:, _GATE_COL0:]], axis=1)
               for l in range(depth)]
    w_branch_b = w_branch.astype(BF16)
    w_out_b = w_out.astype(BF16)
    gmlp_ws_b = gmlp_ws.astype(BF16).reshape(depth, GMLP_GROUPS // 2, 2, GMLP_CHUNK, GMLP_CHUNK)
    gmlp_ws_b = gmlp_ws_b.transpose(0, 1, 3, 2, 4).reshape(depth, GMLP_GROUPS // 2, GMLP_CHUNK, 2 * GMLP_CHUNK)
    bs_tab = jnp.repeat(jnp.swapaxes(gmlp_bs, 1, 2), BRANCH_W // GMLP_GROUPS, axis=2)
    head_mean = jnp.asarray(np.kron(np.eye(N_HEADS), np.ones((HEAD_DIM, HEAD_DIM)) / HEAD_DIM), BF16)
    cos_x, sin_x = _rope_tables(n_tok, True)
    cos_c, sin_c = _rope_tables(n_ctx, False)

    c_all = jnp.concatenate([c, jnp.broadcast_to(c_ctx[None, :], (8, D_MODEL))], axis=0)
    mod = _modulation(c_all, w_ada, b_ada)
    lower = _lower_bounds(hgrn_lb_logits)

    zero_state = jnp.zeros((batch, N_PAIRS, LANES, LANES), F32)
    tm_x = _row_tile(n_tok)
    tm_c = _row_tile(n_ctx)

    for l in range(depth):
        with_ctx = l < depth - 1
        mod_x = [mod[l, :batch, i * D_MODEL:(i + 1) * D_MODEL].reshape(batch, 1, D_MODEL) for i in range(3)]
        mod_c = [mod[l, batch:batch + 1, i * D_MODEL:(i + 1) * D_MODEL].reshape(1, 1, D_MODEL) for i in range(3)]
        gpre = g_pre[l].reshape(1, D_MODEL)
        gpost = g_post[l].reshape(1, D_MODEL)
        gn = gmlp_norm_g[l].reshape(1, BRANCH_W)
        hn = hgrn_norm_g[l].reshape(1, BRANCH_W)
        lb = lower[l].reshape(1, 2, BRANCH_W)
        bias = _attention_bias(na_rpb[l], n_tok // GRID_W)

        (_, qp_c, k_c, v_c, bx_c, mx_c, dq_c, di_c, lff_c, kf_c, lfb_c, kb_c) = _proj_call(
            ctx, mod_c[0], mod_c[1], gpre, w_proj[l], cos_c, sin_c, gn, gmlp_ws_b[l], bs_tab[l], lb,
            tm_c, False)
        of_c, ob_c, st_f, st_b = _hgrn_call(dq_c, di_c, lff_c, kf_c, lfb_c, kb_c, zero_state, zero_state,
                                            min(HGRN_ROWS, n_ctx))

        (qr, qp, k, v, bx, mx, dq, di, lff, kf, lfb, kb) = _proj_call(
            x, mod_x[0], mod_x[1], gpre, w_proj[l], cos_x, sin_x, gn, gmlp_ws_b[l], bs_tab[l], lb,
            tm_x, True)
        oa = _nattn_call(qr, qp, k, v, k_c, v_c, bias)
        ob = _fourier_call(bx, fnet_w[l])
        of, obk, _, _ = _hgrn_call(dq, di, lff, kf, lfb, kb, st_f, st_b, min(HGRN_ROWS, n_tok))
        x = _merge_call(x, mod_x[0], mod_x[1], mod_x[2], gpre, gpost, w_merge[l], oa, ob, mx, of, obk,
                        hn, head_mean, w_branch_b[l], w_out_b[l], tm_x, True)

        if with_ctx:
            oa_c = _cattn_call(qp_c, k_c, v_c)
            ob_c2 = _fourier_ctx_call(bx_c, fnet_w[l])
            ctx = _merge_call(ctx, mod_c[0], mod_c[1], mod_c[2], gpre, gpost, w_merge[l], oa_c, ob_c2, mx_c,
                              of_c, ob_c, hn, head_mean, w_branch_b[l], w_out_b[l], tm_c, False)
    return x
```

```python
import functools

import numpy as np
import jax
import jax.numpy as jnp
from jax import lax
from jax.experimental import pallas as pl
from jax.experimental.pallas import tpu as pltpu

F32 = jnp.float32
BF16 = jnp.bfloat16

D_MODEL = 1024
BRANCH_W = 512
N_BRANCH = 4
GRID_W = 64
HEAD_DIM = 64
N_HEADS = 8
LANES = 128
N_PAIRS = BRANCH_W // LANES
NA_WIN_H = 8
NA_WIN_W = 16
NA_TILE_ROWS = 4
NA_BAND_ROWS = 12
NA_VCHUNK = 256
NA_DEN_ROWS = 16
LOG2E = 1.4426950408889634
ROPE_THETA = 10000.0
ROPE_FREQS = 16
FNET_GROUPS = 4
FNET_GROUP_W = 128
GMLP_CHUNK = 128
GMLP_GROUPS = 8
HGRN_CHUNK = 64
HGRN_SUB = 16
HGRN_EXP_CLAMP = 115.0
HGRN_ROWS = 256
HGRN_BATCH_ROWS = 2
EPS = 1e-6
F_FLOOR = 1e-30
NEG_INF = -1e30

VMEM_LIMIT = 56 * 2**20

_IN_COL = {'a_q': 0, 'a_k': 1, 'a_v': 2, 'a_g': 3, 'b_x': 4, 'b_g': 5, 'c_u': 6, 'c_v': 7, 'c_g': 8,
           'd_q': 9, 'd_f_fwd': 10, 'd_f_bwd': 11, 'd_i': 12, 'd_g': 13}
_PROJ_NAMES = ('a_q', 'a_k', 'a_v', 'b_x', 'c_v', 'd_q', 'd_f_fwd', 'd_f_bwd', 'd_i')
_MERGE_NAMES = ('c_u', 'a_g', 'b_g', 'c_g', 'd_g')
_GATE_COL0 = 14 * BRANCH_W


def _cparams(sem):
    return pltpu.CompilerParams(dimension_semantics=sem, vmem_limit_bytes=VMEM_LIMIT)


def _resident(shape):
    nd = len(shape)
    return pl.BlockSpec(shape, lambda *_: (0,) * nd, pipeline_mode=pl.Buffered(1))


def _silu(t):
    return t * jax.nn.sigmoid(t)


def _lane_iota(shape):
    return lax.broadcasted_iota(jnp.int32, shape, len(shape) - 1)


def _dot(a, b):
    return jnp.dot(a, b, preferred_element_type=F32)


def _dot_nt(a, b):
    return lax.dot_general(a, b, (((1,), (1,)), ((), ())), preferred_element_type=F32)


def _dot_tn(a, b):
    return lax.dot_general(a, b, (((0,), (0,)), ((), ())), preferred_element_type=F32)


def _split2(t):
    hi = t.astype(BF16)
    return hi, (t - hi.astype(F32)).astype(BF16)


def _normed_input(x, gpre, sc, sh):
    ms = jnp.mean(x * x, axis=-1, keepdims=True)
    h = x * lax.rsqrt(ms + EPS) * gpre
    return h * (1.0 + sc) + sh


def _mod_kernel(c_ref, w_ref, b_ref, o_ref):
    s = _silu(c_ref[...]).astype(BF16)
    o_ref[0] = _dot(s, w_ref[0].astype(BF16)) + b_ref[0]


def _modulation(c_all, w_ada, b_ada):
    depth = w_ada.shape[0]
    rows = c_all.shape[0]
    tn = 1024
    return pl.pallas_call(
        _mod_kernel,
        out_shape=jax.ShapeDtypeStruct((depth, rows, 3 * D_MODEL), F32),
        grid=(depth, 3 * D_MODEL // tn),
        in_specs=[pl.BlockSpec((rows, D_MODEL), lambda l, j: (0, 0)),
                  pl.BlockSpec((1, D_MODEL, tn), lambda l, j: (l, 0, j)),
                  pl.BlockSpec((1, 1, tn), lambda l, j: (l, 0, j))],
        out_specs=pl.BlockSpec((1, rows, tn), lambda l, j: (l, 0, j)),
        compiler_params=_cparams(("arbitrary", "arbitrary")),
        name="adaln_modulation",
    )(c_all, w_ada, b_ada.reshape(depth, 1, 3 * D_MODEL))


def _lb_kernel(lg_ref, o_ref):
    depth = lg_ref.shape[0]
    lg = [lg_ref[l] for l in range(depth)]
    m = functools.reduce(jnp.maximum, lg)
    e = [jnp.exp(t - m) for t in lg]
    tot = functools.reduce(lambda a, b: a + b, e)
    sm = [t / tot for t in e]
    run = jnp.zeros_like(sm[0])
    for l in range(depth):
        run = run + sm[l]
        o_ref[l] = jnp.maximum(run - sm[0], 0.0)


def _lower_bounds(lb_logits):
    return pl.pallas_call(
        _lb_kernel,
        out_shape=jax.ShapeDtypeStruct(lb_logits.shape, F32),
        name="hgrn_lower_bounds",
    )(lb_logits)


def _rope(t, cos, sin_signed, first_half):
    up = pltpu.roll(t, LANES - ROPE_FREQS, 1)
    down = pltpu.roll(t, ROPE_FREQS, 1)
    return t * cos + jnp.where(first_half, up, down) * sin_signed


def _proj_kernel(x_ref, sh_ref, sc_ref, gpre_ref, w_ref, cos_ref, sin_ref, gn_ref, ws_ref, bs_ref,
                 lb_ref, qr_ref, qp_ref, kr_ref, v_ref, bx_ref, mx_ref, dq_ref, di_ref,
                 lff_ref, kf_ref, lfb_ref, kb_ref):
    tm = x_ref.shape[1]
    hb = _normed_input(x_ref[0], gpre_ref[...], sc_ref[0], sh_ref[0]).astype(BF16)

    def proj(name):
        j = _PROJ_NAMES.index(name)
        return _dot(hb, w_ref[:, j * BRANCH_W:(j + 1) * BRANCH_W])

    cos = cos_ref[...]
    sin_signed = sin_ref[...]
    first_half = (_lane_iota((tm, LANES)) % (2 * ROPE_FREQS)) < ROPE_FREQS

    def rope_all(t):
        return jnp.concatenate(
            [_rope(t[:, p * LANES:(p + 1) * LANES], cos, sin_signed, first_half) for p in range(N_PAIRS)],
            axis=1)

    for name, d, lf_ref, k_ref in (('d_f_fwd', 0, lff_ref, kf_ref), ('d_f_bwd', 1, lfb_ref, kb_ref)):
        z = proj(name)
        lb = lb_ref[0, d:d + 1, :]
        sg = jax.nn.sigmoid(z)
        lf_ref[0] = jnp.log(jnp.maximum(lb + (1.0 - lb) * sg, F_FLOOR))
        k_ref[0] = ((1.0 - lb) * (1.0 - sg)).astype(BF16)

    cv = proj('c_v')
    vn = (cv * lax.rsqrt(jnp.mean(cv * cv, axis=-1, keepdims=True) + EPS) * gn_ref[...]).astype(BF16)
    low_group = _lane_iota((GMLP_CHUNK, LANES)) < (LANES // 2)
    for ch in range(tm // GMLP_CHUNK):
        r0 = ch * GMLP_CHUNK
        for p in range(N_PAIRS):
            slab = vn[r0:r0 + GMLP_CHUNK, p * LANES:(p + 1) * LANES]
            zero = jnp.zeros_like(slab)
            stacked = jnp.concatenate([jnp.where(low_group, slab, zero), jnp.where(low_group, zero, slab)], axis=0)
            mixed = _dot(ws_ref[p], stacked) + bs_ref[:, p * LANES:(p + 1) * LANES]
            mx_ref[0, r0:r0 + GMLP_CHUNK, p * LANES:(p + 1) * LANES] = mixed.astype(BF16)

    q = proj('a_q') * (HEAD_DIM ** -0.5 * LOG2E)
    q_rot = rope_all(q)
    kr_ref[0] = rope_all(proj('a_k')).astype(BF16)
    v = proj('a_v')
    for ch in range(tm // NA_VCHUNK):
        rs = slice(ch * NA_VCHUNK, (ch + 1) * NA_VCHUNK)
        qp_ref[0, ch] = q[rs].T.astype(BF16)
        qr_ref[0, ch] = q_rot[rs].T.astype(BF16)
        v_ref[0, ch] = v[rs].T.astype(BF16)
    bx_ref[0] = proj('b_x').astype(BF16)
    dq_ref[0] = proj('d_q').astype(BF16)
    di_ref[0] = proj('d_i').astype(BF16)


def _proj_call(x, sh, sc, gpre, w_proj, cos, sin_signed, gn, ws, bs_tab, lb, tm, per_batch_mod):
    b, n, _ = x.shape
    nt = n // tm
    mod_map = (lambda i, j: (i, 0, 0)) if per_batch_mod else (lambda i, j: (0, 0, 0))
    row_spec = pl.BlockSpec((1, tm, BRANCH_W), lambda i, j: (i, j, 0))
    vt_spec = pl.BlockSpec((1, tm // NA_VCHUNK, BRANCH_W, NA_VCHUNK), lambda i, j: (i, j, 0, 0))
    bf = jax.ShapeDtypeStruct((b, n, BRANCH_W), BF16)
    vt = jax.ShapeDtypeStruct((b, n // NA_VCHUNK, BRANCH_W, NA_VCHUNK), BF16)
    f32 = jax.ShapeDtypeStruct((b, n, BRANCH_W), F32)
    return pl.pallas_call(
        _proj_kernel,
        out_shape=(vt, vt, bf, vt, bf, bf, bf, bf, f32, bf, f32, bf),
        grid=(b, nt),
        in_specs=[pl.BlockSpec((1, tm, D_MODEL), lambda i, j: (i, j, 0)),
                  pl.BlockSpec((1, 1, D_MODEL), mod_map),
                  pl.BlockSpec((1, 1, D_MODEL), mod_map),
                  _resident((1, D_MODEL)),
                  _resident((D_MODEL, len(_PROJ_NAMES) * BRANCH_W)),
                  pl.BlockSpec((tm, LANES), lambda i, j: (j, 0)),
                  pl.BlockSpec((tm, LANES), lambda i, j: (j, 0)),
                  _resident((1, BRANCH_W)),
                  _resident((GMLP_GROUPS // 2, GMLP_CHUNK, 2 * GMLP_CHUNK)),
                  _resident((GMLP_CHUNK, BRANCH_W)),
                  _resident((1, 2, BRANCH_W))],
        out_specs=(vt_spec, vt_spec, row_spec, vt_spec) + (row_spec,) * 8,
        compiler_params=_cparams(("arbitrary", "arbitrary")),
        name="branch_proj",
    )(x, sh, sc, gpre, w_proj, cos, sin_signed, gn, ws, bs_tab, lb)


def _nattn_kernel(qr_ref, qp_ref, k_ref, vt_ref, kc_ref, vct_ref, bias_ref, o_ref):
    rows = k_ref.shape[1] // GRID_W
    r0 = pl.program_id(1) * NA_TILE_ROWS
    kb0 = jnp.clip(r0 - NA_WIN_H // 2, 0, rows - NA_BAND_ROWS)
    start = pl.multiple_of(kb0 * GRID_W, NA_VCHUNK)
    c0 = kb0 // (NA_VCHUNK // GRID_W)
    band = NA_BAND_ROWS * GRID_W
    nq = NA_TILE_ROWS * GRID_W

    def lanes(p):
        return slice(p * LANES, (p + 1) * LANES)

    head0_rows = lax.broadcasted_iota(jnp.int32, (LANES, nq), 0) < HEAD_DIM

    def stack_heads(qt):
        zero = jnp.zeros_like(qt)
        return jnp.concatenate([jnp.where(head0_rows, qt, zero), jnp.where(head0_rows, zero, qt)], axis=1)

    def scores(p):
        s_ctx = _dot(kc_ref[0, :, lanes(p)], stack_heads(qp_ref[0, 0, lanes(p), :]))
        s_band = _dot(k_ref[0, pl.ds(start, band), lanes(p)], stack_heads(qr_ref[0, 0, lanes(p), :]))
        bias = jnp.concatenate([bias_ref[0, :, 2 * p + hh].reshape(band, nq) for hh in range(2)], axis=1)
        return s_ctx, s_band + bias

    def softmax(s_ctx, s_band):
        p_ctx, p_band = [], []
        for cb in range(2 * nq // LANES):
            cs = slice(cb * LANES, (cb + 1) * LANES)
            sc, sb = s_ctx[:, cs], s_band[:, cs]
            mx = jnp.maximum(jnp.max(sc, axis=0, keepdims=True), jnp.max(sb, axis=0, keepdims=True))
            p_ctx.append(jnp.exp2(sc - mx).astype(BF16))
            p_band.append(jnp.exp2(sb - mx).astype(BF16))
        return jnp.concatenate(p_ctx, axis=1), jnp.concatenate(p_band, axis=1)

    ones_rows = jnp.ones((NA_DEN_ROWS, NA_VCHUNK), BF16)

    def values(p, p_ctx, p_band):
        outs = []
        for hh in range(2):
            qs = slice(hh * nq, (hh + 1) * nq)
            ch = slice(p * LANES + hh * HEAD_DIM, p * LANES + (hh + 1) * HEAD_DIM)
            acc = None
            for j in range(vct_ref.shape[1]):
                lhs = jnp.concatenate([vct_ref[0, j, ch, :], ones_rows], axis=0)
                term = _dot(lhs, p_ctx[j * NA_VCHUNK:(j + 1) * NA_VCHUNK, qs])
                acc = term if acc is None else acc + term
            for j in range(band // NA_VCHUNK):
                lhs = jnp.concatenate([vt_ref[0, c0 + j, ch, :], ones_rows], axis=0)
                acc = acc + _dot(lhs, p_band[j * NA_VCHUNK:(j + 1) * NA_VCHUNK, qs])
            outs.append(acc[0:HEAD_DIM] * (1.0 / acc[HEAD_DIM:HEAD_DIM + 1]))
        o_ref[0, :, lanes(p)] = jnp.concatenate(outs, axis=0).T.astype(BF16)

    s_val, p_val = {}, {}
    for t in range(N_PAIRS + 2):
        if t < N_PAIRS:
            s_val[t] = scores(t)
        if 0 <= t - 1 < N_PAIRS:
            p_val[t - 1] = softmax(*s_val.pop(t - 1))
        if 0 <= t - 2 < N_PAIRS:
            values(t - 2, *p_val.pop(t - 2))


def _nattn_call(qrt, qpt, k, vt, kc, vct, bias):
    b, n, _ = k.shape
    rows = n // GRID_W
    nt = rows // NA_TILE_ROWS
    lc = kc.shape[1]
    nq = NA_TILE_ROWS * GRID_W

    def bias_map(i, t):
        return (jnp.where(t == 0, 0, jnp.where(t == nt - 1, 2, 1)), 0, 0, 0, 0)

    assert nq == NA_VCHUNK
    q_spec = pl.BlockSpec((1, 1, BRANCH_W, NA_VCHUNK), lambda i, t: (i, t, 0, 0))
    o_spec = pl.BlockSpec((1, nq, BRANCH_W), lambda i, t: (i, t, 0))
    full = pl.BlockSpec((1, n, BRANCH_W), lambda i, t: (i, 0, 0))
    full_t = pl.BlockSpec((1, n // NA_VCHUNK, BRANCH_W, NA_VCHUNK), lambda i, t: (i, 0, 0, 0))
    ctx = pl.BlockSpec((1, lc, BRANCH_W), lambda i, t: (i, 0, 0))
    ctx_t = pl.BlockSpec((1, lc // NA_VCHUNK, BRANCH_W, NA_VCHUNK), lambda i, t: (i, 0, 0, 0))
    return pl.pallas_call(
        _nattn_kernel,
        out_shape=jax.ShapeDtypeStruct((b, n, BRANCH_W), BF16),
        grid=(b, nt),
        in_specs=[q_spec, q_spec, full, full_t, ctx, ctx_t,
                  pl.BlockSpec((1, NA_BAND_ROWS, N_HEADS, GRID_W, nq), bias_map)],
        out_specs=o_spec,
        compiler_params=_cparams(("arbitrary", "arbitrary")),
        name="neighbourhood_attention",
    )(qrt, qpt, k, vt, kc, vct, bias)


def _cattn_kernel(qt_ref, k_ref, vt_ref, o_ref):
    lc = k_ref.shape[1]
    head0_rows = lax.broadcasted_iota(jnp.int32, (LANES, lc), 0) < HEAD_DIM
    for p in range(N_PAIRS):
        ls = slice(p * LANES, (p + 1) * LANES)
        qt = jnp.concatenate([qt_ref[0, j, ls, :] for j in range(qt_ref.shape[1])], axis=1)
        vt = jnp.concatenate([vt_ref[0, j, ls, :] for j in range(vt_ref.shape[1])], axis=1)
        k = k_ref[0, :, ls]
        zero = jnp.zeros_like(qt)
        outs = []
        for hh in range(2):
            s = _dot(k, jnp.where(head0_rows, qt, zero) if hh == 0 else jnp.where(head0_rows, zero, qt))
            e = jnp.exp2(s - jnp.max(s, axis=0, keepdims=True))
            acc = _dot(vt[hh * HEAD_DIM:(hh + 1) * HEAD_DIM], e.astype(BF16))
            outs.append(acc * (1.0 / jnp.sum(e, axis=0, keepdims=True)))
        o_ref[0, :, ls] = jnp.concatenate(outs, axis=0).T.astype(BF16)


def _cattn_call(qt, k, vt):
    b, lc, _ = k.shape
    spec = pl.BlockSpec((1, lc, BRANCH_W), lambda i: (i, 0, 0))
    spec_t = pl.BlockSpec((1, lc // NA_VCHUNK, BRANCH_W, NA_VCHUNK), lambda i: (i, 0, 0, 0))
    return pl.pallas_call(
        _cattn_kernel,
        out_shape=jax.ShapeDtypeStruct((b, lc, BRANCH_W), BF16),
        grid=(b,),
        in_specs=[spec_t, spec, spec_t],
        out_specs=spec,
        compiler_params=_cparams(("arbitrary",)),
        name="context_attention",
    )(qt, k, vt)


def _attention_bias(rpb, rows):
    col = np.arange(GRID_W)
    col_start = np.clip(col - NA_WIN_W // 2, 0, GRID_W - NA_WIN_W)
    valid = (col[:, None] >= col_start[None, :]) & (col[:, None] < col_start[None, :] + NA_WIN_W)
    width = 2 * NA_WIN_W - 1
    flipped = jnp.pad(rpb.astype(F32)[:, :, ::-1] * LOG2E, ((0, 0), (0, 0), (GRID_W - NA_WIN_W,) * 2))
    tab = jnp.stack([flipped[:, :, GRID_W - 1 - kc:2 * GRID_W - 1 - kc] for kc in range(GRID_W)], axis=2)
    assert flipped.shape[-1] == width + 2 * (GRID_W - NA_WIN_W) and tab.shape[-1] == GRID_W
    tab = jnp.where(valid[None, None], tab, NEG_INF)
    pad = NA_BAND_ROWS - NA_WIN_H
    tab = jnp.pad(tab.transpose(1, 0, 2, 3), ((pad, pad), (0, 0), (0, 0), (0, 0)), constant_values=NEG_INF)
    kinds = []
    for r0 in (0, NA_TILE_ROWS, rows - NA_TILE_ROWS):
        kb0 = int(np.clip(r0 - NA_WIN_H // 2, 0, rows - NA_BAND_ROWS))
        per_query_row = []
        for r in range(r0, r0 + NA_TILE_ROWS):
            lo = kb0 - r + NA_WIN_H - 1 + pad
            kr = kb0 + np.arange(NA_BAND_ROWS)
            rs = int(np.clip(r - NA_WIN_H // 2, 0, rows - NA_WIN_H))
            in_win = (kr >= rs) & (kr < rs + NA_WIN_H)
            per_query_row.append(jnp.where(in_win[:, None, None, None], tab[lo:lo + NA_BAND_ROWS], NEG_INF))
        kinds.append(jnp.concatenate(per_query_row, axis=3))
    return jnp.stack(kinds)


_KRON = 8


@functools.lru_cache(maxsize=None)
def _fourier_consts(n):
    rows = n // GRID_W
    k1 = np.arange(rows)[:, None, None, None]
    l1 = np.arange(_KRON)[None, :, None, None]
    n1 = np.arange(rows)[None, None, :, None]
    l2 = np.arange(_KRON)[None, None, None, :]
    a_cos, a_sin = [], []
    for j in range(GRID_W // _KRON):
        ang = 2.0 * np.pi * k1 * (GRID_W * n1 + _KRON * j + l1) / n
        same = (l1 == l2)
        a_cos.append((np.cos(ang) * same).reshape(rows * _KRON, rows * _KRON))
        a_sin.append((-np.sin(ang) * same).reshape(rows * _KRON, rows * _KRON))
    k2 = np.arange(GRID_W)[:, None, None, None]
    ang = 2.0 * np.pi * k2 * np.arange(GRID_W)[None, None, None, :] / GRID_W
    same = (np.arange(_KRON)[None, :, None, None] == np.arange(_KRON)[None, None, :, None])
    b_cos = (np.cos(ang) * same).reshape(GRID_W * _KRON, _KRON * GRID_W)
    b_sin = (np.sin(ang) * same).reshape(GRID_W * _KRON, _KRON * GRID_W)
    b_re = np.concatenate([b_cos, b_sin], axis=1)
    b_im = np.concatenate([-b_sin, b_cos], axis=1)
    return (np.stack(a_cos).astype(np.float32), np.stack(a_sin).astype(np.float32),
            b_re.astype(np.float32), b_im.astype(np.float32))


@functools.lru_cache(maxsize=None)
def _channel_dft():
    c = np.arange(FNET_GROUP_W)
    ang = 2.0 * np.pi * np.outer(c, c) / FNET_GROUP_W
    return np.concatenate([np.cos(ang), np.sin(ang)], axis=0).astype(np.float32)


@functools.lru_cache(maxsize=None)
def _dense_dft(n):
    t = np.arange(n)
    ang = 2.0 * np.pi * np.outer(t, t) / n
    return np.cos(ang).astype(np.float32), (-np.sin(ang)).astype(np.float32)


def _fold_channel_map(cs_ref, wf_ref, fold_ref, norm):
    c_hi, c_lo = _split2(cs_ref[...] * norm)
    for g in range(FNET_GROUPS):
        w_hi, w_lo = _split2(wf_ref[g])
        fold_ref[g] = (_dot(c_hi, w_hi) + _dot(c_hi, w_lo) + _dot(c_lo, w_hi)).astype(BF16)


def _channel_stage(xr, xi, fold_ref):
    outs = []
    for g in range(FNET_GROUPS):
        ls = slice(g * FNET_GROUP_W, (g + 1) * FNET_GROUP_W)
        xg = jnp.concatenate([xr[:, ls], xi[:, ls]], axis=1).astype(BF16)
        outs.append(_dot(xg, fold_ref[g]))
    return jnp.concatenate(outs, axis=1)


def _fourier_kernel(x_ref, ac_ref, as_ref, bre_ref, bim_ref, cs_ref, wf_ref, o_ref, s_ref, fold_ref, *, norm):
    @pl.when(pl.program_id(0) == 0)
    def _():
        _fold_channel_map(cs_ref, wf_ref, fold_ref, norm)

    rows = x_ref.shape[1]
    blk = rows * _KRON
    pair = 2 * _KRON
    for jj in range(GRID_W // pair):
        xt = x_ref[0, :, jj * pair:(jj + 1) * pair, :].astype(F32)
        re, im = [], []
        for half in range(2):
            xc = xt[:, half * _KRON:(half + 1) * _KRON, :].reshape(blk, BRANCH_W).astype(BF16)
            re.append(_dot(ac_ref[2 * jj + half], xc).reshape(rows, _KRON, BRANCH_W))
            im.append(_dot(as_ref[2 * jj + half], xc).reshape(rows, _KRON, BRANCH_W))
        s_ref[0, :, jj * pair:(jj + 1) * pair, :] = jnp.concatenate(re, axis=1).astype(BF16)
        s_ref[1, :, jj * pair:(jj + 1) * pair, :] = jnp.concatenate(im, axis=1).astype(BF16)
    sblk = _KRON * GRID_W
    for mm in range(rows // pair):
        ys = []
        for half in range(2):
            m0 = (2 * mm + half) * _KRON
            rhs = jnp.concatenate([s_ref[0, m0:m0 + _KRON].reshape(sblk, BRANCH_W),
                                   s_ref[1, m0:m0 + _KRON].reshape(sblk, BRANCH_W)], axis=0)
            xr = _dot(bre_ref[...], rhs)
            xi = _dot(bim_ref[...], rhs)
            ys.append(_channel_stage(xr, xi, fold_ref).reshape(GRID_W, _KRON, BRANCH_W))
        o_ref[0, :, mm * pair:(mm + 1) * pair, :] = jnp.concatenate(ys, axis=1).astype(BF16)


def _fourier_call(bx, wf):
    b, n, _ = bx.shape
    rows = n // GRID_W
    a_cos, a_sin, b_re, b_im = (jnp.asarray(t, BF16) for t in _fourier_consts(n))
    cs = jnp.asarray(_channel_dft(), F32)
    norm = float(1.0 / np.sqrt(n * FNET_GROUP_W))
    x4 = bx.reshape(b, rows, GRID_W, BRANCH_W)
    out = pl.pallas_call(
        functools.partial(_fourier_kernel, norm=norm),
        out_shape=jax.ShapeDtypeStruct((b, GRID_W, rows, BRANCH_W), BF16),
        grid=(b,),
        in_specs=[pl.BlockSpec((1, rows, GRID_W, BRANCH_W), lambda i: (i, 0, 0, 0)),
                  _resident(a_cos.shape), _resident(a_sin.shape),
                  _resident(b_re.shape), _resident(b_im.shape),
                  _resident(cs.shape), _resident(wf.shape)],
        out_specs=pl.BlockSpec((1, GRID_W, rows, BRANCH_W), lambda i: (i, 0, 0, 0)),
        scratch_shapes=[pltpu.VMEM((2, rows, GRID_W, BRANCH_W), BF16),
                        pltpu.VMEM((FNET_GROUPS, 2 * FNET_GROUP_W, FNET_GROUP_W), BF16)],
        compiler_params=_cparams(("arbitrary",)),
        name="fourier_mix",
    )(x4, a_cos, a_sin, b_re, b_im, cs, wf)
    return out.reshape(b, n, BRANCH_W)


def _fourier_ctx_kernel(x_ref, c_ref, s_ref, cs_ref, wf_ref, o_ref, fold_ref, *, norm):
    @pl.when(pl.program_id(0) == 0)
    def _():
        _fold_channel_map(cs_ref, wf_ref, fold_ref, norm)

    x = x_ref[0]
    xr = _dot(c_ref[...], x)
    xi = _dot(s_ref[...], x)
    o_ref[0] = _channel_stage(xr, xi, fold_ref).astype(BF16)


def _fourier_ctx_call(bx, wf):
    b, n, _ = bx.shape
    cn, sn = (jnp.asarray(t, BF16) for t in _dense_dft(n))
    cs = jnp.asarray(_channel_dft(), F32)
    norm = float(1.0 / np.sqrt(n * FNET_GROUP_W))
    spec = pl.BlockSpec((1, n, BRANCH_W), lambda i: (i, 0, 0))
    return pl.pallas_call(
        functools.partial(_fourier_ctx_kernel, norm=norm),
        out_shape=jax.ShapeDtypeStruct((b, n, BRANCH_W), BF16),
        grid=(b,),
        in_specs=[spec, _resident(cn.shape), _resident(sn.shape), _resident(cs.shape), _resident(wf.shape)],
        out_specs=spec,
        scratch_shapes=[pltpu.VMEM((FNET_GROUPS, 2 * FNET_GROUP_W, FNET_GROUP_W), BF16)],
        compiler_params=_cparams(("arbitrary",)),
        name="fourier_mix_context",
    )(bx, cn, sn, cs, wf)


def _block_diag(t):
    lo = _lane_iota(t.shape) < HEAD_DIM
    z = jnp.zeros_like(t)
    return jnp.concatenate([jnp.where(lo, t, z), jnp.where(lo, z, t)], axis=0)


def _hgrn_needed(j, reverse):
    nsub = HGRN_CHUNK // HGRN_SUB
    return list(range(0, j + 1)) if reverse else list(range(j, nsub))


def _hgrn_prepare(q, k, i, a, reverse):
    c = HGRN_CHUNK
    nsub = c // HGRN_SUB

    def level(r):
        return a[r:r + 1, :]

    zero_row = jnp.zeros((1, BRANCH_W), F32)
    if reverse:
        refs = [level((s + 1) * HGRN_SUB) if s + 1 < nsub else zero_row for s in range(nsub)]
        a_end = a[0:1, :]
    else:
        refs = [level(s * HGRN_SUB - 1) if s > 0 else zero_row for s in range(nsub)]
        a_end = a[c - 1:c, :]
    ref_rows = jnp.concatenate([jnp.broadcast_to(r, (HGRN_SUB, BRANCH_W)) for r in refs], axis=0)
    qf = q.astype(F32)
    kf = k.astype(F32)
    lift = ref_rows - a
    k_own = (kf * jnp.exp2(jnp.minimum(lift, HGRN_EXP_CLAMP))).astype(BF16)

    def q_variant(j):
        parts = []
        for s in _hgrn_needed(j, reverse):
            rs = slice(s * HGRN_SUB, (s + 1) * HGRN_SUB)
            parts.append((qf[rs] * jnp.exp2(a[rs] - refs[j])).astype(BF16))
        return jnp.concatenate(parts, axis=0)

    q_var = [q_variant(j) for j in range(nsub)]
    return dict(
        q_stack=jnp.concatenate(q_var, axis=0),
        q_in=q_var[nsub - 1] if reverse else q_var[0],
        k_own=k_own,
        k_out=(kf * jnp.exp2(a_end - a)).astype(BF16),
        decay_end=jnp.exp2(a_end),
        max_lift=jnp.max(lift, axis=0, keepdims=True),
        i=i, reverse=reverse)


def _hgrn_scores(ops):
    c = HGRN_CHUNK
    nsub = c // HGRN_SUB
    reverse = ops['reverse']
    src = _lane_iota((c, LANES)) % HEAD_DIM
    step = lax.broadcasted_iota(jnp.int32, (c, LANES), 0)
    seen = (src >= step) if reverse else (src <= step)
    src_sub = (_lane_iota((HGRN_SUB, LANES)) % HEAD_DIM) // HGRN_SUB
    where_blk, off = {}, 0
    for j in range(nsub):
        for s in _hgrn_needed(j, reverse):
            where_blk[(j, s)] = off
            off += HGRN_SUB
    out = []
    for p in range(N_PAIRS):
        ls = slice(p * LANES, (p + 1) * LANES)
        res = _dot_nt(ops['q_stack'][:, ls], _block_diag(ops['k_own'][:, ls]))
        rows = []
        for s in range(nsub):
            blk = None
            for j in range(nsub):
                if (j, s) in where_blk:
                    piece = res[where_blk[(j, s)]:where_blk[(j, s)] + HGRN_SUB]
                    blk = piece if blk is None else jnp.where(src_sub == j, piece, blk)
            rows.append(blk)
        out.append(jnp.where(seen, jnp.concatenate(rows, axis=0), 0.0).astype(BF16))
    return out


def _hgrn_local(ops, scores):
    low_rows = lax.broadcasted_iota(jnp.int32, (LANES, LANES), 0) < HEAD_DIM
    same_head = low_rows == (_lane_iota((LANES, LANES)) < HEAD_DIM)
    o_intra, upd = [], []
    for p in range(N_PAIRS):
        ls = slice(p * LANES, (p + 1) * LANES)
        ip = ops['i'][:, ls]
        o_intra.append(_dot(scores[p], _block_diag(ip)))
        upd.append(jnp.where(same_head, _dot_tn(ip, ops['k_out'][:, ls]), 0.0))
    return o_intra, upd


def _hgrn_carry(ops, o_intra, upd, state_ref, d):
    outs = []
    for p in range(N_PAIRS):
        ls = slice(p * LANES, (p + 1) * LANES)
        st = state_ref[d, p]
        outs.append(o_intra[p] + _dot_nt(ops['q_in'][:, ls], st.astype(BF16)))
        state_ref[d, p] = ops['decay_end'][:, ls] * st + upd[p]
    return jnp.concatenate(outs, axis=1)


def _hgrn_exact_tile(q_ref, k_ref, i_ref, a_ref, o_ref, state_ref, bb, d, reverse, q_sc, k_sc, i_sc):
    c = HGRN_CHUNK
    nchunk = q_ref.shape[1] // c
    lane_head = _lane_iota((BRANCH_W, BRANCH_W)) // HEAD_DIM
    row_head = lax.broadcasted_iota(jnp.int32, (BRANCH_W, BRANCH_W), 0) // HEAD_DIM
    head_sum = (lane_head == row_head).astype(BF16)
    step = lax.broadcasted_iota(jnp.int32, (c, BRANCH_W), 0)
    low_rows = lax.broadcasted_iota(jnp.int32, (LANES, LANES), 0) < HEAD_DIM
    same_head = low_rows == (_lane_iota((LANES, LANES)) < HEAD_DIM)
    for cix in (range(nchunk - 1, -1, -1) if reverse else range(nchunk)):
        rs = slice(cix * c, (cix + 1) * c)
        q_sc[...] = q_ref[bb, rs, :].astype(F32)
        k_sc[...] = k_ref[bb, rs, :].astype(F32)
        i_sc[...] = i_ref[bb, rs, :].astype(F32)
        qf = q_sc[...]
        kf = k_sc[...]
        a = a_ref[d, rs, :]

        def one_source(s, acc, a=a, qf=qf, cix=cix):
            a_s = a_ref[d, pl.ds(cix * c + s, 1), :]
            w = qf * (k_sc[pl.ds(s, 1), :] * jnp.exp2(jnp.minimum(a - a_s, 0.0)))
            w = jnp.where((step <= s) if reverse else (step >= s), w, 0.0)
            hi, lo = _split2(w)
            return acc + (_dot(hi, head_sum) + _dot(lo, head_sum)) * i_sc[pl.ds(s, 1), :]

        o_intra = lax.fori_loop(0, c, one_source, jnp.zeros((c, BRANCH_W), F32))
        a_end = a[0:1, :] if reverse else a[c - 1:c, :]
        q_in = (qf * jnp.exp2(a)).astype(BF16)
        k_out = (kf * jnp.exp2(a_end - a)).astype(BF16)
        decay_end = jnp.exp2(a_end)
        ib = i_ref[bb, rs, :]
        outs = []
        for p in range(N_PAIRS):
            ls = slice(p * LANES, (p + 1) * LANES)
            st = state_ref[d, p]
            outs.append(o_intra[:, ls] + _dot_nt(q_in[:, ls], st.astype(BF16)))
            upd = jnp.where(same_head, _dot_tn(ib[:, ls], k_out[:, ls]), 0.0)
            state_ref[d, p] = decay_end[:, ls] * st + upd
        o_ref[bb, rs, :] = jnp.concatenate(outs, axis=1).astype(BF16)


def _hgrn_kernel(qf_ref, if_ref, lff_ref, kf_ref, qb_ref, ib_ref, lfb_ref, kb_ref, s0f_ref, s0b_ref,
                 of_ref, ob_ref, sf_ref, sb_ref, state_ref, backup_ref, a_ref, q_sc, k_sc, i_sc):
    j = pl.program_id(1)
    nb = qf_ref.shape[0]
    nchunk = qf_ref.shape[1] // HGRN_CHUNK

    @pl.when(j == 0)
    def _():
        for bb in range(nb):
            state_ref[2 * bb] = s0f_ref[bb]
            state_ref[2 * bb + 1] = s0b_ref[bb]

    backup_ref[...] = state_ref[...]

    tm = qf_ref.shape[1]
    row = lax.broadcasted_iota(jnp.int32, (tm, tm), 0)
    col = lax.broadcasted_iota(jnp.int32, (tm, tm), 1)
    same_chunk = (row // HGRN_CHUNK) == (col // HGRN_CHUNK)

    def cum(lf, reverse):
        tri = (same_chunk & ((col >= row) if reverse else (col <= row))).astype(BF16)
        hi, lo = _split2(lf * LOG2E)
        return _dot(tri, hi) + _dot(tri, lo)

    a_f = [cum(lff_ref[bb], False) for bb in range(nb)]
    a_b = [cum(lfb_ref[bb], True) for bb in range(nb)]

    todo = []
    for cix in range(nchunk):
        bix = nchunk - 1 - cix
        for bb in range(nb):
            todo.append((bb, 2 * bb, of_ref, slice(cix * HGRN_CHUNK, (cix + 1) * HGRN_CHUNK),
                         qf_ref, kf_ref, if_ref, a_f[bb], False))
            todo.append((bb, 2 * bb + 1, ob_ref, slice(bix * HGRN_CHUNK, (bix + 1) * HGRN_CHUNK),
                         qb_ref, kb_ref, ib_ref, a_b[bb], True))
    ops, scores, local, lifts = {}, {}, {}, []
    for t in range(len(todo) + 3):
        if t < len(todo):
            bb, _, _, rs, q_ref, k_ref, i_ref, a, reverse = todo[t]
            ops[t] = _hgrn_prepare(q_ref[bb, rs, :], k_ref[bb, rs, :], i_ref[bb, rs, :], a[rs, :], reverse)
            lifts.append(ops[t]['max_lift'])
        if 0 <= t - 1 < len(todo):
            scores[t - 1] = _hgrn_scores(ops[t - 1])
        if 0 <= t - 2 < len(todo):
            local[t - 2] = _hgrn_local(ops[t - 2], scores.pop(t - 2))
        if 0 <= t - 3 < len(todo):
            bb, slot, o_ref, rs = todo[t - 3][:4]
            o_ref[bb, rs, :] = _hgrn_carry(ops.pop(t - 3), *local.pop(t - 3), state_ref, slot).astype(BF16)

    @pl.when(jnp.max(functools.reduce(jnp.maximum, lifts)) > HGRN_EXP_CLAMP)
    def _():
        state_ref[...] = backup_ref[...]
        for bb in range(nb):
            a_ref[2 * bb] = a_f[bb]
            a_ref[2 * bb + 1] = a_b[bb]
            _hgrn_exact_tile(qf_ref, kf_ref, if_ref, a_ref, of_ref, state_ref, bb, 2 * bb, False,
                             q_sc, k_sc, i_sc)
            _hgrn_exact_tile(qb_ref, kb_ref, ib_ref, a_ref, ob_ref, state_ref, bb, 2 * bb + 1, True,
                             q_sc, k_sc, i_sc)

    @pl.when(j == pl.num_programs(1) - 1)
    def _():
        for bb in range(nb):
            sf_ref[bb] = state_ref[2 * bb]
            sb_ref[bb] = state_ref[2 * bb + 1]


def _hgrn_call(q, i, lff, kf, lfb, kb, s0f, s0b, tm):
    b, n, _ = q.shape
    nt = n // tm
    nb = HGRN_BATCH_ROWS if b % HGRN_BATCH_ROWS == 0 else 1
    fwd = pl.BlockSpec((nb, tm, BRANCH_W), lambda bi, j: (bi, j, 0))
    bwd = pl.BlockSpec((nb, tm, BRANCH_W), lambda bi, j: (bi, nt - 1 - j, 0))
    st = pl.BlockSpec((nb, N_PAIRS, LANES, LANES), lambda bi, j: (bi, 0, 0, 0))
    o_shape = jax.ShapeDtypeStruct((b, n, BRANCH_W), BF16)
    s_shape = jax.ShapeDtypeStruct((b, N_PAIRS, LANES, LANES), F32)
    return pl.pallas_call(
        _hgrn_kernel,
        out_shape=(o_shape, o_shape, s_shape, s_shape),
        grid=(b // nb, nt),
        in_specs=[fwd, fwd, fwd, fwd, bwd, bwd, bwd, bwd, st, st],
        out_specs=(fwd, bwd, st, st),
        scratch_shapes=[pltpu.VMEM((2 * nb, N_PAIRS, LANES, LANES), F32),
                        pltpu.VMEM((2 * nb, N_PAIRS, LANES, LANES), F32),
                        pltpu.VMEM((2 * nb, tm, BRANCH_W), F32),
                        pltpu.VMEM((HGRN_CHUNK, BRANCH_W), F32),
                        pltpu.VMEM((HGRN_CHUNK, BRANCH_W), F32),
                        pltpu.VMEM((HGRN_CHUNK, BRANCH_W), F32)],
        compiler_params=_cparams(("arbitrary", "arbitrary")),
        name="hgrn_scan",
    )(q, i, lff, kf, q, i, lfb, kb, s0f, s0b)


def _merge_kernel(x_ref, sh_ref, sc_ref, gt_ref, gpre_ref, gpost_ref, w_ref, oa_ref, ob_ref, mx_ref,
                  of_ref, obk_ref, hn_ref, hm_ref, wb_ref, wo_ref, o_ref):
    x = x_ref[0]
    hb = _normed_input(x, gpre_ref[...], sc_ref[0], sh_ref[0]).astype(BF16)

    def proj(name):
        j = _MERGE_NAMES.index(name)
        return _dot(hb, w_ref[:, j * BRANCH_W:(j + 1) * BRANCH_W])

    ya = oa_ref[0].astype(F32) * _silu(proj('a_g'))
    yb = ob_ref[0].astype(F32) * _silu(proj('b_g'))
    yc = proj('c_u') * mx_ref[0].astype(F32) * _silu(proj('c_g'))
    o = of_ref[0].astype(F32) + obk_ref[0].astype(F32)
    ms = _dot((o * o).astype(BF16), hm_ref[...])
    yd = o * lax.rsqrt(ms + EPS) * hn_ref[...] * _silu(proj('d_g'))

    g0 = len(_MERGE_NAMES) * BRANCH_W
    merged = None
    for r, y in enumerate((ya, yb, yc, yd)):
        gate = _dot(hb, w_ref[:, g0 + r * D_MODEL:g0 + (r + 1) * D_MODEL])
        term = jax.nn.sigmoid(gate) * _dot(y.astype(BF16), wb_ref[r])
        merged = term if merged is None else merged + term
    out = _dot(merged.astype(BF16), wo_ref[...])
    post = out * lax.rsqrt(jnp.mean(out * out, axis=-1, keepdims=True) + EPS) * gpost_ref[...]
    o_ref[0] = x + gt_ref[0] * post


def _merge_call(x, sh, sc, gt, gpre, gpost, w_merge, oa, ob, mx, of, obk, hn, hmean, wb, wo, tm,
                per_batch_mod):
    b, n, _ = x.shape
    nt = n // tm
    mod_map = (lambda i, j: (i, 0, 0)) if per_batch_mod else (lambda i, j: (0, 0, 0))
    x_spec = pl.BlockSpec((1, tm, D_MODEL), lambda i, j: (i, j, 0))
    br_spec = pl.BlockSpec((1, tm, BRANCH_W), lambda i, j: (i, j, 0))
    mod_spec = pl.BlockSpec((1, 1, D_MODEL), mod_map)
    return pl.pallas_call(
        _merge_kernel,
        out_shape=jax.ShapeDtypeStruct((b, n, D_MODEL), F32),
        grid=(b, nt),
        in_specs=[x_spec, mod_spec, mod_spec, mod_spec,
                  _resident((1, D_MODEL)), _resident((1, D_MODEL)),
                  _resident(w_merge.shape),
                  br_spec, br_spec, br_spec, br_spec, br_spec,
                  _resident((1, BRANCH_W)), _resident((BRANCH_W, BRANCH_W)),
                  _resident(wb.shape), _resident(wo.shape)],
        out_specs=x_spec,
        compiler_params=_cparams(("arbitrary", "arbitrary")),
        name="branch_merge",
    )(x, sh, sc, gt, gpre, gpost, w_merge, oa, ob, mx, of, obk, hn, hmean, wb, wo)


def _rope_tables(n_tok, rotate):
    if not rotate:
        return jnp.ones((n_tok, LANES), F32), jnp.zeros((n_tok, LANES), F32)
    t = jnp.arange(n_tok, dtype=jnp.int32)
    pos = jnp.stack([t // GRID_W, t % GRID_W], axis=-1).astype(F32)
    inv = ROPE_THETA ** (-jnp.arange(ROPE_FREQS, dtype=F32) * 2.0 / (2 * ROPE_FREQS))
    ang = pos[:, :, None] * inv
    cos = jnp.repeat(jnp.cos(ang)[:, :, None, :], 2, axis=2).reshape(n_tok, HEAD_DIM)
    sin = jnp.sin(ang)
    sin_signed = jnp.stack([-sin, sin], axis=2).reshape(n_tok, HEAD_DIM)
    return jnp.tile(cos, (1, 2)), jnp.tile(sin_signed, (1, 2))


def _gather_cols(w_in_l, names):
    return jnp.concatenate([w_in_l[:, _IN_COL[nm] * BRANCH_W:(_IN_COL[nm] + 1) * BRANCH_W] for nm in names],
                           axis=1)


def _row_tile(n):
    return 512 if n % 512 == 0 else 256


def kernel(x, c, ctx, c_ctx, w_ada, b_ada, g_pre, g_post, w_in, na_rpb, fnet_w, gmlp_norm_g, gmlp_ws,
           gmlp_bs, hgrn_lb_logits, hgrn_norm_g, w_branch, w_out):
    batch, n_tok, _ = x.shape
    n_ctx = ctx.shape[1]
    depth = w_in.shape[0]

    w_in_b = w_in.astype(BF16)
    w_proj = [_gather_cols(w_in_b[l], _PROJ_NAMES) for l in range(depth)]
    w_merge = [jnp.concatenate([_gather_cols(w_in_b[l], _MERGE_NAMES), w_in_b[l][:, _GATE_COL0:]], axis=1)
               for l in range(depth)]
    w_branch_b = w_branch.astype(BF16)
    w_out_b = w_out.astype(BF16)
    gmlp_ws_b = gmlp_ws.astype(BF16).reshape(depth, GMLP_GROUPS // 2, 2, GMLP_CHUNK, GMLP_CHUNK)
    gmlp_ws_b = gmlp_ws_b.transpose(0, 1, 3, 2, 4).reshape(depth, GMLP_GROUPS // 2, GMLP_CHUNK, 2 * GMLP_CHUNK)
    bs_tab = jnp.repeat(jnp.swapaxes(gmlp_bs, 1, 2), BRANCH_W // GMLP_GROUPS, axis=2)
    head_mean = jnp.asarray(np.kron(np.eye(N_HEADS), np.ones((HEAD_DIM, HEAD_DIM)) / HEAD_DIM), BF16)
    cos_x, sin_x = _rope_tables(n_tok, True)
    cos_c, sin_c = _rope_tables(n_ctx, False)

    c_all = jnp.concatenate([c, jnp.broadcast_to(c_ctx[None, :], (8, D_MODEL))], axis=0)
    mod = _modulation(c_all, w_ada, b_ada)
    lower = _lower_bounds(hgrn_lb_logits)

    zero_state = jnp.zeros((batch, N_PAIRS, LANES, LANES), F32)
    tm_x = _row_tile(n_tok)
    tm_c = _row_tile(n_ctx)

    for l in range(depth):
        with_ctx = l < depth - 1
        mod_x = [mod[l, :batch, i * D_MODEL:(i + 1) * D_MODEL].reshape(batch, 1, D_MODEL) for i in range(3)]
        mod_c = [mod[l, batch:batch + 1, i * D_MODEL:(i + 1) * D_MODEL].reshape(1, 1, D_MODEL) for i in range(3)]
        gpre = g_pre[l].reshape(1, D_MODEL)
        gpost = g_post[l].reshape(1, D_MODEL)
        gn = gmlp_norm_g[l].reshape(1, BRANCH_W)
        hn = hgrn_norm_g[l].reshape(1, BRANCH_W)
        lb = lower[l].reshape(1, 2, BRANCH_W)
        bias = _attention_bias(na_rpb[l], n_tok // GRID_W)

        (_, qp_c, k_c, v_c, bx_c, mx_c, dq_c, di_c, lff_c, kf_c, lfb_c, kb_c) = _proj_call(
            ctx, mod_c[0], mod_c[1], gpre, w_proj[l], cos_c, sin_c, gn, gmlp_ws_b[l], bs_tab[l], lb,
            tm_c, False)
        of_c, ob_c, st_f, st_b = _hgrn_call(dq_c, di_c, lff_c, kf_c, lfb_c, kb_c, zero_state, zero_state,
                                            min(HGRN_ROWS, n_ctx))

        (qr, qp, k, v, bx, mx, dq, di, lff, kf, lfb, kb) = _proj_call(
            x, mod_x[0], mod_x[1], gpre, w_proj[l], cos_x, sin_x, gn, gmlp_ws_b[l], bs_tab[l], lb,
            tm_x, True)
        oa = _nattn_call(qr, qp, k, v, k_c, v_c, bias)
        ob = _fourier_call(bx, fnet_w[l])
        of, obk, _, _ = _hgrn_call(dq, di, lff, kf, lfb, kb, st_f, st_b, min(HGRN_ROWS, n_tok))
        x = _merge_call(x, mod_x[0], mod_x[1], mod_x[2], gpre, gpost, w_merge[l], oa, ob, mx, of, obk,
                        hn, head_mean, w_branch_b[l], w_out_b[l], tm_x, True)

        if with_ctx:
            oa_c = _cattn_call(qp_c, k_c, v_c)
            ob_c2 = _fourier_ctx_call(bx_c, fnet_w[l])
            ctx = _merge_call(ctx, mod_c[0], mod_c[1], mod_c[2], gpre, gpost, w_merge[l], oa_c, ob_c2, mx_c,
                              of_c, ob_c, hn, head_mean, w_branch_b[l], w_out_b[l], tm_c, False)
    return x
```

```python
import functools

import numpy as np
import jax
import jax.numpy as jnp
from jax import lax
from jax.experimental import pallas as pl
from jax.experimental.pallas import tpu as pltpu

F32 = jnp.float32
BF16 = jnp.bfloat16

D_MODEL = 1024
BRANCH_W = 512
N_BRANCH = 4
GRID_W = 64
HEAD_DIM = 64
N_HEADS = 8
LANES = 128
N_PAIRS = BRANCH_W // LANES
NA_WIN_H = 8
NA_WIN_W = 16
NA_TILE_ROWS = 4
NA_BAND_ROWS = 12
NA_VCHUNK = 256
NA_DEN_ROWS = 16
LOG2E = 1.4426950408889634
ROPE_THETA = 10000.0
ROPE_FREQS = 16
FNET_GROUPS = 4
FNET_GROUP_W = 128
GMLP_CHUNK = 128
GMLP_GROUPS = 8
HGRN_CHUNK = 64
HGRN_SUB = 16
HGRN_EXP_CLAMP = 115.0
HGRN_ROWS = 256
HGRN_BATCH_ROWS = 2
EPS = 1e-6
F_FLOOR = 1e-30
NEG_INF = -1e30

VMEM_LIMIT = 56 * 2**20

_IN_COL = {'a_q': 0, 'a_k': 1, 'a_v': 2, 'a_g': 3, 'b_x': 4, 'b_g': 5, 'c_u': 6, 'c_v': 7, 'c_g': 8,
           'd_q': 9, 'd_f_fwd': 10, 'd_f_bwd': 11, 'd_i': 12, 'd_g': 13}
_PROJ_NAMES = ('a_q', 'a_k', 'a_v', 'b_x', 'c_v', 'd_q', 'd_f_fwd', 'd_f_bwd', 'd_i')
_MERGE_NAMES = ('c_u', 'a_g', 'b_g', 'c_g', 'd_g')
_GATE_COL0 = 14 * BRANCH_W


def _cparams(sem):
    return pltpu.CompilerParams(dimension_semantics=sem, vmem_limit_bytes=VMEM_LIMIT)


def _resident(shape):
    nd = len(shape)
    return pl.BlockSpec(shape, lambda *_: (0,) * nd, pipeline_mode=pl.Buffered(1))


def _silu(t):
    return t * jax.nn.sigmoid(t)


def _lane_iota(shape):
    return lax.broadcasted_iota(jnp.int32, shape, len(shape) - 1)


def _dot(a, b):
    return jnp.dot(a, b, preferred_element_type=F32)


def _dot_nt(a, b):
    return lax.dot_general(a, b, (((1,), (1,)), ((), ())), preferred_element_type=F32)


def _dot_tn(a, b):
    return lax.dot_general(a, b, (((0,), (0,)), ((), ())), preferred_element_type=F32)


def _split2(t):
    hi = t.astype(BF16)
    return hi, (t - hi.astype(F32)).astype(BF16)


def _normed_input(x, gpre, sc, sh):
    ms = jnp.mean(x * x, axis=-1, keepdims=True)
    h = x * lax.rsqrt(ms + EPS) * gpre
    return h * (1.0 + sc) + sh


def _mod_kernel(c_ref, w_ref, b_ref, o_ref):
    s = _silu(c_ref[...]).astype(BF16)
    o_ref[0] = _dot(s, w_ref[0].astype(BF16)) + b_ref[0]


def _modulation(c_all, w_ada, b_ada):
    depth = w_ada.shape[0]
    rows = c_all.shape[0]
    tn = 1024
    return pl.pallas_call(
        _mod_kernel,
        out_shape=jax.ShapeDtypeStruct((depth, rows, 3 * D_MODEL), F32),
        grid=(depth, 3 * D_MODEL // tn),
        in_specs=[pl.BlockSpec((rows, D_MODEL), lambda l, j: (0, 0)),
                  pl.BlockSpec((1, D_MODEL, tn), lambda l, j: (l, 0, j)),
                  pl.BlockSpec((1, 1, tn), lambda l, j: (l, 0, j))],
        out_specs=pl.BlockSpec((1, rows, tn), lambda l, j: (l, 0, j)),
        compiler_params=_cparams(("arbitrary", "arbitrary")),
        name="adaln_modulation",
    )(c_all, w_ada, b_ada.reshape(depth, 1, 3 * D_MODEL))


def _lb_kernel(lg_ref, o_ref):
    depth = lg_ref.shape[0]
    lg = [lg_ref[l] for l in range(depth)]
    m = functools.reduce(jnp.maximum, lg)
    e = [jnp.exp(t - m) for t in lg]
    tot = functools.reduce(lambda a, b: a + b, e)
    sm = [t / tot for t in e]
    run = jnp.zeros_like(sm[0])
    for l in range(depth):
        run = run + sm[l]
        o_ref[l] = jnp.maximum(run - sm[0], 0.0)


def _lower_bounds(lb_logits):
    return pl.pallas_call(
        _lb_kernel,
        out_shape=jax.ShapeDtypeStruct(lb_logits.shape, F32),
        name="hgrn_lower_bounds",
    )(lb_logits)


def _rope(t, cos, sin_signed, first_half):
    up = pltpu.roll(t, LANES - ROPE_FREQS, 1)
    down = pltpu.roll(t, ROPE_FREQS, 1)
    return t * cos + jnp.where(first_half, up, down) * sin_signed


def _proj_kernel(x_ref, sh_ref, sc_ref, gpre_ref, w_ref, cos_ref, sin_ref, gn_ref, ws_ref, bs_ref,
                 lb_ref, qr_ref, qp_ref, kr_ref, v_ref, bx_ref, mx_ref, dq_ref, di_ref,
                 lff_ref, kf_ref, lfb_ref, kb_ref):
    tm = x_ref.shape[1]
    hb = _normed_input(x_ref[0], gpre_ref[...], sc_ref[0], sh_ref[0]).astype(BF16)

    def proj(name):
        j = _PROJ_NAMES.index(name)
        return _dot(hb, w_ref[:, j * BRANCH_W:(j + 1) * BRANCH_W])

    cos = cos_ref[...]
    sin_signed = sin_ref[...]
    first_half = (_lane_iota((tm, LANES)) % (2 * ROPE_FREQS)) < ROPE_FREQS

    def rope_all(t):
        return jnp.concatenate(
            [_rope(t[:, p * LANES:(p + 1) * LANES], cos, sin_signed, first_half) for p in range(N_PAIRS)],
            axis=1)

    for name, d, lf_ref, k_ref in (('d_f_fwd', 0, lff_ref, kf_ref), ('d_f_bwd', 1, lfb_ref, kb_ref)):
        z = proj(name)
        lb = lb_ref[0, d:d + 1, :]
        sg = jax.nn.sigmoid(z)
        lf_ref[0] = jnp.log(jnp.maximum(lb + (1.0 - lb) * sg, F_FLOOR))
        k_ref[0] = ((1.0 - lb) * (1.0 - sg)).astype(BF16)

    cv = proj('c_v')
    vn = (cv * lax.rsqrt(jnp.mean(cv * cv, axis=-1, keepdims=True) + EPS) * gn_ref[...]).astype(BF16)
    low_group = _lane_iota((GMLP_CHUNK, LANES)) < (LANES // 2)
    for ch in range(tm // GMLP_CHUNK):
        r0 = ch * GMLP_CHUNK
        for p in range(N_PAIRS):
            slab = vn[r0:r0 + GMLP_CHUNK, p * LANES:(p + 1) * LANES]
            zero = jnp.zeros_like(slab)
            stacked = jnp.concatenate([jnp.where(low_group, slab, zero), jnp.where(low_group, zero, slab)], axis=0)
            mixed = _dot(ws_ref[p], stacked) + bs_ref[:, p * LANES:(p + 1) * LANES]
            mx_ref[0, r0:r0 + GMLP_CHUNK, p * LANES:(p + 1) * LANES] = mixed.astype(BF16)

    q = proj('a_q') * (HEAD_DIM ** -0.5 * LOG2E)
    q_rot = rope_all(q)
    kr_ref[0] = rope_all(proj('a_k')).astype(BF16)
    v = proj('a_v')
    for ch in range(tm // NA_VCHUNK):
        rs = slice(ch * NA_VCHUNK, (ch + 1) * NA_VCHUNK)
        qp_ref[0, ch] = q[rs].T.astype(BF16)
        qr_ref[0, ch] = q_rot[rs].T.astype(BF16)
        v_ref[0, ch] = v[rs].T.astype(BF16)
    bx_ref[0] = proj('b_x').astype(BF16)
    dq_ref[0] = proj('d_q').astype(BF16)
    di_ref[0] = proj('d_i').astype(BF16)


def _proj_call(x, sh, sc, gpre, w_proj, cos, sin_signed, gn, ws, bs_tab, lb, tm, per_batch_mod):
    b, n, _ = x.shape
    nt = n // tm
    mod_map = (lambda i, j: (i, 0, 0)) if per_batch_mod else (lambda i, j: (0, 0, 0))
    row_spec = pl.BlockSpec((1, tm, BRANCH_W), lambda i, j: (i, j, 0))
    vt_spec = pl.BlockSpec((1, tm // NA_VCHUNK, BRANCH_W, NA_VCHUNK), lambda i, j: (i, j, 0, 0))
    bf = jax.ShapeDtypeStruct((b, n, BRANCH_W), BF16)
    vt = jax.ShapeDtypeStruct((b, n // NA_VCHUNK, BRANCH_W, NA_VCHUNK), BF16)
    f32 = jax.ShapeDtypeStruct((b, n, BRANCH_W), F32)
    return pl.pallas_call(
        _proj_kernel,
        out_shape=(vt, vt, bf, vt, bf, bf, bf, bf, f32, bf, f32, bf),
        grid=(b, nt),
        in_specs=[pl.BlockSpec((1, tm, D_MODEL), lambda i, j: (i, j, 0)),
                  pl.BlockSpec((1, 1, D_MODEL), mod_map),
                  pl.BlockSpec((1, 1, D_MODEL), mod_map),
                  _resident((1, D_MODEL)),
                  _resident((D_MODEL, len(_PROJ_NAMES) * BRANCH_W)),
                  pl.BlockSpec((tm, LANES), lambda i, j: (j, 0)),
                  pl.BlockSpec((tm, LANES), lambda i, j: (j, 0)),
                  _resident((1, BRANCH_W)),
                  _resident((GMLP_GROUPS // 2, GMLP_CHUNK, 2 * GMLP_CHUNK)),
                  _resident((GMLP_CHUNK, BRANCH_W)),
                  _resident((1, 2, BRANCH_W))],
        out_specs=(vt_spec, vt_spec, row_spec, vt_spec) + (row_spec,) * 8,
        compiler_params=_cparams(("arbitrary", "arbitrary")),
        name="branch_proj",
    )(x, sh, sc, gpre, w_proj, cos, sin_signed, gn, ws, bs_tab, lb)


def _nattn_kernel(qr_ref, qp_ref, k_ref, vt_ref, kc_ref, vct_ref, bias_ref, o_ref):
    rows = k_ref.shape[1] // GRID_W
    r0 = pl.program_id(1) * NA_TILE_ROWS
    kb0 = jnp.clip(r0 - NA_WIN_H // 2, 0, rows - NA_BAND_ROWS)
    start = pl.multiple_of(kb0 * GRID_W, NA_VCHUNK)
    c0 = kb0 // (NA_VCHUNK // GRID_W)
    band = NA_BAND_ROWS * GRID_W
    nq = NA_TILE_ROWS * GRID_W

    def lanes(p):
        return slice(p * LANES, (p + 1) * LANES)

    head0_rows = lax.broadcasted_iota(jnp.int32, (LANES, nq), 0) < HEAD_DIM

    def stack_heads(qt):
        zero = jnp.zeros_like(qt)
        return jnp.concatenate([jnp.where(head0_rows, qt, zero), jnp.where(head0_rows, zero, qt)], axis=1)

    def scores(p):
        s_ctx = _dot(kc_ref[0, :, lanes(p)], stack_heads(qp_ref[0, 0, lanes(p), :]))
        s_band = _dot(k_ref[0, pl.ds(start, band), lanes(p)], stack_heads(qr_ref[0, 0, lanes(p), :]))
        bias = jnp.concatenate([bias_ref[0, :, 2 * p + hh].reshape(band, nq) for hh in range(2)], axis=1)
        return s_ctx, s_band + bias

    def softmax(s_ctx, s_band):
        p_ctx, p_band = [], []
        for cb in range(2 * nq // LANES):
            cs = slice(cb * LANES, (cb + 1) * LANES)
            sc, sb = s_ctx[:, cs], s_band[:, cs]
            mx = jnp.maximum(jnp.max(sc, axis=0, keepdims=True), jnp.max(sb, axis=0, keepdims=True))
            p_ctx.append(jnp.exp2(sc - mx).astype(BF16))
            p_band.append(jnp.exp2(sb - mx).astype(BF16))
        return jnp.concatenate(p_ctx, axis=1), jnp.concatenate(p_band, axis=1)

    ones_rows = jnp.ones((NA_DEN_ROWS, NA_VCHUNK), BF16)

    def values(p, p_ctx, p_band):
        outs = []
        for hh in range(2):
            qs = slice(hh * nq, (hh + 1) * nq)
            ch = slice(p * LANES + hh * HEAD_DIM, p * LANES + (hh + 1) * HEAD_DIM)
            acc = None
            for j in range(vct_ref.shape[1]):
                lhs = jnp.concatenate([vct_ref[0, j, ch, :], ones_rows], axis=0)
                term = _dot(lhs, p_ctx[j * NA_VCHUNK:(j + 1) * NA_VCHUNK, qs])
                acc = term if acc is None else acc + term
            for j in range(band // NA_VCHUNK):
                lhs = jnp.concatenate([vt_ref[0, c0 + j, ch, :], ones_rows], axis=0)
                acc = acc + _dot(lhs, p_band[j * NA_VCHUNK:(j + 1) * NA_VCHUNK, qs])
            outs.append(acc[0:HEAD_DIM] * (1.0 / acc[HEAD_DIM:HEAD_DIM + 1]))
        o_ref[0, :, lanes(p)] = jnp.concatenate(outs, axis=0).T.astype(BF16)

    s_val, p_val = {}, {}
    for t in range(N_PAIRS + 2):
        if t < N_PAIRS:
            s_val[t] = scores(t)
        if 0 <= t - 1 < N_PAIRS:
            p_val[t - 1] = softmax(*s_val.pop(t - 1))
        if 0 <= t - 2 < N_PAIRS:
            values(t - 2, *p_val.pop(t - 2))


def _nattn_call(qrt, qpt, k, vt, kc, vct, bias):
    b, n, _ = k.shape
    rows = n // GRID_W
    nt = rows // NA_TILE_ROWS
    lc = kc.shape[1]
    nq = NA_TILE_ROWS * GRID_W

    def bias_map(i, t):
        return (jnp.where(t == 0, 0, jnp.where(t == nt - 1, 2, 1)), 0, 0, 0, 0)

    assert nq == NA_VCHUNK
    q_spec = pl.BlockSpec((1, 1, BRANCH_W, NA_VCHUNK), lambda i, t: (i, t, 0, 0))
    o_spec = pl.BlockSpec((1, nq, BRANCH_W), lambda i, t: (i, t, 0))
    full = pl.BlockSpec((1, n, BRANCH_W), lambda i, t: (i, 0, 0))
    full_t = pl.BlockSpec((1, n // NA_VCHUNK, BRANCH_W, NA_VCHUNK), lambda i, t: (i, 0, 0, 0))
    ctx = pl.BlockSpec((1, lc, BRANCH_W), lambda i, t: (i, 0, 0))
    ctx_t = pl.BlockSpec((1, lc // NA_VCHUNK, BRANCH_W, NA_VCHUNK), lambda i, t: (i, 0, 0, 0))
    return pl.pallas_call(
        _nattn_kernel,
        out_shape=jax.ShapeDtypeStruct((b, n, BRANCH_W), BF16),
        grid=(b, nt),
        in_specs=[q_spec, q_spec, full, full_t, ctx, ctx_t,
                  pl.BlockSpec((1, NA_BAND_ROWS, N_HEADS, GRID_W, nq), bias_map)],
        out_specs=o_spec,
        compiler_params=_cparams(("arbitrary", "arbitrary")),
        name="neighbourhood_attention",
    )(qrt, qpt, k, vt, kc, vct, bias)


def _cattn_kernel(qt_ref, k_ref, vt_ref, o_ref):
    lc = k_ref.shape[1]
    head0_rows = lax.broadcasted_iota(jnp.int32, (LANES, lc), 0) < HEAD_DIM
    for p in range(N_PAIRS):
        ls = slice(p * LANES, (p + 1) * LANES)
        qt = jnp.concatenate([qt_ref[0, j, ls, :] for j in range(qt_ref.shape[1])], axis=1)
        vt = jnp.concatenate([vt_ref[0, j, ls, :] for j in range(vt_ref.shape[1])], axis=1)
        k = k_ref[0, :, ls]
        zero = jnp.zeros_like(qt)
        outs = []
        for hh in range(2):
            s = _dot(k, jnp.where(head0_rows, qt, zero) if hh == 0 else jnp.where(head0_rows, zero, qt))
            e = jnp.exp2(s - jnp.max(s, axis=0, keepdims=True))
            acc = _dot(vt[hh * HEAD_DIM:(hh + 1) * HEAD_DIM], e.astype(BF16))
            outs.append(acc * (1.0 / jnp.sum(e, axis=0, keepdims=True)))
        o_ref[0, :, ls] = jnp.concatenate(outs, axis=0).T.astype(BF16)


def _cattn_call(qt, k, vt):
    b, lc, _ = k.shape
    spec = pl.BlockSpec((1, lc, BRANCH_W), lambda i: (i, 0, 0))
    spec_t = pl.BlockSpec((1, lc // NA_VCHUNK, BRANCH_W, NA_VCHUNK), lambda i: (i, 0, 0, 0))
    return pl.pallas_call(
        _cattn_kernel,
        out_shape=jax.ShapeDtypeStruct((b, lc, BRANCH_W), BF16),
        grid=(b,),
        in_specs=[spec_t, spec, spec_t],
        out_specs=spec,
        compiler_params=_cparams(("arbitrary",)),
        name="context_attention",
    )(qt, k, vt)


def _attention_bias(rpb, rows):
    col = np.arange(GRID_W)
    col_start = np.clip(col - NA_WIN_W // 2, 0, GRID_W - NA_WIN_W)
    valid = (col[:, None] >= col_start[None, :]) & (col[:, None] < col_start[None, :] + NA_WIN_W)
    width = 2 * NA_WIN_W - 1
    flipped = jnp.pad(rpb.astype(F32)[:, :, ::-1] * LOG2E, ((0, 0), (0, 0), (GRID_W - NA_WIN_W,) * 2))
    tab = jnp.stack([flipped[:, :, GRID_W - 1 - kc:2 * GRID_W - 1 - kc] for kc in range(GRID_W)], axis=2)
    assert flipped.shape[-1] == width + 2 * (GRID_W - NA_WIN_W) and tab.shape[-1] == GRID_W
    tab = jnp.where(valid[None, None], tab, NEG_INF)
    pad = NA_BAND_ROWS - NA_WIN_H
    tab = jnp.pad(tab.transpose(1, 0, 2, 3), ((pad, pad), (0, 0), (0, 0), (0, 0)), constant_values=NEG_INF)
    kinds = []
    for r0 in (0, NA_TILE_ROWS, rows - NA_TILE_ROWS):
        kb0 = int(np.clip(r0 - NA_WIN_H // 2, 0, rows - NA_BAND_ROWS))
        per_query_row = []
        for r in range(r0, r0 + NA_TILE_ROWS):
            lo = kb0 - r + NA_WIN_H - 1 + pad
            kr = kb0 + np.arange(NA_BAND_ROWS)
            rs = int(np.clip(r - NA_WIN_H // 2, 0, rows - NA_WIN_H))
            in_win = (kr >= rs) & (kr < rs + NA_WIN_H)
            per_query_row.append(jnp.where(in_win[:, None, None, None], tab[lo:lo + NA_BAND_ROWS], NEG_INF))
        kinds.append(jnp.concatenate(per_query_row, axis=3))
    return jnp.stack(kinds)


_KRON = 8


@functools.lru_cache(maxsize=None)
def _fourier_consts(n):
    rows = n // GRID_W
    k1 = np.arange(rows)[:, None, None, None]
    l1 = np.arange(_KRON)[None, :, None, None]
    n1 = np.arange(rows)[None, None, :, None]
    l2 = np.arange(_KRON)[None, None, None, :]
    a_cos, a_sin = [], []
    for j in range(GRID_W // _KRON):
        ang = 2.0 * np.pi * k1 * (GRID_W * n1 + _KRON * j + l1) / n
        same = (l1 == l2)
        a_cos.append((np.cos(ang) * same).reshape(rows * _KRON, rows * _KRON))
        a_sin.append((-np.sin(ang) * same).reshape(rows * _KRON, rows * _KRON))
    k2 = np.arange(GRID_W)[:, None, None, None]
    ang = 2.0 * np.pi * k2 * np.arange(GRID_W)[None, None, None, :] / GRID_W
    same = (np.arange(_KRON)[None, :, None, None] == np.arange(_KRON)[None, None, :, None])
    b_cos = (np.cos(ang) * same).reshape(GRID_W * _KRON, _KRON * GRID_W)
    b_sin = (np.sin(ang) * same).reshape(GRID_W * _KRON, _KRON * GRID_W)
    b_re = np.concatenate([b_cos, b_sin], axis=1)
    b_im = np.concatenate([-b_sin, b_cos], axis=1)
    return (np.stack(a_cos).astype(np.float32), np.stack(a_sin).astype(np.float32),
            b_re.astype(np.float32), b_im.astype(np.float32))


@functools.lru_cache(maxsize=None)
def _channel_dft():
    c = np.arange(FNET_GROUP_W)
    ang = 2.0 * np.pi * np.outer(c, c) / FNET_GROUP_W
    return np.concatenate([np.cos(ang), np.sin(ang)], axis=0).astype(np.float32)


@functools.lru_cache(maxsize=None)
def _dense_dft(n):
    t = np.arange(n)
    ang = 2.0 * np.pi * np.outer(t, t) / n
    return np.cos(ang).astype(np.float32), (-np.sin(ang)).astype(np.float32)


def _fold_channel_map(cs_ref, wf_ref, fold_ref, norm):
    c_hi, c_lo = _split2(cs_ref[...] * norm)
    for g in range(FNET_GROUPS):
        w_hi, w_lo = _split2(wf_ref[g])
        fold_ref[g] = (_dot(c_hi, w_hi) + _dot(c_hi, w_lo) + _dot(c_lo, w_hi)).astype(BF16)


def _channel_stage(xr, xi, fold_ref):
    outs = []
    for g in range(FNET_GROUPS):
        ls = slice(g * FNET_GROUP_W, (g + 1) * FNET_GROUP_W)
        xg = jnp.concatenate([xr[:, ls], xi[:, ls]], axis=1).astype(BF16)
        outs.append(_dot(xg, fold_ref[g]))
    return jnp.concatenate(outs, axis=1)


def _fourier_kernel(x_ref, ac_ref, as_ref, bre_ref, bim_ref, cs_ref, wf_ref, o_ref, s_ref, fold_ref, *, norm):
    @pl.when(pl.program_id(0) == 0)
    def _():
        _fold_channel_map(cs_ref, wf_ref, fold_ref, norm)

    rows = x_ref.shape[1]
    blk = rows * _KRON
    pair = 2 * _KRON
    for jj in range(GRID_W // pair):
        xt = x_ref[0, :, jj * pair:(jj + 1) * pair, :].astype(F32)
        re, im = [], []
        for half in range(2):
            xc = xt[:, half * _KRON:(half + 1) * _KRON, :].reshape(blk, BRANCH_W).astype(BF16)
            re.append(_dot(ac_ref[2 * jj + half], xc).reshape(rows, _KRON, BRANCH_W))
            im.append(_dot(as_ref[2 * jj + half], xc).reshape(rows, _KRON, BRANCH_W))
        s_ref[0, :, jj * pair:(jj + 1) * pair, :] = jnp.concatenate(re, axis=1).astype(BF16)
        s_ref[1, :, jj * pair:(jj + 1) * pair, :] = jnp.concatenate(im, axis=1).astype(BF16)
    sblk = _KRON * GRID_W
    for mm in range(rows // pair):
        ys = []
        for half in range(2):
            m0 = (2 * mm + half) * _KRON
            rhs = jnp.concatenate([s_ref[0, m0:m0 + _KRON].reshape(sblk, BRANCH_W),
                                   s_ref[1, m0:m0 + _KRON].reshape(sblk, BRANCH_W)], axis=0)
            xr = _dot(bre_ref[...], rhs)
            xi = _dot(bim_ref[...], rhs)
            ys.append(_channel_stage(xr, xi, fold_ref).reshape(GRID_W, _KRON, BRANCH_W))
        o_ref[0, :, mm * pair:(mm + 1) * pair, :] = jnp.concatenate(ys, axis=1).astype(BF16)


def _fourier_call(bx, wf):
    b, n, _ = bx.shape
    rows = n // GRID_W
    a_cos, a_sin, b_re, b_im = (jnp.asarray(t, BF16) for t in _fourier_consts(n))
    cs = jnp.asarray(_channel_dft(), F32)
    norm = float(1.0 / np.sqrt(n * FNET_GROUP_W))
    x4 = bx.reshape(b, rows, GRID_W, BRANCH_W)
    out = pl.pallas_call(
        functools.partial(_fourier_kernel, norm=norm),
        out_shape=jax.ShapeDtypeStruct((b, GRID_W, rows, BRANCH_W), BF16),
        grid=(b,),
        in_specs=[pl.BlockSpec((1, rows, GRID_W, BRANCH_W), lambda i: (i, 0, 0, 0)),
                  _resident(a_cos.shape), _resident(a_sin.shape),
                  _resident(b_re.shape), _resident(b_im.shape),
                  _resident(cs.shape), _resident(wf.shape)],
        out_specs=pl.BlockSpec((1, GRID_W, rows, BRANCH_W), lambda i: (i, 0, 0, 0)),
        scratch_shapes=[pltpu.VMEM((2, rows, GRID_W, BRANCH_W), BF16),
                        pltpu.VMEM((FNET_GROUPS, 2 * FNET_GROUP_W, FNET_GROUP_W), BF16)],
        compiler_params=_cparams(("arbitrary",)),
        name="fourier_mix",
    )(x4, a_cos, a_sin, b_re, b_im, cs, wf)
    return out.reshape(b, n, BRANCH_W)


def _fourier_ctx_kernel(x_ref, c_ref, s_ref, cs_ref, wf_ref, o_ref, fold_ref, *, norm):
    @pl.when(pl.program_id(0) == 0)
    def _():
        _fold_channel_map(cs_ref, wf_ref, fold_ref, norm)

    x = x_ref[0]
    xr = _dot(c_ref[...], x)
    xi = _dot(s_ref[...], x)
    o_ref[0] = _channel_stage(xr, xi, fold_ref).astype(BF16)


def _fourier_ctx_call(bx, wf):
    b, n, _ = bx.shape
    cn, sn = (jnp.asarray(t, BF16) for t in _dense_dft(n))
    cs = jnp.asarray(_channel_dft(), F32)
    norm = float(1.0 / np.sqrt(n * FNET_GROUP_W))
    spec = pl.BlockSpec((1, n, BRANCH_W), lambda i: (i, 0, 0))
    return pl.pallas_call(
        functools.partial(_fourier_ctx_kernel, norm=norm),
        out_shape=jax.ShapeDtypeStruct((b, n, BRANCH_W), BF16),
        grid=(b,),
        in_specs=[spec, _resident(cn.shape), _resident(sn.shape), _resident(cs.shape), _resident(wf.shape)],
        out_specs=spec,
        scratch_shapes=[pltpu.VMEM((FNET_GROUPS, 2 * FNET_GROUP_W, FNET_GROUP_W), BF16)],
        compiler_params=_cparams(("arbitrary",)),
        name="fourier_mix_context",
    )(bx, cn, sn, cs, wf)


def _block_diag(t):
    lo = _lane_iota(t.shape) < HEAD_DIM
    z = jnp.zeros_like(t)
    return jnp.concatenate([jnp.where(lo, t, z), jnp.where(lo, z, t)], axis=0)


def _hgrn_needed(j, reverse):
    nsub = HGRN_CHUNK // HGRN_SUB
    return list(range(0, j + 1)) if reverse else list(range(j, nsub))


def _hgrn_prepare(q, k, i, a, reverse):
    c = HGRN_CHUNK
    nsub = c // HGRN_SUB

    def level(r):
        return a[r:r + 1, :]

    zero_row = jnp.zeros((1, BRANCH_W), F32)
    if reverse:
        refs = [level((s + 1) * HGRN_SUB) if s + 1 < nsub else zero_row for s in range(nsub)]
        a_end = a[0:1, :]
    else:
        refs = [level(s * HGRN_SUB - 1) if s > 0 else zero_row for s in range(nsub)]
        a_end = a[c - 1:c, :]
    ref_rows = jnp.concatenate([jnp.broadcast_to(r, (HGRN_SUB, BRANCH_W)) for r in refs], axis=0)
    qf = q.astype(F32)
    kf = k.astype(F32)
    lift = ref_rows - a
    k_own = (kf * jnp.exp2(jnp.minimum(lift, HGRN_EXP_CLAMP))).astype(BF16)

    def q_variant(j):
        parts = []
        for s in _hgrn_needed(j, reverse):
            rs = slice(s * HGRN_SUB, (s + 1) * HGRN_SUB)
            parts.append((qf[rs] * jnp.exp2(a[rs] - refs[j])).astype(BF16))
        return jnp.concatenate(parts, axis=0)

    q_var = [q_variant(j) for j in range(nsub)]
    return dict(
        q_stack=jnp.concatenate(q_var, axis=0),
        q_in=q_var[nsub - 1] if reverse else q_var[0],
        k_own=k_own,
        k_out=(kf * jnp.exp2(a_end - a)).astype(BF16),
        decay_end=jnp.exp2(a_end),
        max_lift=jnp.max(lift, axis=0, keepdims=True),
        i=i, reverse=reverse)


def _hgrn_scores(ops):
    c = HGRN_CHUNK
    nsub = c // HGRN_SUB
    reverse = ops['reverse']
    src = _lane_iota((c, LANES)) % HEAD_DIM
    step = lax.broadcasted_iota(jnp.int32, (c, LANES), 0)
    seen = (src >= step) if reverse else (src <= step)
    src_sub = (_lane_iota((HGRN_SUB, LANES)) % HEAD_DIM) // HGRN_SUB
    where_blk, off = {}, 0
    for j in range(nsub):
        for s in _hgrn_needed(j, reverse):
            where_blk[(j, s)] = off
            off += HGRN_SUB
    out = []
    for p in range(N_PAIRS):
        ls = slice(p * LANES, (p + 1) * LANES)
        res = _dot_nt(ops['q_stack'][:, ls], _block_diag(ops['k_own'][:, ls]))
        rows = []
        for s in range(nsub):
            blk = None
            for j in range(nsub):
                if (j, s) in where_blk:
                    piece = res[where_blk[(j, s)]:where_blk[(j, s)] + HGRN_SUB]
                    blk = piece if blk is None else jnp.where(src_sub == j, piece, blk)
            rows.append(blk)
        out.append(jnp.where(seen, jnp.concatenate(rows, axis=0), 0.0).astype(BF16))
    return out


def _hgrn_local(ops, scores):
    low_rows = lax.broadcasted_iota(jnp.int32, (LANES, LANES), 0) < HEAD_DIM
    same_head = low_rows == (_lane_iota((LANES, LANES)) < HEAD_DIM)
    o_intra, upd = [], []
    for p in range(N_PAIRS):
        ls = slice(p * LANES, (p + 1) * LANES)
        ip = ops['i'][:, ls]
        o_intra.append(_dot(scores[p], _block_diag(ip)))
        upd.append(jnp.where(same_head, _dot_tn(ip, ops['k_out'][:, ls]), 0.0))
    return o_intra, upd


def _hgrn_carry(ops, o_intra, upd, state_ref, d):
    outs = []
    for p in range(N_PAIRS):
        ls = slice(p * LANES, (p + 1) * LANES)
        st = state_ref[d, p]
        outs.append(o_intra[p] + _dot_nt(ops['q_in'][:, ls], st.astype(BF16)))
        state_ref[d, p] = ops['decay_end'][:, ls] * st + upd[p]
    return jnp.concatenate(outs, axis=1)


def _hgrn_exact_tile(q_ref, k_ref, i_ref, a_ref, o_ref, state_ref, bb, d, reverse, q_sc, k_sc, i_sc):
    c = HGRN_CHUNK
    nchunk = q_ref.shape[1] // c
    lane_head = _lane_iota((BRANCH_W, BRANCH_W)) // HEAD_DIM
    row_head = lax.broadcasted_iota(jnp.int32, (BRANCH_W, BRANCH_W), 0) // HEAD_DIM
    head_sum = (lane_head == row_head).astype(BF16)
    step = lax.broadcasted_iota(jnp.int32, (c, BRANCH_W), 0)
    low_rows = lax.broadcasted_iota(jnp.int32, (LANES, LANES), 0) < HEAD_DIM
    same_head = low_rows == (_lane_iota((LANES, LANES)) < HEAD_DIM)
    for cix in (range(nchunk - 1, -1, -1) if reverse else range(nchunk)):
        rs = slice(cix * c, (cix + 1) * c)
        q_sc[...] = q_ref[bb, rs, :].astype(F32)
        k_sc[...] = k_ref[bb, rs, :].astype(F32)
        i_sc[...] = i_ref[bb, rs, :].astype(F32)
        qf = q_sc[...]
        kf = k_sc[...]
        a = a_ref[d, rs, :]

        def one_source(s, acc, a=a, qf=qf, cix=cix):
            a_s = a_ref[d, pl.ds(cix * c + s, 1), :]
            w = qf * (k_sc[pl.ds(s, 1), :] * jnp.exp2(jnp.minimum(a - a_s, 0.0)))
            w = jnp.where((step <= s) if reverse else (step >= s), w, 0.0)
            hi, lo = _split2(w)
            return acc + (_dot(hi, head_sum) + _dot(lo, head_sum)) * i_sc[pl.ds(s, 1), :]

        o_intra = lax.fori_loop(0, c, one_source, jnp.zeros((c, BRANCH_W), F32))
        a_end = a[0:1, :] if reverse else a[c - 1:c, :]
        q_in = (qf * jnp.exp2(a)).astype(BF16)
        k_out = (kf * jnp.exp2(a_end - a)).astype(BF16)
        decay_end = jnp.exp2(a_end)
        ib = i_ref[bb, rs, :]
        outs = []
        for p in range(N_PAIRS):
            ls = slice(p * LANES, (p + 1) * LANES)
            st = state_ref[d, p]
            outs.append(o_intra[:, ls] + _dot_nt(q_in[:, ls], st.astype(BF16)))
            upd = jnp.where(same_head, _dot_tn(ib[:, ls], k_out[:, ls]), 0.0)
            state_ref[d, p] = decay_end[:, ls] * st + upd
        o_ref[bb, rs, :] = jnp.concatenate(outs, axis=1).astype(BF16)


def _hgrn_kernel(qf_ref, if_ref, lff_ref, kf_ref, qb_ref, ib_ref, lfb_ref, kb_ref, s0f_ref, s0b_ref,
                 of_ref, ob_ref, sf_ref, sb_ref, state_ref, backup_ref, a_ref, q_sc, k_sc, i_sc):
    j = pl.program_id(1)
    nb = qf_ref.shape[0]
    nchunk = qf_ref.shape[1] // HGRN_CHUNK

    @pl.when(j == 0)
    def _():
        for bb in range(nb):
            state_ref[2 * bb] = s0f_ref[bb]
            state_ref[2 * bb + 1] = s0b_ref[bb]

    backup_ref[...] = state_ref[...]

    tm = qf_ref.shape[1]
    row = lax.broadcasted_iota(jnp.int32, (tm, tm), 0)
    col = lax.broadcasted_iota(jnp.int32, (tm, tm), 1)
    same_chunk = (row // HGRN_CHUNK) == (col // HGRN_CHUNK)

    def cum(lf, reverse):
        tri = (same_chunk & ((col >= row) if reverse else (col <= row))).astype(BF16)
        hi, lo = _split2(lf * LOG2E)
        return _dot(tri, hi) + _dot(tri, lo)

    a_f = [cum(lff_ref[bb], False) for bb in range(nb)]
    a_b = [cum(lfb_ref[bb], True) for bb in range(nb)]

    todo = []
    for cix in range(nchunk):
        bix = nchunk - 1 - cix
        for bb in range(nb):
            todo.append((bb, 2 * bb, of_ref, slice(cix * HGRN_CHUNK, (cix + 1) * HGRN_CHUNK),
                         qf_ref, kf_ref, if_ref, a_f[bb], False))
            todo.append((bb, 2 * bb + 1, ob_ref, slice(bix * HGRN_CHUNK, (bix + 1) * HGRN_CHUNK),
                         qb_ref, kb_ref, ib_ref, a_b[bb], True))
    ops, scores, local, lifts = {}, {}, {}, []
    for t in range(len(todo) + 3):
        if t < len(todo):
            bb, _, _, rs, q_ref, k_ref, i_ref, a, reverse = todo[t]
            ops[t] = _hgrn_prepare(q_ref[bb, rs, :], k_ref[bb, rs, :], i_ref[bb, rs, :], a[rs, :], reverse)
            lifts.append(ops[t]['max_lift'])
        if 0 <= t - 1 < len(todo):
            scores[t - 1] = _hgrn_scores(ops[t - 1])
        if 0 <= t - 2 < len(todo):
            local[t - 2] = _hgrn_local(ops[t - 2], scores.pop(t - 2))
        if 0 <= t - 3 < len(todo):
            bb, slot, o_ref, rs = todo[t - 3][:4]
            o_ref[bb, rs, :] = _hgrn_carry(ops.pop(t - 3), *local.pop(t - 3), state_ref, slot).astype(BF16)

    @pl.when(jnp.max(functools.reduce(jnp.maximum, lifts)) > HGRN_EXP_CLAMP)
    def _():
        state_ref[...] = backup_ref[...]
        for bb in range(nb):
            a_ref[2 * bb] = a_f[bb]
            a_ref[2 * bb + 1] = a_b[bb]
            _hgrn_exact_tile(qf_ref, kf_ref, if_ref, a_ref, of_ref, state_ref, bb, 2 * bb, False,
                             q_sc, k_sc, i_sc)
            _hgrn_exact_tile(qb_ref, kb_ref, ib_ref, a_ref, ob_ref, state_ref, bb, 2 * bb + 1, True,
                             q_sc, k_sc, i_sc)

    @pl.when(j == pl.num_programs(1) - 1)
    def _():
        for bb in range(nb):
            sf_ref[bb] = state_ref[2 * bb]
            sb_ref[bb] = state_ref[2 * bb + 1]


def _hgrn_call(q, i, lff, kf, lfb, kb, s0f, s0b, tm):
    b, n, _ = q.shape
    nt = n // tm
    nb = HGRN_BATCH_ROWS if b % HGRN_BATCH_ROWS == 0 else 1
    fwd = pl.BlockSpec((nb, tm, BRANCH_W), lambda bi, j: (bi, j, 0))
    bwd = pl.BlockSpec((nb, tm, BRANCH_W), lambda bi, j: (bi, nt - 1 - j, 0))
    st = pl.BlockSpec((nb, N_PAIRS, LANES, LANES), lambda bi, j: (bi, 0, 0, 0))
    o_shape = jax.ShapeDtypeStruct((b, n, BRANCH_W), BF16)
    s_shape = jax.ShapeDtypeStruct((b, N_PAIRS, LANES, LANES), F32)
    return pl.pallas_call(
        _hgrn_kernel,
        out_shape=(o_shape, o_shape, s_shape, s_shape),
        grid=(b // nb, nt),
        in_specs=[fwd, fwd, fwd, fwd, bwd, bwd, bwd, bwd, st, st],
        out_specs=(fwd, bwd, st, st),
        scratch_shapes=[pltpu.VMEM((2 * nb, N_PAIRS, LANES, LANES), F32),
                        pltpu.VMEM((2 * nb, N_PAIRS, LANES, LANES), F32),
                        pltpu.VMEM((2 * nb, tm, BRANCH_W), F32),
                        pltpu.VMEM((HGRN_CHUNK, BRANCH_W), F32),
                        pltpu.VMEM((HGRN_CHUNK, BRANCH_W), F32),
                        pltpu.VMEM((HGRN_CHUNK, BRANCH_W), F32)],
        compiler_params=_cparams(("arbitrary", "arbitrary")),
        name="hgrn_scan",
    )(q, i, lff, kf, q, i, lfb, kb, s0f, s0b)


def _head_mean(t):
    lane = _lane_iota((t.shape[0], LANES))
    outs = []
    for p in range(t.shape[1] // LANES):
        s = t[:, p * LANES:(p + 1) * LANES]
        step = HEAD_DIM // 2
        while step >= 1:
            s = s + jnp.where((lane & step) != 0, pltpu.roll(s, step, 1), pltpu.roll(s, LANES - step, 1))
            step //= 2
        outs.append(s)
    return jnp.concatenate(outs, axis=1) * (1.0 / HEAD_DIM)


def _merge_kernel(x_ref, sh_ref, sc_ref, gt_ref, gpre_ref, gpost_ref, w_ref, oa_ref, ob_ref, mx_ref,
                  of_ref, obk_ref, hn_ref, wb_ref, wo_ref, o_ref):
    x = x_ref[0]
    hb = _normed_input(x, gpre_ref[...], sc_ref[0], sh_ref[0]).astype(BF16)

    def proj(name):
        j = _MERGE_NAMES.index(name)
        return _dot(hb, w_ref[:, j * BRANCH_W:(j + 1) * BRANCH_W])

    ya = oa_ref[0].astype(F32) * _silu(proj('a_g'))
    yb = ob_ref[0].astype(F32) * _silu(proj('b_g'))
    yc = proj('c_u') * mx_ref[0].astype(F32) * _silu(proj('c_g'))
    o = of_ref[0].astype(F32) + obk_ref[0].astype(F32)
    yd = o * lax.rsqrt(_head_mean(o * o) + EPS) * hn_ref[...] * _silu(proj('d_g'))

    g0 = len(_MERGE_NAMES) * BRANCH_W
    merged = None
    for r, y in enumerate((ya, yb, yc, yd)):
        gate = _dot(hb, w_ref[:, g0 + r * D_MODEL:g0 + (r + 1) * D_MODEL])
        term = jax.nn.sigmoid(gate) * _dot(y.astype(BF16), wb_ref[r])
        merged = term if merged is None else merged + term
    out = _dot(merged.astype(BF16), wo_ref[...])
    post = out * lax.rsqrt(jnp.mean(out * out, axis=-1, keepdims=True) + EPS) * gpost_ref[...]
    o_ref[0] = x + gt_ref[0] * post


def _merge_call(x, sh, sc, gt, gpre, gpost, w_merge, oa, ob, mx, of, obk, hn, wb, wo, tm,
                per_batch_mod):
    b, n, _ = x.shape
    nt = n // tm
    mod_map = (lambda i, j: (i, 0, 0)) if per_batch_mod else (lambda i, j: (0, 0, 0))
    x_spec = pl.BlockSpec((1, tm, D_MODEL), lambda i, j: (i, j, 0))
    br_spec = pl.BlockSpec((1, tm, BRANCH_W), lambda i, j: (i, j, 0))
    mod_spec = pl.BlockSpec((1, 1, D_MODEL), mod_map)
    return pl.pallas_call(
        _merge_kernel,
        out_shape=jax.ShapeDtypeStruct((b, n, D_MODEL), F32),
        grid=(b, nt),
        in_specs=[x_spec, mod_spec, mod_spec, mod_spec,
                  _resident((1, D_MODEL)), _resident((1, D_MODEL)),
                  _resident(w_merge.shape),
                  br_spec, br_spec, br_spec, br_spec, br_spec,
                  _resident((1, BRANCH_W)), _resident(wb.shape), _resident(wo.shape)],
        out_specs=x_spec,
        compiler_params=_cparams(("arbitrary", "arbitrary")),
        name="branch_merge",
    )(x, sh, sc, gt, gpre, gpost, w_merge, oa, ob, mx, of, obk, hn, wb, wo)


def _rope_tables(n_tok, rotate):
    if not rotate:
        return jnp.ones((n_tok, LANES), F32), jnp.zeros((n_tok, LANES), F32)
    t = jnp.arange(n_tok, dtype=jnp.int32)
    pos = jnp.stack([t // GRID_W, t % GRID_W], axis=-1).astype(F32)
    inv = ROPE_THETA ** (-jnp.arange(ROPE_FREQS, dtype=F32) * 2.0 / (2 * ROPE_FREQS))
    ang = pos[:, :, None] * inv
    cos = jnp.repeat(jnp.cos(ang)[:, :, None, :], 2, axis=2).reshape(n_tok, HEAD_DIM)
    sin = jnp.sin(ang)
    sin_signed = jnp.stack([-sin, sin], axis=2).reshape(n_tok, HEAD_DIM)
    return jnp.tile(cos, (1, 2)), jnp.tile(sin_signed, (1, 2))


def _gather_cols(w_in_l, names):
    return jnp.concatenate([w_in_l[:, _IN_COL[nm] * BRANCH_W:(_IN_COL[nm] + 1) * BRANCH_W] for nm in names],
                           axis=1)


def _row_tile(n):
    return 512 if n % 512 == 0 else 256


def kernel(x, c, ctx, c_ctx, w_ada, b_ada, g_pre, g_post, w_in, na_rpb, fnet_w, gmlp_norm_g, gmlp_ws,
           gmlp_bs, hgrn_lb_logits, hgrn_norm_g, w_branch, w_out):
    batch, n_tok, _ = x.shape
    n_ctx = ctx.shape[1]
    depth = w_in.shape[0]

    w_in_b = w_in.astype(BF16)
    w_proj = [_gather_cols(w_in_b[l], _PROJ_NAMES) for l in range(depth)]
    w_merge = [jnp.concatenate([_gather_cols(w_in_b[l], _MERGE_NAMES), w_in_b[l][:, _GATE_COL0:]], axis=1)
               for l in range(depth)]
    w_branch_b = w_branch.astype(BF16)
    w_out_b = w_out.astype(BF16)
    gmlp_ws_b = gmlp_ws.astype(BF16).reshape(depth, GMLP_GROUPS // 2, 2, GMLP_CHUNK, GMLP_CHUNK)
    gmlp_ws_b = gmlp_ws_b.transpose(0, 1, 3, 2, 4).reshape(depth, GMLP_GROUPS // 2, GMLP_CHUNK, 2 * GMLP_CHUNK)
    bs_tab = jnp.repeat(jnp.swapaxes(gmlp_bs, 1, 2), BRANCH_W // GMLP_GROUPS, axis=2)
    cos_x, sin_x = _rope_tables(n_tok, True)
    cos_c, sin_c = _rope_tables(n_ctx, False)

    c_all = jnp.concatenate([c, jnp.broadcast_to(c_ctx[None, :], (8, D_MODEL))], axis=0)
    mod = _modulation(c_all, w_ada, b_ada)
    lower = _lower_bounds(hgrn_lb_logits)

    zero_state = jnp.zeros((batch, N_PAIRS, LANES, LANES), F32)
    tm_x = _row_tile(n_tok)
    tm_c = _row_tile(n_ctx)

    for l in range(depth):
        with_ctx = l < depth - 1
        mod_x = [mod[l, :batch, i * D_MODEL:(i + 1) * D_MODEL].reshape(batch, 1, D_MODEL) for i in range(3)]
        mod_c = [mod[l, batch:batch + 1, i * D_MODEL:(i + 1) * D_MODEL].reshape(1, 1, D_MODEL) for i in range(3)]
        gpre = g_pre[l].reshape(1, D_MODEL)
        gpost = g_post[l].reshape(1, D_MODEL)
        gn = gmlp_norm_g[l].reshape(1, BRANCH_W)
        hn = hgrn_norm_g[l].reshape(1, BRANCH_W)
        lb = lower[l].reshape(1, 2, BRANCH_W)
        bias = _attention_bias(na_rpb[l], n_tok // GRID_W)

        (_, qp_c, k_c, v_c, bx_c, mx_c, dq_c, di_c, lff_c, kf_c, lfb_c, kb_c) = _proj_call(
            ctx, mod_c[0], mod_c[1], gpre, w_proj[l], cos_c, sin_c, gn, gmlp_ws_b[l], bs_tab[l], lb,
            tm_c, False)
        of_c, ob_c, st_f, st_b = _hgrn_call(dq_c, di_c, lff_c, kf_c, lfb_c, kb_c, zero_state, zero_state,
                                            min(HGRN_ROWS, n_ctx))

        (qr, qp, k, v, bx, mx, dq, di, lff, kf, lfb, kb) = _proj_call(
            x, mod_x[0], mod_x[1], gpre, w_proj[l], cos_x, sin_x, gn, gmlp_ws_b[l], bs_tab[l], lb,
            tm_x, True)
        oa = _nattn_call(qr, qp, k, v, k_c, v_c, bias)
        ob = _fourier_call(bx, fnet_w[l])
        of, obk, _, _ = _hgrn_call(dq, di, lff, kf, lfb, kb, st_f, st_b, min(HGRN_ROWS, n_tok))
        x = _merge_call(x, mod_x[0], mod_x[1], mod_x[2], gpre, gpost, w_merge[l], oa, ob, mx, of, obk,
                        hn, w_branch_b[l], w_out_b[l], tm_x, True)

        if with_ctx:
            oa_c = _cattn_call(qp_c, k_c, v_c)
            ob_c2 = _fourier_ctx_call(bx_c, fnet_w[l])
            ctx = _merge_call(ctx, mod_c[0], mod_c[1], mod_c[2], gpre, gpost, w_merge[l], oa_c, ob_c2, mx_c,
                              of_c, ob_c, hn, w_branch_b[l], w_out_b[l], tm_c, False)
    return x
```

```python
import functools

import numpy as np
import jax
import jax.numpy as jnp
from jax import lax
from jax.experimental import pallas as pl
from jax.experimental.pallas import tpu as pltpu

F32 = jnp.float32
BF16 = jnp.bfloat16

D_MODEL = 1024
BRANCH_W = 512
N_BRANCH = 4
GRID_W = 64
HEAD_DIM = 64
N_HEADS = 8
LANES = 128
N_PAIRS = BRANCH_W // LANES
NA_WIN_H = 8
NA_WIN_W = 16
NA_TILE_ROWS = 4
NA_BAND_ROWS = 12
NA_VCHUNK = 256
NA_DEN_ROWS = 16
LOG2E = 1.4426950408889634
ROPE_THETA = 10000.0
ROPE_FREQS = 16
FNET_GROUPS = 4
FNET_GROUP_W = 128
GMLP_CHUNK = 128
GMLP_GROUPS = 8
HGRN_CHUNK = 64
HGRN_SUB = 16
HGRN_EXP_CLAMP = 115.0
HGRN_ROWS = 256
HGRN_BATCH_ROWS = 2
EPS = 1e-6
F_FLOOR = 1e-30
NEG_INF = -1e30

VMEM_LIMIT = 56 * 2**20

_IN_COL = {'a_q': 0, 'a_k': 1, 'a_v': 2, 'a_g': 3, 'b_x': 4, 'b_g': 5, 'c_u': 6, 'c_v': 7, 'c_g': 8,
           'd_q': 9, 'd_f_fwd': 10, 'd_f_bwd': 11, 'd_i': 12, 'd_g': 13}
_PROJ_NAMES = ('a_q', 'a_k', 'a_v', 'b_x', 'c_v', 'd_q', 'd_f_fwd', 'd_f_bwd', 'd_i')
_MERGE_NAMES = ('c_u', 'a_g', 'b_g', 'c_g', 'd_g')
_GATE_COL0 = 14 * BRANCH_W


def _cparams(sem):
    return pltpu.CompilerParams(dimension_semantics=sem, vmem_limit_bytes=VMEM_LIMIT)


def _resident(shape):
    nd = len(shape)
    return pl.BlockSpec(shape, lambda *_: (0,) * nd, pipeline_mode=pl.Buffered(1))


def _silu(t):
    return t * jax.nn.sigmoid(t)


def _lane_iota(shape):
    return lax.broadcasted_iota(jnp.int32, shape, len(shape) - 1)


def _dot(a, b):
    return jnp.dot(a, b, preferred_element_type=F32)


def _dot_nt(a, b):
    return lax.dot_general(a, b, (((1,), (1,)), ((), ())), preferred_element_type=F32)


def _dot_tn(a, b):
    return lax.dot_general(a, b, (((0,), (0,)), ((), ())), preferred_element_type=F32)


def _split2(t):
    hi = t.astype(BF16)
    return hi, (t - hi.astype(F32)).astype(BF16)


def _normed_input(x, gpre, sc, sh):
    ms = jnp.mean(x * x, axis=-1, keepdims=True)
    h = x * lax.rsqrt(ms + EPS) * gpre
    return h * (1.0 + sc) + sh


def _mod_kernel(c_ref, w_ref, b_ref, o_ref):
    s = _silu(c_ref[...]).astype(BF16)
    o_ref[0] = _dot(s, w_ref[0].astype(BF16)) + b_ref[0]


def _modulation(c_all, w_ada, b_ada):
    depth = w_ada.shape[0]
    rows = c_all.shape[0]
    tn = 1024
    return pl.pallas_call(
        _mod_kernel,
        out_shape=jax.ShapeDtypeStruct((depth, rows, 3 * D_MODEL), F32),
        grid=(depth, 3 * D_MODEL // tn),
        in_specs=[pl.BlockSpec((rows, D_MODEL), lambda l, j: (0, 0)),
                  pl.BlockSpec((1, D_MODEL, tn), lambda l, j: (l, 0, j)),
                  pl.BlockSpec((1, 1, tn), lambda l, j: (l, 0, j))],
        out_specs=pl.BlockSpec((1, rows, tn), lambda l, j: (l, 0, j)),
        compiler_params=_cparams(("arbitrary", "arbitrary")),
        name="adaln_modulation",
    )(c_all, w_ada, b_ada.reshape(depth, 1, 3 * D_MODEL))


def _lb_kernel(lg_ref, o_ref):
    depth = lg_ref.shape[0]
    lg = [lg_ref[l] for l in range(depth)]
    m = functools.reduce(jnp.maximum, lg)
    e = [jnp.exp(t - m) for t in lg]
    tot = functools.reduce(lambda a, b: a + b, e)
    sm = [t / tot for t in e]
    run = jnp.zeros_like(sm[0])
    for l in range(depth):
        run = run + sm[l]
        o_ref[l] = jnp.maximum(run - sm[0], 0.0)


def _lower_bounds(lb_logits):
    return pl.pallas_call(
        _lb_kernel,
        out_shape=jax.ShapeDtypeStruct(lb_logits.shape, F32),
        name="hgrn_lower_bounds",
    )(lb_logits)


def _rope(t, cos, sin_signed, first_half):
    up = pltpu.roll(t, LANES - ROPE_FREQS, 1)
    down = pltpu.roll(t, ROPE_FREQS, 1)
    return t * cos + jnp.where(first_half, up, down) * sin_signed


def _proj_kernel(x_ref, sh_ref, sc_ref, gpre_ref, w_ref, cos_ref, sin_ref, gn_ref, ws_ref, bs_ref,
                 lb_ref, qr_ref, qp_ref, kr_ref, v_ref, bx_ref, mx_ref, dq_ref, di_ref,
                 lff_ref, kf_ref, lfb_ref, kb_ref):
    tm = x_ref.shape[1]
    hb = _normed_input(x_ref[0], gpre_ref[...], sc_ref[0], sh_ref[0]).astype(BF16)

    def proj(name):
        j = _PROJ_NAMES.index(name)
        return _dot(hb, w_ref[:, j * BRANCH_W:(j + 1) * BRANCH_W])

    cos = cos_ref[...]
    sin_signed = sin_ref[...]
    first_half = (_lane_iota((tm, LANES)) % (2 * ROPE_FREQS)) < ROPE_FREQS

    def rope_all(t):
        return jnp.concatenate(
            [_rope(t[:, p * LANES:(p + 1) * LANES], cos, sin_signed, first_half) for p in range(N_PAIRS)],
            axis=1)

    for name, d, lf_ref, k_ref in (('d_f_fwd', 0, lff_ref, kf_ref), ('d_f_bwd', 1, lfb_ref, kb_ref)):
        z = proj(name)
        lb = lb_ref[0, d:d + 1, :]
        sg = jax.nn.sigmoid(z)
        lf_ref[0] = jnp.log(jnp.maximum(lb + (1.0 - lb) * sg, F_FLOOR))
        k_ref[0] = ((1.0 - lb) * (1.0 - sg)).astype(BF16)

    cv = proj('c_v')
    vn = (cv * lax.rsqrt(jnp.mean(cv * cv, axis=-1, keepdims=True) + EPS) * gn_ref[...]).astype(BF16)
    low_group = _lane_iota((GMLP_CHUNK, LANES)) < (LANES // 2)
    for ch in range(tm // GMLP_CHUNK):
        r0 = ch * GMLP_CHUNK
        for p in range(N_PAIRS):
            slab = vn[r0:r0 + GMLP_CHUNK, p * LANES:(p + 1) * LANES]
            zero = jnp.zeros_like(slab)
            stacked = jnp.concatenate([jnp.where(low_group, slab, zero), jnp.where(low_group, zero, slab)], axis=0)
            mixed = _dot(ws_ref[p], stacked) + bs_ref[:, p * LANES:(p + 1) * LANES]
            mx_ref[0, r0:r0 + GMLP_CHUNK, p * LANES:(p + 1) * LANES] = mixed.astype(BF16)

    q = proj('a_q') * (HEAD_DIM ** -0.5 * LOG2E)
    q_rot = rope_all(q)
    kr_ref[0] = rope_all(proj('a_k')).astype(BF16)
    v = proj('a_v')
    for ch in range(tm // NA_VCHUNK):
        rs = slice(ch * NA_VCHUNK, (ch + 1) * NA_VCHUNK)
        qp_ref[0, ch] = q[rs].T.astype(BF16)
        qr_ref[0, ch] = q_rot[rs].T.astype(BF16)
        v_ref[0, ch] = v[rs].T.astype(BF16)
    bx_ref[0] = proj('b_x').astype(BF16)
    dq_ref[0] = proj('d_q').astype(BF16)
    di_ref[0] = proj('d_i').astype(BF16)


def _proj_call(x, sh, sc, gpre, w_proj, cos, sin_signed, gn, ws, bs_tab, lb, tm, per_batch_mod):
    b, n, _ = x.shape
    nt = n // tm
    mod_map = (lambda i, j: (i, 0, 0)) if per_batch_mod else (lambda i, j: (0, 0, 0))
    row_spec = pl.BlockSpec((1, tm, BRANCH_W), lambda i, j: (i, j, 0))
    vt_spec = pl.BlockSpec((1, tm // NA_VCHUNK, BRANCH_W, NA_VCHUNK), lambda i, j: (i, j, 0, 0))
    bf = jax.ShapeDtypeStruct((b, n, BRANCH_W), BF16)
    vt = jax.ShapeDtypeStruct((b, n // NA_VCHUNK, BRANCH_W, NA_VCHUNK), BF16)
    f32 = jax.ShapeDtypeStruct((b, n, BRANCH_W), F32)
    return pl.pallas_call(
        _proj_kernel,
        out_shape=(vt, vt, bf, vt, bf, bf, bf, bf, f32, bf, f32, bf),
        grid=(b, nt),
        in_specs=[pl.BlockSpec((1, tm, D_MODEL), lambda i, j: (i, j, 0)),
                  pl.BlockSpec((1, 1, D_MODEL), mod_map),
                  pl.BlockSpec((1, 1, D_MODEL), mod_map),
                  _resident((1, D_MODEL)),
                  _resident((D_MODEL, len(_PROJ_NAMES) * BRANCH_W)),
                  pl.BlockSpec((tm, LANES), lambda i, j: (j, 0)),
                  pl.BlockSpec((tm, LANES), lambda i, j: (j, 0)),
                  _resident((1, BRANCH_W)),
                  _resident((GMLP_GROUPS // 2, GMLP_CHUNK, 2 * GMLP_CHUNK)),
                  _resident((GMLP_CHUNK, BRANCH_W)),
                  _resident((1, 2, BRANCH_W))],
        out_specs=(vt_spec, vt_spec, row_spec, vt_spec) + (row_spec,) * 8,
        compiler_params=_cparams(("arbitrary", "arbitrary")),
        name="branch_proj",
    )(x, sh, sc, gpre, w_proj, cos, sin_signed, gn, ws, bs_tab, lb)


def _nattn_kernel(qr_ref, qp_ref, k_ref, vt_ref, kc_ref, vct_ref, bias_ref, o_ref):
    rows = k_ref.shape[1] // GRID_W
    r0 = pl.program_id(1) * NA_TILE_ROWS
    kb0 = jnp.clip(r0 - NA_WIN_H // 2, 0, rows - NA_BAND_ROWS)
    start = pl.multiple_of(kb0 * GRID_W, NA_VCHUNK)
    c0 = kb0 // (NA_VCHUNK // GRID_W)
    band = NA_BAND_ROWS * GRID_W
    nq = NA_TILE_ROWS * GRID_W

    def lanes(p):
        return slice(p * LANES, (p + 1) * LANES)

    head0_rows = lax.broadcasted_iota(jnp.int32, (LANES, nq), 0) < HEAD_DIM

    def stack_heads(qt):
        zero = jnp.zeros_like(qt)
        return jnp.concatenate([jnp.where(head0_rows, qt, zero), jnp.where(head0_rows, zero, qt)], axis=1)

    def scores(p):
        s_ctx = _dot(kc_ref[0, :, lanes(p)], stack_heads(qp_ref[0, 0, lanes(p), :]))
        s_band = _dot(k_ref[0, pl.ds(start, band), lanes(p)], stack_heads(qr_ref[0, 0, lanes(p), :]))
        bias = jnp.concatenate([bias_ref[0, :, 2 * p + hh].reshape(band, nq) for hh in range(2)], axis=1)
        return s_ctx, s_band + bias

    def softmax(s_ctx, s_band):
        p_ctx, p_band = [], []
        for cb in range(2 * nq // LANES):
            cs = slice(cb * LANES, (cb + 1) * LANES)
            sc, sb = s_ctx[:, cs], s_band[:, cs]
            mx = jnp.maximum(jnp.max(sc, axis=0, keepdims=True), jnp.max(sb, axis=0, keepdims=True))
            p_ctx.append(jnp.exp2(sc - mx).astype(BF16))
            p_band.append(jnp.exp2(sb - mx).astype(BF16))
        return jnp.concatenate(p_ctx, axis=1), jnp.concatenate(p_band, axis=1)

    ones_rows = jnp.ones((NA_DEN_ROWS, NA_VCHUNK), BF16)

    def values(p, p_ctx, p_band):
        outs = []
        for hh in range(2):
            qs = slice(hh * nq, (hh + 1) * nq)
            ch = slice(p * LANES + hh * HEAD_DIM, p * LANES + (hh + 1) * HEAD_DIM)
            acc = None
            for j in range(vct_ref.shape[1]):
                lhs = jnp.concatenate([vct_ref[0, j, ch, :], ones_rows], axis=0)
                term = _dot(lhs, p_ctx[j * NA_VCHUNK:(j + 1) * NA_VCHUNK, qs])
                acc = term if acc is None else acc + term
            for j in range(band // NA_VCHUNK):
                lhs = jnp.concatenate([vt_ref[0, c0 + j, ch, :], ones_rows], axis=0)
                acc = acc + _dot(lhs, p_band[j * NA_VCHUNK:(j + 1) * NA_VCHUNK, qs])
            outs.append(acc[0:HEAD_DIM] * (1.0 / acc[HEAD_DIM:HEAD_DIM + 1]))
        o_ref[0, :, lanes(p)] = jnp.concatenate(outs, axis=0).T.astype(BF16)

    s_val, p_val = {}, {}
    for t in range(N_PAIRS + 2):
        if t < N_PAIRS:
            s_val[t] = scores(t)
        if 0 <= t - 1 < N_PAIRS:
            p_val[t - 1] = softmax(*s_val.pop(t - 1))
        if 0 <= t - 2 < N_PAIRS:
            values(t - 2, *p_val.pop(t - 2))


def _nattn_call(qrt, qpt, k, vt, kc, vct, bias):
    b, n, _ = k.shape
    rows = n // GRID_W
    nt = rows // NA_TILE_ROWS
    lc = kc.shape[1]
    nq = NA_TILE_ROWS * GRID_W

    def bias_map(i, t):
        return (jnp.where(t == 0, 0, jnp.where(t == nt - 1, 2, 1)), 0, 0, 0, 0)

    assert nq == NA_VCHUNK
    q_spec = pl.BlockSpec((1, 1, BRANCH_W, NA_VCHUNK), lambda i, t: (i, t, 0, 0))
    o_spec = pl.BlockSpec((1, nq, BRANCH_W), lambda i, t: (i, t, 0))
    full = pl.BlockSpec((1, n, BRANCH_W), lambda i, t: (i, 0, 0))
    full_t = pl.BlockSpec((1, n // NA_VCHUNK, BRANCH_W, NA_VCHUNK), lambda i, t: (i, 0, 0, 0))
    ctx = pl.BlockSpec((1, lc, BRANCH_W), lambda i, t: (i, 0, 0))
    ctx_t = pl.BlockSpec((1, lc // NA_VCHUNK, BRANCH_W, NA_VCHUNK), lambda i, t: (i, 0, 0, 0))
    return pl.pallas_call(
        _nattn_kernel,
        out_shape=jax.ShapeDtypeStruct((b, n, BRANCH_W), BF16),
        grid=(b, nt),
        in_specs=[q_spec, q_spec, full, full_t, ctx, ctx_t,
                  pl.BlockSpec((1, NA_BAND_ROWS, N_HEADS, GRID_W, nq), bias_map)],
        out_specs=o_spec,
        compiler_params=_cparams(("arbitrary", "arbitrary")),
        name="neighbourhood_attention",
    )(qrt, qpt, k, vt, kc, vct, bias)


def _cattn_kernel(qt_ref, k_ref, vt_ref, o_ref):
    lc = k_ref.shape[1]
    head0_rows = lax.broadcasted_iota(jnp.int32, (LANES, lc), 0) < HEAD_DIM
    for p in range(N_PAIRS):
        ls = slice(p * LANES, (p + 1) * LANES)
        qt = jnp.concatenate([qt_ref[0, j, ls, :] for j in range(qt_ref.shape[1])], axis=1)
        vt = jnp.concatenate([vt_ref[0, j, ls, :] for j in range(vt_ref.shape[1])], axis=1)
        k = k_ref[0, :, ls]
        zero = jnp.zeros_like(qt)
        outs = []
        for hh in range(2):
            s = _dot(k, jnp.where(head0_rows, qt, zero) if hh == 0 else jnp.where(head0_rows, zero, qt))
            e = jnp.exp2(s - jnp.max(s, axis=0, keepdims=True))
            acc = _dot(vt[hh * HEAD_DIM:(hh + 1) * HEAD_DIM], e.astype(BF16))
            outs.append(acc * (1.0 / jnp.sum(e, axis=0, keepdims=True)))
        o_ref[0, :, ls] = jnp.concatenate(outs, axis=0).T.astype(BF16)


def _cattn_call(qt, k, vt):
    b, lc, _ = k.shape
    spec = pl.BlockSpec((1, lc, BRANCH_W), lambda i: (i, 0, 0))
    spec_t = pl.BlockSpec((1, lc // NA_VCHUNK, BRANCH_W, NA_VCHUNK), lambda i: (i, 0, 0, 0))
    return pl.pallas_call(
        _cattn_kernel,
        out_shape=jax.ShapeDtypeStruct((b, lc, BRANCH_W), BF16),
        grid=(b,),
        in_specs=[spec_t, spec, spec_t],
        out_specs=spec,
        compiler_params=_cparams(("arbitrary",)),
        name="context_attention",
    )(qt, k, vt)


def _attention_bias(rpb, rows):
    col = np.arange(GRID_W)
    col_start = np.clip(col - NA_WIN_W // 2, 0, GRID_W - NA_WIN_W)
    valid = (col[:, None] >= col_start[None, :]) & (col[:, None] < col_start[None, :] + NA_WIN_W)
    width = 2 * NA_WIN_W - 1
    flipped = jnp.pad(rpb.astype(F32)[:, :, ::-1] * LOG2E, ((0, 0), (0, 0), (GRID_W - NA_WIN_W,) * 2))
    tab = jnp.stack([flipped[:, :, GRID_W - 1 - kc:2 * GRID_W - 1 - kc] for kc in range(GRID_W)], axis=2)
    assert flipped.shape[-1] == width + 2 * (GRID_W - NA_WIN_W) and tab.shape[-1] == GRID_W
    tab = jnp.where(valid[None, None], tab, NEG_INF)
    pad = NA_BAND_ROWS - NA_WIN_H
    tab = jnp.pad(tab.transpose(1, 0, 2, 3), ((pad, pad), (0, 0), (0, 0), (0, 0)), constant_values=NEG_INF)
    kinds = []
    for r0 in (0, NA_TILE_ROWS, rows - NA_TILE_ROWS):
        kb0 = int(np.clip(r0 - NA_WIN_H // 2, 0, rows - NA_BAND_ROWS))
        per_query_row = []
        for r in range(r0, r0 + NA_TILE_ROWS):
            lo = kb0 - r + NA_WIN_H - 1 + pad
            kr = kb0 + np.arange(NA_BAND_ROWS)
            rs = int(np.clip(r - NA_WIN_H // 2, 0, rows - NA_WIN_H))
            in_win = (kr >= rs) & (kr < rs + NA_WIN_H)
            per_query_row.append(jnp.where(in_win[:, None, None, None], tab[lo:lo + NA_BAND_ROWS], NEG_INF))
        kinds.append(jnp.concatenate(per_query_row, axis=3))
    return jnp.stack(kinds)


_KRON = 8


@functools.lru_cache(maxsize=None)
def _fourier_consts(n):
    rows = n // GRID_W
    k1 = np.arange(rows)[:, None, None, None]
    l1 = np.arange(_KRON)[None, :, None, None]
    n1 = np.arange(rows)[None, None, :, None]
    l2 = np.arange(_KRON)[None, None, None, :]
    a_cos, a_sin = [], []
    for j in range(GRID_W // _KRON):
        ang = 2.0 * np.pi * k1 * (GRID_W * n1 + _KRON * j + l1) / n
        same = (l1 == l2)
        a_cos.append((np.cos(ang) * same).reshape(rows * _KRON, rows * _KRON))
        a_sin.append((-np.sin(ang) * same).reshape(rows * _KRON, rows * _KRON))
    k2 = np.arange(GRID_W)[:, None, None, None]
    ang = 2.0 * np.pi * k2 * np.arange(GRID_W)[None, None, None, :] / GRID_W
    same = (np.arange(_KRON)[None, :, None, None] == np.arange(_KRON)[None, None, :, None])
    b_cos = (np.cos(ang) * same).reshape(GRID_W * _KRON, _KRON * GRID_W)
    b_sin = (np.sin(ang) * same).reshape(GRID_W * _KRON, _KRON * GRID_W)
    b_re = np.concatenate([b_cos, b_sin], axis=1)
    b_im = np.concatenate([-b_sin, b_cos], axis=1)
    return (np.stack(a_cos).astype(np.float32), np.stack(a_sin).astype(np.float32),
            b_re.astype(np.float32), b_im.astype(np.float32))


@functools.lru_cache(maxsize=None)
def _channel_dft():
    c = np.arange(FNET_GROUP_W)
    ang = 2.0 * np.pi * np.outer(c, c) / FNET_GROUP_W
    return np.concatenate([np.cos(ang), np.sin(ang)], axis=0).astype(np.float32)


@functools.lru_cache(maxsize=None)
def _dense_dft(n):
    t = np.arange(n)
    ang = 2.0 * np.pi * np.outer(t, t) / n
    return np.cos(ang).astype(np.float32), (-np.sin(ang)).astype(np.float32)


def _fold_channel_map(cs_ref, wf_ref, fold_ref, norm):
    c_hi, c_lo = _split2(cs_ref[...] * norm)
    for g in range(FNET_GROUPS):
        w_hi, w_lo = _split2(wf_ref[g])
        fold_ref[g] = (_dot(c_hi, w_hi) + _dot(c_hi, w_lo) + _dot(c_lo, w_hi)).astype(BF16)


def _channel_stage(xr, xi, fold_ref):
    outs = []
    for g in range(FNET_GROUPS):
        ls = slice(g * FNET_GROUP_W, (g + 1) * FNET_GROUP_W)
        xg = jnp.concatenate([xr[:, ls], xi[:, ls]], axis=1).astype(BF16)
        outs.append(_dot(xg, fold_ref[g]))
    return jnp.concatenate(outs, axis=1)


def _fourier_kernel(x_ref, ac_ref, as_ref, bre_ref, bim_ref, cs_ref, wf_ref, o_ref, s_ref, fold_ref, *, norm):
    @pl.when(pl.program_id(0) == 0)
    def _():
        _fold_channel_map(cs_ref, wf_ref, fold_ref, norm)

    rows = x_ref.shape[1]
    blk = rows * _KRON
    pair = 2 * _KRON
    for jj in range(GRID_W // pair):
        xt = x_ref[0, :, jj * pair:(jj + 1) * pair, :].astype(F32)
        re, im = [], []
        for half in range(2):
            xc = xt[:, half * _KRON:(half + 1) * _KRON, :].reshape(blk, BRANCH_W).astype(BF16)
            re.append(_dot(ac_ref[2 * jj + half], xc).reshape(rows, _KRON, BRANCH_W))
            im.append(_dot(as_ref[2 * jj + half], xc).reshape(rows, _KRON, BRANCH_W))
        s_ref[0, :, jj * pair:(jj + 1) * pair, :] = jnp.concatenate(re, axis=1).astype(BF16)
        s_ref[1, :, jj * pair:(jj + 1) * pair, :] = jnp.concatenate(im, axis=1).astype(BF16)
    sblk = _KRON * GRID_W
    for mm in range(rows // pair):
        ys = []
        for half in range(2):
            m0 = (2 * mm + half) * _KRON
            rhs = jnp.concatenate([s_ref[0, m0:m0 + _KRON].reshape(sblk, BRANCH_W),
                                   s_ref[1, m0:m0 + _KRON].reshape(sblk, BRANCH_W)], axis=0)
            xr = _dot(bre_ref[...], rhs)
            xi = _dot(bim_ref[...], rhs)
            ys.append(_channel_stage(xr, xi, fold_ref).reshape(GRID_W, _KRON, BRANCH_W))
        o_ref[0, :, mm * pair:(mm + 1) * pair, :] = jnp.concatenate(ys, axis=1).astype(BF16)


def _fourier_call(bx, wf):
    b, n, _ = bx.shape
    rows = n // GRID_W
    a_cos, a_sin, b_re, b_im = (jnp.asarray(t, BF16) for t in _fourier_consts(n))
    cs = jnp.asarray(_channel_dft(), F32)
    norm = float(1.0 / np.sqrt(n * FNET_GROUP_W))
    x4 = bx.reshape(b, rows, GRID_W, BRANCH_W)
    out = pl.pallas_call(
        functools.partial(_fourier_kernel, norm=norm),
        out_shape=jax.ShapeDtypeStruct((b, GRID_W, rows, BRANCH_W), BF16),
        grid=(b,),
        in_specs=[pl.BlockSpec((1, rows, GRID_W, BRANCH_W), lambda i: (i, 0, 0, 0)),
                  _resident(a_cos.shape), _resident(a_sin.shape),
                  _resident(b_re.shape), _resident(b_im.shape),
                  _resident(cs.shape), _resident(wf.shape)],
        out_specs=pl.BlockSpec((1, GRID_W, rows, BRANCH_W), lambda i: (i, 0, 0, 0)),
        scratch_shapes=[pltpu.VMEM((2, rows, GRID_W, BRANCH_W), BF16),
                        pltpu.VMEM((FNET_GROUPS, 2 * FNET_GROUP_W, FNET_GROUP_W), BF16)],
        compiler_params=_cparams(("arbitrary",)),
        name="fourier_mix",
    )(x4, a_cos, a_sin, b_re, b_im, cs, wf)
    return out.reshape(b, n, BRANCH_W)


def _fourier_ctx_kernel(x_ref, c_ref, s_ref, cs_ref, wf_ref, o_ref, fold_ref, *, norm):
    @pl.when(pl.program_id(0) == 0)
    def _():
        _fold_channel_map(cs_ref, wf_ref, fold_ref, norm)

    x = x_ref[0]
    xr = _dot(c_ref[...], x)
    xi = _dot(s_ref[...], x)
    o_ref[0] = _channel_stage(xr, xi, fold_ref).astype(BF16)


def _fourier_ctx_call(bx, wf):
    b, n, _ = bx.shape
    cn, sn = (jnp.asarray(t, BF16) for t in _dense_dft(n))
    cs = jnp.asarray(_channel_dft(), F32)
    norm = float(1.0 / np.sqrt(n * FNET_GROUP_W))
    spec = pl.BlockSpec((1, n, BRANCH_W), lambda i: (i, 0, 0))
    return pl.pallas_call(
        functools.partial(_fourier_ctx_kernel, norm=norm),
        out_shape=jax.ShapeDtypeStruct((b, n, BRANCH_W), BF16),
        grid=(b,),
        in_specs=[spec, _resident(cn.shape), _resident(sn.shape), _resident(cs.shape), _resident(wf.shape)],
        out_specs=spec,
        scratch_shapes=[pltpu.VMEM((FNET_GROUPS, 2 * FNET_GROUP_W, FNET_GROUP_W), BF16)],
        compiler_params=_cparams(("arbitrary",)),
        name="fourier_mix_context",
    )(bx, cn, sn, cs, wf)


def _block_diag(t):
    lo = _lane_iota(t.shape) < HEAD_DIM
    z = jnp.zeros_like(t)
    return jnp.concatenate([jnp.where(lo, t, z), jnp.where(lo, z, t)], axis=0)


def _hgrn_needed(j, reverse):
    nsub = HGRN_CHUNK // HGRN_SUB
    return list(range(0, j + 1)) if reverse else list(range(j, nsub))


def _hgrn_prepare(q, k, i, a, reverse):
    c = HGRN_CHUNK
    nsub = c // HGRN_SUB

    def level(r):
        return a[r:r + 1, :]

    zero_row = jnp.zeros((1, BRANCH_W), F32)
    if reverse:
        refs = [level((s + 1) * HGRN_SUB) if s + 1 < nsub else zero_row for s in range(nsub)]
        a_end = a[0:1, :]
    else:
        refs = [level(s * HGRN_SUB - 1) if s > 0 else zero_row for s in range(nsub)]
        a_end = a[c - 1:c, :]
    ref_rows = jnp.concatenate([jnp.broadcast_to(r, (HGRN_SUB, BRANCH_W)) for r in refs], axis=0)
    qf = q.astype(F32)
    kf = k.astype(F32)
    lift = ref_rows - a
    k_own = (kf * jnp.exp2(jnp.minimum(lift, HGRN_EXP_CLAMP))).astype(BF16)

    def q_variant(j):
        parts = []
        for s in _hgrn_needed(j, reverse):
            rs = slice(s * HGRN_SUB, (s + 1) * HGRN_SUB)
            parts.append((qf[rs] * jnp.exp2(a[rs] - refs[j])).astype(BF16))
        return jnp.concatenate(parts, axis=0)

    q_var = [q_variant(j) for j in range(nsub)]
    return dict(
        q_stack=jnp.concatenate(q_var, axis=0),
        q_in=q_var[nsub - 1] if reverse else q_var[0],
        k_own=k_own,
        k_out=(kf * jnp.exp2(a_end - a)).astype(BF16),
        decay_end=jnp.exp2(a_end),
        max_lift=jnp.max(lift, axis=0, keepdims=True),
        i=i, reverse=reverse)


def _hgrn_scores(ops):
    c = HGRN_CHUNK
    nsub = c // HGRN_SUB
    reverse = ops['reverse']
    src = _lane_iota((c, LANES)) % HEAD_DIM
    step = lax.broadcasted_iota(jnp.int32, (c, LANES), 0)
    seen = (src >= step) if reverse else (src <= step)
    src_sub = (_lane_iota((HGRN_SUB, LANES)) % HEAD_DIM) // HGRN_SUB
    where_blk, off = {}, 0
    for j in range(nsub):
        for s in _hgrn_needed(j, reverse):
            where_blk[(j, s)] = off
            off += HGRN_SUB
    out = []
    for p in range(N_PAIRS):
        ls = slice(p * LANES, (p + 1) * LANES)
        res = _dot_nt(ops['q_stack'][:, ls], _block_diag(ops['k_own'][:, ls]))
        rows = []
        for s in range(nsub):
            blk = None
            for j in range(nsub):
                if (j, s) in where_blk:
                    piece = res[where_blk[(j, s)]:where_blk[(j, s)] + HGRN_SUB]
                    blk = piece if blk is None else jnp.where(src_sub == j, piece, blk)
            rows.append(blk)
        out.append(jnp.where(seen, jnp.concatenate(rows, axis=0), 0.0).astype(BF16))
    return out


def _hgrn_local(ops, scores):
    low_rows = lax.broadcasted_iota(jnp.int32, (LANES, LANES), 0) < HEAD_DIM
    same_head = low_rows == (_lane_iota((LANES, LANES)) < HEAD_DIM)
    o_intra, upd = [], []
    for p in range(N_PAIRS):
        ls = slice(p * LANES, (p + 1) * LANES)
        ip = ops['i'][:, ls]
        o_intra.append(_dot(scores[p], _block_diag(ip)))
        upd.append(jnp.where(same_head, _dot_tn(ip, ops['k_out'][:, ls]), 0.0))
    return o_intra, upd


def _hgrn_carry(ops, o_intra, upd, state_ref, d):
    outs = []
    for p in range(N_PAIRS):
        ls = slice(p * LANES, (p + 1) * LANES)
        st = state_ref[d, p]
        outs.append(o_intra[p] + _dot_nt(ops['q_in'][:, ls], st.astype(BF16)))
        state_ref[d, p] = ops['decay_end'][:, ls] * st + upd[p]
    return jnp.concatenate(outs, axis=1)


def _hgrn_exact_tile(q_ref, k_ref, i_ref, a_ref, o_ref, state_ref, bb, d, reverse, q_sc, k_sc, i_sc):
    c = HGRN_CHUNK
    nchunk = q_ref.shape[1] // c
    lane_head = _lane_iota((BRANCH_W, BRANCH_W)) // HEAD_DIM
    row_head = lax.broadcasted_iota(jnp.int32, (BRANCH_W, BRANCH_W), 0) // HEAD_DIM
    head_sum = (lane_head == row_head).astype(BF16)
    step = lax.broadcasted_iota(jnp.int32, (c, BRANCH_W), 0)
    low_rows = lax.broadcasted_iota(jnp.int32, (LANES, LANES), 0) < HEAD_DIM
    same_head = low_rows == (_lane_iota((LANES, LANES)) < HEAD_DIM)
    for cix in (range(nchunk - 1, -1, -1) if reverse else range(nchunk)):
        rs = slice(cix * c, (cix + 1) * c)
        q_sc[...] = q_ref[bb, rs, :].astype(F32)
        k_sc[...] = k_ref[bb, rs, :].astype(F32)
        i_sc[...] = i_ref[bb, rs, :].astype(F32)
        qf = q_sc[...]
        kf = k_sc[...]
        a = a_ref[d, rs, :]

        def one_source(s, acc, a=a, qf=qf, cix=cix):
            a_s = a_ref[d, pl.ds(cix * c + s, 1), :]
            w = qf * (k_sc[pl.ds(s, 1), :] * jnp.exp2(jnp.minimum(a - a_s, 0.0)))
            w = jnp.where((step <= s) if reverse else (step >= s), w, 0.0)
            hi, lo = _split2(w)
            return acc + (_dot(hi, head_sum) + _dot(lo, head_sum)) * i_sc[pl.ds(s, 1), :]

        o_intra = lax.fori_loop(0, c, one_source, jnp.zeros((c, BRANCH_W), F32))
        a_end = a[0:1, :] if reverse else a[c - 1:c, :]
        q_in = (qf * jnp.exp2(a)).astype(BF16)
        k_out = (kf * jnp.exp2(a_end - a)).astype(BF16)
        decay_end = jnp.exp2(a_end)
        ib = i_ref[bb, rs, :]
        outs = []
        for p in range(N_PAIRS):
            ls = slice(p * LANES, (p + 1) * LANES)
            st = state_ref[d, p]
            outs.append(o_intra[:, ls] + _dot_nt(q_in[:, ls], st.astype(BF16)))
            upd = jnp.where(same_head, _dot_tn(ib[:, ls], k_out[:, ls]), 0.0)
            state_ref[d, p] = decay_end[:, ls] * st + upd
        o_ref[bb, rs, :] = jnp.concatenate(outs, axis=1).astype(BF16)


def _hgrn_kernel(qf_ref, if_ref, lff_ref, kf_ref, qb_ref, ib_ref, lfb_ref, kb_ref, s0f_ref, s0b_ref,
                 of_ref, ob_ref, sf_ref, sb_ref, state_ref, backup_ref, a_ref, q_sc, k_sc, i_sc):
    j = pl.program_id(1)
    nb = qf_ref.shape[0]
    nchunk = qf_ref.shape[1] // HGRN_CHUNK

    @pl.when(j == 0)
    def _():
        for bb in range(nb):
            state_ref[2 * bb] = s0f_ref[bb]
            state_ref[2 * bb + 1] = s0b_ref[bb]

    backup_ref[...] = state_ref[...]

    tm = qf_ref.shape[1]
    row = lax.broadcasted_iota(jnp.int32, (tm, tm), 0)
    col = lax.broadcasted_iota(jnp.int32, (tm, tm), 1)
    same_chunk = (row // HGRN_CHUNK) == (col // HGRN_CHUNK)

    def cum(lf, reverse):
        tri = (same_chunk & ((col >= row) if reverse else (col <= row))).astype(BF16)
        hi, lo = _split2(lf * LOG2E)
        return _dot(tri, hi) + _dot(tri, lo)

    a_f = [cum(lff_ref[bb], False) for bb in range(nb)]
    a_b = [cum(lfb_ref[bb], True) for bb in range(nb)]

    todo = []
    for cix in range(nchunk):
        bix = nchunk - 1 - cix
        for bb in range(nb):
            todo.append((bb, 2 * bb, of_ref, slice(cix * HGRN_CHUNK, (cix + 1) * HGRN_CHUNK),
                         qf_ref, kf_ref, if_ref, a_f[bb], False))
            todo.append((bb, 2 * bb + 1, ob_ref, slice(bix * HGRN_CHUNK, (bix + 1) * HGRN_CHUNK),
                         qb_ref, kb_ref, ib_ref, a_b[bb], True))
    ops, scores, local, lifts = {}, {}, {}, []
    for t in range(len(todo) + 3):
        if t < len(todo):
            bb, _, _, rs, q_ref, k_ref, i_ref, a, reverse = todo[t]
            ops[t] = _hgrn_prepare(q_ref[bb, rs, :], k_ref[bb, rs, :], i_ref[bb, rs, :], a[rs, :], reverse)
            lifts.append(ops[t]['max_lift'])
        if 0 <= t - 1 < len(todo):
            scores[t - 1] = _hgrn_scores(ops[t - 1])
        if 0 <= t - 2 < len(todo):
            local[t - 2] = _hgrn_local(ops[t - 2], scores.pop(t - 2))
        if 0 <= t - 3 < len(todo):
            bb, slot, o_ref, rs = todo[t - 3][:4]
            o_ref[bb, rs, :] = _hgrn_carry(ops.pop(t - 3), *local.pop(t - 3), state_ref, slot).astype(BF16)

    @pl.when(jnp.max(functools.reduce(jnp.maximum, lifts)) > HGRN_EXP_CLAMP)
    def _():
        state_ref[...] = backup_ref[...]
        for bb in range(nb):
            a_ref[2 * bb] = a_f[bb]
            a_ref[2 * bb + 1] = a_b[bb]
            _hgrn_exact_tile(qf_ref, kf_ref, if_ref, a_ref, of_ref, state_ref, bb, 2 * bb, False,
                             q_sc, k_sc, i_sc)
            _hgrn_exact_tile(qb_ref, kb_ref, ib_ref, a_ref, ob_ref, state_ref, bb, 2 * bb + 1, True,
                             q_sc, k_sc, i_sc)

    @pl.when(j == pl.num_programs(1) - 1)
    def _():
        for bb in range(nb):
            sf_ref[bb] = state_ref[2 * bb]
            sb_ref[bb] = state_ref[2 * bb + 1]


def _hgrn_call(q, i, lff, kf, lfb, kb, s0f, s0b, tm):
    b, n, _ = q.shape
    nt = n // tm
    nb = HGRN_BATCH_ROWS if b % HGRN_BATCH_ROWS == 0 else 1
    fwd = pl.BlockSpec((nb, tm, BRANCH_W), lambda bi, j: (bi, j, 0))
    bwd = pl.BlockSpec((nb, tm, BRANCH_W), lambda bi, j: (bi, nt - 1 - j, 0))
    st = pl.BlockSpec((nb, N_PAIRS, LANES, LANES), lambda bi, j: (bi, 0, 0, 0))
    o_shape = jax.ShapeDtypeStruct((b, n, BRANCH_W), BF16)
    s_shape = jax.ShapeDtypeStruct((b, N_PAIRS, LANES, LANES), F32)
    return pl.pallas_call(
        _hgrn_kernel,
        out_shape=(o_shape, o_shape, s_shape, s_shape),
        grid=(b // nb, nt),
        in_specs=[fwd, fwd, fwd, fwd, bwd, bwd, bwd, bwd, st, st],
        out_specs=(fwd, bwd, st, st),
        scratch_shapes=[pltpu.VMEM((2 * nb, N_PAIRS, LANES, LANES), F32),
                        pltpu.VMEM((2 * nb, N_PAIRS, LANES, LANES), F32),
                        pltpu.VMEM((2 * nb, tm, BRANCH_W), F32),
                        pltpu.VMEM((HGRN_CHUNK, BRANCH_W), F32),
                        pltpu.VMEM((HGRN_CHUNK, BRANCH_W), F32),
                        pltpu.VMEM((HGRN_CHUNK, BRANCH_W), F32)],
        compiler_params=_cparams(("arbitrary", "arbitrary")),
        name="hgrn_scan",
    )(q, i, lff, kf, q, i, lfb, kb, s0f, s0b)


def _merge_kernel(x_ref, sh_ref, sc_ref, gt_ref, gpre_ref, gpost_ref, w_ref, oa_ref, ob_ref, mx_ref,
                  of_ref, obk_ref, hn_ref, hm_ref, wb_ref, wo_ref, o_ref):
    x = x_ref[0]
    hb = _normed_input(x, gpre_ref[...], sc_ref[0], sh_ref[0]).astype(BF16)

    def proj(name):
        j = _MERGE_NAMES.index(name)
        return _dot(hb, w_ref[:, j * BRANCH_W:(j + 1) * BRANCH_W])

    ya = oa_ref[0].astype(F32) * _silu(proj('a_g'))
    yb = ob_ref[0].astype(F32) * _silu(proj('b_g'))
    yc = proj('c_u') * mx_ref[0].astype(F32) * _silu(proj('c_g'))
    o = of_ref[0].astype(F32) + obk_ref[0].astype(F32)
    ms = _dot((o * o).astype(BF16), hm_ref[...])
    yd = o * lax.rsqrt(ms + EPS) * hn_ref[...] * _silu(proj('d_g'))

    g0 = len(_MERGE_NAMES) * BRANCH_W
    merged = None
    for r, y in enumerate((ya, yb, yc, yd)):
        gate = _dot(hb, w_ref[:, g0 + r * D_MODEL:g0 + (r + 1) * D_MODEL])
        term = jax.nn.sigmoid(gate) * _dot(y.astype(BF16), wb_ref[r])
        merged = term if merged is None else merged + term
    out = _dot(merged.astype(BF16), wo_ref[...])
    post = out * lax.rsqrt(jnp.mean(out * out, axis=-1, keepdims=True) + EPS) * gpost_ref[...]
    o_ref[0] = x + gt_ref[0] * post


def _merge_call(x, sh, sc, gt, gpre, gpost, w_merge, oa, ob, mx, of, obk, hn, hmean, wb, wo, tm,
                per_batch_mod):
    b, n, _ = x.shape
    nt = n // tm
    mod_map = (lambda i, j: (i, 0, 0)) if per_batch_mod else (lambda i, j: (0, 0, 0))
    x_spec = pl.BlockSpec((1, tm, D_MODEL), lambda i, j: (i, j, 0))
    br_spec = pl.BlockSpec((1, tm, BRANCH_W), lambda i, j: (i, j, 0))
    mod_spec = pl.BlockSpec((1, 1, D_MODEL), mod_map)
    return pl.pallas_call(
        _merge_kernel,
        out_shape=jax.ShapeDtypeStruct((b, n, D_MODEL), F32),
        grid=(b, nt),
        in_specs=[x_spec, mod_spec, mod_spec, mod_spec,
                  _resident((1, D_MODEL)), _resident((1, D_MODEL)),
                  _resident(w_merge.shape),
                  br_spec, br_spec, br_spec, br_spec, br_spec,
                  _resident((1, BRANCH_W)), _resident((BRANCH_W, BRANCH_W)),
                  _resident(wb.shape), _resident(wo.shape)],
        out_specs=x_spec,
        compiler_params=_cparams(("arbitrary", "arbitrary")),
        name="branch_merge",
    )(x, sh, sc, gt, gpre, gpost, w_merge, oa, ob, mx, of, obk, hn, hmean, wb, wo)


def _rope_tables(n_tok, rotate):
    if not rotate:
        return jnp.ones((n_tok, LANES), F32), jnp.zeros((n_tok, LANES), F32)
    t = jnp.arange(n_tok, dtype=jnp.int32)
    pos = jnp.stack([t // GRID_W, t % GRID_W], axis=-1).astype(F32)
    inv = ROPE_THETA ** (-jnp.arange(ROPE_FREQS, dtype=F32) * 2.0 / (2 * ROPE_FREQS))
    ang = pos[:, :, None] * inv
    cos = jnp.repeat(jnp.cos(ang)[:, :, None, :], 2, axis=2).reshape(n_tok, HEAD_DIM)
    sin = jnp.sin(ang)
    sin_signed = jnp.stack([-sin, sin], axis=2).reshape(n_tok, HEAD_DIM)
    return jnp.tile(cos, (1, 2)), jnp.tile(sin_signed, (1, 2))


def _gather_cols(w_in_l, names):
    return jnp.concatenate([w_in_l[:, _IN_COL[nm] * BRANCH_W:(_IN_COL[nm] + 1) * BRANCH_W] for nm in names],
                           axis=1)


def _row_tile(n):
    return 512 if n % 512 == 0 else 256


def kernel(x, c, ctx, c_ctx, w_ada, b_ada, g_pre, g_post, w_in, na_rpb, fnet_w, gmlp_norm_g, gmlp_ws,
           gmlp_bs, hgrn_lb_logits, hgrn_norm_g, w_branch, w_out):
    batch, n_tok, _ = x.shape
    n_ctx = ctx.shape[1]
    depth = w_in.shape[0]

    w_proj = [_gather_cols(w_in[l], _PROJ_NAMES).astype(BF16) for l in range(depth)]
    w_merge = [jnp.concatenate([_gather_cols(w_in[l], _MERGE_NAMES), w_in[l][:, _GATE_COL0:]], axis=1).astype(BF16)
               for l in range(depth)]
    w_branch_b = w_branch.astype(BF16)
    w_out_b = w_out.astype(BF16)
    gmlp_ws_b = gmlp_ws.astype(BF16).reshape(depth, GMLP_GROUPS // 2, 2, GMLP_CHUNK, GMLP_CHUNK)
    gmlp_ws_b = gmlp_ws_b.transpose(0, 1, 3, 2, 4).reshape(depth, GMLP_GROUPS // 2, GMLP_CHUNK, 2 * GMLP_CHUNK)
    bs_tab = jnp.repeat(jnp.swapaxes(gmlp_bs, 1, 2), BRANCH_W // GMLP_GROUPS, axis=2)
    head_mean = jnp.asarray(np.kron(np.eye(N_HEADS), np.ones((HEAD_DIM, HEAD_DIM)) / HEAD_DIM), BF16)
    cos_x, sin_x = _rope_tables(n_tok, True)
    cos_c, sin_c = _rope_tables(n_ctx, False)

    c_all = jnp.concatenate([c, jnp.broadcast_to(c_ctx[None, :], (8, D_MODEL))], axis=0)
    mod = _modulation(c_all, w_ada, b_ada)
    lower = _lower_bounds(hgrn_lb_logits)

    zero_state = jnp.zeros((batch, N_PAIRS, LANES, LANES), F32)
    tm_x = _row_tile(n_tok)
    tm_c = _row_tile(n_ctx)

    for l in range(depth):
        with_ctx = l < depth - 1
        mod_x = [mod[l, :batch, i * D_MODEL:(i + 1) * D_MODEL].reshape(batch, 1, D_MODEL) for i in range(3)]
        mod_c = [mod[l, batch:batch + 1, i * D_MODEL:(i + 1) * D_MODEL].reshape(1, 1, D_MODEL) for i in range(3)]
        gpre = g_pre[l].reshape(1, D_MODEL)
        gpost = g_post[l].reshape(1, D_MODEL)
        gn = gmlp_norm_g[l].reshape(1, BRANCH_W)
        hn = hgrn_norm_g[l].reshape(1, BRANCH_W)
        lb = lower[l].reshape(1, 2, BRANCH_W)
        bias = _attention_bias(na_rpb[l], n_tok // GRID_W)

        (_, qp_c, k_c, v_c, bx_c, mx_c, dq_c, di_c, lff_c, kf_c, lfb_c, kb_c) = _proj_call(
            ctx, mod_c[0], mod_c[1], gpre, w_proj[l], cos_c, sin_c, gn, gmlp_ws_b[l], bs_tab[l], lb,
            tm_c, False)
        of_c, ob_c, st_f, st_b = _hgrn_call(dq_c, di_c, lff_c, kf_c, lfb_c, kb_c, zero_state, zero_state,
                                            min(HGRN_ROWS, n_ctx))

        (qr, qp, k, v, bx, mx, dq, di, lff, kf, lfb, kb) = _proj_call(
            x, mod_x[0], mod_x[1], gpre, w_proj[l], cos_x, sin_x, gn, gmlp_ws_b[l], bs_tab[l], lb,
            tm_x, True)
        oa = _nattn_call(qr, qp, k, v, k_c, v_c, bias)
        ob = _fourier_call(bx, fnet_w[l])
        of, obk, _, _ = _hgrn_call(dq, di, lff, kf, lfb, kb, st_f, st_b, min(HGRN_ROWS, n_tok))
        x = _merge_call(x, mod_x[0], mod_x[1], mod_x[2], gpre, gpost, w_merge[l], oa, ob, mx, of, obk,
                        hn, head_mean, w_branch_b[l], w_out_b[l], tm_x, True)

        if with_ctx:
            oa_c = _cattn_call(qp_c, k_c, v_c)
            ob_c2 = _fourier_ctx_call(bx_c, fnet_w[l])
            ctx = _merge_call(ctx, mod_c[0], mod_c[1], mod_c[2], gpre, gpost, w_merge[l], oa_c, ob_c2, mx_c,
                              of_c, ob_c, hn, head_mean, w_branch_b[l], w_out_b[l], tm_c, False)
    return x
```

```python
import functools

import numpy as np
import jax
import jax.numpy as jnp
from jax import lax
from jax.experimental import pallas as pl
from jax.experimental.pallas import tpu as pltpu

F32 = jnp.float32
BF16 = jnp.bfloat16

D_MODEL = 1024
BRANCH_W = 512
N_BRANCH = 4
GRID_W = 64
HEAD_DIM = 64
N_HEADS = 8
LANES = 128
N_PAIRS = BRANCH_W // LANES
NA_WIN_H = 8
NA_WIN_W = 16
NA_TILE_ROWS = 4
NA_BAND_ROWS = 12
NA_VCHUNK = 256
NA_DEN_ROWS = 16
LOG2E = 1.4426950408889634
ROPE_THETA = 10000.0
ROPE_FREQS = 16
FNET_GROUPS = 4
FNET_GROUP_W = 128
GMLP_CHUNK = 128
GMLP_GROUPS = 8
HGRN_CHUNK = 64
HGRN_SUB = 16
HGRN_EXP_CLAMP = 115.0
HGRN_ROWS = 256
HGRN_BATCH_ROWS = 2
EPS = 1e-6
F_FLOOR = 1e-30
NEG_INF = -1e30

VMEM_LIMIT = 56 * 2**20

_IN_COL = {'a_q': 0, 'a_k': 1, 'a_v': 2, 'a_g': 3, 'b_x': 4, 'b_g': 5, 'c_u': 6, 'c_v': 7, 'c_g': 8,
           'd_q': 9, 'd_f_fwd': 10, 'd_f_bwd': 11, 'd_i': 12, 'd_g': 13}
_PROJ_NAMES = ('a_q', 'a_k', 'a_v', 'b_x', 'c_v', 'd_q', 'd_f_fwd', 'd_f_bwd', 'd_i')
_MERGE_NAMES = ('c_u', 'a_g', 'b_g', 'c_g', 'd_g')
_GATE_COL0 = 14 * BRANCH_W


def _cparams(sem):
    return pltpu.CompilerParams(dimension_semantics=sem, vmem_limit_bytes=VMEM_LIMIT)


def _resident(shape):
    nd = len(shape)
    return pl.BlockSpec(shape, lambda *_: (0,) * nd, pipeline_mode=pl.Buffered(1))


def _silu(t):
    return t * jax.nn.sigmoid(t)


def _lane_iota(shape):
    return lax.broadcasted_iota(jnp.int32, shape, len(shape) - 1)


def _dot(a, b):
    return jnp.dot(a, b, preferred_element_type=F32)


def _dot_nt(a, b):
    return lax.dot_general(a, b, (((1,), (1,)), ((), ())), preferred_element_type=F32)


def _dot_tn(a, b):
    return lax.dot_general(a, b, (((0,), (0,)), ((), ())), preferred_element_type=F32)


def _split2(t):
    hi = t.astype(BF16)
    return hi, (t - hi.astype(F32)).astype(BF16)


def _normed_input(x, gpre, sc, sh):
    ms = jnp.mean(x * x, axis=-1, keepdims=True)
    h = x * lax.rsqrt(ms + EPS) * gpre
    return h * (1.0 + sc) + sh


def _mod_kernel(c_ref, w_ref, b_ref, o_ref):
    s = _silu(c_ref[...]).astype(BF16)
    o_ref[0] = _dot(s, w_ref[0].astype(BF16)) + b_ref[0]


def _modulation(c_all, w_ada, b_ada):
    depth = w_ada.shape[0]
    rows = c_all.shape[0]
    tn = 1024
    return pl.pallas_call(
        _mod_kernel,
        out_shape=jax.ShapeDtypeStruct((depth, rows, 3 * D_MODEL), F32),
        grid=(depth, 3 * D_MODEL // tn),
        in_specs=[pl.BlockSpec((rows, D_MODEL), lambda l, j: (0, 0)),
                  pl.BlockSpec((1, D_MODEL, tn), lambda l, j: (l, 0, j)),
                  pl.BlockSpec((1, 1, tn), lambda l, j: (l, 0, j))],
        out_specs=pl.BlockSpec((1, rows, tn), lambda l, j: (l, 0, j)),
        compiler_params=_cparams(("arbitrary", "arbitrary")),
        name="adaln_modulation",
    )(c_all, w_ada, b_ada.reshape(depth, 1, 3 * D_MODEL))


def _lb_kernel(lg_ref, o_ref):
    depth = lg_ref.shape[0]
    lg = [lg_ref[l] for l in range(depth)]
    m = functools.reduce(jnp.maximum, lg)
    e = [jnp.exp(t - m) for t in lg]
    tot = functools.reduce(lambda a, b: a + b, e)
    sm = [t / tot for t in e]
    run = jnp.zeros_like(sm[0])
    for l in range(depth):
        run = run + sm[l]
        o_ref[l] = jnp.maximum(run - sm[0], 0.0)


def _lower_bounds(lb_logits):
    return pl.pallas_call(
        _lb_kernel,
        out_shape=jax.ShapeDtypeStruct(lb_logits.shape, F32),
        name="hgrn_lower_bounds",
    )(lb_logits)


def _rope(t, cos, sin_signed, first_half):
    up = pltpu.roll(t, LANES - ROPE_FREQS, 1)
    down = pltpu.roll(t, ROPE_FREQS, 1)
    return t * cos + jnp.where(first_half, up, down) * sin_signed


def _proj_kernel(x_ref, sh_ref, sc_ref, gpre_ref, w_ref, cos_ref, sin_ref, gn_ref, ws_ref, bs_ref,
                 lb_ref, qr_ref, qp_ref, kr_ref, v_ref, bx_ref, mx_ref, dq_ref, di_ref,
                 lff_ref, kf_ref, lfb_ref, kb_ref):
    tm = x_ref.shape[1]
    hb = _normed_input(x_ref[0], gpre_ref[...], sc_ref[0], sh_ref[0]).astype(BF16)

    def proj(name):
        j = _PROJ_NAMES.index(name)
        return _dot(hb, w_ref[:, j * BRANCH_W:(j + 1) * BRANCH_W])

    cos = cos_ref[...]
    sin_signed = sin_ref[...]
    first_half = (_lane_iota((tm, LANES)) % (2 * ROPE_FREQS)) < ROPE_FREQS

    def rope_all(t):
        return jnp.concatenate(
            [_rope(t[:, p * LANES:(p + 1) * LANES], cos, sin_signed, first_half) for p in range(N_PAIRS)],
            axis=1)

    for name, d, lf_ref, k_ref in (('d_f_fwd', 0, lff_ref, kf_ref), ('d_f_bwd', 1, lfb_ref, kb_ref)):
        z = proj(name)
        lb = lb_ref[0, d:d + 1, :]
        sg = jax.nn.sigmoid(z)
        lf_ref[0] = jnp.log(jnp.maximum(lb + (1.0 - lb) * sg, F_FLOOR))
        k_ref[0] = ((1.0 - lb) * (1.0 - sg)).astype(BF16)

    cv = proj('c_v')
    vn = (cv * lax.rsqrt(jnp.mean(cv * cv, axis=-1, keepdims=True) + EPS) * gn_ref[...]).astype(BF16)
    low_group = _lane_iota((GMLP_CHUNK, LANES)) < (LANES // 2)
    for ch in range(tm // GMLP_CHUNK):
        r0 = ch * GMLP_CHUNK
        for p in range(N_PAIRS):
            slab = vn[r0:r0 + GMLP_CHUNK, p * LANES:(p + 1) * LANES]
            zero = jnp.zeros_like(slab)
            stacked = jnp.concatenate([jnp.where(low_group, slab, zero), jnp.where(low_group, zero, slab)], axis=0)
            mixed = _dot(ws_ref[p], stacked) + bs_ref[:, p * LANES:(p + 1) * LANES]
            mx_ref[0, r0:r0 + GMLP_CHUNK, p * LANES:(p + 1) * LANES] = mixed.astype(BF16)

    q = proj('a_q') * (HEAD_DIM ** -0.5 * LOG2E)
    q_rot = rope_all(q)
    kr_ref[0] = rope_all(proj('a_k')).astype(BF16)
    v = proj('a_v')
    for ch in range(tm // NA_VCHUNK):
        rs = slice(ch * NA_VCHUNK, (ch + 1) * NA_VCHUNK)
        qp_ref[0, ch] = q[rs].T.astype(BF16)
        qr_ref[0, ch] = q_rot[rs].T.astype(BF16)
        v_ref[0, ch] = v[rs].T.astype(BF16)
    bx_ref[0] = proj('b_x').astype(BF16)
    dq_ref[0] = proj('d_q').astype(BF16)
    di_ref[0] = proj('d_i').astype(BF16)


def _proj_call(x, sh, sc, gpre, w_proj, cos, sin_signed, gn, ws, bs_tab, lb, tm, per_batch_mod):
    b, n, _ = x.shape
    nt = n // tm
    mod_map = (lambda i, j: (i, 0, 0)) if per_batch_mod else (lambda i, j: (0, 0, 0))
    row_spec = pl.BlockSpec((1, tm, BRANCH_W), lambda i, j: (i, j, 0))
    vt_spec = pl.BlockSpec((1, tm // NA_VCHUNK, BRANCH_W, NA_VCHUNK), lambda i, j: (i, j, 0, 0))
    bf = jax.ShapeDtypeStruct((b, n, BRANCH_W), BF16)
    vt = jax.ShapeDtypeStruct((b, n // NA_VCHUNK, BRANCH_W, NA_VCHUNK), BF16)
    f32 = jax.ShapeDtypeStruct((b, n, BRANCH_W), F32)
    return pl.pallas_call(
        _proj_kernel,
        out_shape=(vt, vt, bf, vt, bf, bf, bf, bf, f32, bf, f32, bf),
        grid=(b, nt),
        in_specs=[pl.BlockSpec((1, tm, D_MODEL), lambda i, j: (i, j, 0)),
                  pl.BlockSpec((1, 1, D_MODEL), mod_map),
                  pl.BlockSpec((1, 1, D_MODEL), mod_map),
                  _resident((1, D_MODEL)),
                  _resident((D_MODEL, len(_PROJ_NAMES) * BRANCH_W)),
                  pl.BlockSpec((tm, LANES), lambda i, j: (j, 0)),
                  pl.BlockSpec((tm, LANES), lambda i, j: (j, 0)),
                  _resident((1, BRANCH_W)),
                  _resident((GMLP_GROUPS // 2, GMLP_CHUNK, 2 * GMLP_CHUNK)),
                  _resident((GMLP_CHUNK, BRANCH_W)),
                  _resident((1, 2, BRANCH_W))],
        out_specs=(vt_spec, vt_spec, row_spec, vt_spec) + (row_spec,) * 8,
        compiler_params=_cparams(("arbitrary", "arbitrary")),
        name="branch_proj",
    )(x, sh, sc, gpre, w_proj, cos, sin_signed, gn, ws, bs_tab, lb)


def _nattn_kernel(qr_ref, qp_ref, k_ref, vt_ref, kc_ref, vct_ref, bias_ref, o_ref):
    rows = k_ref.shape[1] // GRID_W
    r0 = pl.program_id(1) * NA_TILE_ROWS
    kb0 = jnp.clip(r0 - NA_WIN_H // 2, 0, rows - NA_BAND_ROWS)
    start = pl.multiple_of(kb0 * GRID_W, NA_VCHUNK)
    c0 = kb0 // (NA_VCHUNK // GRID_W)
    band = NA_BAND_ROWS * GRID_W
    nq = NA_TILE_ROWS * GRID_W

    def lanes(p):
        return slice(p * LANES, (p + 1) * LANES)

    head0_rows = lax.broadcasted_iota(jnp.int32, (LANES, nq), 0) < HEAD_DIM

    def stack_heads(qt):
        zero = jnp.zeros_like(qt)
        return jnp.concatenate([jnp.where(head0_rows, qt, zero), jnp.where(head0_rows, zero, qt)], axis=1)

    def scores(p):
        s_ctx = _dot(kc_ref[0, :, lanes(p)], stack_heads(qp_ref[0, 0, lanes(p), :]))
        s_band = _dot(k_ref[0, pl.ds(start, band), lanes(p)], stack_heads(qr_ref[0, 0, lanes(p), :]))
        bias = jnp.concatenate([bias_ref[0, :, 2 * p + hh].reshape(band, nq) for hh in range(2)], axis=1)
        return s_ctx, s_band + bias

    def softmax(s_ctx, s_band):
        p_ctx, p_band = [], []
        for cb in range(2 * nq // LANES):
            cs = slice(cb * LANES, (cb + 1) * LANES)
            sc, sb = s_ctx[:, cs], s_band[:, cs]
            mx = jnp.maximum(jnp.max(sc, axis=0, keepdims=True), jnp.max(sb, axis=0, keepdims=True))
            p_ctx.append(jnp.exp2(sc - mx).astype(BF16))
            p_band.append(jnp.exp2(sb - mx).astype(BF16))
        return jnp.concatenate(p_ctx, axis=1), jnp.concatenate(p_band, axis=1)

    ones_rows = jnp.ones((NA_DEN_ROWS, NA_VCHUNK), BF16)

    def values(p, p_ctx, p_band):
        outs = []
        for hh in range(2):
            qs = slice(hh * nq, (hh + 1) * nq)
            ch = slice(p * LANES + hh * HEAD_DIM, p * LANES + (hh + 1) * HEAD_DIM)
            acc = None
            for j in range(vct_ref.shape[1]):
                lhs = jnp.concatenate([vct_ref[0, j, ch, :], ones_rows], axis=0)
                term = _dot(lhs, p_ctx[j * NA_VCHUNK:(j + 1) * NA_VCHUNK, qs])
                acc = term if acc is None else acc + term
            for j in range(band // NA_VCHUNK):
                lhs = jnp.concatenate([vt_ref[0, c0 + j, ch, :], ones_rows], axis=0)
                acc = acc + _dot(lhs, p_band[j * NA_VCHUNK:(j + 1) * NA_VCHUNK, qs])
            outs.append(acc[0:HEAD_DIM] * (1.0 / acc[HEAD_DIM:HEAD_DIM + 1]))
        o_ref[0, :, lanes(p)] = jnp.concatenate(outs, axis=0).T.astype(BF16)

    s_val, p_val = {}, {}
    for t in range(N_PAIRS + 2):
        if t < N_PAIRS:
            s_val[t] = scores(t)
        if 0 <= t - 1 < N_PAIRS:
            p_val[t - 1] = softmax(*s_val.pop(t - 1))
        if 0 <= t - 2 < N_PAIRS:
            values(t - 2, *p_val.pop(t - 2))


def _nattn_call(qrt, qpt, k, vt, kc, vct, bias):
    b, n, _ = k.shape
    rows = n // GRID_W
    nt = rows // NA_TILE_ROWS
    lc = kc.shape[1]
    nq = NA_TILE_ROWS * GRID_W

    def bias_map(i, t):
        return (jnp.where(t == 0, 0, jnp.where(t == nt - 1, 2, 1)), 0, 0, 0, 0)

    assert nq == NA_VCHUNK
    q_spec = pl.BlockSpec((1, 1, BRANCH_W, NA_VCHUNK), lambda i, t: (i, t, 0, 0))
    o_spec = pl.BlockSpec((1, nq, BRANCH_W), lambda i, t: (i, t, 0))
    full = pl.BlockSpec((1, n, BRANCH_W), lambda i, t: (i, 0, 0))
    full_t = pl.BlockSpec((1, n // NA_VCHUNK, BRANCH_W, NA_VCHUNK), lambda i, t: (i, 0, 0, 0))
    ctx = pl.BlockSpec((1, lc, BRANCH_W), lambda i, t: (i, 0, 0))
    ctx_t = pl.BlockSpec((1, lc // NA_VCHUNK, BRANCH_W, NA_VCHUNK), lambda i, t: (i, 0, 0, 0))
    return pl.pallas_call(
        _nattn_kernel,
        out_shape=jax.ShapeDtypeStruct((b, n, BRANCH_W), BF16),
        grid=(b, nt),
        in_specs=[q_spec, q_spec, full, full_t, ctx, ctx_t,
                  pl.BlockSpec((1, NA_BAND_ROWS, N_HEADS, GRID_W, nq), bias_map)],
        out_specs=o_spec,
        compiler_params=_cparams(("arbitrary", "arbitrary")),
        name="neighbourhood_attention",
    )(qrt, qpt, k, vt, kc, vct, bias)


def _cattn_kernel(qt_ref, k_ref, vt_ref, o_ref):
    lc = k_ref.shape[1]
    head0_rows = lax.broadcasted_iota(jnp.int32, (LANES, lc), 0) < HEAD_DIM
    for p in range(N_PAIRS):
        ls = slice(p * LANES, (p + 1) * LANES)
        qt = jnp.concatenate([qt_ref[0, j, ls, :] for j in range(qt_ref.shape[1])], axis=1)
        vt = jnp.concatenate([vt_ref[0, j, ls, :] for j in range(vt_ref.shape[1])], axis=1)
        k = k_ref[0, :, ls]
        zero = jnp.zeros_like(qt)
        outs = []
        for hh in range(2):
            s = _dot(k, jnp.where(head0_rows, qt, zero) if hh == 0 else jnp.where(head0_rows, zero, qt))
            e = jnp.exp2(s - jnp.max(s, axis=0, keepdims=True))
            acc = _dot(vt[hh * HEAD_DIM:(hh + 1) * HEAD_DIM], e.astype(BF16))
            outs.append(acc * (1.0 / jnp.sum(e, axis=0, keepdims=True)))
        o_ref[0, :, ls] = jnp.concatenate(outs, axis=0).T.astype(BF16)


def _cattn_call(qt, k, vt):
    b, lc, _ = k.shape
    spec = pl.BlockSpec((1, lc, BRANCH_W), lambda i: (i, 0, 0))
    spec_t = pl.BlockSpec((1, lc // NA_VCHUNK, BRANCH_W, NA_VCHUNK), lambda i: (i, 0, 0, 0))
    return pl.pallas_call(
        _cattn_kernel,
        out_shape=jax.ShapeDtypeStruct((b, lc, BRANCH_W), BF16),
        grid=(b,),
        in_specs=[spec_t, spec, spec_t],
        out_specs=spec,
        compiler_params=_cparams(("arbitrary",)),
        name="context_attention",
    )(qt, k, vt)


def _attention_bias(rpb, rows):
    col = np.arange(GRID_W)
    col_start = np.clip(col - NA_WIN_W // 2, 0, GRID_W - NA_WIN_W)
    valid = (col[:, None] >= col_start[None, :]) & (col[:, None] < col_start[None, :] + NA_WIN_W)
    width = 2 * NA_WIN_W - 1
    flipped = jnp.pad(rpb.astype(F32)[:, :, ::-1] * LOG2E, ((0, 0), (0, 0), (GRID_W - NA_WIN_W,) * 2))
    tab = jnp.stack([flipped[:, :, GRID_W - 1 - kc:2 * GRID_W - 1 - kc] for kc in range(GRID_W)], axis=2)
    assert flipped.shape[-1] == width + 2 * (GRID_W - NA_WIN_W) and tab.shape[-1] == GRID_W
    tab = jnp.where(valid[None, None], tab, NEG_INF)
    pad = NA_BAND_ROWS - NA_WIN_H
    tab = jnp.pad(tab.transpose(1, 0, 2, 3), ((pad, pad), (0, 0), (0, 0), (0, 0)), constant_values=NEG_INF)
    kinds = []
    for r0 in (0, NA_TILE_ROWS, rows - NA_TILE_ROWS):
        kb0 = int(np.clip(r0 - NA_WIN_H // 2, 0, rows - NA_BAND_ROWS))
        per_query_row = []
        for r in range(r0, r0 + NA_TILE_ROWS):
            lo = kb0 - r + NA_WIN_H - 1 + pad
            kr = kb0 + np.arange(NA_BAND_ROWS)
            rs = int(np.clip(r - NA_WIN_H // 2, 0, rows - NA_WIN_H))
            in_win = (kr >= rs) & (kr < rs + NA_WIN_H)
            per_query_row.append(jnp.where(in_win[:, None, None, None], tab[lo:lo + NA_BAND_ROWS], NEG_INF))
        kinds.append(jnp.concatenate(per_query_row, axis=3))
    return jnp.stack(kinds)


_KRON = 8


@functools.lru_cache(maxsize=None)
def _fourier_consts(n):
    rows = n // GRID_W
    k1 = np.arange(rows)[:, None, None, None]
    l1 = np.arange(_KRON)[None, :, None, None]
    n1 = np.arange(rows)[None, None, :, None]
    l2 = np.arange(_KRON)[None, None, None, :]
    a_cos, a_sin = [], []
    for j in range(GRID_W // _KRON):
        ang = 2.0 * np.pi * k1 * (GRID_W * n1 + _KRON * j + l1) / n
        same = (l1 == l2)
        a_cos.append((np.cos(ang) * same).reshape(rows * _KRON, rows * _KRON))
        a_sin.append((-np.sin(ang) * same).reshape(rows * _KRON, rows * _KRON))
    k2 = np.arange(GRID_W)[:, None, None, None]
    ang = 2.0 * np.pi * k2 * np.arange(GRID_W)[None, None, None, :] / GRID_W
    same = (np.arange(_KRON)[None, :, None, None] == np.arange(_KRON)[None, None, :, None])
    b_cos = (np.cos(ang) * same).reshape(GRID_W * _KRON, _KRON * GRID_W)
    b_sin = (np.sin(ang) * same).reshape(GRID_W * _KRON, _KRON * GRID_W)
    b_re = np.concatenate([b_cos, b_sin], axis=1)
    b_im = np.concatenate([-b_sin, b_cos], axis=1)
    return (np.stack(a_cos).astype(np.float32), np.stack(a_sin).astype(np.float32),
            b_re.astype(np.float32), b_im.astype(np.float32))


@functools.lru_cache(maxsize=None)
def _channel_dft():
    c = np.arange(FNET_GROUP_W)
    ang = 2.0 * np.pi * np.outer(c, c) / FNET_GROUP_W
    return np.concatenate([np.cos(ang), np.sin(ang)], axis=0).astype(np.float32)


@functools.lru_cache(maxsize=None)
def _dense_dft(n):
    t = np.arange(n)
    ang = 2.0 * np.pi * np.outer(t, t) / n
    return np.cos(ang).astype(np.float32), (-np.sin(ang)).astype(np.float32)


def _fold_channel_map(cs_ref, wf_ref, fold_ref, norm):
    c_hi, c_lo = _split2(cs_ref[...] * norm)
    for g in range(FNET_GROUPS):
        w_hi, w_lo = _split2(wf_ref[g])
        fold_ref[g] = (_dot(c_hi, w_hi) + _dot(c_hi, w_lo) + _dot(c_lo, w_hi)).astype(BF16)


def _channel_stage(xr, xi, fold_ref):
    outs = []
    for g in range(FNET_GROUPS):
        ls = slice(g * FNET_GROUP_W, (g + 1) * FNET_GROUP_W)
        xg = jnp.concatenate([xr[:, ls], xi[:, ls]], axis=1).astype(BF16)
        outs.append(_dot(xg, fold_ref[g]))
    return jnp.concatenate(outs, axis=1)


def _fourier_kernel(x_ref, ac_ref, as_ref, bre_ref, bim_ref, cs_ref, wf_ref, o_ref, s_ref, fold_ref, *, norm):
    @pl.when(pl.program_id(0) == 0)
    def _():
        _fold_channel_map(cs_ref, wf_ref, fold_ref, norm)

    rows = x_ref.shape[1]
    blk = rows * _KRON
    pair = 2 * _KRON
    for jj in range(GRID_W // pair):
        xt = x_ref[0, :, jj * pair:(jj + 1) * pair, :].astype(F32)
        re, im = [], []
        for half in range(2):
            xc = xt[:, half * _KRON:(half + 1) * _KRON, :].reshape(blk, BRANCH_W).astype(BF16)
            re.append(_dot(ac_ref[2 * jj + half], xc).reshape(rows, _KRON, BRANCH_W))
            im.append(_dot(as_ref[2 * jj + half], xc).reshape(rows, _KRON, BRANCH_W))
        s_ref[0, :, jj * pair:(jj + 1) * pair, :] = jnp.concatenate(re, axis=1).astype(BF16)
        s_ref[1, :, jj * pair:(jj + 1) * pair, :] = jnp.concatenate(im, axis=1).astype(BF16)
    sblk = _KRON * GRID_W
    for mm in range(rows // pair):
        ys = []
        for half in range(2):
            m0 = (2 * mm + half) * _KRON
            rhs = jnp.concatenate([s_ref[0, m0:m0 + _KRON].reshape(sblk, BRANCH_W),
                                   s_ref[1, m0:m0 + _KRON].reshape(sblk, BRANCH_W)], axis=0)
            xr = _dot(bre_ref[...], rhs)
            xi = _dot(bim_ref[...], rhs)
            ys.append(_channel_stage(xr, xi, fold_ref).reshape(GRID_W, _KRON, BRANCH_W))
        o_ref[0, :, mm * pair:(mm + 1) * pair, :] = jnp.concatenate(ys, axis=1).astype(BF16)


def _fourier_call(bx, wf):
    b, n, _ = bx.shape
    rows = n // GRID_W
    a_cos, a_sin, b_re, b_im = (jnp.asarray(t, BF16) for t in _fourier_consts(n))
    cs = jnp.asarray(_channel_dft(), F32)
    norm = float(1.0 / np.sqrt(n * FNET_GROUP_W))
    x4 = bx.reshape(b, rows, GRID_W, BRANCH_W)
    out = pl.pallas_call(
        functools.partial(_fourier_kernel, norm=norm),
        out_shape=jax.ShapeDtypeStruct((b, GRID_W, rows, BRANCH_W), BF16),
        grid=(b,),
        in_specs=[pl.BlockSpec((1, rows, GRID_W, BRANCH_W), lambda i: (i, 0, 0, 0)),
                  _resident(a_cos.shape), _resident(a_sin.shape),
                  _resident(b_re.shape), _resident(b_im.shape),
                  _resident(cs.shape), _resident(wf.shape)],
        out_specs=pl.BlockSpec((1, GRID_W, rows, BRANCH_W), lambda i: (i, 0, 0, 0)),
        scratch_shapes=[pltpu.VMEM((2, rows, GRID_W, BRANCH_W), BF16),
                        pltpu.VMEM((FNET_GROUPS, 2 * FNET_GROUP_W, FNET_GROUP_W), BF16)],
        compiler_params=_cparams(("arbitrary",)),
        name="fourier_mix",
    )(x4, a_cos, a_sin, b_re, b_im, cs, wf)
    return out.reshape(b, n, BRANCH_W)


def _fourier_ctx_kernel(x_ref, c_ref, s_ref, cs_ref, wf_ref, o_ref, fold_ref, *, norm):
    @pl.when(pl.program_id(0) == 0)
    def _():
        _fold_channel_map(cs_ref, wf_ref, fold_ref, norm)

    x = x_ref[0]
    xr = _dot(c_ref[...], x)
    xi = _dot(s_ref[...], x)
    o_ref[0] = _channel_stage(xr, xi, fold_ref).astype(BF16)


def _fourier_ctx_call(bx, wf):
    b, n, _ = bx.shape
    cn, sn = (jnp.asarray(t, BF16) for t in _dense_dft(n))
    cs = jnp.asarray(_channel_dft(), F32)
    norm = float(1.0 / np.sqrt(n * FNET_GROUP_W))
    spec = pl.BlockSpec((1, n, BRANCH_W), lambda i: (i, 0, 0))
    return pl.pallas_call(
        functools.partial(_fourier_ctx_kernel, norm=norm),
        out_shape=jax.ShapeDtypeStruct((b, n, BRANCH_W), BF16),
        grid=(b,),
        in_specs=[spec, _resident(cn.shape), _resident(sn.shape), _resident(cs.shape), _resident(wf.shape)],
        out_specs=spec,
        scratch_shapes=[pltpu.VMEM((FNET_GROUPS, 2 * FNET_GROUP_W, FNET_GROUP_W), BF16)],
        compiler_params=_cparams(("arbitrary",)),
        name="fourier_mix_context",
    )(bx, cn, sn, cs, wf)


def _block_diag(t):
    lo = _lane_iota(t.shape) < HEAD_DIM
    z = jnp.zeros_like(t)
    return jnp.concatenate([jnp.where(lo, t, z), jnp.where(lo, z, t)], axis=0)


def _hgrn_needed(j, reverse):
    nsub = HGRN_CHUNK // HGRN_SUB
    return list(range(0, j + 1)) if reverse else list(range(j, nsub))


def _hgrn_prepare(q, k, i, a, reverse):
    c = HGRN_CHUNK
    nsub = c // HGRN_SUB

    def level(r):
        return a[r:r + 1, :]

    zero_row = jnp.zeros((1, BRANCH_W), F32)
    if reverse:
        refs = [level((s + 1) * HGRN_SUB) if s + 1 < nsub else zero_row for s in range(nsub)]
        a_end = a[0:1, :]
    else:
        refs = [level(s * HGRN_SUB - 1) if s > 0 else zero_row for s in range(nsub)]
        a_end = a[c - 1:c, :]
    ref_rows = jnp.concatenate([jnp.broadcast_to(r, (HGRN_SUB, BRANCH_W)) for r in refs], axis=0)
    qf = q.astype(F32)
    kf = k.astype(F32)
    lift = ref_rows - a
    k_own = (kf * jnp.exp2(jnp.minimum(lift, HGRN_EXP_CLAMP))).astype(BF16)

    def q_variant(j):
        parts = []
        for s in _hgrn_needed(j, reverse):
            rs = slice(s * HGRN_SUB, (s + 1) * HGRN_SUB)
            parts.append((qf[rs] * jnp.exp2(a[rs] - refs[j])).astype(BF16))
        return jnp.concatenate(parts, axis=0)

    q_var = [q_variant(j) for j in range(nsub)]
    return dict(
        q_stack=jnp.concatenate(q_var, axis=0),
        q_in=q_var[nsub - 1] if reverse else q_var[0],
        k_own=k_own,
        k_out=(kf * jnp.exp2(a_end - a)).astype(BF16),
        decay_end=jnp.exp2(a_end),
        max_lift=jnp.max(lift, axis=0, keepdims=True),
        i=i, reverse=reverse)


def _hgrn_scores(ops):
    c = HGRN_CHUNK
    nsub = c // HGRN_SUB
    reverse = ops['reverse']
    src = _lane_iota((c, LANES)) % HEAD_DIM
    step = lax.broadcasted_iota(jnp.int32, (c, LANES), 0)
    seen = (src >= step) if reverse else (src <= step)
    src_sub = (_lane_iota((HGRN_SUB, LANES)) % HEAD_DIM) // HGRN_SUB
    where_blk, off = {}, 0
    for j in range(nsub):
        for s in _hgrn_needed(j, reverse):
            where_blk[(j, s)] = off
            off += HGRN_SUB
    out = []
    for p in range(N_PAIRS):
        ls = slice(p * LANES, (p + 1) * LANES)
        res = _dot_nt(ops['q_stack'][:, ls], _block_diag(ops['k_own'][:, ls]))
        rows = []
        for s in range(nsub):
            blk = None
            for j in range(nsub):
                if (j, s) in where_blk:
                    piece = res[where_blk[(j, s)]:where_blk[(j, s)] + HGRN_SUB]
                    blk = piece if blk is None else jnp.where(src_sub == j, piece, blk)
            rows.append(blk)
        out.append(jnp.where(seen, jnp.concatenate(rows, axis=0), 0.0).astype(BF16))
    return out


def _hgrn_local(ops, scores):
    low_rows = lax.broadcasted_iota(jnp.int32, (LANES, LANES), 0) < HEAD_DIM
    same_head = low_rows == (_lane_iota((LANES, LANES)) < HEAD_DIM)
    o_intra, upd = [], []
    for p in range(N_PAIRS):
        ls = slice(p * LANES, (p + 1) * LANES)
        ip = ops['i'][:, ls]
        o_intra.append(_dot(scores[p], _block_diag(ip)))
        upd.append(jnp.where(same_head, _dot_tn(ip, ops['k_out'][:, ls]), 0.0))
    return o_intra, upd


def _hgrn_carry(ops, o_intra, upd, state_ref, d):
    outs = []
    for p in range(N_PAIRS):
        ls = slice(p * LANES, (p + 1) * LANES)
        st = state_ref[d, p]
        outs.append(o_intra[p] + _dot_nt(ops['q_in'][:, ls], st.astype(BF16)))
        state_ref[d, p] = ops['decay_end'][:, ls] * st + upd[p]
    return jnp.concatenate(outs, axis=1)


def _hgrn_exact_tile(q_ref, k_ref, i_ref, a_ref, o_ref, state_ref, bb, d, reverse, q_sc, k_sc, i_sc):
    c = HGRN_CHUNK
    nchunk = q_ref.shape[1] // c
    lane_head = _lane_iota((BRANCH_W, BRANCH_W)) // HEAD_DIM
    row_head = lax.broadcasted_iota(jnp.int32, (BRANCH_W, BRANCH_W), 0) // HEAD_DIM
    head_sum = (lane_head == row_head).astype(BF16)
    step = lax.broadcasted_iota(jnp.int32, (c, BRANCH_W), 0)
    low_rows = lax.broadcasted_iota(jnp.int32, (LANES, LANES), 0) < HEAD_DIM
    same_head = low_rows == (_lane_iota((LANES, LANES)) < HEAD_DIM)
    for cix in (range(nchunk - 1, -1, -1) if reverse else range(nchunk)):
        rs = slice(cix * c, (cix + 1) * c)
        q_sc[...] = q_ref[bb, rs, :].astype(F32)
        k_sc[...] = k_ref[bb, rs, :].astype(F32)
        i_sc[...] = i_ref[bb, rs, :].astype(F32)
        qf = q_sc[...]
        kf = k_sc[...]
        a = a_ref[d, rs, :]

        def one_source(s, acc, a=a, qf=qf, cix=cix):
            a_s = a_ref[d, pl.ds(cix * c + s, 1), :]
            w = qf * (k_sc[pl.ds(s, 1), :] * jnp.exp2(jnp.minimum(a - a_s, 0.0)))
            w = jnp.where((step <= s) if reverse else (step >= s), w, 0.0)
            hi, lo = _split2(w)
            return acc + (_dot(hi, head_sum) + _dot(lo, head_sum)) * i_sc[pl.ds(s, 1), :]

        o_intra = lax.fori_loop(0, c, one_source, jnp.zeros((c, BRANCH_W), F32))
        a_end = a[0:1, :] if reverse else a[c - 1:c, :]
        q_in = (qf * jnp.exp2(a)).astype(BF16)
        k_out = (kf * jnp.exp2(a_end - a)).astype(BF16)
        decay_end = jnp.exp2(a_end)
        ib = i_ref[bb, rs, :]
        outs = []
        for p in range(N_PAIRS):
            ls = slice(p * LANES, (p + 1) * LANES)
            st = state_ref[d, p]
            outs.append(o_intra[:, ls] + _dot_nt(q_in[:, ls], st.astype(BF16)))
            upd = jnp.where(same_head, _dot_tn(ib[:, ls], k_out[:, ls]), 0.0)
            state_ref[d, p] = decay_end[:, ls] * st + upd
        o_ref[bb, rs, :] = jnp.concatenate(outs, axis=1).astype(BF16)


def _hgrn_kernel(qf_ref, if_ref, lff_ref, kf_ref, qb_ref, ib_ref, lfb_ref, kb_ref, s0f_ref, s0b_ref,
                 of_ref, ob_ref, sf_ref, sb_ref, state_ref, backup_ref, a_ref, q_sc, k_sc, i_sc):
    j = pl.program_id(1)
    nb = qf_ref.shape[0]
    nchunk = qf_ref.shape[1] // HGRN_CHUNK

    @pl.when(j == 0)
    def _():
        for bb in range(nb):
            state_ref[2 * bb] = s0f_ref[bb]
            state_ref[2 * bb + 1] = s0b_ref[bb]

    backup_ref[...] = state_ref[...]

    tm = qf_ref.shape[1]
    row = lax.broadcasted_iota(jnp.int32, (tm, tm), 0)
    col = lax.broadcasted_iota(jnp.int32, (tm, tm), 1)
    same_chunk = (row // HGRN_CHUNK) == (col // HGRN_CHUNK)

    def cum(lf, reverse):
        tri = (same_chunk & ((col >= row) if reverse else (col <= row))).astype(BF16)
        hi, lo = _split2(lf * LOG2E)
        return _dot(tri, hi) + _dot(tri, lo)

    a_f = [cum(lff_ref[bb], False) for bb in range(nb)]
    a_b = [cum(lfb_ref[bb], True) for bb in range(nb)]

    todo = []
    for cix in range(nchunk):
        bix = nchunk - 1 - cix
        for bb in range(nb):
            todo.append((bb, 2 * bb, of_ref, slice(cix * HGRN_CHUNK, (cix + 1) * HGRN_CHUNK),
                         qf_ref, kf_ref, if_ref, a_f[bb], False))
            todo.append((bb, 2 * bb + 1, ob_ref, slice(bix * HGRN_CHUNK, (bix + 1) * HGRN_CHUNK),
                         qb_ref, kb_ref, ib_ref, a_b[bb], True))
    ops, scores, local, lifts = {}, {}, {}, []
    for t in range(len(todo) + 3):
        if t < len(todo):
            bb, _, _, rs, q_ref, k_ref, i_ref, a, reverse = todo[t]
            ops[t] = _hgrn_prepare(q_ref[bb, rs, :], k_ref[bb, rs, :], i_ref[bb, rs, :], a[rs, :], reverse)
            lifts.append(ops[t]['max_lift'])
        if 0 <= t - 1 < len(todo):
            scores[t - 1] = _hgrn_scores(ops[t - 1])
        if 0 <= t - 2 < len(todo):
            local[t - 2] = _hgrn_local(ops[t - 2], scores.pop(t - 2))
        if 0 <= t - 3 < len(todo):
            bb, slot, o_ref, rs = todo[t - 3][:4]
            o_ref[bb, rs, :] = _hgrn_carry(ops.pop(t - 3), *local.pop(t - 3), state_ref, slot).astype(BF16)

    @pl.when(jnp.max(functools.reduce(jnp.maximum, lifts)) > HGRN_EXP_CLAMP)
    def _():
        state_ref[...] = backup_ref[...]
        for bb in range(nb):
            a_ref[2 * bb] = a_f[bb]
            a_ref[2 * bb + 1] = a_b[bb]
            _hgrn_exact_tile(qf_ref, kf_ref, if_ref, a_ref, of_ref, state_ref, bb, 2 * bb, False,
                             q_sc, k_sc, i_sc)
            _hgrn_exact_tile(qb_ref, kb_ref, ib_ref, a_ref, ob_ref, state_ref, bb, 2 * bb + 1, True,
                             q_sc, k_sc, i_sc)

    @pl.when(j == pl.num_programs(1) - 1)
    def _():
        for bb in range(nb):
            sf_ref[bb] = state_ref[2 * bb]
            sb_ref[bb] = state_ref[2 * bb + 1]


def _hgrn_call(q, i, lff, kf, lfb, kb, s0f, s0b, tm):
    b, n, _ = q.shape
    nt = n // tm
    nb = HGRN_BATCH_ROWS if b % HGRN_BATCH_ROWS == 0 else 1
    fwd = pl.BlockSpec((nb, tm, BRANCH_W), lambda bi, j: (bi, j, 0))
    bwd = pl.BlockSpec((nb, tm, BRANCH_W), lambda bi, j: (bi, nt - 1 - j, 0))
    st = pl.BlockSpec((nb, N_PAIRS, LANES, LANES), lambda bi, j: (bi, 0, 0, 0))
    o_shape = jax.ShapeDtypeStruct((b, n, BRANCH_W), BF16)
    s_shape = jax.ShapeDtypeStruct((b, N_PAIRS, LANES, LANES), F32)
    return pl.pallas_call(
        _hgrn_kernel,
        out_shape=(o_shape, o_shape, s_shape, s_shape),
        grid=(b // nb, nt),
        in_specs=[fwd, fwd, fwd, fwd, bwd, bwd, bwd, bwd, st, st],
        out_specs=(fwd, bwd, st, st),
        scratch_shapes=[pltpu.VMEM((2 * nb, N_PAIRS, LANES, LANES), F32),
                        pltpu.VMEM((2 * nb, N_PAIRS, LANES, LANES), F32),
                        pltpu.VMEM((2 * nb, tm, BRANCH_W), F32),
                        pltpu.VMEM((HGRN_CHUNK, BRANCH_W), F32),
                        pltpu.VMEM((HGRN_CHUNK, BRANCH_W), F32),
                        pltpu.VMEM((HGRN_CHUNK, BRANCH_W), F32)],
        compiler_params=_cparams(("arbitrary", "arbitrary")),
        name="hgrn_scan",
    )(q, i, lff, kf, q, i, lfb, kb, s0f, s0b)


def _merge_kernel(x_ref, sh_ref, sc_ref, gt_ref, gpre_ref, gpost_ref, w_ref, oa_ref, ob_ref, mx_ref,
                  of_ref, obk_ref, hn_ref, hm_ref, wb_ref, wo_ref, o_ref):
    x = x_ref[0]
    hb = _normed_input(x, gpre_ref[...], sc_ref[0], sh_ref[0]).astype(BF16)

    def proj(name):
        j = _MERGE_NAMES.index(name)
        return _dot(hb, w_ref[:, j * BRANCH_W:(j + 1) * BRANCH_W])

    ya = oa_ref[0].astype(F32) * _silu(proj('a_g'))
    yb = ob_ref[0].astype(F32) * _silu(proj('b_g'))
    yc = proj('c_u') * mx_ref[0].astype(F32) * _silu(proj('c_g'))
    o = of_ref[0].astype(F32) + obk_ref[0].astype(F32)
    ms = _dot((o * o).astype(BF16), hm_ref[...])
    yd = o * lax.rsqrt(ms + EPS) * hn_ref[...] * _silu(proj('d_g'))

    g0 = len(_MERGE_NAMES) * BRANCH_W
    merged = None
    for r, y in enumerate((ya, yb, yc, yd)):
        gate = _dot(hb, w_ref[:, g0 + r * D_MODEL:g0 + (r + 1) * D_MODEL])
        term = jax.nn.sigmoid(gate) * _dot(y.astype(BF16), wb_ref[r])
        merged = term if merged is None else merged + term
    out = _dot(merged.astype(BF16), wo_ref[...])
    post = out * lax.rsqrt(jnp.mean(out * out, axis=-1, keepdims=True) + EPS) * gpost_ref[...]
    o_ref[0] = x + gt_ref[0] * post


def _merge_call(x, sh, sc, gt, gpre, gpost, w_merge, oa, ob, mx, of, obk, hn, hmean, wb, wo, tm,
                per_batch_mod):
    b, n, _ = x.shape
    nt = n // tm
    mod_map = (lambda i, j: (i, 0, 0)) if per_batch_mod else (lambda i, j: (0, 0, 0))
    x_spec = pl.BlockSpec((1, tm, D_MODEL), lambda i, j: (i, j, 0))
    br_spec = pl.BlockSpec((1, tm, BRANCH_W), lambda i, j: (i, j, 0))
    mod_spec = pl.BlockSpec((1, 1, D_MODEL), mod_map)
    return pl.pallas_call(
        _merge_kernel,
        out_shape=jax.ShapeDtypeStruct((b, n, D_MODEL), F32),
        grid=(b, nt),
        in_specs=[x_spec, mod_spec, mod_spec, mod_spec,
                  _resident((1, D_MODEL)), _resident((1, D_MODEL)),
                  _resident(w_merge.shape),
                  br_spec, br_spec, br_spec, br_spec, br_spec,
                  _resident((1, BRANCH_W)), _resident((BRANCH_W, BRANCH_W)),
                  _resident(wb.shape), _resident(wo.shape)],
        out_specs=x_spec,
        compiler_params=_cparams(("arbitrary", "arbitrary")),
        name="branch_merge",
    )(x, sh, sc, gt, gpre, gpost, w_merge, oa, ob, mx, of, obk, hn, hmean, wb, wo)


def _rope_tables(n_tok, rotate):
    if not rotate:
        return jnp.ones((n_tok, LANES), F32), jnp.zeros((n_tok, LANES), F32)
    t = jnp.arange(n_tok, dtype=jnp.int32)
    pos = jnp.stack([t // GRID_W, t % GRID_W], axis=-1).astype(F32)
    inv = ROPE_THETA ** (-jnp.arange(ROPE_FREQS, dtype=F32) * 2.0 / (2 * ROPE_FREQS))
    ang = pos[:, :, None] * inv
    cos = jnp.repeat(jnp.cos(ang)[:, :, None, :], 2, axis=2).reshape(n_tok, HEAD_DIM)
    sin = jnp.sin(ang)
    sin_signed = jnp.stack([-sin, sin], axis=2).reshape(n_tok, HEAD_DIM)
    return jnp.tile(cos, (1, 2)), jnp.tile(sin_signed, (1, 2))


def _gather_cols(w_in_l, names):
    return jnp.concatenate([w_in_l[:, _IN_COL[nm] * BRANCH_W:(_IN_COL[nm] + 1) * BRANCH_W] for nm in names],
                           axis=1)


def _row_tile(n):
    return 512 if n % 512 == 0 else 256


def kernel(x, c, ctx, c_ctx, w_ada, b_ada, g_pre, g_post, w_in, na_rpb, fnet_w, gmlp_norm_g, gmlp_ws,
           gmlp_bs, hgrn_lb_logits, hgrn_norm_g, w_branch, w_out):
    batch, n_tok, _ = x.shape
    n_ctx = ctx.shape[1]
    depth = w_in.shape[0]

    w_proj = [_gather_cols(w_in[l], _PROJ_NAMES).astype(BF16) for l in range(depth)]
    w_merge = [jnp.concatenate([_gather_cols(w_in[l], _MERGE_NAMES), w_in[l][:, _GATE_COL0:]], axis=1).astype(BF16)
               for l in range(depth)]
    w_branch_b = w_branch.astype(BF16)
    w_out_b = w_out.astype(BF16)
    gmlp_ws_b = gmlp_ws.astype(BF16).reshape(depth, GMLP_GROUPS // 2, 2, GMLP_CHUNK, GMLP_CHUNK)
    gmlp_ws_b = gmlp_ws_b.transpose(0, 1, 3, 2, 4).reshape(depth, GMLP_GROUPS // 2, GMLP_CHUNK, 2 * GMLP_CHUNK)
    bs_tab = jnp.repeat(jnp.swapaxes(gmlp_bs, 1, 2), BRANCH_W // GMLP_GROUPS, axis=2)
    head_mean = jnp.asarray(np.kron(np.eye(N_HEADS), np.ones((HEAD_DIM, HEAD_DIM)) / HEAD_DIM), BF16)
    cos_x, sin_x = _rope_tables(n_tok, True)
    cos_c, sin_c = _rope_tables(n_ctx, False)

    c_all = jnp.concatenate([c, jnp.broadcast_to(c_ctx[None, :], (8, D_MODEL))], axis=0)
    mod = _modulation(c_all, w_ada, b_ada)
    lower = _lower_bounds(hgrn_lb_logits)

    zero_state = jnp.zeros((batch, N_PAIRS, LANES, LANES), F32)
    tm_x = _row_tile(n_tok)
    tm_c = _row_tile(n_ctx)

    for l in range(depth):
        with_ctx = l < depth - 1
        mod_x = [mod[l, :batch, i * D_MODEL:(i + 1) * D_MODEL].reshape(batch, 1, D_MODEL) for i in range(3)]
        mod_c = [mod[l, batch:batch + 1, i * D_MODEL:(i + 1) * D_MODEL].reshape(1, 1, D_MODEL) for i in range(3)]
        gpre = g_pre[l].reshape(1, D_MODEL)
        gpost = g_post[l].reshape(1, D_MODEL)
        gn = gmlp_norm_g[l].reshape(1, BRANCH_W)
        hn = hgrn_norm_g[l].reshape(1, BRANCH_W)
        lb = lower[l].reshape(1, 2, BRANCH_W)
        bias = _attention_bias(na_rpb[l], n_tok // GRID_W)

        (_, qp_c, k_c, v_c, bx_c, mx_c, dq_c, di_c, lff_c, kf_c, lfb_c, kb_c) = _proj_call(
            ctx, mod_c[0], mod_c[1], gpre, w_proj[l], cos_c, sin_c, gn, gmlp_ws_b[l], bs_tab[l], lb,
            tm_c, False)
        of_c, ob_c, st_f, st_b = _hgrn_call(dq_c, di_c, lff_c, kf_c, lfb_c, kb_c, zero_state, zero_state,
                                            min(HGRN_ROWS, n_ctx))

        (qr, qp, k, v, bx, mx, dq, di, lff, kf, lfb, kb) = _proj_call(
            x, mod_x[0], mod_x[1], gpre, w_proj[l], cos_x, sin_x, gn, gmlp_ws_b[l], bs_tab[l], lb,
            2 * tm_x if n_tok % (2 * tm_x) == 0 else tm_x, True)
        oa = _nattn_call(qr, qp, k, v, k_c, v_c, bias)
        ob = _fourier_call(bx, fnet_w[l])
        of, obk, _, _ = _hgrn_call(dq, di, lff, kf, lfb, kb, st_f, st_b, min(HGRN_ROWS, n_tok))
        x = _merge_call(x, mod_x[0], mod_x[1], mod_x[2], gpre, gpost, w_merge[l], oa, ob, mx, of, obk,
                        hn, head_mean, w_branch_b[l], w_out_b[l], tm_x, True)

        if with_ctx:
            oa_c = _cattn_call(qp_c, k_c, v_c)
            ob_c2 = _fourier_ctx_call(bx_c, fnet_w[l])
            ctx = _merge_call(ctx, mod_c[0], mod_c[1], mod_c[2], gpre, gpost, w_merge[l], oa_c, ob_c2, mx_c,
                              of_c, ob_c, hn, head_mean, w_branch_b[l], w_out_b[l], tm_c, False)
    return x
```

```python
import functools

import numpy as np
import jax
import jax.numpy as jnp
from jax import lax
from jax.experimental import pallas as pl
from jax.experimental.pallas import tpu as pltpu

F32 = jnp.float32
BF16 = jnp.bfloat16

D_MODEL = 1024
BRANCH_W = 512
N_BRANCH = 4
GRID_W = 64
HEAD_DIM = 64
N_HEADS = 8
LANES = 128
N_PAIRS = BRANCH_W // LANES
NA_WIN_H = 8
NA_WIN_W = 16
NA_TILE_ROWS = 4
NA_BAND_ROWS = 12
NA_VCHUNK = 256
NA_DEN_ROWS = 16
LOG2E = 1.4426950408889634
ROPE_THETA = 10000.0
ROPE_FREQS = 16
FNET_GROUPS = 4
FNET_GROUP_W = 128
GMLP_CHUNK = 128
GMLP_GROUPS = 8
HGRN_CHUNK = 64
HGRN_SUB = 16
HGRN_EXP_CLAMP = 115.0
HGRN_ROWS = 256
MERGE_SUB_ROWS = 512
HGRN_BATCH_ROWS = 2
EPS = 1e-6
F_FLOOR = 1e-30
NEG_INF = -1e30

VMEM_LIMIT = 56 * 2**20

_IN_COL = {'a_q': 0, 'a_k': 1, 'a_v': 2, 'a_g': 3, 'b_x': 4, 'b_g': 5, 'c_u': 6, 'c_v': 7, 'c_g': 8,
           'd_q': 9, 'd_f_fwd': 10, 'd_f_bwd': 11, 'd_i': 12, 'd_g': 13}
_PROJ_NAMES = ('a_q', 'a_k', 'a_v', 'b_x', 'c_v', 'd_q', 'd_f_fwd', 'd_f_bwd', 'd_i')
_MERGE_NAMES = ('c_u', 'a_g', 'b_g', 'c_g', 'd_g')
_GATE_COL0 = 14 * BRANCH_W


def _cparams(sem):
    return pltpu.CompilerParams(dimension_semantics=sem, vmem_limit_bytes=VMEM_LIMIT)


def _resident(shape):
    nd = len(shape)
    return pl.BlockSpec(shape, lambda *_: (0,) * nd, pipeline_mode=pl.Buffered(1))


def _silu(t):
    return t * jax.nn.sigmoid(t)


def _lane_iota(shape):
    return lax.broadcasted_iota(jnp.int32, shape, len(shape) - 1)


def _dot(a, b):
    return jnp.dot(a, b, preferred_element_type=F32)


def _dot_nt(a, b):
    return lax.dot_general(a, b, (((1,), (1,)), ((), ())), preferred_element_type=F32)


def _dot_tn(a, b):
    return lax.dot_general(a, b, (((0,), (0,)), ((), ())), preferred_element_type=F32)


def _split2(t):
    hi = t.astype(BF16)
    return hi, (t - hi.astype(F32)).astype(BF16)


def _normed_input(x, gpre, sc, sh):
    ms = jnp.mean(x * x, axis=-1, keepdims=True)
    h = x * lax.rsqrt(ms + EPS) * gpre
    return h * (1.0 + sc) + sh


def _mod_kernel(c_ref, w_ref, b_ref, o_ref):
    s = _silu(c_ref[...]).astype(BF16)
    o_ref[0] = _dot(s, w_ref[0].astype(BF16)) + b_ref[0]


def _modulation(c_all, w_ada, b_ada):
    depth = w_ada.shape[0]
    rows = c_all.shape[0]
    tn = 1024
    return pl.pallas_call(
        _mod_kernel,
        out_shape=jax.ShapeDtypeStruct((depth, rows, 3 * D_MODEL), F32),
        grid=(depth, 3 * D_MODEL // tn),
        in_specs=[pl.BlockSpec((rows, D_MODEL), lambda l, j: (0, 0)),
                  pl.BlockSpec((1, D_MODEL, tn), lambda l, j: (l, 0, j)),
                  pl.BlockSpec((1, 1, tn), lambda l, j: (l, 0, j))],
        out_specs=pl.BlockSpec((1, rows, tn), lambda l, j: (l, 0, j)),
        compiler_params=_cparams(("arbitrary", "arbitrary")),
        name="adaln_modulation",
    )(c_all, w_ada, b_ada.reshape(depth, 1, 3 * D_MODEL))


def _lb_kernel(lg_ref, o_ref):
    depth = lg_ref.shape[0]
    lg = [lg_ref[l] for l in range(depth)]
    m = functools.reduce(jnp.maximum, lg)
    e = [jnp.exp(t - m) for t in lg]
    tot = functools.reduce(lambda a, b: a + b, e)
    sm = [t / tot for t in e]
    run = jnp.zeros_like(sm[0])
    for l in range(depth):
        run = run + sm[l]
        o_ref[l] = jnp.maximum(run - sm[0], 0.0)


def _lower_bounds(lb_logits):
    return pl.pallas_call(
        _lb_kernel,
        out_shape=jax.ShapeDtypeStruct(lb_logits.shape, F32),
        name="hgrn_lower_bounds",
    )(lb_logits)


def _rope(t, cos, sin_signed, first_half):
    up = pltpu.roll(t, LANES - ROPE_FREQS, 1)
    down = pltpu.roll(t, ROPE_FREQS, 1)
    return t * cos + jnp.where(first_half, up, down) * sin_signed


def _proj_kernel(x_ref, sh_ref, sc_ref, gpre_ref, w_ref, cos_ref, sin_ref, gn_ref, ws_ref, bs_ref,
                 lb_ref, qr_ref, qp_ref, kr_ref, v_ref, bx_ref, mx_ref, dq_ref, di_ref,
                 lff_ref, kf_ref, lfb_ref, kb_ref):
    tm = x_ref.shape[1]
    hb = _normed_input(x_ref[0], gpre_ref[...], sc_ref[0], sh_ref[0]).astype(BF16)

    def proj(name):
        j = _PROJ_NAMES.index(name)
        return _dot(hb, w_ref[:, j * BRANCH_W:(j + 1) * BRANCH_W])

    cos = cos_ref[...]
    sin_signed = sin_ref[...]
    first_half = (_lane_iota((tm, LANES)) % (2 * ROPE_FREQS)) < ROPE_FREQS

    def rope_all(t):
        return jnp.concatenate(
            [_rope(t[:, p * LANES:(p + 1) * LANES], cos, sin_signed, first_half) for p in range(N_PAIRS)],
            axis=1)

    for name, d, lf_ref, k_ref in (('d_f_fwd', 0, lff_ref, kf_ref), ('d_f_bwd', 1, lfb_ref, kb_ref)):
        z = proj(name)
        lb = lb_ref[0, d:d + 1, :]
        sg = jax.nn.sigmoid(z)
        lf_ref[0] = jnp.log(jnp.maximum(lb + (1.0 - lb) * sg, F_FLOOR))
        k_ref[0] = ((1.0 - lb) * (1.0 - sg)).astype(BF16)

    cv = proj('c_v')
    vn = (cv * lax.rsqrt(jnp.mean(cv * cv, axis=-1, keepdims=True) + EPS) * gn_ref[...]).astype(BF16)
    low_group = _lane_iota((GMLP_CHUNK, LANES)) < (LANES // 2)
    for ch in range(tm // GMLP_CHUNK):
        r0 = ch * GMLP_CHUNK
        for p in range(N_PAIRS):
            slab = vn[r0:r0 + GMLP_CHUNK, p * LANES:(p + 1) * LANES]
            zero = jnp.zeros_like(slab)
            stacked = jnp.concatenate([jnp.where(low_group, slab, zero), jnp.where(low_group, zero, slab)], axis=0)
            mixed = _dot(ws_ref[p], stacked) + bs_ref[:, p * LANES:(p + 1) * LANES]
            mx_ref[0, r0:r0 + GMLP_CHUNK, p * LANES:(p + 1) * LANES] = mixed.astype(BF16)

    q = proj('a_q') * (HEAD_DIM ** -0.5 * LOG2E)
    q_rot = rope_all(q)
    kr_ref[0] = rope_all(proj('a_k')).astype(BF16)
    v = proj('a_v')
    for ch in range(tm // NA_VCHUNK):
        rs = slice(ch * NA_VCHUNK, (ch + 1) * NA_VCHUNK)
        qp_ref[0, ch] = q[rs].T.astype(BF16)
        qr_ref[0, ch] = q_rot[rs].T.astype(BF16)
        v_ref[0, ch] = v[rs].T.astype(BF16)
    bx_ref[0] = proj('b_x').astype(BF16)
    dq_ref[0] = proj('d_q').astype(BF16)
    di_ref[0] = proj('d_i').astype(BF16)


def _proj_call(x, sh, sc, gpre, w_proj, cos, sin_signed, gn, ws, bs_tab, lb, tm, per_batch_mod):
    b, n, _ = x.shape
    nt = n // tm
    mod_map = (lambda i, j: (i, 0, 0)) if per_batch_mod else (lambda i, j: (0, 0, 0))
    row_spec = pl.BlockSpec((1, tm, BRANCH_W), lambda i, j: (i, j, 0))
    vt_spec = pl.BlockSpec((1, tm // NA_VCHUNK, BRANCH_W, NA_VCHUNK), lambda i, j: (i, j, 0, 0))
    bf = jax.ShapeDtypeStruct((b, n, BRANCH_W), BF16)
    vt = jax.ShapeDtypeStruct((b, n // NA_VCHUNK, BRANCH_W, NA_VCHUNK), BF16)
    f32 = jax.ShapeDtypeStruct((b, n, BRANCH_W), F32)
    return pl.pallas_call(
        _proj_kernel,
        out_shape=(vt, vt, bf, vt, bf, bf, bf, bf, f32, bf, f32, bf),
        grid=(b, nt),
        in_specs=[pl.BlockSpec((1, tm, D_MODEL), lambda i, j: (i, j, 0)),
                  pl.BlockSpec((1, 1, D_MODEL), mod_map),
                  pl.BlockSpec((1, 1, D_MODEL), mod_map),
                  _resident((1, D_MODEL)),
                  _resident((D_MODEL, len(_PROJ_NAMES) * BRANCH_W)),
                  pl.BlockSpec((tm, LANES), lambda i, j: (j, 0)),
                  pl.BlockSpec((tm, LANES), lambda i, j: (j, 0)),
                  _resident((1, BRANCH_W)),
                  _resident((GMLP_GROUPS // 2, GMLP_CHUNK, 2 * GMLP_CHUNK)),
                  _resident((GMLP_CHUNK, BRANCH_W)),
                  _resident((1, 2, BRANCH_W))],
        out_specs=(vt_spec, vt_spec, row_spec, vt_spec) + (row_spec,) * 8,
        compiler_params=_cparams(("arbitrary", "arbitrary")),
        name="branch_proj",
    )(x, sh, sc, gpre, w_proj, cos, sin_signed, gn, ws, bs_tab, lb)


def _nattn_kernel(qr_ref, qp_ref, k_ref, vt_ref, kc_ref, vct_ref, bias_ref, o_ref):
    rows = k_ref.shape[1] // GRID_W
    r0 = pl.program_id(1) * NA_TILE_ROWS
    kb0 = jnp.clip(r0 - NA_WIN_H // 2, 0, rows - NA_BAND_ROWS)
    start = pl.multiple_of(kb0 * GRID_W, NA_VCHUNK)
    c0 = kb0 // (NA_VCHUNK // GRID_W)
    band = NA_BAND_ROWS * GRID_W
    nq = NA_TILE_ROWS * GRID_W

    def lanes(p):
        return slice(p * LANES, (p + 1) * LANES)

    head0_rows = lax.broadcasted_iota(jnp.int32, (LANES, nq), 0) < HEAD_DIM

    def stack_heads(qt):
        zero = jnp.zeros_like(qt)
        return jnp.concatenate([jnp.where(head0_rows, qt, zero), jnp.where(head0_rows, zero, qt)], axis=1)

    def scores(p):
        s_ctx = _dot(kc_ref[0, :, lanes(p)], stack_heads(qp_ref[0, 0, lanes(p), :]))
        s_band = _dot(k_ref[0, pl.ds(start, band), lanes(p)], stack_heads(qr_ref[0, 0, lanes(p), :]))
        bias = jnp.concatenate([bias_ref[0, :, 2 * p + hh].reshape(band, nq) for hh in range(2)], axis=1)
        return s_ctx, s_band + bias

    def softmax(s_ctx, s_band):
        p_ctx, p_band = [], []
        for cb in range(2 * nq // LANES):
            cs = slice(cb * LANES, (cb + 1) * LANES)
            sc, sb = s_ctx[:, cs], s_band[:, cs]
            mx = jnp.maximum(jnp.max(sc, axis=0, keepdims=True), jnp.max(sb, axis=0, keepdims=True))
            p_ctx.append(jnp.exp2(sc - mx).astype(BF16))
            p_band.append(jnp.exp2(sb - mx).astype(BF16))
        return jnp.concatenate(p_ctx, axis=1), jnp.concatenate(p_band, axis=1)

    ones_rows = jnp.ones((NA_DEN_ROWS, NA_VCHUNK), BF16)

    def values(p, p_ctx, p_band):
        outs = []
        for hh in range(2):
            qs = slice(hh * nq, (hh + 1) * nq)
            ch = slice(p * LANES + hh * HEAD_DIM, p * LANES + (hh + 1) * HEAD_DIM)
            acc = None
            for j in range(vct_ref.shape[1]):
                lhs = jnp.concatenate([vct_ref[0, j, ch, :], ones_rows], axis=0)
                term = _dot(lhs, p_ctx[j * NA_VCHUNK:(j + 1) * NA_VCHUNK, qs])
                acc = term if acc is None else acc + term
            for j in range(band // NA_VCHUNK):
                lhs = jnp.concatenate([vt_ref[0, c0 + j, ch, :], ones_rows], axis=0)
                acc = acc + _dot(lhs, p_band[j * NA_VCHUNK:(j + 1) * NA_VCHUNK, qs])
            outs.append(acc[0:HEAD_DIM] * (1.0 / acc[HEAD_DIM:HEAD_DIM + 1]))
        o_ref[0, :, lanes(p)] = jnp.concatenate(outs, axis=0).T.astype(BF16)

    s_val, p_val = {}, {}
    for t in range(N_PAIRS + 2):
        if t < N_PAIRS:
            s_val[t] = scores(t)
        if 0 <= t - 1 < N_PAIRS:
            p_val[t - 1] = softmax(*s_val.pop(t - 1))
        if 0 <= t - 2 < N_PAIRS:
            values(t - 2, *p_val.pop(t - 2))


def _nattn_call(qrt, qpt, k, vt, kc, vct, bias):
    b, n, _ = k.shape
    rows = n // GRID_W
    nt = rows // NA_TILE_ROWS
    lc = kc.shape[1]
    nq = NA_TILE_ROWS * GRID_W

    def bias_map(i, t):
        return (jnp.where(t == 0, 0, jnp.where(t == nt - 1, 2, 1)), 0, 0, 0, 0)

    assert nq == NA_VCHUNK
    q_spec = pl.BlockSpec((1, 1, BRANCH_W, NA_VCHUNK), lambda i, t: (i, t, 0, 0))
    o_spec = pl.BlockSpec((1, nq, BRANCH_W), lambda i, t: (i, t, 0))
    full = pl.BlockSpec((1, n, BRANCH_W), lambda i, t: (i, 0, 0))
    full_t = pl.BlockSpec((1, n // NA_VCHUNK, BRANCH_W, NA_VCHUNK), lambda i, t: (i, 0, 0, 0))
    ctx = pl.BlockSpec((1, lc, BRANCH_W), lambda i, t: (i, 0, 0))
    ctx_t = pl.BlockSpec((1, lc // NA_VCHUNK, BRANCH_W, NA_VCHUNK), lambda i, t: (i, 0, 0, 0))
    return pl.pallas_call(
        _nattn_kernel,
        out_shape=jax.ShapeDtypeStruct((b, n, BRANCH_W), BF16),
        grid=(b, nt),
        in_specs=[q_spec, q_spec, full, full_t, ctx, ctx_t,
                  pl.BlockSpec((1, NA_BAND_ROWS, N_HEADS, GRID_W, nq), bias_map)],
        out_specs=o_spec,
        compiler_params=_cparams(("arbitrary", "arbitrary")),
        name="neighbourhood_attention",
    )(qrt, qpt, k, vt, kc, vct, bias)


def _cattn_kernel(qt_ref, k_ref, vt_ref, o_ref):
    lc = k_ref.shape[1]
    head0_rows = lax.broadcasted_iota(jnp.int32, (LANES, lc), 0) < HEAD_DIM
    for p in range(N_PAIRS):
        ls = slice(p * LANES, (p + 1) * LANES)
        qt = jnp.concatenate([qt_ref[0, j, ls, :] for j in range(qt_ref.shape[1])], axis=1)
        vt = jnp.concatenate([vt_ref[0, j, ls, :] for j in range(vt_ref.shape[1])], axis=1)
        k = k_ref[0, :, ls]
        zero = jnp.zeros_like(qt)
        outs = []
        for hh in range(2):
            s = _dot(k, jnp.where(head0_rows, qt, zero) if hh == 0 else jnp.where(head0_rows, zero, qt))
            e = jnp.exp2(s - jnp.max(s, axis=0, keepdims=True))
            acc = _dot(vt[hh * HEAD_DIM:(hh + 1) * HEAD_DIM], e.astype(BF16))
            outs.append(acc * (1.0 / jnp.sum(e, axis=0, keepdims=True)))
        o_ref[0, :, ls] = jnp.concatenate(outs, axis=0).T.astype(BF16)


def _cattn_call(qt, k, vt):
    b, lc, _ = k.shape
    spec = pl.BlockSpec((1, lc, BRANCH_W), lambda i: (i, 0, 0))
    spec_t = pl.BlockSpec((1, lc // NA_VCHUNK, BRANCH_W, NA_VCHUNK), lambda i: (i, 0, 0, 0))
    return pl.pallas_call(
        _cattn_kernel,
        out_shape=jax.ShapeDtypeStruct((b, lc, BRANCH_W), BF16),
        grid=(b,),
        in_specs=[spec_t, spec, spec_t],
        out_specs=spec,
        compiler_params=_cparams(("arbitrary",)),
        name="context_attention",
    )(qt, k, vt)


def _attention_bias(rpb, rows):
    col = np.arange(GRID_W)
    col_start = np.clip(col - NA_WIN_W // 2, 0, GRID_W - NA_WIN_W)
    valid = (col[:, None] >= col_start[None, :]) & (col[:, None] < col_start[None, :] + NA_WIN_W)
    width = 2 * NA_WIN_W - 1
    flipped = jnp.pad(rpb.astype(F32)[:, :, ::-1] * LOG2E, ((0, 0), (0, 0), (GRID_W - NA_WIN_W,) * 2))
    tab = jnp.stack([flipped[:, :, GRID_W - 1 - kc:2 * GRID_W - 1 - kc] for kc in range(GRID_W)], axis=2)
    assert flipped.shape[-1] == width + 2 * (GRID_W - NA_WIN_W) and tab.shape[-1] == GRID_W
    tab = jnp.where(valid[None, None], tab, NEG_INF)
    pad = NA_BAND_ROWS - NA_WIN_H
    tab = jnp.pad(tab.transpose(1, 0, 2, 3), ((pad, pad), (0, 0), (0, 0), (0, 0)), constant_values=NEG_INF)
    kinds = []
    for r0 in (0, NA_TILE_ROWS, rows - NA_TILE_ROWS):
        kb0 = int(np.clip(r0 - NA_WIN_H // 2, 0, rows - NA_BAND_ROWS))
        per_query_row = []
        for r in range(r0, r0 + NA_TILE_ROWS):
            lo = kb0 - r + NA_WIN_H - 1 + pad
            kr = kb0 + np.arange(NA_BAND_ROWS)
            rs = int(np.clip(r - NA_WIN_H // 2, 0, rows - NA_WIN_H))
            in_win = (kr >= rs) & (kr < rs + NA_WIN_H)
            per_query_row.append(jnp.where(in_win[:, None, None, None], tab[lo:lo + NA_BAND_ROWS], NEG_INF))
        kinds.append(jnp.concatenate(per_query_row, axis=3))
    return jnp.stack(kinds)


_KRON = 8


@functools.lru_cache(maxsize=None)
def _fourier_consts(n):
    rows = n // GRID_W
    k1 = np.arange(rows)[:, None, None, None]
    l1 = np.arange(_KRON)[None, :, None, None]
    n1 = np.arange(rows)[None, None, :, None]
    l2 = np.arange(_KRON)[None, None, None, :]
    a_cos, a_sin = [], []
    for j in range(GRID_W // _KRON):
        ang = 2.0 * np.pi * k1 * (GRID_W * n1 + _KRON * j + l1) / n
        same = (l1 == l2)
        a_cos.append((np.cos(ang) * same).reshape(rows * _KRON, rows * _KRON))
        a_sin.append((-np.sin(ang) * same).reshape(rows * _KRON, rows * _KRON))
    k2 = np.arange(GRID_W)[:, None, None, None]
    ang = 2.0 * np.pi * k2 * np.arange(GRID_W)[None, None, None, :] / GRID_W
    same = (np.arange(_KRON)[None, :, None, None] == np.arange(_KRON)[None, None, :, None])
    b_cos = (np.cos(ang) * same).reshape(GRID_W * _KRON, _KRON * GRID_W)
    b_sin = (np.sin(ang) * same).reshape(GRID_W * _KRON, _KRON * GRID_W)
    b_re = np.concatenate([b_cos, b_sin], axis=1)
    b_im = np.concatenate([-b_sin, b_cos], axis=1)
    return (np.stack(a_cos).astype(np.float32), np.stack(a_sin).astype(np.float32),
            b_re.astype(np.float32), b_im.astype(np.float32))


@functools.lru_cache(maxsize=None)
def _channel_dft():
    c = np.arange(FNET_GROUP_W)
    ang = 2.0 * np.pi * np.outer(c, c) / FNET_GROUP_W
    return np.concatenate([np.cos(ang), np.sin(ang)], axis=0).astype(np.float32)


@functools.lru_cache(maxsize=None)
def _dense_dft(n):
    t = np.arange(n)
    ang = 2.0 * np.pi * np.outer(t, t) / n
    return np.cos(ang).astype(np.float32), (-np.sin(ang)).astype(np.float32)


def _fold_channel_map(cs_ref, wf_ref, fold_ref, norm):
    c_hi, c_lo = _split2(cs_ref[...] * norm)
    for g in range(FNET_GROUPS):
        w_hi, w_lo = _split2(wf_ref[g])
        fold_ref[g] = (_dot(c_hi, w_hi) + _dot(c_hi, w_lo) + _dot(c_lo, w_hi)).astype(BF16)


def _channel_stage(xr, xi, fold_ref):
    outs = []
    for g in range(FNET_GROUPS):
        ls = slice(g * FNET_GROUP_W, (g + 1) * FNET_GROUP_W)
        xg = jnp.concatenate([xr[:, ls], xi[:, ls]], axis=1).astype(BF16)
        outs.append(_dot(xg, fold_ref[g]))
    return jnp.concatenate(outs, axis=1)


def _fourier_kernel(x_ref, ac_ref, as_ref, bre_ref, bim_ref, cs_ref, wf_ref, o_ref, s_ref, fold_ref, *, norm):
    @pl.when(pl.program_id(0) == 0)
    def _():
        _fold_channel_map(cs_ref, wf_ref, fold_ref, norm)

    rows = x_ref.shape[1]
    blk = rows * _KRON
    pair = 2 * _KRON
    for jj in range(GRID_W // pair):
        xt = x_ref[0, :, jj * pair:(jj + 1) * pair, :].astype(F32)
        re, im = [], []
        for half in range(2):
            xc = xt[:, half * _KRON:(half + 1) * _KRON, :].reshape(blk, BRANCH_W).astype(BF16)
            re.append(_dot(ac_ref[2 * jj + half], xc).reshape(rows, _KRON, BRANCH_W))
            im.append(_dot(as_ref[2 * jj + half], xc).reshape(rows, _KRON, BRANCH_W))
        s_ref[0, :, jj * pair:(jj + 1) * pair, :] = jnp.concatenate(re, axis=1).astype(BF16)
        s_ref[1, :, jj * pair:(jj + 1) * pair, :] = jnp.concatenate(im, axis=1).astype(BF16)
    sblk = _KRON * GRID_W
    for mm in range(rows // pair):
        ys = []
        for half in range(2):
            m0 = (2 * mm + half) * _KRON
            rhs = jnp.concatenate([s_ref[0, m0:m0 + _KRON].reshape(sblk, BRANCH_W),
                                   s_ref[1, m0:m0 + _KRON].reshape(sblk, BRANCH_W)], axis=0)
            xr = _dot(bre_ref[...], rhs)
            xi = _dot(bim_ref[...], rhs)
            ys.append(_channel_stage(xr, xi, fold_ref).reshape(GRID_W, _KRON, BRANCH_W))
        o_ref[0, :, mm * pair:(mm + 1) * pair, :] = jnp.concatenate(ys, axis=1).astype(BF16)


def _fourier_call(bx, wf):
    b, n, _ = bx.shape
    rows = n // GRID_W
    a_cos, a_sin, b_re, b_im = (jnp.asarray(t, BF16) for t in _fourier_consts(n))
    cs = jnp.asarray(_channel_dft(), F32)
    norm = float(1.0 / np.sqrt(n * FNET_GROUP_W))
    x4 = bx.reshape(b, rows, GRID_W, BRANCH_W)
    out = pl.pallas_call(
        functools.partial(_fourier_kernel, norm=norm),
        out_shape=jax.ShapeDtypeStruct((b, GRID_W, rows, BRANCH_W), BF16),
        grid=(b,),
        in_specs=[pl.BlockSpec((1, rows, GRID_W, BRANCH_W), lambda i: (i, 0, 0, 0)),
                  _resident(a_cos.shape), _resident(a_sin.shape),
                  _resident(b_re.shape), _resident(b_im.shape),
                  _resident(cs.shape), _resident(wf.shape)],
        out_specs=pl.BlockSpec((1, GRID_W, rows, BRANCH_W), lambda i: (i, 0, 0, 0)),
        scratch_shapes=[pltpu.VMEM((2, rows, GRID_W, BRANCH_W), BF16),
                        pltpu.VMEM((FNET_GROUPS, 2 * FNET_GROUP_W, FNET_GROUP_W), BF16)],
        compiler_params=_cparams(("arbitrary",)),
        name="fourier_mix",
    )(x4, a_cos, a_sin, b_re, b_im, cs, wf)
    return out.reshape(b, n, BRANCH_W)


def _fourier_ctx_kernel(x_ref, c_ref, s_ref, cs_ref, wf_ref, o_ref, fold_ref, *, norm):
    @pl.when(pl.program_id(0) == 0)
    def _():
        _fold_channel_map(cs_ref, wf_ref, fold_ref, norm)

    x = x_ref[0]
    xr = _dot(c_ref[...], x)
    xi = _dot(s_ref[...], x)
    o_ref[0] = _channel_stage(xr, xi, fold_ref).astype(BF16)


def _fourier_ctx_call(bx, wf):
    b, n, _ = bx.shape
    cn, sn = (jnp.asarray(t, BF16) for t in _dense_dft(n))
    cs = jnp.asarray(_channel_dft(), F32)
    norm = float(1.0 / np.sqrt(n * FNET_GROUP_W))
    spec = pl.BlockSpec((1, n, BRANCH_W), lambda i: (i, 0, 0))
    return pl.pallas_call(
        functools.partial(_fourier_ctx_kernel, norm=norm),
        out_shape=jax.ShapeDtypeStruct((b, n, BRANCH_W), BF16),
        grid=(b,),
        in_specs=[spec, _resident(cn.shape), _resident(sn.shape), _resident(cs.shape), _resident(wf.shape)],
        out_specs=spec,
        scratch_shapes=[pltpu.VMEM((FNET_GROUPS, 2 * FNET_GROUP_W, FNET_GROUP_W), BF16)],
        compiler_params=_cparams(("arbitrary",)),
        name="fourier_mix_context",
    )(bx, cn, sn, cs, wf)


def _block_diag(t):
    lo = _lane_iota(t.shape) < HEAD_DIM
    z = jnp.zeros_like(t)
    return jnp.concatenate([jnp.where(lo, t, z), jnp.where(lo, z, t)], axis=0)


def _hgrn_needed(j, reverse):
    nsub = HGRN_CHUNK // HGRN_SUB
    return list(range(0, j + 1)) if reverse else list(range(j, nsub))


def _hgrn_prepare(q, k, i, a, reverse):
    c = HGRN_CHUNK
    nsub = c // HGRN_SUB

    def level(r):
        return a[r:r + 1, :]

    zero_row = jnp.zeros((1, BRANCH_W), F32)
    if reverse:
        refs = [level((s + 1) * HGRN_SUB) if s + 1 < nsub else zero_row for s in range(nsub)]
        a_end = a[0:1, :]
    else:
        refs = [level(s * HGRN_SUB - 1) if s > 0 else zero_row for s in range(nsub)]
        a_end = a[c - 1:c, :]
    ref_rows = jnp.concatenate([jnp.broadcast_to(r, (HGRN_SUB, BRANCH_W)) for r in refs], axis=0)
    qf = q.astype(F32)
    kf = k.astype(F32)
    lift = ref_rows - a
    k_own = (kf * jnp.exp2(jnp.minimum(lift, HGRN_EXP_CLAMP))).astype(BF16)

    def q_variant(j):
        parts = []
        for s in _hgrn_needed(j, reverse):
            rs = slice(s * HGRN_SUB, (s + 1) * HGRN_SUB)
            parts.append((qf[rs] * jnp.exp2(a[rs] - refs[j])).astype(BF16))
        return jnp.concatenate(parts, axis=0)

    q_var = [q_variant(j) for j in range(nsub)]
    return dict(
        q_stack=jnp.concatenate(q_var, axis=0),
        q_in=q_var[nsub - 1] if reverse else q_var[0],
        k_own=k_own,
        k_out=(kf * jnp.exp2(a_end - a)).astype(BF16),
        decay_end=jnp.exp2(a_end),
        max_lift=jnp.max(lift, axis=0, keepdims=True),
        i=i, reverse=reverse)


def _hgrn_scores(ops):
    c = HGRN_CHUNK
    nsub = c // HGRN_SUB
    reverse = ops['reverse']
    src = _lane_iota((c, LANES)) % HEAD_DIM
    step = lax.broadcasted_iota(jnp.int32, (c, LANES), 0)
    seen = (src >= step) if reverse else (src <= step)
    src_sub = (_lane_iota((HGRN_SUB, LANES)) % HEAD_DIM) // HGRN_SUB
    where_blk, off = {}, 0
    for j in range(nsub):
        for s in _hgrn_needed(j, reverse):
            where_blk[(j, s)] = off
            off += HGRN_SUB
    out = []
    for p in range(N_PAIRS):
        ls = slice(p * LANES, (p + 1) * LANES)
        res = _dot_nt(ops['q_stack'][:, ls], _block_diag(ops['k_own'][:, ls]))
        rows = []
        for s in range(nsub):
            blk = None
            for j in range(nsub):
                if (j, s) in where_blk:
                    piece = res[where_blk[(j, s)]:where_blk[(j, s)] + HGRN_SUB]
                    blk = piece if blk is None else jnp.where(src_sub == j, piece, blk)
            rows.append(blk)
        out.append(jnp.where(seen, jnp.concatenate(rows, axis=0), 0.0).astype(BF16))
    return out


def _hgrn_local(ops, scores):
    low_rows = lax.broadcasted_iota(jnp.int32, (LANES, LANES), 0) < HEAD_DIM
    same_head = low_rows == (_lane_iota((LANES, LANES)) < HEAD_DIM)
    o_intra, upd = [], []
    for p in range(N_PAIRS):
        ls = slice(p * LANES, (p + 1) * LANES)
        ip = ops['i'][:, ls]
        o_intra.append(_dot(scores[p], _block_diag(ip)))
        upd.append(jnp.where(same_head, _dot_tn(ip, ops['k_out'][:, ls]), 0.0))
    return o_intra, upd


def _hgrn_carry(ops, o_intra, upd, state_ref, d):
    outs = []
    for p in range(N_PAIRS):
        ls = slice(p * LANES, (p + 1) * LANES)
        st = state_ref[d, p]
        outs.append(o_intra[p] + _dot_nt(ops['q_in'][:, ls], st.astype(BF16)))
        state_ref[d, p] = ops['decay_end'][:, ls] * st + upd[p]
    return jnp.concatenate(outs, axis=1)


def _hgrn_exact_tile(q_ref, k_ref, i_ref, a_ref, o_ref, state_ref, bb, d, reverse, q_sc, k_sc, i_sc):
    c = HGRN_CHUNK
    nchunk = q_ref.shape[1] // c
    lane_head = _lane_iota((BRANCH_W, BRANCH_W)) // HEAD_DIM
    row_head = lax.broadcasted_iota(jnp.int32, (BRANCH_W, BRANCH_W), 0) // HEAD_DIM
    head_sum = (lane_head == row_head).astype(BF16)
    step = lax.broadcasted_iota(jnp.int32, (c, BRANCH_W), 0)
    low_rows = lax.broadcasted_iota(jnp.int32, (LANES, LANES), 0) < HEAD_DIM
    same_head = low_rows == (_lane_iota((LANES, LANES)) < HEAD_DIM)
    for cix in (range(nchunk - 1, -1, -1) if reverse else range(nchunk)):
        rs = slice(cix * c, (cix + 1) * c)
        q_sc[...] = q_ref[bb, rs, :].astype(F32)
        k_sc[...] = k_ref[bb, rs, :].astype(F32)
        i_sc[...] = i_ref[bb, rs, :].astype(F32)
        qf = q_sc[...]
        kf = k_sc[...]
        a = a_ref[d, rs, :]

        def one_source(s, acc, a=a, qf=qf, cix=cix):
            a_s = a_ref[d, pl.ds(cix * c + s, 1), :]
            w = qf * (k_sc[pl.ds(s, 1), :] * jnp.exp2(jnp.minimum(a - a_s, 0.0)))
            w = jnp.where((step <= s) if reverse else (step >= s), w, 0.0)
            hi, lo = _split2(w)
            return acc + (_dot(hi, head_sum) + _dot(lo, head_sum)) * i_sc[pl.ds(s, 1), :]

        o_intra = lax.fori_loop(0, c, one_source, jnp.zeros((c, BRANCH_W), F32))
        a_end = a[0:1, :] if reverse else a[c - 1:c, :]
        q_in = (qf * jnp.exp2(a)).astype(BF16)
        k_out = (kf * jnp.exp2(a_end - a)).astype(BF16)
        decay_end = jnp.exp2(a_end)
        ib = i_ref[bb, rs, :]
        outs = []
        for p in range(N_PAIRS):
            ls = slice(p * LANES, (p + 1) * LANES)
            st = state_ref[d, p]
            outs.append(o_intra[:, ls] + _dot_nt(q_in[:, ls], st.astype(BF16)))
            upd = jnp.where(same_head, _dot_tn(ib[:, ls], k_out[:, ls]), 0.0)
            state_ref[d, p] = decay_end[:, ls] * st + upd
        o_ref[bb, rs, :] = jnp.concatenate(outs, axis=1).astype(BF16)


def _hgrn_kernel(qf_ref, if_ref, lff_ref, kf_ref, qb_ref, ib_ref, lfb_ref, kb_ref, s0f_ref, s0b_ref,
                 of_ref, ob_ref, sf_ref, sb_ref, state_ref, backup_ref, a_ref, q_sc, k_sc, i_sc):
    j = pl.program_id(1)
    nb = qf_ref.shape[0]
    nchunk = qf_ref.shape[1] // HGRN_CHUNK

    @pl.when(j == 0)
    def _():
        for bb in range(nb):
            state_ref[2 * bb] = s0f_ref[bb]
            state_ref[2 * bb + 1] = s0b_ref[bb]

    backup_ref[...] = state_ref[...]

    tm = qf_ref.shape[1]
    row = lax.broadcasted_iota(jnp.int32, (tm, tm), 0)
    col = lax.broadcasted_iota(jnp.int32, (tm, tm), 1)
    same_chunk = (row // HGRN_CHUNK) == (col // HGRN_CHUNK)

    def cum(lf, reverse):
        tri = (same_chunk & ((col >= row) if reverse else (col <= row))).astype(BF16)
        hi, lo = _split2(lf * LOG2E)
        return _dot(tri, hi) + _dot(tri, lo)

    a_f = [cum(lff_ref[bb], False) for bb in range(nb)]
    a_b = [cum(lfb_ref[bb], True) for bb in range(nb)]

    todo = []
    for cix in range(nchunk):
        bix = nchunk - 1 - cix
        for bb in range(nb):
            todo.append((bb, 2 * bb, of_ref, slice(cix * HGRN_CHUNK, (cix + 1) * HGRN_CHUNK),
                         qf_ref, kf_ref, if_ref, a_f[bb], False))
            todo.append((bb, 2 * bb + 1, ob_ref, slice(bix * HGRN_CHUNK, (bix + 1) * HGRN_CHUNK),
                         qb_ref, kb_ref, ib_ref, a_b[bb], True))
    ops, scores, local, lifts = {}, {}, {}, []
    for t in range(len(todo) + 3):
        if t < len(todo):
            bb, _, _, rs, q_ref, k_ref, i_ref, a, reverse = todo[t]
            ops[t] = _hgrn_prepare(q_ref[bb, rs, :], k_ref[bb, rs, :], i_ref[bb, rs, :], a[rs, :], reverse)
            lifts.append(ops[t]['max_lift'])
        if 0 <= t - 1 < len(todo):
            scores[t - 1] = _hgrn_scores(ops[t - 1])
        if 0 <= t - 2 < len(todo):
            local[t - 2] = _hgrn_local(ops[t - 2], scores.pop(t - 2))
        if 0 <= t - 3 < len(todo):
            bb, slot, o_ref, rs = todo[t - 3][:4]
            o_ref[bb, rs, :] = _hgrn_carry(ops.pop(t - 3), *local.pop(t - 3), state_ref, slot).astype(BF16)

    @pl.when(jnp.max(functools.reduce(jnp.maximum, lifts)) > HGRN_EXP_CLAMP)
    def _():
        state_ref[...] = backup_ref[...]
        for bb in range(nb):
            a_ref[2 * bb] = a_f[bb]
            a_ref[2 * bb + 1] = a_b[bb]
            _hgrn_exact_tile(qf_ref, kf_ref, if_ref, a_ref, of_ref, state_ref, bb, 2 * bb, False,
                             q_sc, k_sc, i_sc)
            _hgrn_exact_tile(qb_ref, kb_ref, ib_ref, a_ref, ob_ref, state_ref, bb, 2 * bb + 1, True,
                             q_sc, k_sc, i_sc)

    @pl.when(j == pl.num_programs(1) - 1)
    def _():
        for bb in range(nb):
            sf_ref[bb] = state_ref[2 * bb]
            sb_ref[bb] = state_ref[2 * bb + 1]


def _hgrn_call(q, i, lff, kf, lfb, kb, s0f, s0b, tm):
    b, n, _ = q.shape
    nt = n // tm
    nb = HGRN_BATCH_ROWS if b % HGRN_BATCH_ROWS == 0 else 1
    fwd = pl.BlockSpec((nb, tm, BRANCH_W), lambda bi, j: (bi, j, 0))
    bwd = pl.BlockSpec((nb, tm, BRANCH_W), lambda bi, j: (bi, nt - 1 - j, 0))
    st = pl.BlockSpec((nb, N_PAIRS, LANES, LANES), lambda bi, j: (bi, 0, 0, 0))
    o_shape = jax.ShapeDtypeStruct((b, n, BRANCH_W), BF16)
    s_shape = jax.ShapeDtypeStruct((b, N_PAIRS, LANES, LANES), F32)
    return pl.pallas_call(
        _hgrn_kernel,
        out_shape=(o_shape, o_shape, s_shape, s_shape),
        grid=(b // nb, nt),
        in_specs=[fwd, fwd, fwd, fwd, bwd, bwd, bwd, bwd, st, st],
        out_specs=(fwd, bwd, st, st),
        scratch_shapes=[pltpu.VMEM((2 * nb, N_PAIRS, LANES, LANES), F32),
                        pltpu.VMEM((2 * nb, N_PAIRS, LANES, LANES), F32),
                        pltpu.VMEM((2 * nb, tm, BRANCH_W), F32),
                        pltpu.VMEM((HGRN_CHUNK, BRANCH_W), F32),
                        pltpu.VMEM((HGRN_CHUNK, BRANCH_W), F32),
                        pltpu.VMEM((HGRN_CHUNK, BRANCH_W), F32)],
        compiler_params=_cparams(("arbitrary", "arbitrary")),
        name="hgrn_scan",
    )(q, i, lff, kf, q, i, lfb, kb, s0f, s0b)


def _merge_kernel(x_ref, sh_ref, sc_ref, gt_ref, gpre_ref, gpost_ref, w_ref, oa_ref, ob_ref, mx_ref,
                  of_ref, obk_ref, hn_ref, hm_ref, wb_ref, wo_ref, o_ref):
    tm = x_ref.shape[1]
    sub = min(tm, MERGE_SUB_ROWS)
    for s0 in range(0, tm, sub):
        _merge_rows(slice(s0, s0 + sub), x_ref, sh_ref, sc_ref, gt_ref, gpre_ref, gpost_ref, w_ref, oa_ref,
                    ob_ref, mx_ref, of_ref, obk_ref, hn_ref, hm_ref, wb_ref, wo_ref, o_ref)


def _merge_rows(rs, x_ref, sh_ref, sc_ref, gt_ref, gpre_ref, gpost_ref, w_ref, oa_ref, ob_ref, mx_ref,
                of_ref, obk_ref, hn_ref, hm_ref, wb_ref, wo_ref, o_ref):
    x = x_ref[0, rs, :]
    hb = _normed_input(x, gpre_ref[...], sc_ref[0], sh_ref[0]).astype(BF16)

    def proj(name):
        j = _MERGE_NAMES.index(name)
        return _dot(hb, w_ref[:, j * BRANCH_W:(j + 1) * BRANCH_W])

    ya = oa_ref[0, rs, :].astype(F32) * _silu(proj('a_g'))
    yb = ob_ref[0, rs, :].astype(F32) * _silu(proj('b_g'))
    yc = proj('c_u') * mx_ref[0, rs, :].astype(F32) * _silu(proj('c_g'))
    o = of_ref[0, rs, :].astype(F32) + obk_ref[0, rs, :].astype(F32)
    ms = _dot((o * o).astype(BF16), hm_ref[...])
    yd = o * lax.rsqrt(ms + EPS) * hn_ref[...] * _silu(proj('d_g'))

    g0 = len(_MERGE_NAMES) * BRANCH_W
    merged = None
    for r, y in enumerate((ya, yb, yc, yd)):
        gate = _dot(hb, w_ref[:, g0 + r * D_MODEL:g0 + (r + 1) * D_MODEL])
        term = jax.nn.sigmoid(gate) * _dot(y.astype(BF16), wb_ref[r])
        merged = term if merged is None else merged + term
    out = _dot(merged.astype(BF16), wo_ref[...])
    post = out * lax.rsqrt(jnp.mean(out * out, axis=-1, keepdims=True) + EPS) * gpost_ref[...]
    o_ref[0, rs, :] = x + gt_ref[0] * post


def _merge_call(x, sh, sc, gt, gpre, gpost, w_merge, oa, ob, mx, of, obk, hn, hmean, wb, wo, tm,
                per_batch_mod):
    b, n, _ = x.shape
    nt = n // tm
    mod_map = (lambda i, j: (i, 0, 0)) if per_batch_mod else (lambda i, j: (0, 0, 0))
    x_spec = pl.BlockSpec((1, tm, D_MODEL), lambda i, j: (i, j, 0))
    br_spec = pl.BlockSpec((1, tm, BRANCH_W), lambda i, j: (i, j, 0))
    mod_spec = pl.BlockSpec((1, 1, D_MODEL), mod_map)
    return pl.pallas_call(
        _merge_kernel,
        out_shape=jax.ShapeDtypeStruct((b, n, D_MODEL), F32),
        grid=(b, nt),
        in_specs=[x_spec, mod_spec, mod_spec, mod_spec,
                  _resident((1, D_MODEL)), _resident((1, D_MODEL)),
                  _resident(w_merge.shape),
                  br_spec, br_spec, br_spec, br_spec, br_spec,
                  _resident((1, BRANCH_W)), _resident((BRANCH_W, BRANCH_W)),
                  _resident(wb.shape), _resident(wo.shape)],
        out_specs=x_spec,
        compiler_params=_cparams(("arbitrary", "arbitrary")),
        name="branch_merge",
    )(x, sh, sc, gt, gpre, gpost, w_merge, oa, ob, mx, of, obk, hn, hmean, wb, wo)


def _rope_tables(n_tok, rotate):
    if not rotate:
        return jnp.ones((n_tok, LANES), F32), jnp.zeros((n_tok, LANES), F32)
    t = jnp.arange(n_tok, dtype=jnp.int32)
    pos = jnp.stack([t // GRID_W, t % GRID_W], axis=-1).astype(F32)
    inv = ROPE_THETA ** (-jnp.arange(ROPE_FREQS, dtype=F32) * 2.0 / (2 * ROPE_FREQS))
    ang = pos[:, :, None] * inv
    cos = jnp.repeat(jnp.cos(ang)[:, :, None, :], 2, axis=2).reshape(n_tok, HEAD_DIM)
    sin = jnp.sin(ang)
    sin_signed = jnp.stack([-sin, sin], axis=2).reshape(n_tok, HEAD_DIM)
    return jnp.tile(cos, (1, 2)), jnp.tile(sin_signed, (1, 2))


def _gather_cols(w_in_l, names):
    return jnp.concatenate([w_in_l[:, _IN_COL[nm] * BRANCH_W:(_IN_COL[nm] + 1) * BRANCH_W] for nm in names],
                           axis=1)


def _row_tile(n):
    return 512 if n % 512 == 0 else 256


def kernel(x, c, ctx, c_ctx, w_ada, b_ada, g_pre, g_post, w_in, na_rpb, fnet_w, gmlp_norm_g, gmlp_ws,
           gmlp_bs, hgrn_lb_logits, hgrn_norm_g, w_branch, w_out):
    batch, n_tok, _ = x.shape
    n_ctx = ctx.shape[1]
    depth = w_in.shape[0]

    w_proj = [_gather_cols(w_in[l], _PROJ_NAMES).astype(BF16) for l in range(depth)]
    w_merge = [jnp.concatenate([_gather_cols(w_in[l], _MERGE_NAMES), w_in[l][:, _GATE_COL0:]], axis=1).astype(BF16)
               for l in range(depth)]
    w_branch_b = w_branch.astype(BF16)
    w_out_b = w_out.astype(BF16)
    gmlp_ws_b = gmlp_ws.astype(BF16).reshape(depth, GMLP_GROUPS // 2, 2, GMLP_CHUNK, GMLP_CHUNK)
    gmlp_ws_b = gmlp_ws_b.transpose(0, 1, 3, 2, 4).reshape(depth, GMLP_GROUPS // 2, GMLP_CHUNK, 2 * GMLP_CHUNK)
    bs_tab = jnp.repeat(jnp.swapaxes(gmlp_bs, 1, 2), BRANCH_W // GMLP_GROUPS, axis=2)
    head_mean = jnp.asarray(np.kron(np.eye(N_HEADS), np.ones((HEAD_DIM, HEAD_DIM)) / HEAD_DIM), BF16)
    cos_x, sin_x = _rope_tables(n_tok, True)
    cos_c, sin_c = _rope_tables(n_ctx, False)

    c_all = jnp.concatenate([c, jnp.broadcast_to(c_ctx[None, :], (8, D_MODEL))], axis=0)
    mod = _modulation(c_all, w_ada, b_ada)
    lower = _lower_bounds(hgrn_lb_logits)

    zero_state = jnp.zeros((batch, N_PAIRS, LANES, LANES), F32)
    tm_x = _row_tile(n_tok)
    tm_c = _row_tile(n_ctx)

    for l in range(depth):
        with_ctx = l < depth - 1
        mod_x = [mod[l, :batch, i * D_MODEL:(i + 1) * D_MODEL].reshape(batch, 1, D_MODEL) for i in range(3)]
        mod_c = [mod[l, batch:batch + 1, i * D_MODEL:(i + 1) * D_MODEL].reshape(1, 1, D_MODEL) for i in range(3)]
        gpre = g_pre[l].reshape(1, D_MODEL)
        gpost = g_post[l].reshape(1, D_MODEL)
        gn = gmlp_norm_g[l].reshape(1, BRANCH_W)
        hn = hgrn_norm_g[l].reshape(1, BRANCH_W)
        lb = lower[l].reshape(1, 2, BRANCH_W)
        bias = _attention_bias(na_rpb[l], n_tok // GRID_W)

        (_, qp_c, k_c, v_c, bx_c, mx_c, dq_c, di_c, lff_c, kf_c, lfb_c, kb_c) = _proj_call(
            ctx, mod_c[0], mod_c[1], gpre, w_proj[l], cos_c, sin_c, gn, gmlp_ws_b[l], bs_tab[l], lb,
            tm_c, False)
        of_c, ob_c, st_f, st_b = _hgrn_call(dq_c, di_c, lff_c, kf_c, lfb_c, kb_c, zero_state, zero_state,
                                            min(HGRN_ROWS, n_ctx))

        (qr, qp, k, v, bx, mx, dq, di, lff, kf, lfb, kb) = _proj_call(
            x, mod_x[0], mod_x[1], gpre, w_proj[l], cos_x, sin_x, gn, gmlp_ws_b[l], bs_tab[l], lb,
            2 * tm_x if n_tok % (2 * tm_x) == 0 else tm_x, True)
        oa = _nattn_call(qr, qp, k, v, k_c, v_c, bias)
        ob = _fourier_call(bx, fnet_w[l])
        of, obk, _, _ = _hgrn_call(dq, di, lff, kf, lfb, kb, st_f, st_b, min(HGRN_ROWS, n_tok))
        x = _merge_call(x, mod_x[0], mod_x[1], mod_x[2], gpre, gpost, w_merge[l], oa, ob, mx, of, obk,
                        hn, head_mean, w_branch_b[l], w_out_b[l],
                        2 * tm_x if n_tok % (2 * tm_x) == 0 else tm_x, True)

        if with_ctx:
            oa_c = _cattn_call(qp_c, k_c, v_c)
            ob_c2 = _fourier_ctx_call(bx_c, fnet_w[l])
            ctx = _merge_call(ctx, mod_c[0], mod_c[1], mod_c[2], gpre, gpost, w_merge[l], oa_c, ob_c2, mx_c,
                              of_c, ob_c, hn, head_mean, w_branch_b[l], w_out_b[l], tm_c, False)
    return x
```
